```python
import numpy as np
import jax
import jax.numpy as jnp
from jax import lax

D_MODEL = 1024
BATCH = 4
SEQ = 4096
DEPTH = 2

D_MIX = D_MODEL
HEAD_DIM = 64
H_FOX = 6
FOX_W = H_FOX * HEAD_DIM
CONV_GROUPS = 4
CONV_W = CONV_GROUPS * HEAD_DIM
CONV_K = 3
H_MLA = 6
Q_LORA = 256
KV_LORA = 256
QK_NOPE = 64
QK_ROPE = 32
V_DIM = 64
MLA_W = H_MLA * V_DIM
N_OUT_HEADS = D_MIX // HEAD_DIM
IN_W = 3 * FOX_W + H_FOX + 3 * CONV_W + Q_LORA + KV_LORA + QK_ROPE
ROPE_THETA = 10000.0
Q_BLOCK = 128
D_FF = 2816
N_EXPERTS = 8
TOP_K = 2
D_EXPERT = 3584
MOE_BLOCK = 128
N_DENSE = (DEPTH + 1) // 2
N_MOE = DEPTH // 2
ALPHA = (2 * DEPTH) ** 0.25
BETA = (8 * DEPTH) ** -0.25
LN_EPS = 1e-5
RMS_EPS = 1e-6

kernel_name = 'hybrid_fox_conv_mla_deepnorm_moe'


def _layer_norm(x, g, b):
    xf = x.astype(jnp.float32)
    xc = xf - jnp.mean(xf, axis=-1, keepdims=True)
    var = jnp.mean(xc * xc, axis=-1, keepdims=True)
    return (xc * lax.rsqrt(var + LN_EPS) * g + b).astype(x.dtype)


def _rms_norm(x, g):
    xf = x.astype(jnp.float32)
    return (xf * lax.rsqrt(jnp.mean(xf * xf, axis=-1, keepdims=True) + RMS_EPS) * g).astype(x.dtype)


def _rope(x, cos, sin):
    xf = x.astype(jnp.float32)
    x1, x2 = jnp.split(xf, 2, axis=-1)
    return jnp.concatenate([x1 * cos - x2 * sin, x2 * cos + x1 * sin], axis=-1).astype(x.dtype)


def _block_causal_attention(q, k, v, log_decay=None):
    bsz, seq, nh, dk = q.shape
    nblk = seq // Q_BLOCK
    scale = dk ** -0.5
    key_pos = jnp.arange(seq)
    q_blocks = q.reshape(bsz, nblk, Q_BLOCK, nh, dk).transpose(1, 0, 2, 3, 4)
    xs = (q_blocks, jnp.arange(nblk))
    if log_decay is not None:
        fk = log_decay.transpose(0, 2, 1)
        xs = xs + (log_decay.reshape(bsz, nblk, Q_BLOCK, nh).transpose(1, 0, 2, 3),)

    def one_block(blk):
        qb, i = blk[0], blk[1]
        s = jnp.einsum('bqhd,bkhd->bhqk', qb, k, preferred_element_type=jnp.float32) * scale
        if log_decay is not None:
            s = s + blk[2].transpose(0, 2, 1)[..., None] - fk[:, :, None, :]
        q_pos = i * Q_BLOCK + jnp.arange(Q_BLOCK)
        s = jnp.where(key_pos[None, :] <= q_pos[:, None], s, -jnp.inf)
        p = jax.nn.softmax(s, axis=-1)
        return jnp.einsum('bhqk,bkhd->bqhd', p.astype(v.dtype), v)

    out = lax.map(one_block, xs)
    return out.transpose(1, 0, 2, 3, 4).reshape(bsz, seq, nh, v.shape[-1])


def _causal_depthwise_conv(u, w):
    return lax.conv_general_dilated(
        u, w[:, None, :], window_strides=(1,), padding=[(CONV_K - 1, 0)],
        dimension_numbers=('NWC', 'WIO', 'NWC'), feature_group_count=u.shape[-1])


def _hybrid_mixer(h, w_in, b_f, w_conv, q_g, kv_g, w_uq, w_ukv, head_g, w_o, cos, sin):
    bsz, seq, _ = h.shape
    z = h @ w_in
    sizes = [FOX_W, FOX_W, FOX_W, H_FOX, CONV_W, CONV_W, CONV_W, Q_LORA, KV_LORA, QK_ROPE]
    (fq, fk, fv, f_logit, b_gate, c_gate, h_in, c_q, c_kv, k_rope) = jnp.split(
        z, np.cumsum(sizes)[:-1].tolist(), axis=-1)

    log_f = jax.nn.log_sigmoid(f_logit.astype(jnp.float32) + b_f.astype(jnp.float32))
    cum_log_f = jnp.cumsum(log_f, axis=1)
    o_fox = _block_causal_attention(
        fq.reshape(bsz, seq, H_FOX, HEAD_DIM), fk.reshape(bsz, seq, H_FOX, HEAD_DIM),
        fv.reshape(bsz, seq, H_FOX, HEAD_DIM), cum_log_f)

    o_conv = b_gate * _causal_depthwise_conv(c_gate * h_in, w_conv)

    q = (_rms_norm(c_q, q_g) @ w_uq).reshape(bsz, seq, H_MLA, QK_NOPE + QK_ROPE)
    q = jnp.concatenate([q[..., :QK_NOPE], _rope(q[..., QK_NOPE:], cos[:, None, :], sin[:, None, :])], axis=-1)
    kv = (_rms_norm(c_kv, kv_g) @ w_ukv).reshape(bsz, seq, H_MLA, QK_NOPE + V_DIM)
    k_r = _rope(k_rope, cos, sin)
    k = jnp.concatenate(
        [kv[..., :QK_NOPE], jnp.broadcast_to(k_r[:, :, None, :], (bsz, seq, H_MLA, QK_ROPE))], axis=-1)
    o_mla = _block_causal_attention(q, k, kv[..., QK_NOPE:])

    o = jnp.concatenate(
        [o_fox.reshape(bsz, seq, FOX_W), o_conv, o_mla.reshape(bsz, seq, MLA_W)], axis=-1)
    o = _rms_norm(o.reshape(bsz, seq, N_OUT_HEADS, HEAD_DIM), head_g.reshape(N_OUT_HEADS, HEAD_DIM))
    return o.reshape(bsz, seq, D_MIX) @ w_o


def _swiglu(x, wg, wu, wd):
    return (jax.nn.silu(x @ wg) * (x @ wu)) @ wd


def _moe_swiglu(h, w_router, wg, wu, wd):
    bsz, seq, dm = h.shape
    n_tok = bsz * seq
    xf = h.reshape(n_tok, dm)
    logits = (xf @ w_router).astype(jnp.float32)
    top_logits, top_idx = lax.top_k(logits, TOP_K)
    top_w = jax.nn.softmax(top_logits, axis=-1)
    n_asg = n_tok * TOP_K
    e_flat = top_idx.reshape(n_asg)
    tok_flat = jnp.arange(n_asg) // TOP_K
    order = jnp.argsort(e_flat)
    sorted_e = e_flat[order]
    counts = jnp.bincount(e_flat, length=N_EXPERTS)
    padded = (counts + MOE_BLOCK - 1) // MOE_BLOCK * MOE_BLOCK
    pad_end = jnp.cumsum(padded)
    pad_start = pad_end - padded
    start = jnp.cumsum(counts) - counts
    dest_sorted = pad_start[sorted_e] + (jnp.arange(n_asg) - start[sorted_e])
    n_blocks = -(-n_asg // MOE_BLOCK) + N_EXPERTS
    n_rows = n_blocks * MOE_BLOCK
    row_tok = jnp.full((n_rows,), n_tok, dtype=jnp.int32).at[dest_sorted].set(tok_flat[order].astype(jnp.int32))
    x_rows = jnp.concatenate([xf, jnp.zeros((1, dm), xf.dtype)], axis=0)[row_tok]
    block_e = jnp.minimum(
        jnp.searchsorted(pad_end, jnp.arange(n_blocks) * MOE_BLOCK, side='right'), N_EXPERTS - 1)

    def expert_block(args):
        xb, e = args
        return _swiglu(xb, wg[e], wu[e], wd[e])

    y_rows = lax.map(expert_block, (x_rows.reshape(n_blocks, MOE_BLOCK, dm), block_e)).reshape(n_rows, dm)
    dest = jnp.zeros((n_asg,), dtype=dest_sorted.dtype).at[order].set(dest_sorted)
    y = y_rows[dest].reshape(n_tok, TOP_K, dm)
    out = jnp.einsum('tk,tkd->td', top_w.astype(y.dtype), y)
    return out.reshape(bsz, seq, dm)


def setup_inputs(seed: int = 0) -> dict:
    key = jax.random.key(seed)
    ks = jax.random.split(key, 24)

    def nrm(k, shape, scale):
        return jax.random.normal(k, shape, jnp.float32) * scale

    return {
        'x': nrm(ks[0], (BATCH, SEQ, D_MODEL), 1.0),
        'c': nrm(ks[1], (BATCH, D_MODEL), 1.0),
        'w_mod': nrm(ks[2], (DEPTH, D_MODEL, 6 * D_MODEL), 0.3 * D_MODEL ** -0.5),
        'b_mod': nrm(ks[3], (DEPTH, 6 * D_MODEL), 0.02),
        'w_in': nrm(ks[4], (DEPTH, D_MODEL, IN_W), D_MODEL ** -0.5),
        'b_fgate': jnp.linspace(1.0, 5.0, H_FOX, dtype=jnp.float32)[None, :] + nrm(ks[5], (DEPTH, H_FOX), 0.1),
        'w_conv': nrm(ks[6], (DEPTH, CONV_K, CONV_W), CONV_K ** -0.5),
        'q_norm_g': 1.0 + nrm(ks[7], (DEPTH, Q_LORA), 0.02),
        'kv_norm_g': 1.0 + nrm(ks[8], (DEPTH, KV_LORA), 0.02),
        'w_uq': nrm(ks[9], (DEPTH, Q_LORA, H_MLA * (QK_NOPE + QK_ROPE)), Q_LORA ** -0.5),
        'w_ukv': nrm(ks[10], (DEPTH, KV_LORA, H_MLA * (QK_NOPE + V_DIM)), KV_LORA ** -0.5),
        'head_norm_g': 1.0 + nrm(ks[11], (DEPTH, D_MIX), 0.02),
        'w_o': nrm(ks[12], (DEPTH, D_MIX, D_MODEL), BETA * D_MIX ** -0.5),
        'ln1_g': 1.0 + nrm(ks[13], (DEPTH, D_MODEL), 0.02),
        'ln1_b': nrm(ks[14], (DEPTH, D_MODEL), 0.02),
        'ln2_g': 1.0 + nrm(ks[15], (DEPTH, D_MODEL), 0.02),
        'ln2_b': nrm(ks[16], (DEPTH, D_MODEL), 0.02),
        'ffn_w_gate': nrm(ks[17], (N_DENSE, D_MODEL, D_FF), D_MODEL ** -0.5),
        'ffn_w_up': nrm(ks[18], (N_DENSE, D_MODEL, D_FF), D_MODEL ** -0.5),
        'ffn_w_down': nrm(ks[19], (N_DENSE, D_FF, D_MODEL), BETA * D_FF ** -0.5),
        'router_w': nrm(ks[20], (N_MOE, D_MODEL, N_EXPERTS), D_MODEL ** -0.5),
        'exp_w_gate': nrm(ks[21], (N_MOE, N_EXPERTS, D_MODEL, D_EXPERT), D_MODEL ** -0.5),
        'exp_w_up': nrm(ks[22], (N_MOE, N_EXPERTS, D_MODEL, D_EXPERT), D_MODEL ** -0.5),
        'exp_w_down': nrm(ks[23], (N_MOE, N_EXPERTS, D_EXPERT, D_MODEL), BETA * D_EXPERT ** -0.5),
    }


def reference(x, c, w_mod, b_mod, w_in, b_fgate, w_conv, q_norm_g, kv_norm_g, w_uq, w_ukv,
              head_norm_g, w_o, ln1_g, ln1_b, ln2_g, ln2_b, ffn_w_gate, ffn_w_up, ffn_w_down,
              router_w, exp_w_gate, exp_w_up, exp_w_down):
    seq = x.shape[1]
    pos = jnp.arange(seq, dtype=jnp.float32)
    inv_freq = ROPE_THETA ** (-jnp.arange(0, QK_ROPE, 2, dtype=jnp.float32) / QK_ROPE)
    ang = pos[:, None] * inv_freq[None, :]
    cos, sin = jnp.cos(ang), jnp.sin(ang)
    c_act = jax.nn.silu(c)
    for layer in range(DEPTH):
        mod = c_act @ w_mod[layer] + b_mod[layer]
        shift1, scale1, gate1, shift2, scale2, gate2 = [m[:, None, :] for m in jnp.split(mod, 6, axis=-1)]
        h = x * (1.0 + scale1) + shift1
        y = _hybrid_mixer(h, w_in[layer], b_fgate[layer], w_conv[layer], q_norm_g[layer],
                          kv_norm_g[layer], w_uq[layer], w_ukv[layer], head_norm_g[layer],
                          w_o[layer], cos, sin)
        x = _layer_norm(ALPHA * x + (1.0 + gate1) * y, ln1_g[layer], ln1_b[layer])
        h = x * (1.0 + scale2) + shift2
        j = layer // 2
        if layer % 2 == 0:
            y = _swiglu(h, ffn_w_gate[j], ffn_w_up[j], ffn_w_down[j])
        else:
            y = _moe_swiglu(h, router_w[j], exp_w_gate[j], exp_w_up[j], exp_w_down[j])
        x = _layer_norm(ALPHA * x + (1.0 + gate2) * y, ln2_g[layer], ln2_b[layer])
    return x
```

```python
import functools

import numpy as np
import jax
import jax.numpy as jnp
from jax import lax
from jax.experimental import pallas as pl
from jax.experimental.pallas import tpu as pltpu

F32 = jnp.float32
BF16 = jnp.bfloat16

D_MODEL = 1024
HEAD_DIM = 64
H_FOX = 6
FOX_W = H_FOX * HEAD_DIM
CONV_W = 256
CONV_K = 3
H_MLA = 6
Q_LORA = 256
KV_LORA = 256
QK_NOPE = 64
QK_ROPE = 32
V_DIM = 64
MLA_W = H_MLA * V_DIM
N_HEADS = H_FOX + H_MLA
ROPE_THETA = 10000.0
N_EXPERTS = 8
TOP_K = 2
LN_EPS = 1e-5
RMS_EPS = 1e-6

LANES = 128
SUBLANES = 8
VMEM_LIMIT_BYTES = 56 * 1024 * 1024

_QF0, _KF0, _VF0, _FL0 = 0, 768, 1536, 1920
_BG0, _CG0, _HI0, _CQ0, _CKV0, _KR0, _KRS0, _NA = 2048, 2304, 2560, 2816, 3072, 3328, 3456, 3584
_F_Q_LANE = 64
_N_SPLIT = 3


def _const_spec(shape):
    zeros = (0,) * len(shape)
    return pl.BlockSpec(shape, lambda *_: zeros, pipeline_mode=pl.Buffered(1))


def _silu(v):
    return v * (1.0 / (1.0 + jnp.exp(-v)))


def _split3(v):
    hi = v.astype(BF16)
    r1 = v - hi.astype(F32)
    mid = r1.astype(BF16)
    lo = (r1 - mid.astype(F32)).astype(BF16)
    return hi, mid, lo


def _group_mean_sq(v, gmat_ref):
    sq = v * v
    hi = sq.astype(BF16)
    lo = (sq - hi.astype(F32)).astype(BF16)
    g = gmat_ref[...]
    return (jnp.dot(hi, g, preferred_element_type=F32) + jnp.dot(lo, g, preferred_element_type=F32))


def _layer_norm_rows(r, g, b):
    mu = jnp.mean(r, axis=-1, keepdims=True)
    rc = r - mu
    var = jnp.mean(rc * rc, axis=-1, keepdims=True)
    return rc * lax.rsqrt(var + LN_EPS) * g + b


def _mod_kernel(c_ref, w_ref, b_ref, o_ref):
    act = _silu(c_ref[...]).astype(BF16)
    o_ref[0] = jnp.dot(act, w_ref[0].astype(BF16), preferred_element_type=F32) + b_ref[0]


def _modulation(c_pad, w_mod, b_mod):
    depth, d, n = w_mod.shape
    tn = 1024
    return pl.pallas_call(
        _mod_kernel,
        grid=(depth, n // tn),
        in_specs=[pl.BlockSpec((SUBLANES, d), lambda l, j: (0, 0)),
                  pl.BlockSpec((1, d, tn), lambda l, j: (l, 0, j)),
                  pl.BlockSpec((1, 1, tn), lambda l, j: (l, 0, j))],
        out_specs=pl.BlockSpec((1, SUBLANES, tn), lambda l, j: (l, 0, j)),
        out_shape=jax.ShapeDtypeStruct((depth, SUBLANES, n), F32),
        compiler_params=pltpu.CompilerParams(dimension_semantics=("arbitrary", "arbitrary")),
        name="modulation",
    )(c_pad, w_mod, b_mod.reshape(depth, 1, n))


def _inproj_kernel(x_ref, sc_ref, sh_ref, wa_ref, cq_ref, sq_ref, ck_ref, sk_ref, wuq_ref, wukv_ref,
                   qg_ref, kvg_ref, bf_ref, wconv_ref, hgc_ref, tri_ref, pq_ref, pk_ref, oq_ref, ok_ref,
                   gm_ref, qp_ref, kp_ref, vp_ref, oc_ref, fcarry_ref, ubuf_ref, *, ts):
    @pl.when(pl.program_id(1) == 0)
    def _():
        fcarry_ref[...] = jnp.zeros_like(fcarry_ref)
        ubuf_ref[pl.ds(0, SUBLANES), :] = jnp.zeros((SUBLANES, CONV_W), F32)

    hb = (x_ref[0] * (1.0 + sc_ref[0]) + sh_ref[0]).astype(BF16)

    def proj(lo, hi):
        return jnp.dot(hb, wa_ref[:, lo:hi], preferred_element_type=F32)

    a = proj(_FL0, _FL0 + LANES) + bf_ref[...]
    logf = jnp.minimum(a, 0.0) - jnp.log1p(jnp.exp(-jnp.abs(a)))
    tri = tri_ref[...]
    csum = sum(jnp.dot(tri, part, preferred_element_type=F32) for part in _split3(logf))
    fcum = fcarry_ref[...] + csum
    fcarry_ref[...] = fcum[ts - 1:ts, :]
    fsplit = jnp.concatenate(_split3(fcum), axis=1)
    qf = (proj(_QF0, _QF0 + 768) * (HEAD_DIM ** -0.5)
          + jnp.dot(fsplit, pq_ref[...], preferred_element_type=F32) + oq_ref[...])
    kf = (proj(_KF0, _KF0 + 768)
          + jnp.dot(fsplit, pk_ref[...], preferred_element_type=F32) + ok_ref[...])
    for h in range(H_FOX):
        qp_ref[0, h] = qf[:, h * LANES:(h + 1) * LANES].astype(BF16)
        kp_ref[0, h] = kf[:, h * LANES:(h + 1) * LANES].astype(BF16)
    vp_ref[0, :, 0:FOX_W] = proj(_VF0, _VF0 + FOX_W).astype(BF16)

    u = proj(_CG0, _CG0 + CONV_W) * proj(_HI0, _HI0 + CONV_W)
    ubuf_ref[pl.ds(SUBLANES, ts), :] = u
    u1 = ubuf_ref[pl.ds(SUBLANES - 1, ts), :]
    u2 = ubuf_ref[pl.ds(SUBLANES - 2, ts), :]
    ubuf_ref[pl.ds(0, SUBLANES), :] = u[ts - SUBLANES:ts, :]
    wc = wconv_ref[...]
    oc = proj(_BG0, _BG0 + CONV_W) * (wc[0:1, :] * u2 + wc[1:2, :] * u1 + wc[2:3, :] * u)
    ocn = oc * lax.rsqrt(_group_mean_sq(oc, gm_ref) + RMS_EPS) * hgc_ref[...]
    oc_ref[0] = ocn.astype(BF16)

    def rms(v, g):
        return (v * lax.rsqrt(jnp.mean(v * v, axis=-1, keepdims=True) + RMS_EPS) * g).astype(BF16)

    cqn = rms(proj(_CQ0, _CQ0 + Q_LORA), qg_ref[...])
    qm = jnp.dot(cqn, wuq_ref[:, 0:768], preferred_element_type=F32)
    qs = jnp.dot(cqn, wuq_ref[:, 768:1536], preferred_element_type=F32)
    kvn = rms(proj(_CKV0, _CKV0 + KV_LORA), kvg_ref[...])
    kn = jnp.dot(kvn, wukv_ref[:, 0:768], preferred_element_type=F32)
    vp_ref[0, :, FOX_W:FOX_W + MLA_W] = jnp.dot(kvn, wukv_ref[:, 768:768 + MLA_W],
                                                 preferred_element_type=F32).astype(BF16)
    krr = proj(_KR0, _KR0 + LANES) * ck_ref[...] + proj(_KRS0, _KRS0 + LANES) * sk_ref[...]
    cq, sq = cq_ref[...], sq_ref[...]
    for h in range(H_MLA):
        sl = slice(h * LANES, (h + 1) * LANES)
        qp_ref[0, H_FOX + h] = (qm[:, sl] * cq + qs[:, sl] * sq).astype(BF16)
        kp_ref[0, H_FOX + h] = (kn[:, sl] + krr).astype(BF16)


def _inproj(x, scale1, shift1, lw, consts, ts):
    bsz, seq, d = x.shape
    kern = functools.partial(_inproj_kernel, ts=ts)
    row = lambda b, j: (b, 0, 0)
    tab = pl.BlockSpec((ts, LANES), lambda b, j: (j, 0))
    return pl.pallas_call(
        kern,
        grid=(bsz, seq // ts),
        in_specs=[pl.BlockSpec((1, ts, d), lambda b, j: (b, j, 0)),
                  pl.BlockSpec((1, 1, d), row), pl.BlockSpec((1, 1, d), row),
                  _const_spec((d, _NA)), tab, tab, tab, tab,
                  _const_spec((Q_LORA, 1536)), _const_spec((KV_LORA, 768 + MLA_W)),
                  _const_spec((1, Q_LORA)), _const_spec((1, KV_LORA)), _const_spec((1, LANES)),
                  _const_spec((SUBLANES, CONV_W)), _const_spec((1, CONV_W)),
                  _const_spec((ts, ts)), _const_spec((_N_SPLIT * LANES, 768)),
                  _const_spec((_N_SPLIT * LANES, 768)), _const_spec((1, 768)), _const_spec((1, 768)),
                  _const_spec((CONV_W, CONV_W))],
        out_specs=[pl.BlockSpec((1, N_HEADS, ts, LANES), lambda b, j: (b, 0, j, 0)),
                   pl.BlockSpec((1, N_HEADS, ts, LANES), lambda b, j: (b, 0, j, 0)),
                   pl.BlockSpec((1, ts, FOX_W + MLA_W), lambda b, j: (b, j, 0)),
                   pl.BlockSpec((1, ts, CONV_W), lambda b, j: (b, j, 0))],
        out_shape=[jax.ShapeDtypeStruct((bsz, N_HEADS, seq, LANES), BF16),
                   jax.ShapeDtypeStruct((bsz, N_HEADS, seq, LANES), BF16),
                   jax.ShapeDtypeStruct((bsz, seq, FOX_W + MLA_W), BF16),
                   jax.ShapeDtypeStruct((bsz, seq, CONV_W), BF16)],
        scratch_shapes=[pltpu.VMEM((1, LANES), F32), pltpu.VMEM((ts + SUBLANES, CONV_W), F32)],
        compiler_params=pltpu.CompilerParams(dimension_semantics=("arbitrary", "arbitrary"),
                                             vmem_limit_bytes=VMEM_LIMIT_BYTES),
        name="inproj",
    )(x, scale1, shift1, lw["wa"], consts["cq"], consts["sq"], consts["ck"], consts["sk"],
      lw["wuq"], lw["wukv"], lw["qg"], lw["kvg"], lw["bf"], lw["wconv"], lw["hg_conv"],
      consts["tri"], consts["pq"], consts["pk"], consts["oq"], consts["ok"], consts["gm256"])


def _attn_kernel(qi_ref, kj_ref, q_ref, k_ref, v_ref, g_ref, gm_ref, o_ref, m_ref, l_ref, acc_ref, *, tq):
    step = pl.program_id(2)
    qi = qi_ref[step]
    kj = kj_ref[step]

    @pl.when(kj == 0)
    def _():
        m_ref[...] = jnp.full(m_ref.shape, -jnp.inf, F32)
        l_ref[...] = jnp.zeros_like(l_ref)
        acc_ref[...] = jnp.zeros_like(acc_ref)

    first_head = lax.broadcasted_iota(jnp.int32, (tq, LANES), 1) < HEAD_DIM

    def update(masked):
        v = v_ref[0]
        alphas, pvs = [], []
        for h in range(2):
            s = lax.dot_general(q_ref[0, h], k_ref[0, h], (((1,), (1,)), ((), ())),
                                preferred_element_type=F32)
            if masked:
                row = lax.broadcasted_iota(jnp.int32, s.shape, 0)
                col = lax.broadcasted_iota(jnp.int32, s.shape, 1)
                s = jnp.where(col <= row, s, -jnp.inf)
            m_prev = m_ref[h]
            m_new = jnp.maximum(m_prev, jnp.max(s, axis=-1, keepdims=True))
            alpha = jnp.exp(m_prev - m_new)
            p = jnp.exp(s - m_new)
            l_ref[h] = alpha * l_ref[h] + jnp.sum(p, axis=-1, keepdims=True)
            m_ref[h] = m_new
            pvs.append(jnp.dot(p.astype(BF16), v, preferred_element_type=F32))
            alphas.append(alpha)
        acc_ref[...] = (jnp.where(first_head, alphas[0], alphas[1]) * acc_ref[...]
                        + jnp.where(first_head, pvs[0], pvs[1]))

    @pl.when(kj < qi)
    def _():
        update(False)

    @pl.when(kj == qi)
    def _():
        update(True)
        o = acc_ref[...] / jnp.where(first_head, l_ref[0], l_ref[1])
        on = o * lax.rsqrt(_group_mean_sq(o, gm_ref) + RMS_EPS) * g_ref[...]
        o_ref[0] = on.astype(BF16)


def _attention(qp, kp, vp, g_attn, gm128, tq):
    bsz, nh, seq, _ = qp.shape
    nq = seq // tq
    qi_tab = np.concatenate([np.full(i + 1, i) for i in range(nq)]).astype(np.int32)
    kj_tab = np.concatenate([np.arange(i + 1) for i in range(nq)]).astype(np.int32)
    kern = functools.partial(_attn_kernel, tq=tq)
    grid_spec = pltpu.PrefetchScalarGridSpec(
        num_scalar_prefetch=2,
        grid=(bsz, nh // 2, len(qi_tab)),
        in_specs=[pl.BlockSpec((1, 2, tq, LANES), lambda b, p, s, qi, kj: (b, p, qi[s], 0)),
                  pl.BlockSpec((1, 2, tq, LANES), lambda b, p, s, qi, kj: (b, p, kj[s], 0)),
                  pl.BlockSpec((1, tq, LANES), lambda b, p, s, qi, kj: (b, kj[s], p)),
                  pl.BlockSpec((1, LANES), lambda b, p, s, qi, kj: (0, p)),
                  _const_spec((LANES, LANES))],
        out_specs=pl.BlockSpec((1, tq, LANES), lambda b, p, s, qi, kj: (b, qi[s], p)),
        scratch_shapes=[pltpu.VMEM((2, tq, 1), F32), pltpu.VMEM((2, tq, 1), F32),
                        pltpu.VMEM((tq, LANES), F32)])
    return pl.pallas_call(
        kern,
        grid_spec=grid_spec,
        out_shape=jax.ShapeDtypeStruct((bsz, seq, (nh // 2) * LANES), BF16),
        compiler_params=pltpu.CompilerParams(
            dimension_semantics=("arbitrary", "arbitrary", "arbitrary"),
            vmem_limit_bytes=VMEM_LIMIT_BYTES),
        name="attention",
    )(jnp.asarray(qi_tab), jnp.asarray(kj_tab), qp, kp, vp, g_attn, gm128)


def _outproj_kernel(oa_ref, oc_ref, wa_ref, wc_ref, x_ref, gate_ref, g_ref, b_ref, o_ref, *, alpha):
    y = (jnp.dot(oa_ref[0], wa_ref[...], preferred_element_type=F32)
         + jnp.dot(oc_ref[0], wc_ref[...], preferred_element_type=F32))
    r = alpha * x_ref[0] + (1.0 + gate_ref[0]) * y
    o_ref[0] = _layer_norm_rows(r, g_ref[...], b_ref[...])


def _outproj(oa, oc, lw, x, gate1, alpha, tm):
    bsz, seq, d = x.shape
    wa_rows = oa.shape[-1]
    row = lambda b, j: (b, 0, 0)
    return pl.pallas_call(
        functools.partial(_outproj_kernel, alpha=alpha),
        grid=(bsz, seq // tm),
        in_specs=[pl.BlockSpec((1, tm, wa_rows), lambda b, j: (b, j, 0)),
                  pl.BlockSpec((1, tm, CONV_W), lambda b, j: (b, j, 0)),
                  _const_spec((wa_rows, d)), _const_spec((CONV_W, d)),
                  pl.BlockSpec((1, tm, d), lambda b, j: (b, j, 0)),
                  pl.BlockSpec((1, 1, d), row), _const_spec((1, d)), _const_spec((1, d))],
        out_specs=pl.BlockSpec((1, tm, d), lambda b, j: (b, j, 0)),
        out_shape=jax.ShapeDtypeStruct((bsz, seq, d), F32),
        compiler_params=pltpu.CompilerParams(dimension_semantics=("arbitrary", "arbitrary"),
                                             vmem_limit_bytes=VMEM_LIMIT_BYTES),
        name="outproj",
    )(oa, oc, lw["wo_a"], lw["wo_c"], x, gate1, lw["ln1_g"], lw["ln1_b"])


def _ffn_kernel(x_ref, sc_ref, sh_ref, gate_ref, wg_ref, wu_ref, wd_ref, g_ref, b_ref, o_ref, *, alpha, tf):
    x = x_ref[0]
    hb = (x * (1.0 + sc_ref[0]) + sh_ref[0]).astype(BF16)
    acc = jnp.zeros(x.shape, F32)
    for c in range(wg_ref.shape[1] // tf):
        sl = slice(c * tf, (c + 1) * tf)
        gt = jnp.dot(hb, wg_ref[:, sl], preferred_element_type=F32)
        up = jnp.dot(hb, wu_ref[:, sl], preferred_element_type=F32)
        acc = acc + jnp.dot((_silu(gt) * up).astype(BF16), wd_ref[sl, :], preferred_element_type=F32)
    r = alpha * x + (1.0 + gate_ref[0]) * acc
    o_ref[0] = _layer_norm_rows(r, g_ref[...], b_ref[...])


def _ffn_dense(x, scale2, shift2, gate2, wg, wu, wd, ln_g, ln_b, alpha, tm):
    bsz, seq, d = x.shape
    dff = wg.shape[1]
    tf = dff // 2 if (dff // 2) % LANES == 0 else dff
    row = lambda b, j: (b, 0, 0)
    return pl.pallas_call(
        functools.partial(_ffn_kernel, alpha=alpha, tf=tf),
        grid=(bsz, seq // tm),
        in_specs=[pl.BlockSpec((1, tm, d), lambda b, j: (b, j, 0)),
                  pl.BlockSpec((1, 1, d), row), pl.BlockSpec((1, 1, d), row), pl.BlockSpec((1, 1, d), row),
                  _const_spec((d, dff)), _const_spec((d, dff)), _const_spec((dff, d)),
                  _const_spec((1, d)), _const_spec((1, d))],
        out_specs=pl.BlockSpec((1, tm, d), lambda b, j: (b, j, 0)),
        out_shape=jax.ShapeDtypeStruct((bsz, seq, d), F32),
        compiler_params=pltpu.CompilerParams(dimension_semantics=("arbitrary", "arbitrary"),
                                             vmem_limit_bytes=VMEM_LIMIT_BYTES),
        name="ffn_dense",
    )(x, scale2, shift2, gate2, wg, wu, wd, ln_g, ln_b)


def _router_kernel(x_ref, sc_ref, sh_ref, wr_ref, h_ref, idx_ref, w_ref):
    h = x_ref[0] * (1.0 + sc_ref[0]) + sh_ref[0]
    h_ref[0] = h
    h_hi = h.astype(BF16)
    h_lo = (h - h_hi.astype(F32)).astype(BF16)
    w_hi, w_lo = wr_ref[0], wr_ref[1]
    logits = (jnp.dot(h_hi, w_hi, preferred_element_type=F32)
              + jnp.dot(h_hi, w_lo, preferred_element_type=F32)
              + jnp.dot(h_lo, w_hi, preferred_element_type=F32))
    lane = lax.broadcasted_iota(jnp.int32, logits.shape, 1).astype(F32)
    neg = -jnp.inf
    lg = jnp.where(lane < N_EXPERTS, logits, neg)
    m1 = jnp.max(lg, axis=-1, keepdims=True)
    i1 = jnp.min(jnp.where(lg == m1, lane, float(LANES)), axis=-1, keepdims=True)
    lg2 = jnp.where(lane == i1, neg, lg)
    m2 = jnp.max(lg2, axis=-1, keepdims=True)
    i2 = jnp.min(jnp.where(lg2 == m2, lane, float(LANES)), axis=-1, keepdims=True)
    e2 = jnp.exp(m2 - m1)
    denom = 1.0 + e2
    idx_ref[...] = jnp.where(lane == 0.0, i1, i2).astype(jnp.int32)
    w_ref[...] = jnp.where(lane == 0.0, 1.0 / denom, e2 / denom)


def _router(x, scale2, shift2, wr2, tm):
    bsz, seq, d = x.shape
    nj = seq // tm
    row = lambda b, j: (b, 0, 0)
    return pl.pallas_call(
        _router_kernel,
        grid=(bsz, nj),
        in_specs=[pl.BlockSpec((1, tm, d), lambda b, j: (b, j, 0)),
                  pl.BlockSpec((1, 1, d), row), pl.BlockSpec((1, 1, d), row),
                  _const_spec((2, d, LANES))],
        out_specs=[pl.BlockSpec((1, tm, d), lambda b, j: (b, j, 0)),
                   pl.BlockSpec((tm, LANES), lambda b, j: (b * nj + j, 0)),
                   pl.BlockSpec((tm, LANES), lambda b, j: (b * nj + j, 0))],
        out_shape=[jax.ShapeDtypeStruct((bsz, seq, d), F32),
                   jax.ShapeDtypeStruct((bsz * seq, LANES), jnp.int32),
                   jax.ShapeDtypeStruct((bsz * seq, LANES), F32)],
        compiler_params=pltpu.CompilerParams(dimension_semantics=("arbitrary", "arbitrary"),
                                             vmem_limit_bytes=VMEM_LIMIT_BYTES),
        name="router",
    )(x, scale2, shift2, wr2)


def _row_copy(src_hbm, dst_ref, sem, src_row, dst_row):
    return pltpu.make_async_copy(src_hbm.at[pl.ds(src_row, 1)], dst_ref.at[pl.ds(dst_row, 1)], sem)


def _gather_kernel(idx_ref, src_hbm, o_ref, sem, *, rows):
    def start(r, carry):
        _row_copy(src_hbm, o_ref, sem, idx_ref[0, 0, r], r).start()
        return carry

    def wait(r, carry):
        _row_copy(src_hbm, o_ref, sem, 0, r).wait()
        return carry

    lax.fori_loop(0, rows, start, 0)
    lax.fori_loop(0, rows, wait, 0)


def _gather_rows(src, row_idx, rows):
    n_rows = row_idx.shape[0]
    d = src.shape[1]
    nb = n_rows // rows
    return pl.pallas_call(
        functools.partial(_gather_kernel, rows=rows),
        grid=(nb,),
        in_specs=[pl.BlockSpec((1, 1, rows), lambda i: (i, 0, 0), memory_space=pltpu.SMEM),
                  pl.BlockSpec(memory_space=pl.ANY)],
        out_specs=pl.BlockSpec((rows, d), lambda i: (i, 0)),
        out_shape=jax.ShapeDtypeStruct((n_rows, d), src.dtype),
        scratch_shapes=[pltpu.SemaphoreType.DMA(())],
        compiler_params=pltpu.CompilerParams(dimension_semantics=("arbitrary",)),
        name="moe_gather",
    )(row_idx.reshape(nb, 1, rows), src)


def _expert_kernel(be_ref, nu_ref, x_ref, wg_ref, wu_ref, wd_ref, o_ref, acc_ref, xb_ref):
    i = pl.program_id(0)
    f = pl.program_id(1)
    used = i < nu_ref[0]

    @pl.when(jnp.logical_and(used, f == 0))
    def _():
        xb_ref[...] = x_ref[...].astype(BF16)
        acc_ref[...] = jnp.zeros_like(acc_ref)

    @pl.when(used)
    def _():
        xb = xb_ref[...]
        gt = jnp.dot(xb, wg_ref[0], preferred_element_type=F32)
        up = jnp.dot(xb, wu_ref[0], preferred_element_type=F32)
        acc_ref[...] += jnp.dot((_silu(gt) * up).astype(BF16), wd_ref[0], preferred_element_type=F32)

    last = f == pl.num_programs(1) - 1

    @pl.when(jnp.logical_and(used, last))
    def _():
        o_ref[...] = acc_ref[...]

    @pl.when(jnp.logical_and(jnp.logical_not(used), last))
    def _():
        o_ref[...] = jnp.zeros_like(o_ref)


def _experts(x_rows, block_e, n_used, wg, wu, wd, tm, tf):
    n_rows, d = x_rows.shape
    n_exp, _, dff = wg.shape
    nb, nf = n_rows // tm, dff // tf

    def xmap(i, f, be, nu):
        return (jnp.minimum(i, nu[0] - 1), 0)

    def fidx(i, f, nu):
        return jnp.where(i < nu[0], f, nf - 1)

    grid_spec = pltpu.PrefetchScalarGridSpec(
        num_scalar_prefetch=2,
        grid=(nb, nf),
        in_specs=[pl.BlockSpec((tm, d), xmap),
                  pl.BlockSpec((1, d, tf), lambda i, f, be, nu: (be[i], 0, fidx(i, f, nu))),
                  pl.BlockSpec((1, d, tf), lambda i, f, be, nu: (be[i], 0, fidx(i, f, nu))),
                  pl.BlockSpec((1, tf, d), lambda i, f, be, nu: (be[i], fidx(i, f, nu), 0))],
        out_specs=pl.BlockSpec((tm, d), lambda i, f, be, nu: (i, 0)),
        scratch_shapes=[pltpu.VMEM((tm, d), F32), pltpu.VMEM((tm, d), BF16)])
    return pl.pallas_call(
        _expert_kernel,
        grid_spec=grid_spec,
        out_shape=jax.ShapeDtypeStruct((n_rows, d), F32),
        compiler_params=pltpu.CompilerParams(dimension_semantics=("arbitrary", "arbitrary"),
                                             vmem_limit_bytes=VMEM_LIMIT_BYTES),
        name="moe_experts",
    )(block_e, n_used, x_rows, wg, wu, wd)


def _combine_kernel(dest_ref, y_hbm, w_ref, x_ref, gate_ref, g_ref, b_ref, o_ref, ybuf_ref, sem, *, tc, alpha):
    def start(r, carry):
        _row_copy(y_hbm, ybuf_ref, sem, dest_ref[0, 0, r], r).start()
        return carry

    def wait(r, carry):
        _row_copy(y_hbm, ybuf_ref, sem, 0, r).wait()
        return carry

    lax.fori_loop(0, TOP_K * tc, start, 0)
    lax.fori_loop(0, TOP_K * tc, wait, 0)
    w = w_ref[...]
    y = w[:, 0:1] * ybuf_ref[pl.ds(0, tc), :] + w[:, 1:2] * ybuf_ref[pl.ds(tc, tc), :]
    r = alpha * x_ref[0] + (1.0 + gate_ref[0]) * y
    o_ref[0] = _layer_norm_rows(r, g_ref[...], b_ref[...])


def _combine(y_rows, dest_blocks, top_w, x, gate2, ln_g, ln_b, alpha, tc):
    bsz, seq, d = x.shape
    nj = seq // tc
    row = lambda b, j: (b, 0, 0)
    return pl.pallas_call(
        functools.partial(_combine_kernel, tc=tc, alpha=alpha),
        grid=(bsz, nj),
        in_specs=[pl.BlockSpec((1, 1, TOP_K * tc), lambda b, j: (b * nj + j, 0, 0), memory_space=pltpu.SMEM),
                  pl.BlockSpec(memory_space=pl.ANY),
                  pl.BlockSpec((tc, LANES), lambda b, j: (b * nj + j, 0)),
                  pl.BlockSpec((1, tc, d), lambda b, j: (b, j, 0)),
                  pl.BlockSpec((1, 1, d), row), _const_spec((1, d)), _const_spec((1, d))],
        out_specs=pl.BlockSpec((1, tc, d), lambda b, j: (b, j, 0)),
        out_shape=jax.ShapeDtypeStruct((bsz, seq, d), F32),
        scratch_shapes=[pltpu.VMEM((TOP_K * tc, d), F32), pltpu.SemaphoreType.DMA(())],
        compiler_params=pltpu.CompilerParams(dimension_semantics=("arbitrary", "arbitrary")),
        name="moe_combine",
    )(dest_blocks, y_rows, top_w, x, gate2, ln_g, ln_b)


def _moe(x1, scale2, shift2, gate2, wr2, wg, wu, wd, ln_g, ln_b, alpha, tm_tok, tm, tf, tc):
    bsz, seq, d = x1.shape
    n_tok = bsz * seq
    h2, idx128, w128 = _router(x1, scale2, shift2, wr2, tm_tok)
    e_flat = idx128[:, :TOP_K].reshape(n_tok * TOP_K)
    onehot = (e_flat[:, None] == jnp.arange(N_EXPERTS, dtype=jnp.int32)[None, :]).astype(jnp.int32)
    csum = jnp.cumsum(onehot, axis=0)
    counts = csum[-1]
    rank = jnp.sum((csum - onehot) * onehot, axis=1)
    padded = (counts + tm - 1) // tm * tm
    pad_end = jnp.cumsum(padded)
    pad_start = pad_end - padded
    dest = pad_start[e_flat] + rank
    nb = n_tok * TOP_K // tm + N_EXPERTS
    n_rows = nb * tm
    tok_flat = jnp.arange(n_tok * TOP_K, dtype=jnp.int32) // TOP_K
    row_tok = jnp.zeros((n_rows,), jnp.int32).at[dest].set(tok_flat)
    block_e = jnp.minimum(jnp.searchsorted(pad_end, jnp.arange(nb, dtype=jnp.int32) * tm, side="right"),
                          N_EXPERTS - 1).astype(jnp.int32)
    n_used = (pad_end[-1:] // tm).astype(jnp.int32)
    x_rows = _gather_rows(h2.reshape(n_tok, d), row_tok, 256)
    y_rows = _experts(x_rows, block_e, n_used, wg, wu, wd, tm, tf)
    dest_blocks = dest.reshape(n_tok // tc, tc, TOP_K).transpose(0, 2, 1).reshape(n_tok // tc, 1, TOP_K * tc)
    return _combine(y_rows, dest_blocks.astype(jnp.int32), w128, x1, gate2, ln_g, ln_b, alpha, tc)


def _head_pad(w, heads, width):
    d = w.shape[0]
    w3 = w.reshape(d, heads, width)
    return jnp.pad(w3, ((0, 0), (0, 0), (0, LANES - width))).reshape(d, heads * LANES)


def _layer_weights(layer, w_in, b_fgate, w_conv, q_norm_g, kv_norm_g, w_uq, w_ukv, head_norm_g, w_o,
                   ln1_g, ln1_b):
    d = w_in.shape[1]
    sizes = [FOX_W, FOX_W, FOX_W, H_FOX, CONV_W, CONV_W, CONV_W, Q_LORA, KV_LORA, QK_ROPE]
    offs = np.concatenate([[0], np.cumsum(sizes)])
    wi = w_in[layer]
    fq, fk, fv, fl, bg, cg, hin, cq, ckv, kr = [wi[:, offs[i]:offs[i + 1]] for i in range(len(sizes))]
    half = QK_ROPE // 2
    z64 = jnp.zeros((d, QK_NOPE), F32)
    z32 = jnp.zeros((d, LANES - QK_NOPE - QK_ROPE), F32)
    wa = jnp.concatenate([
        _head_pad(fq, H_FOX, HEAD_DIM), _head_pad(fk, H_FOX, HEAD_DIM), fv,
        jnp.pad(fl, ((0, 0), (0, LANES - H_FOX))), bg, cg, hin, cq, ckv,
        jnp.concatenate([z64, kr, z32], axis=1),
        jnp.concatenate([z64, kr[:, half:], kr[:, :half], z32], axis=1)], axis=1).astype(BF16)
    assert wa.shape[1] == _NA
    q3 = w_uq[layer].reshape(Q_LORA, H_MLA, QK_NOPE + QK_ROPE)
    q_main = jnp.pad(q3, ((0, 0), (0, 0), (0, LANES - QK_NOPE - QK_ROPE)))
    q_swap = jnp.concatenate([jnp.zeros((Q_LORA, H_MLA, QK_NOPE), F32), q3[:, :, QK_NOPE + half:],
                              q3[:, :, QK_NOPE:QK_NOPE + half],
                              jnp.zeros((Q_LORA, H_MLA, LANES - QK_NOPE - QK_ROPE), F32)], axis=2)
    wuq = jnp.concatenate([q_main.reshape(Q_LORA, H_MLA * LANES), q_swap.reshape(Q_LORA, H_MLA * LANES)],
                          axis=1).astype(BF16)
    kv3 = w_ukv[layer].reshape(KV_LORA, H_MLA, QK_NOPE + V_DIM)
    k_nope = jnp.pad(kv3[:, :, :QK_NOPE], ((0, 0), (0, 0), (0, LANES - QK_NOPE))).reshape(KV_LORA, H_MLA * LANES)
    wukv = jnp.concatenate([k_nope, kv3[:, :, QK_NOPE:].reshape(KV_LORA, MLA_W)], axis=1).astype(BF16)
    hg = head_norm_g[layer]
    wo = w_o[layer]
    c0, c1 = FOX_W, FOX_W + CONV_W
    return {
        "wa": wa, "wuq": wuq, "wukv": wukv,
        "qg": q_norm_g[layer].reshape(1, Q_LORA), "kvg": kv_norm_g[layer].reshape(1, KV_LORA),
        "bf": jnp.pad(b_fgate[layer], (0, LANES - H_FOX)).reshape(1, LANES),
        "wconv": jnp.pad(w_conv[layer], ((0, SUBLANES - CONV_K), (0, 0))),
        "hg_conv": hg[c0:c1].reshape(1, CONV_W),
        "hg_attn": jnp.concatenate([hg[:c0], hg[c1:]]).reshape(1, FOX_W + MLA_W),
        "wo_a": jnp.concatenate([wo[:c0], wo[c1:]], axis=0).astype(BF16),
        "wo_c": wo[c0:c1].astype(BF16),
        "ln1_g": ln1_g[layer].reshape(1, d), "ln1_b": ln1_b[layer].reshape(1, d),
    }


def _constants(seq, ts):
    pos = jnp.arange(seq, dtype=F32)
    inv_freq = ROPE_THETA ** (-jnp.arange(0, QK_ROPE, 2, dtype=F32) / QK_ROPE)
    ang = pos[:, None] * inv_freq[None, :]
    cos, sin = jnp.cos(ang), jnp.sin(ang)
    pad_r = jnp.zeros((seq, LANES - QK_NOPE - QK_ROPE), F32)
    ctab = jnp.concatenate([jnp.ones((seq, QK_NOPE), F32), cos, cos, pad_r], axis=1)
    stab = jnp.concatenate([jnp.zeros((seq, QK_NOPE), F32), -sin, sin, pad_r], axis=1)
    q_scale = (QK_NOPE + QK_ROPE) ** -0.5
    pq = np.zeros((_N_SPLIT * LANES, H_FOX * LANES), np.float32)
    pk = np.zeros((_N_SPLIT * LANES, H_FOX * LANES), np.float32)
    oq = np.zeros((1, H_FOX * LANES), np.float32)
    ok = np.zeros((1, H_FOX * LANES), np.float32)
    for h in range(H_FOX):
        for part in range(_N_SPLIT):
            pq[part * LANES + h, h * LANES + _F_Q_LANE + part] = 1.0
            pk[part * LANES + h, h * LANES + _F_Q_LANE + _N_SPLIT + part] = -1.0
            oq[0, h * LANES + _F_Q_LANE + _N_SPLIT + part] = 1.0
            ok[0, h * LANES + _F_Q_LANE + part] = 1.0

    def group_mean(n):
        gidx = np.arange(n) // HEAD_DIM
        return (gidx[:, None] == gidx[None, :]).astype(np.float32) / HEAD_DIM

    return {
        "cq": ctab * q_scale, "sq": stab * q_scale, "ck": ctab, "sk": stab,
        "tri": jnp.asarray(np.tril(np.ones((ts, ts), np.float32)), BF16),
        "pq": jnp.asarray(pq, BF16), "pk": jnp.asarray(pk, BF16),
        "oq": jnp.asarray(oq), "ok": jnp.asarray(ok),
        "gm256": jnp.asarray(group_mean(CONV_W), BF16), "gm128": jnp.asarray(group_mean(LANES), BF16),
    }


def _tile(n, pref):
    t = min(n, pref)
    assert n % t == 0, (n, pref)
    return t


def kernel(x, c, w_mod, b_mod, w_in, b_fgate, w_conv, q_norm_g, kv_norm_g, w_uq, w_ukv, head_norm_g, w_o, ln1_g, ln1_b, ln2_g, ln2_b, ffn_w_gate, ffn_w_up, ffn_w_down, router_w, exp_w_gate, exp_w_up, exp_w_down):
    bsz, seq, d = x.shape
    depth = w_mod.shape[0]
    assert d == D_MODEL and bsz <= SUBLANES
    alpha = (2 * depth) ** 0.25
    ts = _tile(seq, 512)
    tq = _tile(seq, 512)
    tm_e = 512
    tf_e = 512
    tc = _tile(seq, 128)
    assert (bsz * seq * TOP_K) % tm_e == 0 and exp_w_gate.shape[-1] % tf_e == 0

    consts = _constants(seq, ts)
    c_pad = jnp.pad(c, ((0, SUBLANES - bsz), (0, 0)))
    mod = _modulation(c_pad, w_mod, b_mod)[:, :bsz, :]

    for layer in range(depth):
        m6 = mod[layer].reshape(bsz, 6, 1, d)
        shift1, scale1, gate1, shift2, scale2, gate2 = [m6[:, i] for i in range(6)]
        lw = _layer_weights(layer, w_in, b_fgate, w_conv, q_norm_g, kv_norm_g, w_uq, w_ukv, head_norm_g,
                            w_o, ln1_g, ln1_b)
        qp, kp, vp, oc = _inproj(x, scale1, shift1, lw, consts, ts)
        oa = _attention(qp, kp, vp, lw["hg_attn"], consts["gm128"], tq)
        x = _outproj(oa, oc, lw, x, gate1, alpha, ts)
        j = layer // 2
        g2, b2 = ln2_g[layer].reshape(1, d), ln2_b[layer].reshape(1, d)
        if layer % 2 == 0:
            x = _ffn_dense(x, scale2, shift2, gate2, ffn_w_gate[j].astype(BF16), ffn_w_up[j].astype(BF16),
                           ffn_w_down[j].astype(BF16), g2, b2, alpha, ts)
        else:
            wr = jnp.pad(router_w[j], ((0, 0), (0, LANES - N_EXPERTS)))
            wr_hi = wr.astype(BF16)
            wr_lo = (wr - wr_hi.astype(F32)).astype(BF16)
            x = _moe(x, scale2, shift2, gate2, jnp.stack([wr_hi, wr_lo]),
                     exp_w_gate[j].astype(BF16), exp_w_up[j].astype(BF16), exp_w_down[j].astype(BF16),
                     g2, b2, alpha, ts, tm_e, tf_e, tc)
    return x
```

```python
import functools

import numpy as np
import jax
import jax.numpy as jnp
from jax import lax
from jax.experimental import pallas as pl
from jax.experimental.pallas import tpu as pltpu

F32 = jnp.float32
BF16 = jnp.bfloat16

D_MODEL = 1024
HEAD_DIM = 64
H_FOX = 6
FOX_W = H_FOX * HEAD_DIM
CONV_W = 256
CONV_K = 3
H_MLA = 6
Q_LORA = 256
KV_LORA = 256
QK_NOPE = 64
QK_ROPE = 32
V_DIM = 64
MLA_W = H_MLA * V_DIM
N_HEADS = H_FOX + H_MLA
ROPE_THETA = 10000.0
N_EXPERTS = 8
TOP_K = 2
LN_EPS = 1e-5
RMS_EPS = 1e-6

LANES = 128
SUBLANES = 8
VMEM_LIMIT_BYTES = 56 * 1024 * 1024

_HW = 6 * LANES
_QF0, _KF0, _VF0, _FL0 = 0, _HW, 2 * _HW, 3 * _HW
_BG0 = _FL0 + LANES
_CG0, _HI0, _CQ0, _CKV0 = _BG0 + CONV_W, _BG0 + 2 * CONV_W, _BG0 + 3 * CONV_W, _BG0 + 3 * CONV_W + Q_LORA
_KR0 = _CKV0 + KV_LORA
_KRS0 = _KR0 + LANES
_NA = _KRS0 + LANES
LOG2E = 1.4426950408889634
_F_Q_LANE = 64
_N_SPLIT = 3


def _const_spec(shape):
    zeros = (0,) * len(shape)
    return pl.BlockSpec(shape, lambda *_: zeros, pipeline_mode=pl.Buffered(1))


def _silu(v):
    return v * (1.0 / (1.0 + jnp.exp(-v)))


def _split3(v):
    hi = v.astype(BF16)
    r1 = v - hi.astype(F32)
    mid = r1.astype(BF16)
    lo = (r1 - mid.astype(F32)).astype(BF16)
    return hi, mid, lo


def _group_mean_sq(v, gmat_ref):
    sq = v * v
    hi = sq.astype(BF16)
    lo = (sq - hi.astype(F32)).astype(BF16)
    g = gmat_ref[...]
    return (jnp.dot(hi, g, preferred_element_type=F32) + jnp.dot(lo, g, preferred_element_type=F32))


def _layer_norm_rows(r, g, b):
    mu = jnp.mean(r, axis=-1, keepdims=True)
    rc = r - mu
    var = jnp.mean(rc * rc, axis=-1, keepdims=True)
    return rc * lax.rsqrt(var + LN_EPS) * g + b


def _mod_kernel(c_ref, w_ref, b_ref, o_ref):
    act = _silu(c_ref[...]).astype(BF16)
    o_ref[0] = jnp.dot(act, w_ref[0].astype(BF16), preferred_element_type=F32) + b_ref[0]


def _modulation(c_pad, w_mod, b_mod):
    depth, d, n = w_mod.shape
    tn = 1024
    return pl.pallas_call(
        _mod_kernel,
        grid=(depth, n // tn),
        in_specs=[pl.BlockSpec((SUBLANES, d), lambda l, j: (0, 0)),
                  pl.BlockSpec((1, d, tn), lambda l, j: (l, 0, j)),
                  pl.BlockSpec((1, 1, tn), lambda l, j: (l, 0, j))],
        out_specs=pl.BlockSpec((1, SUBLANES, tn), lambda l, j: (l, 0, j)),
        out_shape=jax.ShapeDtypeStruct((depth, SUBLANES, n), F32),
        compiler_params=pltpu.CompilerParams(dimension_semantics=("arbitrary", "arbitrary")),
        name="modulation",
    )(c_pad, w_mod, b_mod.reshape(depth, 1, n))


def _inproj_kernel(x_ref, sc_ref, sh_ref, wa_ref, cq_ref, sq_ref, ck_ref, sk_ref, wuq_ref, wukv_ref,
                   qg_ref, kvg_ref, bf_ref, wconv_ref, hgc_ref, tri_ref, pq_ref, pk_ref, oq_ref, ok_ref,
                   ov_ref, gm_ref, qp_ref, kp_ref, vt_ref, oc_ref, fcarry_ref, ubuf_ref, *, ts):
    @pl.when(pl.program_id(1) == 0)
    def _():
        fcarry_ref[...] = jnp.zeros_like(fcarry_ref)
        ubuf_ref[pl.ds(0, SUBLANES), :] = jnp.zeros((SUBLANES, CONV_W), F32)

    hb = (x_ref[0] * (1.0 + sc_ref[0]) + sh_ref[0]).astype(BF16)

    def proj(lo, hi):
        return jnp.dot(hb, wa_ref[:, lo:hi], preferred_element_type=F32)

    a = proj(_FL0, _FL0 + LANES) + bf_ref[...]
    logf = jnp.minimum(a, 0.0) - jnp.log1p(jnp.exp(-jnp.abs(a)))
    tri = tri_ref[...]
    csum = sum(jnp.dot(tri, part, preferred_element_type=F32) for part in _split3(logf))
    fcum = fcarry_ref[...] + csum
    fcarry_ref[...] = fcum[ts - 1:ts, :]
    fsplit = jnp.concatenate(_split3(fcum * LOG2E), axis=1)
    qf = (proj(_QF0, _QF0 + _HW) * (HEAD_DIM ** -0.5 * LOG2E)
          + jnp.dot(fsplit, pq_ref[...], preferred_element_type=F32) + oq_ref[...])
    kf = (proj(_KF0, _KF0 + _HW)
          + jnp.dot(fsplit, pk_ref[...], preferred_element_type=F32) + ok_ref[...])
    vf = proj(_VF0, _VF0 + _HW) + ov_ref[...]
    for h in range(H_FOX):
        sl = slice(h * LANES, (h + 1) * LANES)
        qp_ref[0, h] = qf[:, sl].astype(BF16)
        kp_ref[0, h] = kf[:, sl].astype(BF16)
        vt_ref[0, h] = vf[:, sl].T.astype(BF16)

    u = proj(_CG0, _CG0 + CONV_W) * proj(_HI0, _HI0 + CONV_W)
    ubuf_ref[pl.ds(SUBLANES, ts), :] = u
    u1 = ubuf_ref[pl.ds(SUBLANES - 1, ts), :]
    u2 = ubuf_ref[pl.ds(SUBLANES - 2, ts), :]
    ubuf_ref[pl.ds(0, SUBLANES), :] = u[ts - SUBLANES:ts, :]
    wc = wconv_ref[...]
    oc = proj(_BG0, _BG0 + CONV_W) * (wc[0:1, :] * u2 + wc[1:2, :] * u1 + wc[2:3, :] * u)
    ocn = oc * lax.rsqrt(_group_mean_sq(oc, gm_ref) + RMS_EPS) * hgc_ref[...]
    oc_ref[0] = ocn.astype(BF16)

    def rms(v, g):
        return (v * lax.rsqrt(jnp.mean(v * v, axis=-1, keepdims=True) + RMS_EPS) * g).astype(BF16)

    cqn = rms(proj(_CQ0, _CQ0 + Q_LORA), qg_ref[...])
    qm = jnp.dot(cqn, wuq_ref[:, 0:768], preferred_element_type=F32)
    qs = jnp.dot(cqn, wuq_ref[:, 768:1536], preferred_element_type=F32)
    kvn = rms(proj(_CKV0, _CKV0 + KV_LORA), kvg_ref[...])
    kn = jnp.dot(kvn, wukv_ref[:, 0:_HW], preferred_element_type=F32)
    vm = jnp.dot(kvn, wukv_ref[:, _HW:2 * _HW], preferred_element_type=F32) + ov_ref[...]
    krr = proj(_KR0, _KR0 + LANES) * ck_ref[...] + proj(_KRS0, _KRS0 + LANES) * sk_ref[...]
    cq, sq = cq_ref[...], sq_ref[...]
    for h in range(H_MLA):
        sl = slice(h * LANES, (h + 1) * LANES)
        qp_ref[0, H_FOX + h] = (qm[:, sl] * cq + qs[:, sl] * sq).astype(BF16)
        kp_ref[0, H_FOX + h] = (kn[:, sl] + krr).astype(BF16)
        vt_ref[0, H_FOX + h] = vm[:, sl].T.astype(BF16)


def _inproj(x, scale1, shift1, lw, consts, ts):
    bsz, seq, d = x.shape
    kern = functools.partial(_inproj_kernel, ts=ts)
    row = lambda b, j: (b, 0, 0)
    tab = pl.BlockSpec((ts, LANES), lambda b, j: (j, 0))
    return pl.pallas_call(
        kern,
        grid=(bsz, seq // ts),
        in_specs=[pl.BlockSpec((1, ts, d), lambda b, j: (b, j, 0)),
                  pl.BlockSpec((1, 1, d), row), pl.BlockSpec((1, 1, d), row),
                  _const_spec((d, _NA)), tab, tab, tab, tab,
                  _const_spec((Q_LORA, 2 * _HW)), _const_spec((KV_LORA, 2 * _HW)),
                  _const_spec((1, Q_LORA)), _const_spec((1, KV_LORA)), _const_spec((1, LANES)),
                  _const_spec((SUBLANES, CONV_W)), _const_spec((1, CONV_W)),
                  _const_spec((ts, ts)), _const_spec((_N_SPLIT * LANES, _HW)),
                  _const_spec((_N_SPLIT * LANES, _HW)), _const_spec((1, _HW)), _const_spec((1, _HW)),
                  _const_spec((1, _HW)), _const_spec((CONV_W, CONV_W))],
        out_specs=[pl.BlockSpec((1, N_HEADS, ts, LANES), lambda b, j: (b, 0, j, 0)),
                   pl.BlockSpec((1, N_HEADS, ts, LANES), lambda b, j: (b, 0, j, 0)),
                   pl.BlockSpec((1, N_HEADS, LANES, ts), lambda b, j: (b, 0, 0, j)),
                   pl.BlockSpec((1, ts, CONV_W), lambda b, j: (b, j, 0))],
        out_shape=[jax.ShapeDtypeStruct((bsz, N_HEADS, seq, LANES), BF16),
                   jax.ShapeDtypeStruct((bsz, N_HEADS, seq, LANES), BF16),
                   jax.ShapeDtypeStruct((bsz, N_HEADS, LANES, seq), BF16),
                   jax.ShapeDtypeStruct((bsz, seq, CONV_W), BF16)],
        scratch_shapes=[pltpu.VMEM((1, LANES), F32), pltpu.VMEM((ts + SUBLANES, CONV_W), F32)],
        compiler_params=pltpu.CompilerParams(dimension_semantics=("arbitrary", "arbitrary"),
                                             vmem_limit_bytes=VMEM_LIMIT_BYTES),
        name="inproj",
    )(x, scale1, shift1, lw["wa"], consts["cq"], consts["sq"], consts["ck"], consts["sk"],
      lw["wuq"], lw["wukv"], lw["qg"], lw["kvg"], lw["bf"], lw["wconv"], lw["hg_conv"],
      consts["tri"], consts["pq"], consts["pk"], consts["oq"], consts["ok"], consts["ov"], consts["gm256"])


def _attn_kernel(qi_ref, kj_ref, q_ref, k_ref, vt_ref, g_ref, gm_ref, o_ref, m_ref, acc_ref):
    step = pl.program_id(2)
    qi = qi_ref[step]
    kj = kj_ref[step]

    @pl.when(kj == 0)
    def _():
        m_ref[...] = jnp.full(m_ref.shape, -jnp.inf, F32)
        acc_ref[...] = jnp.zeros_like(acc_ref)

    def update(masked):
        for h in range(2):
            st = lax.dot_general(k_ref[0, h], q_ref[0, h], (((1,), (1,)), ((), ())),
                                 preferred_element_type=F32)
            if masked:
                key = lax.broadcasted_iota(jnp.int32, st.shape, 0)
                qry = lax.broadcasted_iota(jnp.int32, st.shape, 1)
                st = jnp.where(key <= qry, st, -jnp.inf)
            m_prev = m_ref[h]
            m_new = jnp.maximum(m_prev, jnp.max(st, axis=0, keepdims=True))
            pt = jnp.exp2(st - m_new).astype(BF16)
            acc_ref[h] = (jnp.exp2(m_prev - m_new) * acc_ref[h]
                          + jnp.dot(vt_ref[0, h], pt, preferred_element_type=F32))
            m_ref[h] = m_new

    @pl.when(kj < qi)
    def _():
        update(False)

    @pl.when(kj == qi)
    def _():
        update(True)
        ot = jnp.concatenate([acc_ref[h, 0:HEAD_DIM, :] / acc_ref[h, HEAD_DIM:2 * HEAD_DIM, :]
                              for h in range(2)], axis=0)
        o = ot.T
        on = o * lax.rsqrt(_group_mean_sq(o, gm_ref) + RMS_EPS) * g_ref[...]
        o_ref[0] = on.astype(BF16)


def _attention(qp, kp, vt, g_attn, gm128, tq):
    bsz, nh, seq, _ = qp.shape
    nq = seq // tq
    qi_tab = np.concatenate([np.full(i + 1, i) for i in range(nq)]).astype(np.int32)
    kj_tab = np.concatenate([np.arange(i + 1) for i in range(nq)]).astype(np.int32)
    grid_spec = pltpu.PrefetchScalarGridSpec(
        num_scalar_prefetch=2,
        grid=(bsz, nh // 2, len(qi_tab)),
        in_specs=[pl.BlockSpec((1, 2, tq, LANES), lambda b, p, s, qi, kj: (b, p, qi[s], 0)),
                  pl.BlockSpec((1, 2, tq, LANES), lambda b, p, s, qi, kj: (b, p, kj[s], 0)),
                  pl.BlockSpec((1, 2, LANES, tq), lambda b, p, s, qi, kj: (b, p, 0, kj[s])),
                  pl.BlockSpec((1, LANES), lambda b, p, s, qi, kj: (0, p)),
                  _const_spec((LANES, LANES))],
        out_specs=pl.BlockSpec((1, tq, LANES), lambda b, p, s, qi, kj: (b, qi[s], p)),
        scratch_shapes=[pltpu.VMEM((2, 1, tq), F32), pltpu.VMEM((2, LANES, tq), F32)])
    return pl.pallas_call(
        _attn_kernel,
        grid_spec=grid_spec,
        out_shape=jax.ShapeDtypeStruct((bsz, seq, (nh // 2) * LANES), BF16),
        compiler_params=pltpu.CompilerParams(
            dimension_semantics=("arbitrary", "arbitrary", "arbitrary"),
            vmem_limit_bytes=VMEM_LIMIT_BYTES),
        name="attention",
    )(jnp.asarray(qi_tab), jnp.asarray(kj_tab), qp, kp, vt, g_attn, gm128)


def _outproj_kernel(oa_ref, oc_ref, wa_ref, wc_ref, x_ref, gate_ref, g_ref, b_ref, o_ref, *, alpha):
    y = (jnp.dot(oa_ref[0], wa_ref[...], preferred_element_type=F32)
         + jnp.dot(oc_ref[0], wc_ref[...], preferred_element_type=F32))
    r = alpha * x_ref[0] + (1.0 + gate_ref[0]) * y
    o_ref[0] = _layer_norm_rows(r, g_ref[...], b_ref[...])


def _outproj(oa, oc, lw, x, gate1, alpha, tm):
    bsz, seq, d = x.shape
    wa_rows = oa.shape[-1]
    row = lambda b, j: (b, 0, 0)
    return pl.pallas_call(
        functools.partial(_outproj_kernel, alpha=alpha),
        grid=(bsz, seq // tm),
        in_specs=[pl.BlockSpec((1, tm, wa_rows), lambda b, j: (b, j, 0)),
                  pl.BlockSpec((1, tm, CONV_W), lambda b, j: (b, j, 0)),
                  _const_spec((wa_rows, d)), _const_spec((CONV_W, d)),
                  pl.BlockSpec((1, tm, d), lambda b, j: (b, j, 0)),
                  pl.BlockSpec((1, 1, d), row), _const_spec((1, d)), _const_spec((1, d))],
        out_specs=pl.BlockSpec((1, tm, d), lambda b, j: (b, j, 0)),
        out_shape=jax.ShapeDtypeStruct((bsz, seq, d), F32),
        compiler_params=pltpu.CompilerParams(dimension_semantics=("arbitrary", "arbitrary"),
                                             vmem_limit_bytes=VMEM_LIMIT_BYTES),
        name="outproj",
    )(oa, oc, lw["wo_a"], lw["wo_c"], x, gate1, lw["ln1_g"], lw["ln1_b"])


def _ffn_kernel(x_ref, sc_ref, sh_ref, gate_ref, wg_ref, wu_ref, wd_ref, g_ref, b_ref, o_ref, *, alpha, tf):
    x = x_ref[0]
    hb = (x * (1.0 + sc_ref[0]) + sh_ref[0]).astype(BF16)
    acc = jnp.zeros(x.shape, F32)
    for c in range(wg_ref.shape[1] // tf):
        sl = slice(c * tf, (c + 1) * tf)
        gt = jnp.dot(hb, wg_ref[:, sl], preferred_element_type=F32)
        up = jnp.dot(hb, wu_ref[:, sl], preferred_element_type=F32)
        acc = acc + jnp.dot((_silu(gt) * up).astype(BF16), wd_ref[sl, :], preferred_element_type=F32)
    r = alpha * x + (1.0 + gate_ref[0]) * acc
    o_ref[0] = _layer_norm_rows(r, g_ref[...], b_ref[...])


def _ffn_dense(x, scale2, shift2, gate2, wg, wu, wd, ln_g, ln_b, alpha, tm):
    bsz, seq, d = x.shape
    dff = wg.shape[1]
    tf = dff // 2 if (dff // 2) % LANES == 0 else dff
    row = lambda b, j: (b, 0, 0)
    return pl.pallas_call(
        functools.partial(_ffn_kernel, alpha=alpha, tf=tf),
        grid=(bsz, seq // tm),
        in_specs=[pl.BlockSpec((1, tm, d), lambda b, j: (b, j, 0)),
                  pl.BlockSpec((1, 1, d), row), pl.BlockSpec((1, 1, d), row), pl.BlockSpec((1, 1, d), row),
                  _const_spec((d, dff)), _const_spec((d, dff)), _const_spec((dff, d)),
                  _const_spec((1, d)), _const_spec((1, d))],
        out_specs=pl.BlockSpec((1, tm, d), lambda b, j: (b, j, 0)),
        out_shape=jax.ShapeDtypeStruct((bsz, seq, d), F32),
        compiler_params=pltpu.CompilerParams(dimension_semantics=("arbitrary", "arbitrary"),
                                             vmem_limit_bytes=VMEM_LIMIT_BYTES),
        name="ffn_dense",
    )(x, scale2, shift2, gate2, wg, wu, wd, ln_g, ln_b)


def _router_kernel(x_ref, sc_ref, sh_ref, wr_ref, h_ref, idx_ref, w_ref):
    h = x_ref[0] * (1.0 + sc_ref[0]) + sh_ref[0]
    h_ref[0] = h
    h_hi = h.astype(BF16)
    h_lo = (h - h_hi.astype(F32)).astype(BF16)
    w_hi, w_lo = wr_ref[0], wr_ref[1]
    logits = (jnp.dot(h_hi, w_hi, preferred_element_type=F32)
              + jnp.dot(h_hi, w_lo, preferred_element_type=F32)
              + jnp.dot(h_lo, w_hi, preferred_element_type=F32))
    lane = lax.broadcasted_iota(jnp.int32, logits.shape, 1).astype(F32)
    neg = -jnp.inf
    lg = jnp.where(lane < N_EXPERTS, logits, neg)
    m1 = jnp.max(lg, axis=-1, keepdims=True)
    i1 = jnp.min(jnp.where(lg == m1, lane, float(LANES)), axis=-1, keepdims=True)
    lg2 = jnp.where(lane == i1, neg, lg)
    m2 = jnp.max(lg2, axis=-1, keepdims=True)
    i2 = jnp.min(jnp.where(lg2 == m2, lane, float(LANES)), axis=-1, keepdims=True)
    e2 = jnp.exp(m2 - m1)
    denom = 1.0 + e2
    idx_ref[...] = jnp.where(lane == 0.0, i1, i2).astype(jnp.int32)
    w_ref[...] = jnp.where(lane == 0.0, 1.0 / denom, e2 / denom)


def _router(x, scale2, shift2, wr2, tm):
    bsz, seq, d = x.shape
    nj = seq // tm
    row = lambda b, j: (b, 0, 0)
    return pl.pallas_call(
        _router_kernel,
        grid=(bsz, nj),
        in_specs=[pl.BlockSpec((1, tm, d), lambda b, j: (b, j, 0)),
                  pl.BlockSpec((1, 1, d), row), pl.BlockSpec((1, 1, d), row),
                  _const_spec((2, d, LANES))],
        out_specs=[pl.BlockSpec((1, tm, d), lambda b, j: (b, j, 0)),
                   pl.BlockSpec((tm, LANES), lambda b, j: (b * nj + j, 0)),
                   pl.BlockSpec((tm, LANES), lambda b, j: (b * nj + j, 0))],
        out_shape=[jax.ShapeDtypeStruct((bsz, seq, d), F32),
                   jax.ShapeDtypeStruct((bsz * seq, LANES), jnp.int32),
                   jax.ShapeDtypeStruct((bsz * seq, LANES), F32)],
        compiler_params=pltpu.CompilerParams(dimension_semantics=("arbitrary", "arbitrary"),
                                             vmem_limit_bytes=VMEM_LIMIT_BYTES),
        name="router",
    )(x, scale2, shift2, wr2)


def _row_copy(src_hbm, dst_ref, sem, src_row, dst_row):
    return pltpu.make_async_copy(src_hbm.at[pl.ds(src_row, 1)], dst_ref.at[pl.ds(dst_row, 1)], sem)


def _gather_kernel(idx_ref, src_hbm, o_ref, sem, *, rows):
    def start(r, carry):
        _row_copy(src_hbm, o_ref, sem, idx_ref[0, 0, r], r).start()
        return carry

    def wait(r, carry):
        _row_copy(src_hbm, o_ref, sem, 0, r).wait()
        return carry

    lax.fori_loop(0, rows, start, 0)
    lax.fori_loop(0, rows, wait, 0)


def _gather_rows(src, row_idx, rows):
    n_rows = row_idx.shape[0]
    d = src.shape[1]
    nb = n_rows // rows
    return pl.pallas_call(
        functools.partial(_gather_kernel, rows=rows),
        grid=(nb,),
        in_specs=[pl.BlockSpec((1, 1, rows), lambda i: (i, 0, 0), memory_space=pltpu.SMEM),
                  pl.BlockSpec(memory_space=pl.ANY)],
        out_specs=pl.BlockSpec((rows, d), lambda i: (i, 0)),
        out_shape=jax.ShapeDtypeStruct((n_rows, d), src.dtype),
        scratch_shapes=[pltpu.SemaphoreType.DMA(())],
        compiler_params=pltpu.CompilerParams(dimension_semantics=("arbitrary",)),
        name="moe_gather",
    )(row_idx.reshape(nb, 1, rows), src)


def _expert_kernel(be_ref, nu_ref, x_ref, wg_ref, wu_ref, wd_ref, o_ref, acc_ref, xb_ref):
    i = pl.program_id(0)
    f = pl.program_id(1)
    used = i < nu_ref[0]

    @pl.when(jnp.logical_and(used, f == 0))
    def _():
        xb_ref[...] = x_ref[...].astype(BF16)
        acc_ref[...] = jnp.zeros_like(acc_ref)

    @pl.when(used)
    def _():
        xb = xb_ref[...]
        gt = jnp.dot(xb, wg_ref[0], preferred_element_type=F32)
        up = jnp.dot(xb, wu_ref[0], preferred_element_type=F32)
        acc_ref[...] += jnp.dot((_silu(gt) * up).astype(BF16), wd_ref[0], preferred_element_type=F32)

    last = f == pl.num_programs(1) - 1

    @pl.when(jnp.logical_and(used, last))
    def _():
        o_ref[...] = acc_ref[...]

    @pl.when(jnp.logical_and(jnp.logical_not(used), last))
    def _():
        o_ref[...] = jnp.zeros_like(o_ref)


def _experts(x_rows, block_e, n_used, wg, wu, wd, tm, tf):
    n_rows, d = x_rows.shape
    n_exp, _, dff = wg.shape
    nb, nf = n_rows // tm, dff // tf

    def xmap(i, f, be, nu):
        return (jnp.minimum(i, nu[0] - 1), 0)

    def fidx(i, f, nu):
        return jnp.where(i < nu[0], f, nf - 1)

    grid_spec = pltpu.PrefetchScalarGridSpec(
        num_scalar_prefetch=2,
        grid=(nb, nf),
        in_specs=[pl.BlockSpec((tm, d), xmap),
                  pl.BlockSpec((1, d, tf), lambda i, f, be, nu: (be[i], 0, fidx(i, f, nu))),
                  pl.BlockSpec((1, d, tf), lambda i, f, be, nu: (be[i], 0, fidx(i, f, nu))),
                  pl.BlockSpec((1, tf, d), lambda i, f, be, nu: (be[i], fidx(i, f, nu), 0))],
        out_specs=pl.BlockSpec((tm, d), lambda i, f, be, nu: (i, 0)),
        scratch_shapes=[pltpu.VMEM((tm, d), F32), pltpu.VMEM((tm, d), BF16)])
    return pl.pallas_call(
        _expert_kernel,
        grid_spec=grid_spec,
        out_shape=jax.ShapeDtypeStruct((n_rows, d), F32),
        compiler_params=pltpu.CompilerParams(dimension_semantics=("arbitrary", "arbitrary"),
                                             vmem_limit_bytes=VMEM_LIMIT_BYTES),
        name="moe_experts",
    )(block_e, n_used, x_rows, wg, wu, wd)


def _combine_kernel(dest_ref, y_hbm, w_ref, x_ref, gate_ref, g_ref, b_ref, o_ref, ybuf_ref, sem, *, tc, alpha):
    def start(r, carry):
        _row_copy(y_hbm, ybuf_ref, sem, dest_ref[0, 0, r], r).start()
        return carry

    def wait(r, carry):
        _row_copy(y_hbm, ybuf_ref, sem, 0, r).wait()
        return carry

    lax.fori_loop(0, TOP_K * tc, start, 0)
    lax.fori_loop(0, TOP_K * tc, wait, 0)
    w = w_ref[...]
    y = w[:, 0:1] * ybuf_ref[pl.ds(0, tc), :] + w[:, 1:2] * ybuf_ref[pl.ds(tc, tc), :]
    r = alpha * x_ref[0] + (1.0 + gate_ref[0]) * y
    o_ref[0] = _layer_norm_rows(r, g_ref[...], b_ref[...])


def _combine(y_rows, dest_blocks, top_w, x, gate2, ln_g, ln_b, alpha, tc):
    bsz, seq, d = x.shape
    nj = seq // tc
    row = lambda b, j: (b, 0, 0)
    return pl.pallas_call(
        functools.partial(_combine_kernel, tc=tc, alpha=alpha),
        grid=(bsz, nj),
        in_specs=[pl.BlockSpec((1, 1, TOP_K * tc), lambda b, j: (b * nj + j, 0, 0), memory_space=pltpu.SMEM),
                  pl.BlockSpec(memory_space=pl.ANY),
                  pl.BlockSpec((tc, LANES), lambda b, j: (b * nj + j, 0)),
                  pl.BlockSpec((1, tc, d), lambda b, j: (b, j, 0)),
                  pl.BlockSpec((1, 1, d), row), _const_spec((1, d)), _const_spec((1, d))],
        out_specs=pl.BlockSpec((1, tc, d), lambda b, j: (b, j, 0)),
        out_shape=jax.ShapeDtypeStruct((bsz, seq, d), F32),
        scratch_shapes=[pltpu.VMEM((TOP_K * tc, d), F32), pltpu.SemaphoreType.DMA(())],
        compiler_params=pltpu.CompilerParams(dimension_semantics=("arbitrary", "arbitrary")),
        name="moe_combine",
    )(dest_blocks, y_rows, top_w, x, gate2, ln_g, ln_b)


def _moe(x1, scale2, shift2, gate2, wr2, wg, wu, wd, ln_g, ln_b, alpha, tm_tok, tm, tf, tc):
    bsz, seq, d = x1.shape
    n_tok = bsz * seq
    h2, idx128, w128 = _router(x1, scale2, shift2, wr2, tm_tok)
    e_flat = idx128[:, :TOP_K].reshape(n_tok * TOP_K)
    onehot = (e_flat[:, None] == jnp.arange(N_EXPERTS, dtype=jnp.int32)[None, :]).astype(jnp.int32)
    csum = jnp.cumsum(onehot, axis=0)
    counts = csum[-1]
    rank = jnp.sum((csum - onehot) * onehot, axis=1)
    padded = (counts + tm - 1) // tm * tm
    pad_end = jnp.cumsum(padded)
    pad_start = pad_end - padded
    dest = pad_start[e_flat] + rank
    nb = n_tok * TOP_K // tm + N_EXPERTS
    n_rows = nb * tm
    tok_flat = jnp.arange(n_tok * TOP_K, dtype=jnp.int32) // TOP_K
    row_tok = jnp.zeros((n_rows,), jnp.int32).at[dest].set(tok_flat)
    block_e = jnp.minimum(jnp.searchsorted(pad_end, jnp.arange(nb, dtype=jnp.int32) * tm, side="right"),
                          N_EXPERTS - 1).astype(jnp.int32)
    n_used = (pad_end[-1:] // tm).astype(jnp.int32)
    x_rows = _gather_rows(h2.reshape(n_tok, d), row_tok, 256)
    y_rows = _experts(x_rows, block_e, n_used, wg, wu, wd, tm, tf)
    dest_blocks = dest.reshape(n_tok // tc, tc, TOP_K).transpose(0, 2, 1).reshape(n_tok // tc, 1, TOP_K * tc)
    return _combine(y_rows, dest_blocks.astype(jnp.int32), w128, x1, gate2, ln_g, ln_b, alpha, tc)


def _head_pad(w, heads, width):
    d = w.shape[0]
    w3 = w.reshape(d, heads, width)
    return jnp.pad(w3, ((0, 0), (0, 0), (0, LANES - width))).reshape(d, heads * LANES)


def _layer_weights(layer, w_in, b_fgate, w_conv, q_norm_g, kv_norm_g, w_uq, w_ukv, head_norm_g, w_o,
                   ln1_g, ln1_b):
    d = w_in.shape[1]
    sizes = [FOX_W, FOX_W, FOX_W, H_FOX, CONV_W, CONV_W, CONV_W, Q_LORA, KV_LORA, QK_ROPE]
    offs = np.concatenate([[0], np.cumsum(sizes)])
    wi = w_in[layer]
    fq, fk, fv, fl, bg, cg, hin, cq, ckv, kr = [wi[:, offs[i]:offs[i + 1]] for i in range(len(sizes))]
    half = QK_ROPE // 2
    z64 = jnp.zeros((d, QK_NOPE), F32)
    z32 = jnp.zeros((d, LANES - QK_NOPE - QK_ROPE), F32)
    wa = jnp.concatenate([
        _head_pad(fq, H_FOX, HEAD_DIM), _head_pad(fk, H_FOX, HEAD_DIM), _head_pad(fv, H_FOX, HEAD_DIM),
        jnp.pad(fl, ((0, 0), (0, LANES - H_FOX))), bg, cg, hin, cq, ckv,
        jnp.concatenate([z64, kr, z32], axis=1),
        jnp.concatenate([z64, kr[:, half:], kr[:, :half], z32], axis=1)], axis=1).astype(BF16)
    assert wa.shape[1] == _NA
    q3 = w_uq[layer].reshape(Q_LORA, H_MLA, QK_NOPE + QK_ROPE)
    q_main = jnp.pad(q3, ((0, 0), (0, 0), (0, LANES - QK_NOPE - QK_ROPE)))
    q_swap = jnp.concatenate([jnp.zeros((Q_LORA, H_MLA, QK_NOPE), F32), q3[:, :, QK_NOPE + half:],
                              q3[:, :, QK_NOPE:QK_NOPE + half],
                              jnp.zeros((Q_LORA, H_MLA, LANES - QK_NOPE - QK_ROPE), F32)], axis=2)
    wuq = jnp.concatenate([q_main.reshape(Q_LORA, H_MLA * LANES), q_swap.reshape(Q_LORA, H_MLA * LANES)],
                          axis=1).astype(BF16)
    kv3 = w_ukv[layer].reshape(KV_LORA, H_MLA, QK_NOPE + V_DIM)
    k_nope = jnp.pad(kv3[:, :, :QK_NOPE], ((0, 0), (0, 0), (0, LANES - QK_NOPE))).reshape(KV_LORA, H_MLA * LANES)
    v_pad = jnp.pad(kv3[:, :, QK_NOPE:], ((0, 0), (0, 0), (0, LANES - V_DIM))).reshape(KV_LORA, H_MLA * LANES)
    wukv = jnp.concatenate([k_nope, v_pad], axis=1).astype(BF16)
    hg = head_norm_g[layer]
    wo = w_o[layer]
    c0, c1 = FOX_W, FOX_W + CONV_W
    return {
        "wa": wa, "wuq": wuq, "wukv": wukv,
        "qg": q_norm_g[layer].reshape(1, Q_LORA), "kvg": kv_norm_g[layer].reshape(1, KV_LORA),
        "bf": jnp.pad(b_fgate[layer], (0, LANES - H_FOX)).reshape(1, LANES),
        "wconv": jnp.pad(w_conv[layer], ((0, SUBLANES - CONV_K), (0, 0))),
        "hg_conv": hg[c0:c1].reshape(1, CONV_W),
        "hg_attn": jnp.concatenate([hg[:c0], hg[c1:]]).reshape(1, FOX_W + MLA_W),
        "wo_a": jnp.concatenate([wo[:c0], wo[c1:]], axis=0).astype(BF16),
        "wo_c": wo[c0:c1].astype(BF16),
        "ln1_g": ln1_g[layer].reshape(1, d), "ln1_b": ln1_b[layer].reshape(1, d),
    }


def _constants(seq, ts):
    pos = jnp.arange(seq, dtype=F32)
    inv_freq = ROPE_THETA ** (-jnp.arange(0, QK_ROPE, 2, dtype=F32) / QK_ROPE)
    ang = pos[:, None] * inv_freq[None, :]
    cos, sin = jnp.cos(ang), jnp.sin(ang)
    pad_r = jnp.zeros((seq, LANES - QK_NOPE - QK_ROPE), F32)
    ctab = jnp.concatenate([jnp.ones((seq, QK_NOPE), F32), cos, cos, pad_r], axis=1)
    stab = jnp.concatenate([jnp.zeros((seq, QK_NOPE), F32), -sin, sin, pad_r], axis=1)
    q_scale = (QK_NOPE + QK_ROPE) ** -0.5 * LOG2E
    ov = np.tile(np.concatenate([np.zeros(V_DIM, np.float32), np.ones(LANES - V_DIM, np.float32)]), 6)[None, :]
    pq = np.zeros((_N_SPLIT * LANES, H_FOX * LANES), np.float32)
    pk = np.zeros((_N_SPLIT * LANES, H_FOX * LANES), np.float32)
    oq = np.zeros((1, H_FOX * LANES), np.float32)
    ok = np.zeros((1, H_FOX * LANES), np.float32)
    for h in range(H_FOX):
        for part in range(_N_SPLIT):
            pq[part * LANES + h, h * LANES + _F_Q_LANE + part] = 1.0
            pk[part * LANES + h, h * LANES + _F_Q_LANE + _N_SPLIT + part] = -1.0
            oq[0, h * LANES + _F_Q_LANE + _N_SPLIT + part] = 1.0
            ok[0, h * LANES + _F_Q_LANE + part] = 1.0

    def group_mean(n):
        gidx = np.arange(n) // HEAD_DIM
        return (gidx[:, None] == gidx[None, :]).astype(np.float32) / HEAD_DIM

    return {
        "cq": ctab * q_scale, "sq": stab * q_scale, "ck": ctab, "sk": stab, "ov": jnp.asarray(ov),
        "tri": jnp.asarray(np.tril(np.ones((ts, ts), np.float32)), BF16),
        "pq": jnp.asarray(pq, BF16), "pk": jnp.asarray(pk, BF16),
        "oq": jnp.asarray(oq), "ok": jnp.asarray(ok),
        "gm256": jnp.asarray(group_mean(CONV_W), BF16), "gm128": jnp.asarray(group_mean(LANES), BF16),
    }


def _tile(n, pref):
    t = min(n, pref)
    assert n % t == 0, (n, pref)
    return t


def kernel(x, c, w_mod, b_mod, w_in, b_fgate, w_conv, q_norm_g, kv_norm_g, w_uq, w_ukv, head_norm_g, w_o, ln1_g, ln1_b, ln2_g, ln2_b, ffn_w_gate, ffn_w_up, ffn_w_down, router_w, exp_w_gate, exp_w_up, exp_w_down):
    bsz, seq, d = x.shape
    depth = w_mod.shape[0]
    assert d == D_MODEL and bsz <= SUBLANES
    alpha = (2 * depth) ** 0.25
    ts = _tile(seq, 512)
    tq = _tile(seq, 512)
    tm_e = 512
    tf_e = 512
    tc = _tile(seq, 128)
    assert (bsz * seq * TOP_K) % tm_e == 0 and exp_w_gate.shape[-1] % tf_e == 0

    consts = _constants(seq, ts)
    c_pad = jnp.pad(c, ((0, SUBLANES - bsz), (0, 0)))
    mod = _modulation(c_pad, w_mod, b_mod)[:, :bsz, :]

    for layer in range(depth):
        m6 = mod[layer].reshape(bsz, 6, 1, d)
        shift1, scale1, gate1, shift2, scale2, gate2 = [m6[:, i] for i in range(6)]
        lw = _layer_weights(layer, w_in, b_fgate, w_conv, q_norm_g, kv_norm_g, w_uq, w_ukv, head_norm_g,
                            w_o, ln1_g, ln1_b)
        qp, kp, vt, oc = _inproj(x, scale1, shift1, lw, consts, ts)
        oa = _attention(qp, kp, vt, lw["hg_attn"], consts["gm128"], tq)
        x = _outproj(oa, oc, lw, x, gate1, alpha, ts)
        j = layer // 2
        g2, b2 = ln2_g[layer].reshape(1, d), ln2_b[layer].reshape(1, d)
        if layer % 2 == 0:
            x = _ffn_dense(x, scale2, shift2, gate2, ffn_w_gate[j].astype(BF16), ffn_w_up[j].astype(BF16),
                           ffn_w_down[j].astype(BF16), g2, b2, alpha, ts)
        else:
            wr = jnp.pad(router_w[j], ((0, 0), (0, LANES - N_EXPERTS)))
            wr_hi = wr.astype(BF16)
            wr_lo = (wr - wr_hi.astype(F32)).astype(BF16)
            x = _moe(x, scale2, shift2, gate2, jnp.stack([wr_hi, wr_lo]),
                     exp_w_gate[j].astype(BF16), exp_w_up[j].astype(BF16), exp_w_down[j].astype(BF16),
                     g2, b2, alpha, ts, tm_e, tf_e, tc)
    return x
```

```python
import functools

import numpy as np
import jax
import jax.numpy as jnp
from jax import lax
from jax.experimental import pallas as pl
from jax.experimental.pallas import tpu as pltpu

F32 = jnp.float32
BF16 = jnp.bfloat16

D_MODEL = 1024
HEAD_DIM = 64
H_FOX = 6
FOX_W = H_FOX * HEAD_DIM
CONV_W = 256
CONV_K = 3
H_MLA = 6
Q_LORA = 256
KV_LORA = 256
QK_NOPE = 64
QK_ROPE = 32
V_DIM = 64
MLA_W = H_MLA * V_DIM
N_HEADS = H_FOX + H_MLA
ROPE_THETA = 10000.0
N_EXPERTS = 8
TOP_K = 2
LN_EPS = 1e-5
RMS_EPS = 1e-6

LANES = 128
SUBLANES = 8
VMEM_LIMIT_BYTES = 56 * 1024 * 1024

_HW = 6 * LANES
_QF0, _KF0, _VF0, _FL0 = 0, _HW, 2 * _HW, 3 * _HW
_BG0 = _FL0 + LANES
_CG0, _HI0, _CQ0, _CKV0 = _BG0 + CONV_W, _BG0 + 2 * CONV_W, _BG0 + 3 * CONV_W, _BG0 + 3 * CONV_W + Q_LORA
_KR0 = _CKV0 + KV_LORA
_KRS0 = _KR0 + LANES
_NA = _KRS0 + LANES
LOG2E = 1.4426950408889634
_F_Q_LANE = 64
_N_SPLIT = 3


def _const_spec(shape):
    zeros = (0,) * len(shape)
    return pl.BlockSpec(shape, lambda *_: zeros, pipeline_mode=pl.Buffered(1))


def _silu(v):
    return v * (1.0 / (1.0 + jnp.exp(-v)))


def _split3(v):
    hi = v.astype(BF16)
    r1 = v - hi.astype(F32)
    mid = r1.astype(BF16)
    lo = (r1 - mid.astype(F32)).astype(BF16)
    return hi, mid, lo


def _group_mean_sq(v, gmat_ref):
    sq = v * v
    hi = sq.astype(BF16)
    lo = (sq - hi.astype(F32)).astype(BF16)
    g = gmat_ref[...]
    return (jnp.dot(hi, g, preferred_element_type=F32) + jnp.dot(lo, g, preferred_element_type=F32))


def _layer_norm_rows(r, g, b):
    mu = jnp.mean(r, axis=-1, keepdims=True)
    rc = r - mu
    var = jnp.mean(rc * rc, axis=-1, keepdims=True)
    return rc * lax.rsqrt(var + LN_EPS) * g + b


def _mod_kernel(c_ref, w_ref, b_ref, o_ref):
    act = _silu(c_ref[...]).astype(BF16)
    o_ref[0] = jnp.dot(act, w_ref[0].astype(BF16), preferred_element_type=F32) + b_ref[0]


def _modulation(c_pad, w_mod, b_mod):
    depth, d, n = w_mod.shape
    tn = 1024
    return pl.pallas_call(
        _mod_kernel,
        grid=(depth, n // tn),
        in_specs=[pl.BlockSpec((SUBLANES, d), lambda l, j: (0, 0)),
                  pl.BlockSpec((1, d, tn), lambda l, j: (l, 0, j)),
                  pl.BlockSpec((1, 1, tn), lambda l, j: (l, 0, j))],
        out_specs=pl.BlockSpec((1, SUBLANES, tn), lambda l, j: (l, 0, j)),
        out_shape=jax.ShapeDtypeStruct((depth, SUBLANES, n), F32),
        compiler_params=pltpu.CompilerParams(dimension_semantics=("arbitrary", "arbitrary")),
        name="modulation",
    )(c_pad, w_mod, b_mod.reshape(depth, 1, n))


def _inproj_kernel(x_ref, sc_ref, sh_ref, wa_ref, cq_ref, sq_ref, ck_ref, sk_ref, wuq_ref, wukv_ref,
                   qg_ref, kvg_ref, bf_ref, wconv_ref, hgc_ref, tri_ref, pq_ref, pk_ref, oq_ref, ok_ref,
                   ov_ref, gm_ref, qp_ref, kp_ref, vt_ref, oc_ref, fcarry_ref, ubuf_ref, *, ts):
    @pl.when(pl.program_id(1) == 0)
    def _():
        fcarry_ref[...] = jnp.zeros_like(fcarry_ref)
        ubuf_ref[pl.ds(0, SUBLANES), :] = jnp.zeros((SUBLANES, CONV_W), F32)

    hb = (x_ref[0] * (1.0 + sc_ref[0]) + sh_ref[0]).astype(BF16)

    def proj(lo, hi):
        return jnp.dot(hb, wa_ref[:, lo:hi], preferred_element_type=F32)

    a = proj(_FL0, _FL0 + LANES) + bf_ref[...]
    logf = jnp.minimum(a, 0.0) - jnp.log1p(jnp.exp(-jnp.abs(a)))
    tri = tri_ref[...]
    csum = sum(jnp.dot(tri, part, preferred_element_type=F32) for part in _split3(logf))
    fcum = fcarry_ref[...] + csum
    fcarry_ref[...] = fcum[ts - 1:ts, :]
    fsplit = jnp.concatenate(_split3(fcum * LOG2E), axis=1)
    qf = (proj(_QF0, _QF0 + _HW) * (HEAD_DIM ** -0.5 * LOG2E)
          + jnp.dot(fsplit, pq_ref[...], preferred_element_type=F32) + oq_ref[...])
    kf = (proj(_KF0, _KF0 + _HW)
          + jnp.dot(fsplit, pk_ref[...], preferred_element_type=F32) + ok_ref[...])
    vf = proj(_VF0, _VF0 + _HW) + ov_ref[...]
    for h in range(H_FOX):
        sl = slice(h * LANES, (h + 1) * LANES)
        qp_ref[0, h] = qf[:, sl].astype(BF16)
        kp_ref[0, h] = kf[:, sl].astype(BF16)
        vt_ref[0, h] = vf[:, sl].T.astype(BF16)

    u = proj(_CG0, _CG0 + CONV_W) * proj(_HI0, _HI0 + CONV_W)
    ubuf_ref[pl.ds(SUBLANES, ts), :] = u
    u1 = ubuf_ref[pl.ds(SUBLANES - 1, ts), :]
    u2 = ubuf_ref[pl.ds(SUBLANES - 2, ts), :]
    ubuf_ref[pl.ds(0, SUBLANES), :] = u[ts - SUBLANES:ts, :]
    wc = wconv_ref[...]
    oc = proj(_BG0, _BG0 + CONV_W) * (wc[0:1, :] * u2 + wc[1:2, :] * u1 + wc[2:3, :] * u)
    ocn = oc * lax.rsqrt(_group_mean_sq(oc, gm_ref) + RMS_EPS) * hgc_ref[...]
    oc_ref[0] = ocn.astype(BF16)

    def rms(v, g):
        return (v * lax.rsqrt(jnp.mean(v * v, axis=-1, keepdims=True) + RMS_EPS) * g).astype(BF16)

    cqn = rms(proj(_CQ0, _CQ0 + Q_LORA), qg_ref[...])
    qm = jnp.dot(cqn, wuq_ref[:, 0:768], preferred_element_type=F32)
    qs = jnp.dot(cqn, wuq_ref[:, 768:1536], preferred_element_type=F32)
    kvn = rms(proj(_CKV0, _CKV0 + KV_LORA), kvg_ref[...])
    kn = jnp.dot(kvn, wukv_ref[:, 0:_HW], preferred_element_type=F32)
    vm = jnp.dot(kvn, wukv_ref[:, _HW:2 * _HW], preferred_element_type=F32) + ov_ref[...]
    krr = proj(_KR0, _KR0 + LANES) * ck_ref[...] + proj(_KRS0, _KRS0 + LANES) * sk_ref[...]
    cq, sq = cq_ref[...], sq_ref[...]
    for h in range(H_MLA):
        sl = slice(h * LANES, (h + 1) * LANES)
        qp_ref[0, H_FOX + h] = (qm[:, sl] * cq + qs[:, sl] * sq).astype(BF16)
        kp_ref[0, H_FOX + h] = (kn[:, sl] + krr).astype(BF16)
        vt_ref[0, H_FOX + h] = vm[:, sl].T.astype(BF16)


def _inproj(x, scale1, shift1, lw, consts, ts):
    bsz, seq, d = x.shape
    kern = functools.partial(_inproj_kernel, ts=ts)
    row = lambda b, j: (b, 0, 0)
    tab = pl.BlockSpec((ts, LANES), lambda b, j: (j, 0))
    return pl.pallas_call(
        kern,
        grid=(bsz, seq // ts),
        in_specs=[pl.BlockSpec((1, ts, d), lambda b, j: (b, j, 0)),
                  pl.BlockSpec((1, 1, d), row), pl.BlockSpec((1, 1, d), row),
                  _const_spec((d, _NA)), tab, tab, tab, tab,
                  _const_spec((Q_LORA, 2 * _HW)), _const_spec((KV_LORA, 2 * _HW)),
                  _const_spec((1, Q_LORA)), _const_spec((1, KV_LORA)), _const_spec((1, LANES)),
                  _const_spec((SUBLANES, CONV_W)), _const_spec((1, CONV_W)),
                  _const_spec((ts, ts)), _const_spec((_N_SPLIT * LANES, _HW)),
                  _const_spec((_N_SPLIT * LANES, _HW)), _const_spec((1, _HW)), _const_spec((1, _HW)),
                  _const_spec((1, _HW)), _const_spec((CONV_W, CONV_W))],
        out_specs=[pl.BlockSpec((1, N_HEADS, ts, LANES), lambda b, j: (b, 0, j, 0)),
                   pl.BlockSpec((1, N_HEADS, ts, LANES), lambda b, j: (b, 0, j, 0)),
                   pl.BlockSpec((1, N_HEADS, LANES, ts), lambda b, j: (b, 0, 0, j)),
                   pl.BlockSpec((1, ts, CONV_W), lambda b, j: (b, j, 0))],
        out_shape=[jax.ShapeDtypeStruct((bsz, N_HEADS, seq, LANES), BF16),
                   jax.ShapeDtypeStruct((bsz, N_HEADS, seq, LANES), BF16),
                   jax.ShapeDtypeStruct((bsz, N_HEADS, LANES, seq), BF16),
                   jax.ShapeDtypeStruct((bsz, seq, CONV_W), BF16)],
        scratch_shapes=[pltpu.VMEM((1, LANES), F32), pltpu.VMEM((ts + SUBLANES, CONV_W), F32)],
        compiler_params=pltpu.CompilerParams(dimension_semantics=("arbitrary", "arbitrary"),
                                             vmem_limit_bytes=VMEM_LIMIT_BYTES),
        name="inproj",
    )(x, scale1, shift1, lw["wa"], consts["cq"], consts["sq"], consts["ck"], consts["sk"],
      lw["wuq"], lw["wukv"], lw["qg"], lw["kvg"], lw["bf"], lw["wconv"], lw["hg_conv"],
      consts["tri"], consts["pq"], consts["pk"], consts["oq"], consts["ok"], consts["ov"], consts["gm256"])


def _attn_kernel(qi_ref, kj_ref, q_ref, k_ref, vt_ref, g_ref, gm_ref, o_ref, m_ref, acc_ref, *, nh):
    step = pl.program_id(2)
    qi = qi_ref[step]
    kj = kj_ref[step]

    @pl.when(kj == 0)
    def _():
        m_ref[...] = jnp.full(m_ref.shape, -jnp.inf, F32)
        acc_ref[...] = jnp.zeros_like(acc_ref)

    def update(masked):
        sts = [lax.dot_general(k_ref[0, h], q_ref[0, h], (((1,), (1,)), ((), ())),
                               preferred_element_type=F32) for h in range(nh)]
        pts, alphas = [], []
        for h in range(nh):
            st = sts[h]
            if masked:
                key = lax.broadcasted_iota(jnp.int32, st.shape, 0)
                qry = lax.broadcasted_iota(jnp.int32, st.shape, 1)
                st = jnp.where(key <= qry, st, -jnp.inf)
            m_prev = m_ref[h]
            m_new = jnp.maximum(m_prev, jnp.max(st, axis=0, keepdims=True))
            pts.append(jnp.exp2(st - m_new).astype(BF16))
            alphas.append(jnp.exp2(m_prev - m_new))
            m_ref[h] = m_new
        for h in range(nh):
            acc_ref[h] = alphas[h] * acc_ref[h] + jnp.dot(vt_ref[0, h], pts[h], preferred_element_type=F32)

    @pl.when(kj < qi)
    def _():
        update(False)

    @pl.when(kj == qi)
    def _():
        update(True)
        for pair in range(nh // 2):
            ot = jnp.concatenate([acc_ref[h, 0:HEAD_DIM, :] / acc_ref[h, HEAD_DIM:2 * HEAD_DIM, :]
                                  for h in (2 * pair, 2 * pair + 1)], axis=0)
            o = ot.T
            sl = slice(pair * LANES, (pair + 1) * LANES)
            on = o * lax.rsqrt(_group_mean_sq(o, gm_ref) + RMS_EPS) * g_ref[:, sl]
            o_ref[0, :, sl] = on.astype(BF16)


def _attention(qp, kp, vt, g_attn, gm128, tq, nh):
    bsz, n_heads, seq, _ = qp.shape
    nq = seq // tq
    ow = (nh // 2) * LANES
    qi_tab = np.concatenate([np.full(i + 1, i) for i in range(nq)]).astype(np.int32)
    kj_tab = np.concatenate([np.arange(i + 1) for i in range(nq)]).astype(np.int32)
    grid_spec = pltpu.PrefetchScalarGridSpec(
        num_scalar_prefetch=2,
        grid=(bsz, n_heads // nh, len(qi_tab)),
        in_specs=[pl.BlockSpec((1, nh, tq, LANES), lambda b, p, s, qi, kj: (b, p, qi[s], 0)),
                  pl.BlockSpec((1, nh, tq, LANES), lambda b, p, s, qi, kj: (b, p, kj[s], 0)),
                  pl.BlockSpec((1, nh, LANES, tq), lambda b, p, s, qi, kj: (b, p, 0, kj[s])),
                  pl.BlockSpec((1, ow), lambda b, p, s, qi, kj: (0, p)),
                  _const_spec((LANES, LANES))],
        out_specs=pl.BlockSpec((1, tq, ow), lambda b, p, s, qi, kj: (b, qi[s], p)),
        scratch_shapes=[pltpu.VMEM((nh, 1, tq), F32), pltpu.VMEM((nh, LANES, tq), F32)])
    return pl.pallas_call(
        functools.partial(_attn_kernel, nh=nh),
        grid_spec=grid_spec,
        out_shape=jax.ShapeDtypeStruct((bsz, seq, (n_heads // 2) * LANES), BF16),
        compiler_params=pltpu.CompilerParams(
            dimension_semantics=("arbitrary", "arbitrary", "arbitrary"),
            vmem_limit_bytes=VMEM_LIMIT_BYTES),
        name="attention",
    )(jnp.asarray(qi_tab), jnp.asarray(kj_tab), qp, kp, vt, g_attn, gm128)


def _outproj_kernel(oa_ref, oc_ref, wa_ref, wc_ref, x_ref, gate_ref, g_ref, b_ref, o_ref, *, alpha):
    y = (jnp.dot(oa_ref[0], wa_ref[...], preferred_element_type=F32)
         + jnp.dot(oc_ref[0], wc_ref[...], preferred_element_type=F32))
    r = alpha * x_ref[0] + (1.0 + gate_ref[0]) * y
    o_ref[0] = _layer_norm_rows(r, g_ref[...], b_ref[...])


def _outproj(oa, oc, lw, x, gate1, alpha, tm):
    bsz, seq, d = x.shape
    wa_rows = oa.shape[-1]
    row = lambda b, j: (b, 0, 0)
    return pl.pallas_call(
        functools.partial(_outproj_kernel, alpha=alpha),
        grid=(bsz, seq // tm),
        in_specs=[pl.BlockSpec((1, tm, wa_rows), lambda b, j: (b, j, 0)),
                  pl.BlockSpec((1, tm, CONV_W), lambda b, j: (b, j, 0)),
                  _const_spec((wa_rows, d)), _const_spec((CONV_W, d)),
                  pl.BlockSpec((1, tm, d), lambda b, j: (b, j, 0)),
                  pl.BlockSpec((1, 1, d), row), _const_spec((1, d)), _const_spec((1, d))],
        out_specs=pl.BlockSpec((1, tm, d), lambda b, j: (b, j, 0)),
        out_shape=jax.ShapeDtypeStruct((bsz, seq, d), F32),
        compiler_params=pltpu.CompilerParams(dimension_semantics=("arbitrary", "arbitrary"),
                                             vmem_limit_bytes=VMEM_LIMIT_BYTES),
        name="outproj",
    )(oa, oc, lw["wo_a"], lw["wo_c"], x, gate1, lw["ln1_g"], lw["ln1_b"])


def _ffn_kernel(x_ref, sc_ref, sh_ref, gate_ref, wg_ref, wu_ref, wd_ref, g_ref, b_ref, o_ref, *, alpha, tf):
    x = x_ref[0]
    hb = (x * (1.0 + sc_ref[0]) + sh_ref[0]).astype(BF16)
    acc = jnp.zeros(x.shape, F32)
    for c in range(wg_ref.shape[1] // tf):
        sl = slice(c * tf, (c + 1) * tf)
        gt = jnp.dot(hb, wg_ref[:, sl], preferred_element_type=F32)
        up = jnp.dot(hb, wu_ref[:, sl], preferred_element_type=F32)
        acc = acc + jnp.dot((_silu(gt) * up).astype(BF16), wd_ref[sl, :], preferred_element_type=F32)
    r = alpha * x + (1.0 + gate_ref[0]) * acc
    o_ref[0] = _layer_norm_rows(r, g_ref[...], b_ref[...])


def _ffn_dense(x, scale2, shift2, gate2, wg, wu, wd, ln_g, ln_b, alpha, tm):
    bsz, seq, d = x.shape
    dff = wg.shape[1]
    tf = dff // 2 if (dff // 2) % LANES == 0 else dff
    row = lambda b, j: (b, 0, 0)
    return pl.pallas_call(
        functools.partial(_ffn_kernel, alpha=alpha, tf=tf),
        grid=(bsz, seq // tm),
        in_specs=[pl.BlockSpec((1, tm, d), lambda b, j: (b, j, 0)),
                  pl.BlockSpec((1, 1, d), row), pl.BlockSpec((1, 1, d), row), pl.BlockSpec((1, 1, d), row),
                  _const_spec((d, dff)), _const_spec((d, dff)), _const_spec((dff, d)),
                  _const_spec((1, d)), _const_spec((1, d))],
        out_specs=pl.BlockSpec((1, tm, d), lambda b, j: (b, j, 0)),
        out_shape=jax.ShapeDtypeStruct((bsz, seq, d), F32),
        compiler_params=pltpu.CompilerParams(dimension_semantics=("arbitrary", "arbitrary"),
                                             vmem_limit_bytes=VMEM_LIMIT_BYTES),
        name="ffn_dense",
    )(x, scale2, shift2, gate2, wg, wu, wd, ln_g, ln_b)


def _to_token_tiles(ref, v, n):
    for c in range(v.shape[1] // LANES):
        ref[pl.ds(c, n, stride=SUBLANES), :] = v[:, c * LANES:(c + 1) * LANES]


def _from_token_tiles(ref, n, first=0, tiles_per_row=1):
    stride = tiles_per_row * SUBLANES
    return jnp.concatenate([ref[pl.ds(first * SUBLANES + c, n, stride=stride), :] for c in range(SUBLANES)],
                           axis=1)


def _router_kernel(x_ref, sc_ref, sh_ref, wr_ref, h_ref, idx_ref, w_ref, *, tm, n_tiles):
    @pl.when(pl.program_id(0) == n_tiles)
    def _():
        h_ref[...] = jnp.zeros_like(h_ref)

    @pl.when(pl.program_id(0) < n_tiles)
    def _():
        _router_body(x_ref, sc_ref, sh_ref, wr_ref, h_ref, idx_ref, w_ref, tm)


def _router_body(x_ref, sc_ref, sh_ref, wr_ref, h_ref, idx_ref, w_ref, tm):
    h = x_ref[0] * (1.0 + sc_ref[0]) + sh_ref[0]
    _to_token_tiles(h_ref, h, tm)
    h_hi = h.astype(BF16)
    h_lo = (h - h_hi.astype(F32)).astype(BF16)
    w_hi, w_lo = wr_ref[0], wr_ref[1]
    logits = (jnp.dot(h_hi, w_hi, preferred_element_type=F32)
              + jnp.dot(h_hi, w_lo, preferred_element_type=F32)
              + jnp.dot(h_lo, w_hi, preferred_element_type=F32))
    lane = lax.broadcasted_iota(jnp.int32, logits.shape, 1).astype(F32)
    neg = -jnp.inf
    lg = jnp.where(lane < N_EXPERTS, logits, neg)
    m1 = jnp.max(lg, axis=-1, keepdims=True)
    i1 = jnp.min(jnp.where(lg == m1, lane, float(LANES)), axis=-1, keepdims=True)
    lg2 = jnp.where(lane == i1, neg, lg)
    m2 = jnp.max(lg2, axis=-1, keepdims=True)
    i2 = jnp.min(jnp.where(lg2 == m2, lane, float(LANES)), axis=-1, keepdims=True)
    e2 = jnp.exp(m2 - m1)
    denom = 1.0 + e2
    idx_ref[...] = jnp.where(lane == 0.0, i1, i2).astype(jnp.int32)
    w_ref[...] = jnp.where(lane == 0.0, 1.0 / denom, e2 / denom)


def _router(x, scale2, shift2, wr2, tm):
    bsz, seq, d = x.shape
    assert d == SUBLANES * LANES
    nj = seq // tm
    n_tiles = bsz * nj
    last = n_tiles - 1
    xmap = lambda i: (jnp.minimum(i, last) // nj, jnp.minimum(i, last) % nj, 0)
    row = lambda i: (jnp.minimum(i, last) // nj, 0, 0)
    tok = lambda i: (jnp.minimum(i, last), 0)
    return pl.pallas_call(
        functools.partial(_router_kernel, tm=tm, n_tiles=n_tiles),
        grid=(n_tiles + 1,),
        in_specs=[pl.BlockSpec((1, tm, d), xmap),
                  pl.BlockSpec((1, 1, d), row), pl.BlockSpec((1, 1, d), row),
                  _const_spec((2, d, LANES))],
        out_specs=[pl.BlockSpec((tm * SUBLANES, LANES), lambda i: (i, 0)),
                   pl.BlockSpec((tm, LANES), tok), pl.BlockSpec((tm, LANES), tok)],
        out_shape=[jax.ShapeDtypeStruct(((n_tiles + 1) * tm * SUBLANES, LANES), F32),
                   jax.ShapeDtypeStruct((bsz * seq, LANES), jnp.int32),
                   jax.ShapeDtypeStruct((bsz * seq, LANES), F32)],
        compiler_params=pltpu.CompilerParams(dimension_semantics=("arbitrary",),
                                             vmem_limit_bytes=VMEM_LIMIT_BYTES),
        name="router",
    )(x, scale2, shift2, wr2)


_COPY_UNROLL = 8


def _tile_copy_kernel(si_ref, di_ref, src_hbm, dst_hbm, sem, *, n):
    step = pl.program_id(0)

    def tile(ref, t):
        return ref.at[pl.ds(pl.multiple_of(t * SUBLANES, SUBLANES), SUBLANES)]

    def issue(g, carry):
        for u in range(_COPY_UNROLL):
            r = g * _COPY_UNROLL + u
            pltpu.make_async_copy(tile(src_hbm, si_ref[0, 0, r]), tile(dst_hbm, di_ref[0, 0, r]),
                                  sem).start(priority=u % 2)
        return carry

    lax.fori_loop(0, n // _COPY_UNROLL, issue, 0)

    def wait_one_step():
        span = pl.ds(0, n * SUBLANES)
        pltpu.make_async_copy(src_hbm.at[span], dst_hbm.at[span], sem).wait()

    @pl.when(step > 0)
    def _():
        wait_one_step()

    @pl.when(step == pl.num_programs(0) - 1)
    def _():
        wait_one_step()


def _tile_copy(src, src_idx, dst_idx, n_dst_tiles, name):
    n = 512
    total = src_idx.shape[0]
    assert total % n == 0 and n % _COPY_UNROLL == 0
    steps = total // n
    idx_spec = pl.BlockSpec((1, 1, n), lambda i: (i, 0, 0), memory_space=pltpu.SMEM)
    return pl.pallas_call(
        functools.partial(_tile_copy_kernel, n=n),
        grid=(steps,),
        in_specs=[idx_spec, idx_spec, pl.BlockSpec(memory_space=pl.ANY)],
        out_specs=pl.BlockSpec(memory_space=pl.ANY),
        out_shape=jax.ShapeDtypeStruct((n_dst_tiles * SUBLANES, LANES), src.dtype),
        scratch_shapes=[pltpu.SemaphoreType.DMA(())],
        compiler_params=pltpu.CompilerParams(dimension_semantics=("arbitrary",), has_side_effects=True),
        name=name,
    )(src_idx.astype(jnp.int32).reshape(steps, 1, n), dst_idx.astype(jnp.int32).reshape(steps, 1, n), src)


def _expert_kernel(be_ref, nu_ref, x_ref, wg_ref, wu_ref, wd_ref, o_ref, acc_ref, xb_ref, *, tm):
    i = pl.program_id(0)
    f = pl.program_id(1)
    used = i < nu_ref[0]

    @pl.when(jnp.logical_and(used, f == 0))
    def _():
        xb_ref[...] = _from_token_tiles(x_ref, tm).astype(BF16)
        acc_ref[...] = jnp.zeros_like(acc_ref)

    @pl.when(used)
    def _():
        xb = xb_ref[...]
        gt = jnp.dot(xb, wg_ref[0], preferred_element_type=F32)
        up = jnp.dot(xb, wu_ref[0], preferred_element_type=F32)
        acc_ref[...] += jnp.dot((_silu(gt) * up).astype(BF16), wd_ref[0], preferred_element_type=F32)

    last = f == pl.num_programs(1) - 1

    @pl.when(jnp.logical_and(used, last))
    def _():
        _to_token_tiles(o_ref, acc_ref[...], tm)

    @pl.when(jnp.logical_and(jnp.logical_not(used), last))
    def _():
        o_ref[...] = jnp.zeros_like(o_ref)


def _experts(x_tiles, nb, block_e, n_used, wg, wu, wd, tm, tf):
    n_exp, d, dff = wg.shape
    nf = dff // tf

    def xmap(i, f, be, nu):
        return (jnp.minimum(i, nu[0] - 1), 0)

    def fidx(i, f, nu):
        return jnp.where(i < nu[0], f, nf - 1)

    grid_spec = pltpu.PrefetchScalarGridSpec(
        num_scalar_prefetch=2,
        grid=(nb, nf),
        in_specs=[pl.BlockSpec((tm * SUBLANES, LANES), xmap),
                  pl.BlockSpec((1, d, tf), lambda i, f, be, nu: (be[i], 0, fidx(i, f, nu))),
                  pl.BlockSpec((1, d, tf), lambda i, f, be, nu: (be[i], 0, fidx(i, f, nu))),
                  pl.BlockSpec((1, tf, d), lambda i, f, be, nu: (be[i], fidx(i, f, nu), 0))],
        out_specs=pl.BlockSpec((tm * SUBLANES, LANES), lambda i, f, be, nu: (i, 0)),
        scratch_shapes=[pltpu.VMEM((tm, d), F32), pltpu.VMEM((tm, d), BF16)])
    return pl.pallas_call(
        functools.partial(_expert_kernel, tm=tm),
        grid_spec=grid_spec,
        out_shape=jax.ShapeDtypeStruct((nb * tm * SUBLANES, LANES), F32),
        compiler_params=pltpu.CompilerParams(dimension_semantics=("arbitrary", "arbitrary"),
                                             vmem_limit_bytes=VMEM_LIMIT_BYTES),
        name="moe_experts",
    )(block_e, n_used, x_tiles, wg, wu, wd)


def _combine_kernel(y_ref, w_ref, x_ref, gate_ref, g_ref, b_ref, o_ref, *, tc, alpha):
    w = w_ref[...]
    y = (w[:, 0:1] * _from_token_tiles(y_ref, tc, 0, TOP_K)
         + w[:, 1:2] * _from_token_tiles(y_ref, tc, 1, TOP_K))
    r = alpha * x_ref[0] + (1.0 + gate_ref[0]) * y
    o_ref[0] = _layer_norm_rows(r, g_ref[...], b_ref[...])


def _combine(y_sel, top_w, x, gate2, ln_g, ln_b, alpha, tc):
    bsz, seq, d = x.shape
    nj = seq // tc
    row = lambda b, j: (b, 0, 0)
    return pl.pallas_call(
        functools.partial(_combine_kernel, tc=tc, alpha=alpha),
        grid=(bsz, nj),
        in_specs=[pl.BlockSpec((tc * TOP_K * SUBLANES, LANES), lambda b, j: (b * nj + j, 0)),
                  pl.BlockSpec((tc, LANES), lambda b, j: (b * nj + j, 0)),
                  pl.BlockSpec((1, tc, d), lambda b, j: (b, j, 0)),
                  pl.BlockSpec((1, 1, d), row), _const_spec((1, d)), _const_spec((1, d))],
        out_specs=pl.BlockSpec((1, tc, d), lambda b, j: (b, j, 0)),
        out_shape=jax.ShapeDtypeStruct((bsz, seq, d), F32),
        compiler_params=pltpu.CompilerParams(dimension_semantics=("arbitrary", "arbitrary"),
                                             vmem_limit_bytes=VMEM_LIMIT_BYTES),
        name="moe_combine",
    )(y_sel, top_w, x, gate2, ln_g, ln_b)


def _moe(x1, scale2, shift2, gate2, wr2, wg, wu, wd, ln_g, ln_b, alpha, tm_tok, tm, tf, tc):
    bsz, seq, d = x1.shape
    n_tok = bsz * seq
    n_asg = n_tok * TOP_K
    h_tiles, idx128, w128 = _router(x1, scale2, shift2, wr2, tm_tok)
    e_flat = idx128[:, :TOP_K].reshape(n_asg)
    onehot = (e_flat[:, None] == jnp.arange(N_EXPERTS, dtype=jnp.int32)[None, :]).astype(jnp.int32)
    csum = jnp.cumsum(onehot, axis=0)
    counts = csum[-1]
    rank = jnp.sum((csum - onehot) * onehot, axis=1)
    padded = (counts + tm - 1) // tm * tm
    pad_end = jnp.cumsum(padded)
    pad_start = pad_end - padded
    dest = pad_start[e_flat] + rank
    nb = n_asg // tm + N_EXPERTS
    n_rows = nb * tm
    n_fill = n_rows - n_asg
    fill_end = jnp.cumsum(padded - counts)
    slot = jnp.arange(n_fill, dtype=jnp.int32)
    slot_e = jnp.searchsorted(fill_end, slot, side="right").astype(jnp.int32)
    in_group = slot_e < N_EXPERTS
    ge = jnp.minimum(slot_e, N_EXPERTS - 1)
    fill_dst = jnp.where(in_group, (pad_start + counts)[ge] + slot - (fill_end - (padded - counts))[ge],
                         pad_end[-1] + slot - fill_end[-1])
    src_idx = jnp.concatenate([jnp.arange(n_asg, dtype=jnp.int32) // TOP_K, jnp.full((n_fill,), n_tok, jnp.int32)])
    dst_idx = jnp.concatenate([dest, fill_dst])
    x_tiles = _tile_copy(h_tiles, src_idx, dst_idx, n_rows, "moe_gather")
    block_e = jnp.minimum(jnp.searchsorted(pad_end, jnp.arange(nb, dtype=jnp.int32) * tm, side="right"),
                          N_EXPERTS - 1).astype(jnp.int32)
    n_used = (pad_end[-1:] // tm).astype(jnp.int32)
    y_tiles = _experts(x_tiles, nb, block_e, n_used, wg, wu, wd, tm, tf)
    y_sel = _tile_copy(y_tiles, dest, jnp.arange(n_asg, dtype=jnp.int32), n_asg, "moe_ungroup")
    return _combine(y_sel, w128, x1, gate2, ln_g, ln_b, alpha, tc)


def _head_pad(w, heads, width):
    d = w.shape[0]
    w3 = w.reshape(d, heads, width)
    return jnp.pad(w3, ((0, 0), (0, 0), (0, LANES - width))).reshape(d, heads * LANES)


def _layer_weights(layer, w_in, b_fgate, w_conv, q_norm_g, kv_norm_g, w_uq, w_ukv, head_norm_g, w_o,
                   ln1_g, ln1_b):
    d = w_in.shape[1]
    sizes = [FOX_W, FOX_W, FOX_W, H_FOX, CONV_W, CONV_W, CONV_W, Q_LORA, KV_LORA, QK_ROPE]
    offs = np.concatenate([[0], np.cumsum(sizes)])
    wi = w_in[layer]
    fq, fk, fv, fl, bg, cg, hin, cq, ckv, kr = [wi[:, offs[i]:offs[i + 1]] for i in range(len(sizes))]
    half = QK_ROPE // 2
    z64 = jnp.zeros((d, QK_NOPE), F32)
    z32 = jnp.zeros((d, LANES - QK_NOPE - QK_ROPE), F32)
    wa = jnp.concatenate([
        _head_pad(fq, H_FOX, HEAD_DIM), _head_pad(fk, H_FOX, HEAD_DIM), _head_pad(fv, H_FOX, HEAD_DIM),
        jnp.pad(fl, ((0, 0), (0, LANES - H_FOX))), bg, cg, hin, cq, ckv,
        jnp.concatenate([z64, kr, z32], axis=1),
        jnp.concatenate([z64, kr[:, half:], kr[:, :half], z32], axis=1)], axis=1).astype(BF16)
    assert wa.shape[1] == _NA
    q3 = w_uq[layer].reshape(Q_LORA, H_MLA, QK_NOPE + QK_ROPE)
    q_main = jnp.pad(q3, ((0, 0), (0, 0), (0, LANES - QK_NOPE - QK_ROPE)))
    q_swap = jnp.concatenate([jnp.zeros((Q_LORA, H_MLA, QK_NOPE), F32), q3[:, :, QK_NOPE + half:],
                              q3[:, :, QK_NOPE:QK_NOPE + half],
                              jnp.zeros((Q_LORA, H_MLA, LANES - QK_NOPE - QK_ROPE), F32)], axis=2)
    wuq = jnp.concatenate([q_main.reshape(Q_LORA, H_MLA * LANES), q_swap.reshape(Q_LORA, H_MLA * LANES)],
                          axis=1).astype(BF16)
    kv3 = w_ukv[layer].reshape(KV_LORA, H_MLA, QK_NOPE + V_DIM)
    k_nope = jnp.pad(kv3[:, :, :QK_NOPE], ((0, 0), (0, 0), (0, LANES - QK_NOPE))).reshape(KV_LORA, H_MLA * LANES)
    v_pad = jnp.pad(kv3[:, :, QK_NOPE:], ((0, 0), (0, 0), (0, LANES - V_DIM))).reshape(KV_LORA, H_MLA * LANES)
    wukv = jnp.concatenate([k_nope, v_pad], axis=1).astype(BF16)
    hg = head_norm_g[layer]
    wo = w_o[layer]
    c0, c1 = FOX_W, FOX_W + CONV_W
    return {
        "wa": wa, "wuq": wuq, "wukv": wukv,
        "qg": q_norm_g[layer].reshape(1, Q_LORA), "kvg": kv_norm_g[layer].reshape(1, KV_LORA),
        "bf": jnp.pad(b_fgate[layer], (0, LANES - H_FOX)).reshape(1, LANES),
        "wconv": jnp.pad(w_conv[layer], ((0, SUBLANES - CONV_K), (0, 0))),
        "hg_conv": hg[c0:c1].reshape(1, CONV_W),
        "hg_attn": jnp.concatenate([hg[:c0], hg[c1:]]).reshape(1, FOX_W + MLA_W),
        "wo_a": jnp.concatenate([wo[:c0], wo[c1:]], axis=0).astype(BF16),
        "wo_c": wo[c0:c1].astype(BF16),
        "ln1_g": ln1_g[layer].reshape(1, d), "ln1_b": ln1_b[layer].reshape(1, d),
    }


def _constants(seq, ts):
    pos = jnp.arange(seq, dtype=F32)
    inv_freq = ROPE_THETA ** (-jnp.arange(0, QK_ROPE, 2, dtype=F32) / QK_ROPE)
    ang = pos[:, None] * inv_freq[None, :]
    cos, sin = jnp.cos(ang), jnp.sin(ang)
    pad_r = jnp.zeros((seq, LANES - QK_NOPE - QK_ROPE), F32)
    ctab = jnp.concatenate([jnp.ones((seq, QK_NOPE), F32), cos, cos, pad_r], axis=1)
    stab = jnp.concatenate([jnp.zeros((seq, QK_NOPE), F32), -sin, sin, pad_r], axis=1)
    q_scale = (QK_NOPE + QK_ROPE) ** -0.5 * LOG2E
    ov = np.tile(np.concatenate([np.zeros(V_DIM, np.float32), np.ones(LANES - V_DIM, np.float32)]), 6)[None, :]
    pq = np.zeros((_N_SPLIT * LANES, H_FOX * LANES), np.float32)
    pk = np.zeros((_N_SPLIT * LANES, H_FOX * LANES), np.float32)
    oq = np.zeros((1, H_FOX * LANES), np.float32)
    ok = np.zeros((1, H_FOX * LANES), np.float32)
    for h in range(H_FOX):
        for part in range(_N_SPLIT):
            pq[part * LANES + h, h * LANES + _F_Q_LANE + part] = 1.0
            pk[part * LANES + h, h * LANES + _F_Q_LANE + _N_SPLIT + part] = -1.0
            oq[0, h * LANES + _F_Q_LANE + _N_SPLIT + part] = 1.0
            ok[0, h * LANES + _F_Q_LANE + part] = 1.0

    def group_mean(n):
        gidx = np.arange(n) // HEAD_DIM
        return (gidx[:, None] == gidx[None, :]).astype(np.float32) / HEAD_DIM

    return {
        "cq": ctab * q_scale, "sq": stab * q_scale, "ck": ctab, "sk": stab, "ov": jnp.asarray(ov),
        "tri": jnp.asarray(np.tril(np.ones((ts, ts), np.float32)), BF16),
        "pq": jnp.asarray(pq, BF16), "pk": jnp.asarray(pk, BF16),
        "oq": jnp.asarray(oq), "ok": jnp.asarray(ok),
        "gm256": jnp.asarray(group_mean(CONV_W), BF16), "gm128": jnp.asarray(group_mean(LANES), BF16),
    }


def _tile(n, pref):
    t = min(n, pref)
    assert n % t == 0, (n, pref)
    return t


def kernel(x, c, w_mod, b_mod, w_in, b_fgate, w_conv, q_norm_g, kv_norm_g, w_uq, w_ukv, head_norm_g, w_o, ln1_g, ln1_b, ln2_g, ln2_b, ffn_w_gate, ffn_w_up, ffn_w_down, router_w, exp_w_gate, exp_w_up, exp_w_down):
    bsz, seq, d = x.shape
    depth = w_mod.shape[0]
    assert d == D_MODEL and bsz <= SUBLANES
    alpha = (2 * depth) ** 0.25
    ts = _tile(seq, 512)
    tq = _tile(seq, 512)
    tm_e = 512
    tf_e = 512
    tc = _tile(seq, 256)
    assert (bsz * seq * TOP_K) % tm_e == 0 and exp_w_gate.shape[-1] % tf_e == 0

    consts = _constants(seq, ts)
    c_pad = jnp.pad(c, ((0, SUBLANES - bsz), (0, 0)))
    mod = _modulation(c_pad, w_mod, b_mod)[:, :bsz, :]

    for layer in range(depth):
        m6 = mod[layer].reshape(bsz, 6, 1, d)
        shift1, scale1, gate1, shift2, scale2, gate2 = [m6[:, i] for i in range(6)]
        lw = _layer_weights(layer, w_in, b_fgate, w_conv, q_norm_g, kv_norm_g, w_uq, w_ukv, head_norm_g,
                            w_o, ln1_g, ln1_b)
        qp, kp, vt, oc = _inproj(x, scale1, shift1, lw, consts, ts)
        oa = _attention(qp, kp, vt, lw["hg_attn"], consts["gm128"], tq, 6)
        x = _outproj(oa, oc, lw, x, gate1, alpha, ts)
        j = layer // 2
        g2, b2 = ln2_g[layer].reshape(1, d), ln2_b[layer].reshape(1, d)
        if layer % 2 == 0:
            x = _ffn_dense(x, scale2, shift2, gate2, ffn_w_gate[j].astype(BF16), ffn_w_up[j].astype(BF16),
                           ffn_w_down[j].astype(BF16), g2, b2, alpha, ts)
        else:
            wr = jnp.pad(router_w[j], ((0, 0), (0, LANES - N_EXPERTS)))
            wr_hi = wr.astype(BF16)
            wr_lo = (wr - wr_hi.astype(F32)).astype(BF16)
            x = _moe(x, scale2, shift2, gate2, jnp.stack([wr_hi, wr_lo]),
                     exp_w_gate[j].astype(BF16), exp_w_up[j].astype(BF16), exp_w_down[j].astype(BF16),
                     g2, b2, alpha, ts, tm_e, tf_e, tc)
    return x
```

```python
import functools

import numpy as np
import jax
import jax.numpy as jnp
from jax import lax
from jax.experimental import pallas as pl
from jax.experimental.pallas import tpu as pltpu

F32 = jnp.float32
BF16 = jnp.bfloat16

D_MODEL = 1024
HEAD_DIM = 64
H_FOX = 6
FOX_W = H_FOX * HEAD_DIM
CONV_W = 256
CONV_K = 3
H_MLA = 6
Q_LORA = 256
KV_LORA = 256
QK_NOPE = 64
QK_ROPE = 32
V_DIM = 64
MLA_W = H_MLA * V_DIM
N_HEADS = H_FOX + H_MLA
ROPE_THETA = 10000.0
N_EXPERTS = 8
TOP_K = 2
LN_EPS = 1e-5
RMS_EPS = 1e-6

LANES = 128
SUBLANES = 8
VMEM_LIMIT_BYTES = 56 * 1024 * 1024

_HW = 6 * LANES
_QF0, _KF0, _VF0, _FL0 = 0, _HW, 2 * _HW, 3 * _HW
_BG0 = _FL0 + LANES
_CG0, _HI0, _CQ0, _CKV0 = _BG0 + CONV_W, _BG0 + 2 * CONV_W, _BG0 + 3 * CONV_W, _BG0 + 3 * CONV_W + Q_LORA
_KR0 = _CKV0 + KV_LORA
_KRS0 = _KR0 + LANES
_NA = _KRS0 + LANES
LOG2E = 1.4426950408889634
_F_Q_LANE = 64
_N_SPLIT = 3


def _const_spec(shape):
    zeros = (0,) * len(shape)
    return pl.BlockSpec(shape, lambda *_: zeros, pipeline_mode=pl.Buffered(1))


def _silu(v):
    return v * (1.0 / (1.0 + jnp.exp(-v)))


def _split3(v):
    hi = v.astype(BF16)
    r1 = v - hi.astype(F32)
    mid = r1.astype(BF16)
    lo = (r1 - mid.astype(F32)).astype(BF16)
    return hi, mid, lo


def _group_mean_sq(v, gmat_ref):
    sq = v * v
    hi = sq.astype(BF16)
    lo = (sq - hi.astype(F32)).astype(BF16)
    g = gmat_ref[...]
    return (jnp.dot(hi, g, preferred_element_type=F32) + jnp.dot(lo, g, preferred_element_type=F32))


def _layer_norm_rows(r, g, b):
    mu = jnp.mean(r, axis=-1, keepdims=True)
    rc = r - mu
    var = jnp.mean(rc * rc, axis=-1, keepdims=True)
    return rc * lax.rsqrt(var + LN_EPS) * g + b


def _mod_kernel(c_ref, w_ref, b_ref, o_ref):
    act = _silu(c_ref[...]).astype(BF16)
    o_ref[0] = jnp.dot(act, w_ref[0].astype(BF16), preferred_element_type=F32) + b_ref[0]


def _modulation(c_pad, w_mod, b_mod):
    depth, d, n = w_mod.shape
    tn = 1024
    return pl.pallas_call(
        _mod_kernel,
        grid=(depth, n // tn),
        in_specs=[pl.BlockSpec((SUBLANES, d), lambda l, j: (0, 0)),
                  pl.BlockSpec((1, d, tn), lambda l, j: (l, 0, j)),
                  pl.BlockSpec((1, 1, tn), lambda l, j: (l, 0, j))],
        out_specs=pl.BlockSpec((1, SUBLANES, tn), lambda l, j: (l, 0, j)),
        out_shape=jax.ShapeDtypeStruct((depth, SUBLANES, n), F32),
        compiler_params=pltpu.CompilerParams(dimension_semantics=("arbitrary", "arbitrary")),
        name="modulation",
    )(c_pad, w_mod, b_mod.reshape(depth, 1, n))


def _inproj_kernel(x_ref, sc_ref, sh_ref, wa_ref, cq_ref, sq_ref, ck_ref, sk_ref, wuq_ref, wukv_ref,
                   qg_ref, kvg_ref, bf_ref, wconv_ref, hgc_ref, tri_ref, pq_ref, pk_ref, oq_ref, ok_ref,
                   ov_ref, gm_ref, qp_ref, kp_ref, vt_ref, oc_ref, fcarry_ref, ubuf_ref, *, ts):
    @pl.when(pl.program_id(1) == 0)
    def _():
        fcarry_ref[...] = jnp.zeros_like(fcarry_ref)
        ubuf_ref[pl.ds(0, SUBLANES), :] = jnp.zeros((SUBLANES, CONV_W), F32)

    hb = (x_ref[0] * (1.0 + sc_ref[0]) + sh_ref[0]).astype(BF16)

    def proj(lo, hi):
        return jnp.dot(hb, wa_ref[:, lo:hi], preferred_element_type=F32)

    a = proj(_FL0, _FL0 + LANES) + bf_ref[...]
    logf = jnp.minimum(a, 0.0) - jnp.log1p(jnp.exp(-jnp.abs(a)))
    tri = tri_ref[...]
    csum = sum(jnp.dot(tri, part, preferred_element_type=F32) for part in _split3(logf))
    fcum = fcarry_ref[...] + csum
    fcarry_ref[...] = fcum[ts - 1:ts, :]
    fsplit = jnp.concatenate(_split3(fcum * LOG2E), axis=1)
    qf = (proj(_QF0, _QF0 + _HW) * (HEAD_DIM ** -0.5 * LOG2E)
          + jnp.dot(fsplit, pq_ref[...], preferred_element_type=F32) + oq_ref[...])
    kf = (proj(_KF0, _KF0 + _HW)
          + jnp.dot(fsplit, pk_ref[...], preferred_element_type=F32) + ok_ref[...])
    vf = proj(_VF0, _VF0 + _HW) + ov_ref[...]
    for h in range(H_FOX):
        sl = slice(h * LANES, (h + 1) * LANES)
        qp_ref[0, h] = qf[:, sl].astype(BF16)
        kp_ref[0, h] = kf[:, sl].astype(BF16)
        vt_ref[0, h] = vf[:, sl].T.astype(BF16)

    u = proj(_CG0, _CG0 + CONV_W) * proj(_HI0, _HI0 + CONV_W)
    ubuf_ref[pl.ds(SUBLANES, ts), :] = u
    u1 = ubuf_ref[pl.ds(SUBLANES - 1, ts), :]
    u2 = ubuf_ref[pl.ds(SUBLANES - 2, ts), :]
    ubuf_ref[pl.ds(0, SUBLANES), :] = u[ts - SUBLANES:ts, :]
    wc = wconv_ref[...]
    oc = proj(_BG0, _BG0 + CONV_W) * (wc[0:1, :] * u2 + wc[1:2, :] * u1 + wc[2:3, :] * u)
    ocn = oc * lax.rsqrt(_group_mean_sq(oc, gm_ref) + RMS_EPS) * hgc_ref[...]
    oc_ref[0] = ocn.astype(BF16)

    def rms(v, g):
        return (v * lax.rsqrt(jnp.mean(v * v, axis=-1, keepdims=True) + RMS_EPS) * g).astype(BF16)

    cqn = rms(proj(_CQ0, _CQ0 + Q_LORA), qg_ref[...])
    qm = jnp.dot(cqn, wuq_ref[:, 0:768], preferred_element_type=F32)
    qs = jnp.dot(cqn, wuq_ref[:, 768:1536], preferred_element_type=F32)
    kvn = rms(proj(_CKV0, _CKV0 + KV_LORA), kvg_ref[...])
    kn = jnp.dot(kvn, wukv_ref[:, 0:_HW], preferred_element_type=F32)
    vm = jnp.dot(kvn, wukv_ref[:, _HW:2 * _HW], preferred_element_type=F32) + ov_ref[...]
    krr = proj(_KR0, _KR0 + LANES) * ck_ref[...] + proj(_KRS0, _KRS0 + LANES) * sk_ref[...]
    cq, sq = cq_ref[...], sq_ref[...]
    for h in range(H_MLA):
        sl = slice(h * LANES, (h + 1) * LANES)
        qp_ref[0, H_FOX + h] = (qm[:, sl] * cq + qs[:, sl] * sq).astype(BF16)
        kp_ref[0, H_FOX + h] = (kn[:, sl] + krr).astype(BF16)
        vt_ref[0, H_FOX + h] = vm[:, sl].T.astype(BF16)


def _inproj(x, scale1, shift1, lw, consts, ts):
    bsz, seq, d = x.shape
    kern = functools.partial(_inproj_kernel, ts=ts)
    row = lambda b, j: (b, 0, 0)
    tab = pl.BlockSpec((ts, LANES), lambda b, j: (j, 0))
    return pl.pallas_call(
        kern,
        grid=(bsz, seq // ts),
        in_specs=[pl.BlockSpec((1, ts, d), lambda b, j: (b, j, 0)),
                  pl.BlockSpec((1, 1, d), row), pl.BlockSpec((1, 1, d), row),
                  _const_spec((d, _NA)), tab, tab, tab, tab,
                  _const_spec((Q_LORA, 2 * _HW)), _const_spec((KV_LORA, 2 * _HW)),
                  _const_spec((1, Q_LORA)), _const_spec((1, KV_LORA)), _const_spec((1, LANES)),
                  _const_spec((SUBLANES, CONV_W)), _const_spec((1, CONV_W)),
                  _const_spec((ts, ts)), _const_spec((_N_SPLIT * LANES, _HW)),
                  _const_spec((_N_SPLIT * LANES, _HW)), _const_spec((1, _HW)), _const_spec((1, _HW)),
                  _const_spec((1, _HW)), _const_spec((CONV_W, CONV_W))],
        out_specs=[pl.BlockSpec((1, N_HEADS, ts, LANES), lambda b, j: (b, 0, j, 0)),
                   pl.BlockSpec((1, N_HEADS, ts, LANES), lambda b, j: (b, 0, j, 0)),
                   pl.BlockSpec((1, N_HEADS, LANES, ts), lambda b, j: (b, 0, 0, j)),
                   pl.BlockSpec((1, ts, CONV_W), lambda b, j: (b, j, 0))],
        out_shape=[jax.ShapeDtypeStruct((bsz, N_HEADS, seq, LANES), BF16),
                   jax.ShapeDtypeStruct((bsz, N_HEADS, seq, LANES), BF16),
                   jax.ShapeDtypeStruct((bsz, N_HEADS, LANES, seq), BF16),
                   jax.ShapeDtypeStruct((bsz, seq, CONV_W), BF16)],
        scratch_shapes=[pltpu.VMEM((1, LANES), F32), pltpu.VMEM((ts + SUBLANES, CONV_W), F32)],
        compiler_params=pltpu.CompilerParams(dimension_semantics=("arbitrary", "arbitrary"),
                                             vmem_limit_bytes=VMEM_LIMIT_BYTES),
        name="inproj",
    )(x, scale1, shift1, lw["wa"], consts["cq"], consts["sq"], consts["ck"], consts["sk"],
      lw["wuq"], lw["wukv"], lw["qg"], lw["kvg"], lw["bf"], lw["wconv"], lw["hg_conv"],
      consts["tri"], consts["pq"], consts["pk"], consts["oq"], consts["ok"], consts["ov"], consts["gm256"])


def _attn_kernel(qi_ref, kj_ref, q_ref, k_ref, vt_ref, g_ref, gm_ref, o_ref, m_ref, acc_ref, *, nh):
    step = pl.program_id(2)
    qi = qi_ref[step]
    kj = kj_ref[step]

    @pl.when(kj == 0)
    def _():
        m_ref[...] = jnp.full(m_ref.shape, -jnp.inf, F32)
        acc_ref[...] = jnp.zeros_like(acc_ref)

    def update(masked):
        sts = [lax.dot_general(k_ref[0, h], q_ref[0, h], (((1,), (1,)), ((), ())),
                               preferred_element_type=F32) for h in range(nh)]
        pts, alphas = [], []
        for h in range(nh):
            st = sts[h]
            if masked:
                key = lax.broadcasted_iota(jnp.int32, st.shape, 0)
                qry = lax.broadcasted_iota(jnp.int32, st.shape, 1)
                st = jnp.where(key <= qry, st, -jnp.inf)
            m_prev = m_ref[h]
            m_new = jnp.maximum(m_prev, jnp.max(st, axis=0, keepdims=True))
            pts.append(jnp.exp2(st - m_new).astype(BF16))
            alphas.append(jnp.exp2(m_prev - m_new))
            m_ref[h] = m_new
        for h in range(nh):
            acc_ref[h] = alphas[h] * acc_ref[h] + jnp.dot(vt_ref[0, h], pts[h], preferred_element_type=F32)

    @pl.when(kj < qi)
    def _():
        update(False)

    @pl.when(kj == qi)
    def _():
        update(True)
        for pair in range(nh // 2):
            ot = jnp.concatenate([acc_ref[h, 0:HEAD_DIM, :] / acc_ref[h, HEAD_DIM:2 * HEAD_DIM, :]
                                  for h in (2 * pair, 2 * pair + 1)], axis=0)
            o = ot.T
            sl = slice(pair * LANES, (pair + 1) * LANES)
            on = o * lax.rsqrt(_group_mean_sq(o, gm_ref) + RMS_EPS) * g_ref[:, sl]
            o_ref[0, :, sl] = on.astype(BF16)


def _attention(qp, kp, vt, g_attn, gm128, tq, nh):
    bsz, n_heads, seq, _ = qp.shape
    nq = seq // tq
    ow = (nh // 2) * LANES
    qi_tab = np.concatenate([np.full(i + 1, i) for i in range(nq)]).astype(np.int32)
    kj_tab = np.concatenate([np.arange(i + 1) for i in range(nq)]).astype(np.int32)
    grid_spec = pltpu.PrefetchScalarGridSpec(
        num_scalar_prefetch=2,
        grid=(bsz, n_heads // nh, len(qi_tab)),
        in_specs=[pl.BlockSpec((1, nh, tq, LANES), lambda b, p, s, qi, kj: (b, p, qi[s], 0)),
                  pl.BlockSpec((1, nh, tq, LANES), lambda b, p, s, qi, kj: (b, p, kj[s], 0)),
                  pl.BlockSpec((1, nh, LANES, tq), lambda b, p, s, qi, kj: (b, p, 0, kj[s])),
                  pl.BlockSpec((1, ow), lambda b, p, s, qi, kj: (0, p)),
                  _const_spec((LANES, LANES))],
        out_specs=pl.BlockSpec((1, tq, ow), lambda b, p, s, qi, kj: (b, qi[s], p)),
        scratch_shapes=[pltpu.VMEM((nh, 1, tq), F32), pltpu.VMEM((nh, LANES, tq), F32)])
    return pl.pallas_call(
        functools.partial(_attn_kernel, nh=nh),
        grid_spec=grid_spec,
        out_shape=jax.ShapeDtypeStruct((bsz, seq, (n_heads // 2) * LANES), BF16),
        compiler_params=pltpu.CompilerParams(
            dimension_semantics=("arbitrary", "arbitrary", "arbitrary"),
            vmem_limit_bytes=VMEM_LIMIT_BYTES),
        name="attention",
    )(jnp.asarray(qi_tab), jnp.asarray(kj_tab), qp, kp, vt, g_attn, gm128)


def _outproj_kernel(oa_ref, oc_ref, wa_ref, wc_ref, x_ref, gate_ref, g_ref, b_ref, o_ref, *, alpha):
    y = (jnp.dot(oa_ref[0], wa_ref[...], preferred_element_type=F32)
         + jnp.dot(oc_ref[0], wc_ref[...], preferred_element_type=F32))
    r = alpha * x_ref[0] + (1.0 + gate_ref[0]) * y
    o_ref[0] = _layer_norm_rows(r, g_ref[...], b_ref[...])


def _outproj(oa, oc, lw, x, gate1, alpha, tm):
    bsz, seq, d = x.shape
    wa_rows = oa.shape[-1]
    row = lambda b, j: (b, 0, 0)
    return pl.pallas_call(
        functools.partial(_outproj_kernel, alpha=alpha),
        grid=(bsz, seq // tm),
        in_specs=[pl.BlockSpec((1, tm, wa_rows), lambda b, j: (b, j, 0)),
                  pl.BlockSpec((1, tm, CONV_W), lambda b, j: (b, j, 0)),
                  _const_spec((wa_rows, d)), _const_spec((CONV_W, d)),
                  pl.BlockSpec((1, tm, d), lambda b, j: (b, j, 0)),
                  pl.BlockSpec((1, 1, d), row), _const_spec((1, d)), _const_spec((1, d))],
        out_specs=pl.BlockSpec((1, tm, d), lambda b, j: (b, j, 0)),
        out_shape=jax.ShapeDtypeStruct((bsz, seq, d), F32),
        compiler_params=pltpu.CompilerParams(dimension_semantics=("arbitrary", "arbitrary"),
                                             vmem_limit_bytes=VMEM_LIMIT_BYTES),
        name="outproj",
    )(oa, oc, lw["wo_a"], lw["wo_c"], x, gate1, lw["ln1_g"], lw["ln1_b"])


def _ffn_kernel(x_ref, sc_ref, sh_ref, gate_ref, wg_ref, wu_ref, wd_ref, g_ref, b_ref, o_ref, *, alpha, tf):
    x = x_ref[0]
    hb = (x * (1.0 + sc_ref[0]) + sh_ref[0]).astype(BF16)
    acc = jnp.zeros(x.shape, F32)
    for c in range(wg_ref.shape[1] // tf):
        sl = slice(c * tf, (c + 1) * tf)
        gt = jnp.dot(hb, wg_ref[:, sl], preferred_element_type=F32)
        up = jnp.dot(hb, wu_ref[:, sl], preferred_element_type=F32)
        acc = acc + jnp.dot((_silu(gt) * up).astype(BF16), wd_ref[sl, :], preferred_element_type=F32)
    r = alpha * x + (1.0 + gate_ref[0]) * acc
    o_ref[0] = _layer_norm_rows(r, g_ref[...], b_ref[...])


def _ffn_dense(x, scale2, shift2, gate2, wg, wu, wd, ln_g, ln_b, alpha, tm):
    bsz, seq, d = x.shape
    dff = wg.shape[1]
    tf = dff // 2 if (dff // 2) % LANES == 0 else dff
    row = lambda b, j: (b, 0, 0)
    return pl.pallas_call(
        functools.partial(_ffn_kernel, alpha=alpha, tf=tf),
        grid=(bsz, seq // tm),
        in_specs=[pl.BlockSpec((1, tm, d), lambda b, j: (b, j, 0)),
                  pl.BlockSpec((1, 1, d), row), pl.BlockSpec((1, 1, d), row), pl.BlockSpec((1, 1, d), row),
                  _const_spec((d, dff)), _const_spec((d, dff)), _const_spec((dff, d)),
                  _const_spec((1, d)), _const_spec((1, d))],
        out_specs=pl.BlockSpec((1, tm, d), lambda b, j: (b, j, 0)),
        out_shape=jax.ShapeDtypeStruct((bsz, seq, d), F32),
        compiler_params=pltpu.CompilerParams(dimension_semantics=("arbitrary", "arbitrary"),
                                             vmem_limit_bytes=VMEM_LIMIT_BYTES),
        name="ffn_dense",
    )(x, scale2, shift2, gate2, wg, wu, wd, ln_g, ln_b)


def _to_token_tiles(ref, v, n):
    for c in range(v.shape[1] // LANES):
        ref[pl.ds(c, n, stride=SUBLANES), :] = v[:, c * LANES:(c + 1) * LANES]


def _from_token_tiles(ref, n, first=0, tiles_per_row=1):
    stride = tiles_per_row * SUBLANES
    return jnp.concatenate([ref[pl.ds(first * SUBLANES + c, n, stride=stride), :] for c in range(SUBLANES)],
                           axis=1)


def _router_kernel(x_ref, sc_ref, sh_ref, wr_ref, h_ref, idx_ref, w_ref, *, tm):
    h = x_ref[0] * (1.0 + sc_ref[0]) + sh_ref[0]
    _to_token_tiles(h_ref, h, tm)
    h_hi = h.astype(BF16)
    h_lo = (h - h_hi.astype(F32)).astype(BF16)
    w_hi, w_lo = wr_ref[0], wr_ref[1]
    logits = (jnp.dot(h_hi, w_hi, preferred_element_type=F32)
              + jnp.dot(h_hi, w_lo, preferred_element_type=F32)
              + jnp.dot(h_lo, w_hi, preferred_element_type=F32))
    lane = lax.broadcasted_iota(jnp.int32, logits.shape, 1).astype(F32)
    neg = -jnp.inf
    lg = jnp.where(lane < N_EXPERTS, logits, neg)
    m1 = jnp.max(lg, axis=-1, keepdims=True)
    i1 = jnp.min(jnp.where(lg == m1, lane, float(LANES)), axis=-1, keepdims=True)
    lg2 = jnp.where(lane == i1, neg, lg)
    m2 = jnp.max(lg2, axis=-1, keepdims=True)
    i2 = jnp.min(jnp.where(lg2 == m2, lane, float(LANES)), axis=-1, keepdims=True)
    e2 = jnp.exp(m2 - m1)
    denom = 1.0 + e2
    idx_ref[...] = jnp.where(lane == 0.0, i1, i2).astype(jnp.int32)
    w_ref[...] = jnp.where(lane == 0.0, 1.0 / denom, e2 / denom)


def _router(x, scale2, shift2, wr2, tm):
    bsz, seq, d = x.shape
    assert d == SUBLANES * LANES
    nj = seq // tm
    n_tiles = bsz * nj
    row = lambda i: (i // nj, 0, 0)
    tok = lambda i: (i, 0)
    return pl.pallas_call(
        functools.partial(_router_kernel, tm=tm),
        grid=(n_tiles,),
        in_specs=[pl.BlockSpec((1, tm, d), lambda i: (i // nj, i % nj, 0)),
                  pl.BlockSpec((1, 1, d), row), pl.BlockSpec((1, 1, d), row),
                  _const_spec((2, d, LANES))],
        out_specs=[pl.BlockSpec((tm * SUBLANES, LANES), tok),
                   pl.BlockSpec((tm, LANES), tok), pl.BlockSpec((tm, LANES), tok)],
        out_shape=[jax.ShapeDtypeStruct((n_tiles * tm * SUBLANES, LANES), F32),
                   jax.ShapeDtypeStruct((bsz * seq, LANES), jnp.int32),
                   jax.ShapeDtypeStruct((bsz * seq, LANES), F32)],
        compiler_params=pltpu.CompilerParams(dimension_semantics=("arbitrary",),
                                             vmem_limit_bytes=VMEM_LIMIT_BYTES),
        name="router",
    )(x, scale2, shift2, wr2)


_COPY_UNROLL = 8


def _tile_at(ref, t):
    return ref.at[pl.ds(pl.multiple_of(t * SUBLANES, SUBLANES), SUBLANES)]


def _issue_tile_copies(n, copy_of):
    def group(g, carry):
        copies = [copy_of(g, u) for u in range(_COPY_UNROLL)]
        for u, (src, dst, sem) in enumerate(copies):
            pltpu.make_async_copy(src, dst, sem).start(priority=u % 2)
        return carry

    assert n % _COPY_UNROLL == 0
    lax.fori_loop(0, n // _COPY_UNROLL, group, 0)


def _group_kernel(di_ref, fi_ref, src_ref, dst_hbm, zero_ref, sem, *, tb, nf):
    zero_ref[...] = jnp.zeros_like(zero_ref)
    per_group = _COPY_UNROLL // TOP_K
    _issue_tile_copies(TOP_K * tb, lambda g, u: (_tile_at(src_ref, g * per_group + u // TOP_K),
                                                 _tile_at(dst_hbm, di_ref[0, 0, g * _COPY_UNROLL + u]), sem))
    _issue_tile_copies(nf, lambda g, u: (zero_ref, _tile_at(dst_hbm, fi_ref[0, 0, g * _COPY_UNROLL + u]), sem))
    block = pl.ds(0, tb * SUBLANES)
    for _ in range(TOP_K):
        pltpu.make_async_copy(src_ref, dst_hbm.at[block], sem).wait()

    def wait_fill(r, carry):
        pltpu.make_async_copy(zero_ref, _tile_at(dst_hbm, 0), sem).wait()
        return carry

    lax.fori_loop(0, nf, wait_fill, 0)


def _group_rows(h_tiles, dest, fill_dst, n_rows, tb):
    n_asg = dest.shape[0]
    steps = n_asg // (TOP_K * tb)
    nf = fill_dst.shape[0] // steps
    assert steps * TOP_K * tb == n_asg and steps * nf == fill_dst.shape[0]
    return pl.pallas_call(
        functools.partial(_group_kernel, tb=tb, nf=nf),
        grid=(steps,),
        in_specs=[pl.BlockSpec((1, 1, TOP_K * tb), lambda i: (i, 0, 0), memory_space=pltpu.SMEM),
                  pl.BlockSpec((1, 1, nf), lambda i: (i, 0, 0), memory_space=pltpu.SMEM),
                  pl.BlockSpec((tb * SUBLANES, LANES), lambda i: (i, 0))],
        out_specs=pl.BlockSpec(memory_space=pl.ANY),
        out_shape=jax.ShapeDtypeStruct((n_rows * SUBLANES, LANES), h_tiles.dtype),
        scratch_shapes=[pltpu.VMEM((SUBLANES, LANES), h_tiles.dtype), pltpu.SemaphoreType.DMA(())],
        compiler_params=pltpu.CompilerParams(dimension_semantics=("arbitrary",), has_side_effects=True),
        name="moe_group",
    )(dest.astype(jnp.int32).reshape(steps, 1, TOP_K * tb), fill_dst.astype(jnp.int32).reshape(steps, 1, nf),
      h_tiles)


def _expert_kernel(be_ref, nu_ref, x_ref, wg_ref, wu_ref, wd_ref, o_ref, acc_ref, xb_ref, *, tm):
    i = pl.program_id(0)
    f = pl.program_id(1)
    used = i < nu_ref[0]

    @pl.when(jnp.logical_and(used, f == 0))
    def _():
        xb_ref[...] = _from_token_tiles(x_ref, tm).astype(BF16)
        acc_ref[...] = jnp.zeros_like(acc_ref)

    @pl.when(used)
    def _():
        xb = xb_ref[...]
        gt = jnp.dot(xb, wg_ref[0], preferred_element_type=F32)
        up = jnp.dot(xb, wu_ref[0], preferred_element_type=F32)
        acc_ref[...] += jnp.dot((_silu(gt) * up).astype(BF16), wd_ref[0], preferred_element_type=F32)

    last = f == pl.num_programs(1) - 1

    @pl.when(jnp.logical_and(used, last))
    def _():
        _to_token_tiles(o_ref, acc_ref[...], tm)

    @pl.when(jnp.logical_and(jnp.logical_not(used), last))
    def _():
        o_ref[...] = jnp.zeros_like(o_ref)


def _experts(x_tiles, nb, block_e, n_used, wg, wu, wd, tm, tf):
    n_exp, d, dff = wg.shape
    nf = dff // tf

    def xmap(i, f, be, nu):
        return (jnp.minimum(i, nu[0] - 1), 0)

    def fidx(i, f, nu):
        return jnp.where(i < nu[0], f, nf - 1)

    grid_spec = pltpu.PrefetchScalarGridSpec(
        num_scalar_prefetch=2,
        grid=(nb, nf),
        in_specs=[pl.BlockSpec((tm * SUBLANES, LANES), xmap),
                  pl.BlockSpec((1, d, tf), lambda i, f, be, nu: (be[i], 0, fidx(i, f, nu))),
                  pl.BlockSpec((1, d, tf), lambda i, f, be, nu: (be[i], 0, fidx(i, f, nu))),
                  pl.BlockSpec((1, tf, d), lambda i, f, be, nu: (be[i], fidx(i, f, nu), 0))],
        out_specs=pl.BlockSpec((tm * SUBLANES, LANES), lambda i, f, be, nu: (i, 0)),
        scratch_shapes=[pltpu.VMEM((tm, d), F32), pltpu.VMEM((tm, d), BF16)])
    return pl.pallas_call(
        functools.partial(_expert_kernel, tm=tm),
        grid_spec=grid_spec,
        out_shape=jax.ShapeDtypeStruct((nb * tm * SUBLANES, LANES), F32),
        compiler_params=pltpu.CompilerParams(dimension_semantics=("arbitrary", "arbitrary"),
                                             vmem_limit_bytes=VMEM_LIMIT_BYTES),
        name="moe_experts",
    )(block_e, n_used, x_tiles, wg, wu, wd)


def _combine_kernel(cur_ref, nxt_ref, y_hbm, w_ref, x_ref, gate_ref, g_ref, b_ref, o_ref, ybuf_ref, sems,
                    *, tc, alpha):
    i = pl.program_id(0)
    nt = TOP_K * tc

    def fetch(idx_ref, slot):
        buf, sem = ybuf_ref.at[slot], sems.at[slot]
        _issue_tile_copies(nt, lambda g, u: (_tile_at(y_hbm, idx_ref[0, 0, g * _COPY_UNROLL + u]),
                                             _tile_at(buf, g * _COPY_UNROLL + u), sem))

    @pl.when(i == 0)
    def _():
        fetch(cur_ref, 0)

    @pl.when(i + 1 < pl.num_programs(0))
    def _():
        fetch(nxt_ref, (i + 1) % 2)

    slot = i % 2
    yb = ybuf_ref.at[slot]
    pltpu.make_async_copy(y_hbm.at[pl.ds(0, nt * SUBLANES)], yb, sems.at[slot]).wait()
    w = w_ref[...]
    y = (w[:, 0:1] * _from_token_tiles(yb, tc, 0, TOP_K)
         + w[:, 1:2] * _from_token_tiles(yb, tc, 1, TOP_K))
    r = alpha * x_ref[0] + (1.0 + gate_ref[0]) * y
    o_ref[0] = _layer_norm_rows(r, g_ref[...], b_ref[...])


def _combine(y_tiles, dest, top_w, x, gate2, ln_g, ln_b, alpha, tc):
    bsz, seq, d = x.shape
    nj = seq // tc
    steps = bsz * nj
    nt = TOP_K * tc
    dest_blocks = dest.astype(jnp.int32).reshape(steps, 1, nt)
    return pl.pallas_call(
        functools.partial(_combine_kernel, tc=tc, alpha=alpha),
        grid=(steps,),
        in_specs=[pl.BlockSpec((1, 1, nt), lambda i: (i, 0, 0), memory_space=pltpu.SMEM),
                  pl.BlockSpec((1, 1, nt), lambda i: (jnp.minimum(i + 1, steps - 1), 0, 0),
                               memory_space=pltpu.SMEM),
                  pl.BlockSpec(memory_space=pl.ANY),
                  pl.BlockSpec((tc, LANES), lambda i: (i, 0)),
                  pl.BlockSpec((1, tc, d), lambda i: (i // nj, i % nj, 0)),
                  pl.BlockSpec((1, 1, d), lambda i: (i // nj, 0, 0)), _const_spec((1, d)), _const_spec((1, d))],
        out_specs=pl.BlockSpec((1, tc, d), lambda i: (i // nj, i % nj, 0)),
        out_shape=jax.ShapeDtypeStruct((bsz, seq, d), F32),
        scratch_shapes=[pltpu.VMEM((2, nt * SUBLANES, LANES), F32), pltpu.SemaphoreType.DMA((2,))],
        compiler_params=pltpu.CompilerParams(dimension_semantics=("arbitrary",),
                                             vmem_limit_bytes=VMEM_LIMIT_BYTES),
        name="moe_combine",
    )(dest_blocks, dest_blocks, y_tiles, top_w, x, gate2, ln_g, ln_b)


def _moe(x1, scale2, shift2, gate2, wr2, wg, wu, wd, ln_g, ln_b, alpha, tm_tok, tm, tf, tc):
    bsz, seq, d = x1.shape
    n_tok = bsz * seq
    n_asg = n_tok * TOP_K
    h_tiles, idx128, w128 = _router(x1, scale2, shift2, wr2, tm_tok)
    e_flat = idx128[:, :TOP_K].reshape(n_asg)
    onehot = (e_flat[:, None] == jnp.arange(N_EXPERTS, dtype=jnp.int32)[None, :]).astype(jnp.int32)
    csum = jnp.cumsum(onehot, axis=0)
    counts = csum[-1]
    rank = jnp.sum((csum - onehot) * onehot, axis=1)
    padded = (counts + tm - 1) // tm * tm
    pad_end = jnp.cumsum(padded)
    pad_start = pad_end - padded
    dest = pad_start[e_flat] + rank
    nb = n_asg // tm + N_EXPERTS
    n_rows = nb * tm
    n_fill = n_rows - n_asg
    fill_end = jnp.cumsum(padded - counts)
    slot = jnp.arange(n_fill, dtype=jnp.int32)
    slot_e = jnp.searchsorted(fill_end, slot, side="right").astype(jnp.int32)
    in_group = slot_e < N_EXPERTS
    ge = jnp.minimum(slot_e, N_EXPERTS - 1)
    fill_dst = jnp.where(in_group, (pad_start + counts)[ge] + slot - (fill_end - (padded - counts))[ge],
                         pad_end[-1] + slot - fill_end[-1])
    x_tiles = _group_rows(h_tiles, dest, fill_dst, n_rows, tm_tok)
    block_e = jnp.minimum(jnp.searchsorted(pad_end, jnp.arange(nb, dtype=jnp.int32) * tm, side="right"),
                          N_EXPERTS - 1).astype(jnp.int32)
    n_used = (pad_end[-1:] // tm).astype(jnp.int32)
    y_tiles = _experts(x_tiles, nb, block_e, n_used, wg, wu, wd, tm, tf)
    return _combine(y_tiles, dest, w128, x1, gate2, ln_g, ln_b, alpha, tc)


def _head_pad(w, heads, width):
    d = w.shape[0]
    w3 = w.reshape(d, heads, width)
    return jnp.pad(w3, ((0, 0), (0, 0), (0, LANES - width))).reshape(d, heads * LANES)


def _layer_weights(layer, w_in, b_fgate, w_conv, q_norm_g, kv_norm_g, w_uq, w_ukv, head_norm_g, w_o,
                   ln1_g, ln1_b):
    d = w_in.shape[1]
    sizes = [FOX_W, FOX_W, FOX_W, H_FOX, CONV_W, CONV_W, CONV_W, Q_LORA, KV_LORA, QK_ROPE]
    offs = np.concatenate([[0], np.cumsum(sizes)])
    wi = w_in[layer]
    fq, fk, fv, fl, bg, cg, hin, cq, ckv, kr = [wi[:, offs[i]:offs[i + 1]] for i in range(len(sizes))]
    half = QK_ROPE // 2
    z64 = jnp.zeros((d, QK_NOPE), F32)
    z32 = jnp.zeros((d, LANES - QK_NOPE - QK_ROPE), F32)
    wa = jnp.concatenate([
        _head_pad(fq, H_FOX, HEAD_DIM), _head_pad(fk, H_FOX, HEAD_DIM), _head_pad(fv, H_FOX, HEAD_DIM),
        jnp.pad(fl, ((0, 0), (0, LANES - H_FOX))), bg, cg, hin, cq, ckv,
        jnp.concatenate([z64, kr, z32], axis=1),
        jnp.concatenate([z64, kr[:, half:], kr[:, :half], z32], axis=1)], axis=1).astype(BF16)
    assert wa.shape[1] == _NA
    q3 = w_uq[layer].reshape(Q_LORA, H_MLA, QK_NOPE + QK_ROPE)
    q_main = jnp.pad(q3, ((0, 0), (0, 0), (0, LANES - QK_NOPE - QK_ROPE)))
    q_swap = jnp.concatenate([jnp.zeros((Q_LORA, H_MLA, QK_NOPE), F32), q3[:, :, QK_NOPE + half:],
                              q3[:, :, QK_NOPE:QK_NOPE + half],
                              jnp.zeros((Q_LORA, H_MLA, LANES - QK_NOPE - QK_ROPE), F32)], axis=2)
    wuq = jnp.concatenate([q_main.reshape(Q_LORA, H_MLA * LANES), q_swap.reshape(Q_LORA, H_MLA * LANES)],
                          axis=1).astype(BF16)
    kv3 = w_ukv[layer].reshape(KV_LORA, H_MLA, QK_NOPE + V_DIM)
    k_nope = jnp.pad(kv3[:, :, :QK_NOPE], ((0, 0), (0, 0), (0, LANES - QK_NOPE))).reshape(KV_LORA, H_MLA * LANES)
    v_pad = jnp.pad(kv3[:, :, QK_NOPE:], ((0, 0), (0, 0), (0, LANES - V_DIM))).reshape(KV_LORA, H_MLA * LANES)
    wukv = jnp.concatenate([k_nope, v_pad], axis=1).astype(BF16)
    hg = head_norm_g[layer]
    wo = w_o[layer]
    c0, c1 = FOX_W, FOX_W + CONV_W
    return {
        "wa": wa, "wuq": wuq, "wukv": wukv,
        "qg": q_norm_g[layer].reshape(1, Q_LORA), "kvg": kv_norm_g[layer].reshape(1, KV_LORA),
        "bf": jnp.pad(b_fgate[layer], (0, LANES - H_FOX)).reshape(1, LANES),
        "wconv": jnp.pad(w_conv[layer], ((0, SUBLANES - CONV_K), (0, 0))),
        "hg_conv": hg[c0:c1].reshape(1, CONV_W),
        "hg_attn": jnp.concatenate([hg[:c0], hg[c1:]]).reshape(1, FOX_W + MLA_W),
        "wo_a": jnp.concatenate([wo[:c0], wo[c1:]], axis=0).astype(BF16),
        "wo_c": wo[c0:c1].astype(BF16),
        "ln1_g": ln1_g[layer].reshape(1, d), "ln1_b": ln1_b[layer].reshape(1, d),
    }


def _constants(seq, ts):
    pos = jnp.arange(seq, dtype=F32)
    inv_freq = ROPE_THETA ** (-jnp.arange(0, QK_ROPE, 2, dtype=F32) / QK_ROPE)
    ang = pos[:, None] * inv_freq[None, :]
    cos, sin = jnp.cos(ang), jnp.sin(ang)
    pad_r = jnp.zeros((seq, LANES - QK_NOPE - QK_ROPE), F32)
    ctab = jnp.concatenate([jnp.ones((seq, QK_NOPE), F32), cos, cos, pad_r], axis=1)
    stab = jnp.concatenate([jnp.zeros((seq, QK_NOPE), F32), -sin, sin, pad_r], axis=1)
    q_scale = (QK_NOPE + QK_ROPE) ** -0.5 * LOG2E
    ov = np.tile(np.concatenate([np.zeros(V_DIM, np.float32), np.ones(LANES - V_DIM, np.float32)]), 6)[None, :]
    pq = np.zeros((_N_SPLIT * LANES, H_FOX * LANES), np.float32)
    pk = np.zeros((_N_SPLIT * LANES, H_FOX * LANES), np.float32)
    oq = np.zeros((1, H_FOX * LANES), np.float32)
    ok = np.zeros((1, H_FOX * LANES), np.float32)
    for h in range(H_FOX):
        for part in range(_N_SPLIT):
            pq[part * LANES + h, h * LANES + _F_Q_LANE + part] = 1.0
            pk[part * LANES + h, h * LANES + _F_Q_LANE + _N_SPLIT + part] = -1.0
            oq[0, h * LANES + _F_Q_LANE + _N_SPLIT + part] = 1.0
            ok[0, h * LANES + _F_Q_LANE + part] = 1.0

    def group_mean(n):
        gidx = np.arange(n) // HEAD_DIM
        return (gidx[:, None] == gidx[None, :]).astype(np.float32) / HEAD_DIM

    return {
        "cq": ctab * q_scale, "sq": stab * q_scale, "ck": ctab, "sk": stab, "ov": jnp.asarray(ov),
        "tri": jnp.asarray(np.tril(np.ones((ts, ts), np.float32)), BF16),
        "pq": jnp.asarray(pq, BF16), "pk": jnp.asarray(pk, BF16),
        "oq": jnp.asarray(oq), "ok": jnp.asarray(ok),
        "gm256": jnp.asarray(group_mean(CONV_W), BF16), "gm128": jnp.asarray(group_mean(LANES), BF16),
    }


def _tile(n, pref):
    t = min(n, pref)
    assert n % t == 0, (n, pref)
    return t


def kernel(x, c, w_mod, b_mod, w_in, b_fgate, w_conv, q_norm_g, kv_norm_g, w_uq, w_ukv, head_norm_g, w_o, ln1_g, ln1_b, ln2_g, ln2_b, ffn_w_gate, ffn_w_up, ffn_w_down, router_w, exp_w_gate, exp_w_up, exp_w_down):
    bsz, seq, d = x.shape
    depth = w_mod.shape[0]
    assert d == D_MODEL and bsz <= SUBLANES
    alpha = (2 * depth) ** 0.25
    ts = _tile(seq, 512)
    tq = _tile(seq, 512)
    tm_e = 512
    tf_e = 512
    tc = _tile(seq, 256)
    assert (bsz * seq * TOP_K) % tm_e == 0 and exp_w_gate.shape[-1] % tf_e == 0

    consts = _constants(seq, ts)
    c_pad = jnp.pad(c, ((0, SUBLANES - bsz), (0, 0)))
    mod = _modulation(c_pad, w_mod, b_mod)[:, :bsz, :]

    for layer in range(depth):
        m6 = mod[layer].reshape(bsz, 6, 1, d)
        shift1, scale1, gate1, shift2, scale2, gate2 = [m6[:, i] for i in range(6)]
        lw = _layer_weights(layer, w_in, b_fgate, w_conv, q_norm_g, kv_norm_g, w_uq, w_ukv, head_norm_g,
                            w_o, ln1_g, ln1_b)
        qp, kp, vt, oc = _inproj(x, scale1, shift1, lw, consts, ts)
        oa = _attention(qp, kp, vt, lw["hg_attn"], consts["gm128"], tq, 6)
        x = _outproj(oa, oc, lw, x, gate1, alpha, ts)
        j = layer // 2
        g2, b2 = ln2_g[layer].reshape(1, d), ln2_b[layer].reshape(1, d)
        if layer % 2 == 0:
            x = _ffn_dense(x, scale2, shift2, gate2, ffn_w_gate[j].astype(BF16), ffn_w_up[j].astype(BF16),
                           ffn_w_down[j].astype(BF16), g2, b2, alpha, ts)
        else:
            wr = jnp.pad(router_w[j], ((0, 0), (0, LANES - N_EXPERTS)))
            wr_hi = wr.astype(BF16)
            wr_lo = (wr - wr_hi.astype(F32)).astype(BF16)
            x = _moe(x, scale2, shift2, gate2, jnp.stack([wr_hi, wr_lo]),
                     exp_w_gate[j].astype(BF16), exp_w_up[j].astype(BF16), exp_w_down[j].astype(BF16),
                     g2, b2, alpha, ts, tm_e, tf_e, tc)
    return x
```

```python
import functools

import numpy as np
import jax
import jax.numpy as jnp
from jax import lax
from jax.experimental import pallas as pl
from jax.experimental.pallas import tpu as pltpu

F32 = jnp.float32
BF16 = jnp.bfloat16

D_MODEL = 1024
HEAD_DIM = 64
H_FOX = 6
FOX_W = H_FOX * HEAD_DIM
CONV_W = 256
CONV_K = 3
H_MLA = 6
Q_LORA = 256
KV_LORA = 256
QK_NOPE = 64
QK_ROPE = 32
V_DIM = 64
MLA_W = H_MLA * V_DIM
N_HEADS = H_FOX + H_MLA
ROPE_THETA = 10000.0
N_EXPERTS = 8
TOP_K = 2
LN_EPS = 1e-5
RMS_EPS = 1e-6

LANES = 128
SUBLANES = 8
VMEM_LIMIT_BYTES = 56 * 1024 * 1024

_HW = 6 * LANES
_QF0, _KF0, _VF0, _FL0 = 0, FOX_W, 2 * FOX_W, 3 * FOX_W
_BG0 = _FL0 + LANES
_CG0, _HI0, _CQ0, _CKV0 = _BG0 + CONV_W, _BG0 + 2 * CONV_W, _BG0 + 3 * CONV_W, _BG0 + 3 * CONV_W + Q_LORA
_KR0 = _CKV0 + KV_LORA
_KRS0 = _KR0 + LANES
_NA = _KRS0 + LANES
LOG2E = 1.4426950408889634
_N_SPLIT = 3


def _const_spec(shape):
    zeros = (0,) * len(shape)
    return pl.BlockSpec(shape, lambda *_: zeros, pipeline_mode=pl.Buffered(1))


def _silu(v):
    return v * (1.0 / (1.0 + jnp.exp(-v)))


def _split3(v):
    hi = v.astype(BF16)
    r1 = v - hi.astype(F32)
    mid = r1.astype(BF16)
    lo = (r1 - mid.astype(F32)).astype(BF16)
    return hi, mid, lo


def _group_mean_sq(v, gmat_ref):
    sq = v * v
    hi = sq.astype(BF16)
    lo = (sq - hi.astype(F32)).astype(BF16)
    g = gmat_ref[...]
    return (jnp.dot(hi, g, preferred_element_type=F32) + jnp.dot(lo, g, preferred_element_type=F32))


def _layer_norm_rows(r, g, b):
    mu = jnp.mean(r, axis=-1, keepdims=True)
    rc = r - mu
    var = jnp.mean(rc * rc, axis=-1, keepdims=True)
    return rc * lax.rsqrt(var + LN_EPS) * g + b


def _mod_kernel(c_ref, w_ref, b_ref, o_ref):
    act = _silu(c_ref[...]).astype(BF16)
    o_ref[0] = jnp.dot(act, w_ref[0].astype(BF16), preferred_element_type=F32) + b_ref[0]


def _modulation(c_pad, w_mod, b_mod):
    depth, d, n = w_mod.shape
    tn = 1024
    return pl.pallas_call(
        _mod_kernel,
        grid=(depth, n // tn),
        in_specs=[pl.BlockSpec((SUBLANES, d), lambda l, j: (0, 0)),
                  pl.BlockSpec((1, d, tn), lambda l, j: (l, 0, j)),
                  pl.BlockSpec((1, 1, tn), lambda l, j: (l, 0, j))],
        out_specs=pl.BlockSpec((1, SUBLANES, tn), lambda l, j: (l, 0, j)),
        out_shape=jax.ShapeDtypeStruct((depth, SUBLANES, n), F32),
        compiler_params=pltpu.CompilerParams(dimension_semantics=("arbitrary", "arbitrary")),
        name="modulation",
    )(c_pad, w_mod, b_mod.reshape(depth, 1, n))


def _inproj_kernel(x_ref, sc_ref, sh_ref, wa_ref, cq_ref, sq_ref, ck_ref, sk_ref, wuq_ref, wukv_ref,
                   qg_ref, kvg_ref, bf_ref, wconv_ref, hgc_ref, tri_ref, lm_ref, gm_ref,
                   qp_ref, kp_ref, vt_ref, oc_ref, fcarry_ref, ubuf_ref, *, ts):
    @pl.when(pl.program_id(1) == 0)
    def _():
        fcarry_ref[...] = jnp.zeros_like(fcarry_ref)
        ubuf_ref[pl.ds(0, SUBLANES), :] = jnp.zeros((SUBLANES, CONV_W), F32)

    hb = (x_ref[0] * (1.0 + sc_ref[0]) + sh_ref[0]).astype(BF16)

    def proj(lo, hi):
        return jnp.dot(hb, wa_ref[:, lo:hi], preferred_element_type=F32)

    def put_values_t(head, v_half_t):
        vt_ref[0, head, 0:V_DIM, :] = v_half_t.astype(BF16)
        vt_ref[0, head, V_DIM:LANES, :] = jnp.ones((LANES - V_DIM, ts), BF16)

    a = proj(_FL0, _FL0 + LANES) + bf_ref[...]
    logf = jnp.minimum(a, 0.0) - jnp.log1p(jnp.exp(-jnp.abs(a)))
    tri = tri_ref[...]
    csum = sum(jnp.dot(tri, part, preferred_element_type=F32) for part in _split3(logf))
    fcum = fcarry_ref[...] + csum
    fcarry_ref[...] = fcum[ts - 1:ts, :]
    f_hi, f_mid, f_lo = _split3(fcum * LOG2E)
    lm = lm_ref[...]
    fparts = (f_hi.astype(F32) * lm[0:1, :] + f_mid.astype(F32) * lm[1:2, :]
              + f_lo.astype(F32) * lm[2:3, :])
    lane = lax.broadcasted_iota(jnp.int32, (ts, LANES), 1)
    low_half = lane < HEAD_DIM
    zq = proj(_QF0, _QF0 + FOX_W) * (HEAD_DIM ** -0.5 * LOG2E)
    zk = proj(_KF0, _KF0 + FOX_W)
    zv = proj(_VF0, _VF0 + FOX_W)
    for h in range(H_FOX):
        blk = slice((h // 2) * LANES, (h // 2 + 1) * LANES)
        if h % 2 == 0:
            base, f_mask, one_mask, keep = HEAD_DIM, lm[3:4, :], lm[4:5, :], low_half
        else:
            base, f_mask, one_mask, keep = 0, lm[5:6, :], lm[6:7, :], jnp.logical_not(low_half)
        f_at_q = pltpu.roll(fparts, (base - _N_SPLIT * h) % LANES, axis=1)
        f_at_k = pltpu.roll(fparts, (base + _N_SPLIT - _N_SPLIT * h) % LANES, axis=1)
        qp_ref[0, h] = jnp.where(keep, zq[:, blk], f_at_q * f_mask + one_mask).astype(BF16)
        kp_ref[0, h] = jnp.where(keep, zk[:, blk], f_mask - f_at_k * one_mask).astype(BF16)
    for j in range(H_FOX // 2):
        vt = zv[:, j * LANES:(j + 1) * LANES].T
        put_values_t(2 * j, vt[0:V_DIM])
        put_values_t(2 * j + 1, vt[V_DIM:LANES])

    u = proj(_CG0, _CG0 + CONV_W) * proj(_HI0, _HI0 + CONV_W)
    ubuf_ref[pl.ds(SUBLANES, ts), :] = u
    u1 = ubuf_ref[pl.ds(SUBLANES - 1, ts), :]
    u2 = ubuf_ref[pl.ds(SUBLANES - 2, ts), :]
    ubuf_ref[pl.ds(0, SUBLANES), :] = u[ts - SUBLANES:ts, :]
    wc = wconv_ref[...]
    oc = proj(_BG0, _BG0 + CONV_W) * (wc[0:1, :] * u2 + wc[1:2, :] * u1 + wc[2:3, :] * u)
    ocn = oc * lax.rsqrt(_group_mean_sq(oc, gm_ref) + RMS_EPS) * hgc_ref[...]
    oc_ref[0] = ocn.astype(BF16)

    def rms(v, g):
        return (v * lax.rsqrt(jnp.mean(v * v, axis=-1, keepdims=True) + RMS_EPS) * g).astype(BF16)

    cqn = rms(proj(_CQ0, _CQ0 + Q_LORA), qg_ref[...])
    qm = jnp.dot(cqn, wuq_ref[:, 0:_HW], preferred_element_type=F32)
    qs = jnp.dot(cqn, wuq_ref[:, _HW:2 * _HW], preferred_element_type=F32)
    kvn = rms(proj(_CKV0, _CKV0 + KV_LORA), kvg_ref[...])
    kn = jnp.dot(kvn, wukv_ref[:, 0:_HW], preferred_element_type=F32)
    vm = jnp.dot(kvn, wukv_ref[:, _HW:_HW + MLA_W], preferred_element_type=F32)
    krr = proj(_KR0, _KR0 + LANES) * ck_ref[...] + proj(_KRS0, _KRS0 + LANES) * sk_ref[...]
    cq, sq = cq_ref[...], sq_ref[...]
    for h in range(H_MLA):
        sl = slice(h * LANES, (h + 1) * LANES)
        qp_ref[0, H_FOX + h] = (qm[:, sl] * cq + qs[:, sl] * sq).astype(BF16)
        kp_ref[0, H_FOX + h] = (kn[:, sl] + krr).astype(BF16)
    for j in range(H_MLA // 2):
        vt = vm[:, j * LANES:(j + 1) * LANES].T
        put_values_t(H_FOX + 2 * j, vt[0:V_DIM])
        put_values_t(H_FOX + 2 * j + 1, vt[V_DIM:LANES])


def _inproj(x, scale1, shift1, lw, consts, ts):
    bsz, seq, d = x.shape
    kern = functools.partial(_inproj_kernel, ts=ts)
    row = lambda b, j: (b, 0, 0)
    tab = pl.BlockSpec((ts, LANES), lambda b, j: (j, 0))
    return pl.pallas_call(
        kern,
        grid=(bsz, seq // ts),
        in_specs=[pl.BlockSpec((1, ts, d), lambda b, j: (b, j, 0)),
                  pl.BlockSpec((1, 1, d), row), pl.BlockSpec((1, 1, d), row),
                  _const_spec((d, _NA)), tab, tab, tab, tab,
                  _const_spec((Q_LORA, 2 * _HW)), _const_spec((KV_LORA, _HW + MLA_W)),
                  _const_spec((1, Q_LORA)), _const_spec((1, KV_LORA)), _const_spec((1, LANES)),
                  _const_spec((SUBLANES, CONV_W)), _const_spec((1, CONV_W)),
                  _const_spec((ts, ts)), _const_spec((SUBLANES, LANES)), _const_spec((CONV_W, CONV_W))],
        out_specs=[pl.BlockSpec((1, N_HEADS, ts, LANES), lambda b, j: (b, 0, j, 0)),
                   pl.BlockSpec((1, N_HEADS, ts, LANES), lambda b, j: (b, 0, j, 0)),
                   pl.BlockSpec((1, N_HEADS, LANES, ts), lambda b, j: (b, 0, 0, j)),
                   pl.BlockSpec((1, ts, CONV_W), lambda b, j: (b, j, 0))],
        out_shape=[jax.ShapeDtypeStruct((bsz, N_HEADS, seq, LANES), BF16),
                   jax.ShapeDtypeStruct((bsz, N_HEADS, seq, LANES), BF16),
                   jax.ShapeDtypeStruct((bsz, N_HEADS, LANES, seq), BF16),
                   jax.ShapeDtypeStruct((bsz, seq, CONV_W), BF16)],
        scratch_shapes=[pltpu.VMEM((1, LANES), F32), pltpu.VMEM((ts + SUBLANES, CONV_W), F32)],
        compiler_params=pltpu.CompilerParams(dimension_semantics=("arbitrary", "arbitrary"),
                                             vmem_limit_bytes=VMEM_LIMIT_BYTES),
        name="inproj",
    )(x, scale1, shift1, lw["wa"], consts["cq"], consts["sq"], consts["ck"], consts["sk"],
      lw["wuq"], lw["wukv"], lw["qg"], lw["kvg"], lw["bf"], lw["wconv"], lw["hg_conv"],
      consts["tri"], consts["lane_masks"], consts["gm256"])


def _attn_kernel(qi_ref, kj_ref, q_ref, k_ref, vt_ref, g_ref, gm_ref, o_ref, m_ref, acc_ref, *, nh):
    step = pl.program_id(2)
    qi = qi_ref[step]
    kj = kj_ref[step]

    @pl.when(kj == 0)
    def _():
        m_ref[...] = jnp.full(m_ref.shape, -jnp.inf, F32)
        acc_ref[...] = jnp.zeros_like(acc_ref)

    def update(masked):
        sts = [lax.dot_general(k_ref[0, h], q_ref[0, h], (((1,), (1,)), ((), ())),
                               preferred_element_type=F32) for h in range(nh)]
        pts, alphas = [], []
        for h in range(nh):
            st = sts[h]
            if masked:
                key = lax.broadcasted_iota(jnp.int32, st.shape, 0)
                qry = lax.broadcasted_iota(jnp.int32, st.shape, 1)
                st = jnp.where(key <= qry, st, -jnp.inf)
            m_prev = m_ref[h]
            m_new = jnp.maximum(m_prev, jnp.max(st, axis=0, keepdims=True))
            pts.append(jnp.exp2(st - m_new).astype(BF16))
            alphas.append(jnp.exp2(m_prev - m_new))
            m_ref[h] = m_new
        for h in range(nh):
            acc_ref[h] = alphas[h] * acc_ref[h] + jnp.dot(vt_ref[0, h], pts[h], preferred_element_type=F32)

    @pl.when(kj < qi)
    def _():
        update(False)

    @pl.when(kj == qi)
    def _():
        update(True)
        for pair in range(nh // 2):
            ot = jnp.concatenate([acc_ref[h, 0:HEAD_DIM, :] / acc_ref[h, HEAD_DIM:2 * HEAD_DIM, :]
                                  for h in (2 * pair, 2 * pair + 1)], axis=0)
            o = ot.T
            sl = slice(pair * LANES, (pair + 1) * LANES)
            on = o * lax.rsqrt(_group_mean_sq(o, gm_ref) + RMS_EPS) * g_ref[:, sl]
            o_ref[0, :, sl] = on.astype(BF16)


def _attention(qp, kp, vt, g_attn, gm128, tq, nh):
    bsz, n_heads, seq, _ = qp.shape
    nq = seq // tq
    ow = (nh // 2) * LANES
    qi_tab = np.concatenate([np.full(i + 1, i) for i in range(nq)]).astype(np.int32)
    kj_tab = np.concatenate([np.arange(i + 1) for i in range(nq)]).astype(np.int32)
    grid_spec = pltpu.PrefetchScalarGridSpec(
        num_scalar_prefetch=2,
        grid=(bsz, n_heads // nh, len(qi_tab)),
        in_specs=[pl.BlockSpec((1, nh, tq, LANES), lambda b, p, s, qi, kj: (b, p, qi[s], 0)),
                  pl.BlockSpec((1, nh, tq, LANES), lambda b, p, s, qi, kj: (b, p, kj[s], 0)),
                  pl.BlockSpec((1, nh, LANES, tq), lambda b, p, s, qi, kj: (b, p, 0, kj[s])),
                  pl.BlockSpec((1, ow), lambda b, p, s, qi, kj: (0, p)),
                  _const_spec((LANES, LANES))],
        out_specs=pl.BlockSpec((1, tq, ow), lambda b, p, s, qi, kj: (b, qi[s], p)),
        scratch_shapes=[pltpu.VMEM((nh, 1, tq), F32), pltpu.VMEM((nh, LANES, tq), F32)])
    return pl.pallas_call(
        functools.partial(_attn_kernel, nh=nh),
        grid_spec=grid_spec,
        out_shape=jax.ShapeDtypeStruct((bsz, seq, (n_heads // 2) * LANES), BF16),
        compiler_params=pltpu.CompilerParams(
            dimension_semantics=("arbitrary", "arbitrary", "arbitrary"),
            vmem_limit_bytes=VMEM_LIMIT_BYTES),
        name="attention",
    )(jnp.asarray(qi_tab), jnp.asarray(kj_tab), qp, kp, vt, g_attn, gm128)


def _outproj_kernel(oa_ref, oc_ref, wa_ref, wc_ref, x_ref, gate_ref, g_ref, b_ref, o_ref, *, alpha):
    y = (jnp.dot(oa_ref[0], wa_ref[...], preferred_element_type=F32)
         + jnp.dot(oc_ref[0], wc_ref[...], preferred_element_type=F32))
    r = alpha * x_ref[0] + (1.0 + gate_ref[0]) * y
    o_ref[0] = _layer_norm_rows(r, g_ref[...], b_ref[...])


def _outproj(oa, oc, lw, x, gate1, alpha, tm):
    bsz, seq, d = x.shape
    wa_rows = oa.shape[-1]
    row = lambda b, j: (b, 0, 0)
    return pl.pallas_call(
        functools.partial(_outproj_kernel, alpha=alpha),
        grid=(bsz, seq // tm),
        in_specs=[pl.BlockSpec((1, tm, wa_rows), lambda b, j: (b, j, 0)),
                  pl.BlockSpec((1, tm, CONV_W), lambda b, j: (b, j, 0)),
                  _const_spec((wa_rows, d)), _const_spec((CONV_W, d)),
                  pl.BlockSpec((1, tm, d), lambda b, j: (b, j, 0)),
                  pl.BlockSpec((1, 1, d), row), _const_spec((1, d)), _const_spec((1, d))],
        out_specs=pl.BlockSpec((1, tm, d), lambda b, j: (b, j, 0)),
        out_shape=jax.ShapeDtypeStruct((bsz, seq, d), F32),
        compiler_params=pltpu.CompilerParams(dimension_semantics=("arbitrary", "arbitrary"),
                                             vmem_limit_bytes=VMEM_LIMIT_BYTES),
        name="outproj",
    )(oa, oc, lw["wo_a"], lw["wo_c"], x, gate1, lw["ln1_g"], lw["ln1_b"])


def _ffn_kernel(x_ref, sc_ref, sh_ref, gate_ref, wg_ref, wu_ref, wd_ref, g_ref, b_ref, o_ref, *, alpha, tf):
    x = x_ref[0]
    hb = (x * (1.0 + sc_ref[0]) + sh_ref[0]).astype(BF16)
    acc = jnp.zeros(x.shape, F32)
    for c in range(wg_ref.shape[1] // tf):
        sl = slice(c * tf, (c + 1) * tf)
        gt = jnp.dot(hb, wg_ref[:, sl], preferred_element_type=F32)
        up = jnp.dot(hb, wu_ref[:, sl], preferred_element_type=F32)
        acc = acc + jnp.dot((_silu(gt) * up).astype(BF16), wd_ref[sl, :], preferred_element_type=F32)
    r = alpha * x + (1.0 + gate_ref[0]) * acc
    o_ref[0] = _layer_norm_rows(r, g_ref[...], b_ref[...])


def _ffn_dense(x, scale2, shift2, gate2, wg, wu, wd, ln_g, ln_b, alpha, tm):
    bsz, seq, d = x.shape
    dff = wg.shape[1]
    tf = dff // 2 if (dff // 2) % LANES == 0 else dff
    row = lambda b, j: (b, 0, 0)
    return pl.pallas_call(
        functools.partial(_ffn_kernel, alpha=alpha, tf=tf),
        grid=(bsz, seq // tm),
        in_specs=[pl.BlockSpec((1, tm, d), lambda b, j: (b, j, 0)),
                  pl.BlockSpec((1, 1, d), row), pl.BlockSpec((1, 1, d), row), pl.BlockSpec((1, 1, d), row),
                  _const_spec((d, dff)), _const_spec((d, dff)), _const_spec((dff, d)),
                  _const_spec((1, d)), _const_spec((1, d))],
        out_specs=pl.BlockSpec((1, tm, d), lambda b, j: (b, j, 0)),
        out_shape=jax.ShapeDtypeStruct((bsz, seq, d), F32),
        compiler_params=pltpu.CompilerParams(dimension_semantics=("arbitrary", "arbitrary"),
                                             vmem_limit_bytes=VMEM_LIMIT_BYTES),
        name="ffn_dense",
    )(x, scale2, shift2, gate2, wg, wu, wd, ln_g, ln_b)


def _to_token_tiles(ref, v, n):
    for c in range(v.shape[1] // LANES):
        ref[pl.ds(c, n, stride=SUBLANES), :] = v[:, c * LANES:(c + 1) * LANES]


def _from_token_tiles(ref, n, first=0, tiles_per_row=1):
    stride = tiles_per_row * SUBLANES
    return jnp.concatenate([ref[pl.ds(first * SUBLANES + c, n, stride=stride), :] for c in range(SUBLANES)],
                           axis=1)


def _router_kernel(x_ref, sc_ref, sh_ref, wr_ref, stri_ref, h_ref, idx_ref, w_ref, cnt_ref, carry_ref, *, tm):
    @pl.when(pl.program_id(0) == 0)
    def _():
        carry_ref[...] = jnp.zeros_like(carry_ref)

    h = x_ref[0] * (1.0 + sc_ref[0]) + sh_ref[0]
    _to_token_tiles(h_ref, h, tm)
    h_hi = h.astype(BF16)
    h_lo = (h - h_hi.astype(F32)).astype(BF16)
    w_hi, w_lo = wr_ref[0], wr_ref[1]
    logits = (jnp.dot(h_hi, w_hi, preferred_element_type=F32)
              + jnp.dot(h_hi, w_lo, preferred_element_type=F32)
              + jnp.dot(h_lo, w_hi, preferred_element_type=F32))
    lane = lax.broadcasted_iota(jnp.int32, logits.shape, 1).astype(F32)
    neg = -jnp.inf
    lg = jnp.where(lane < N_EXPERTS, logits, neg)
    m1 = jnp.max(lg, axis=-1, keepdims=True)
    i1 = jnp.min(jnp.where(lg == m1, lane, float(LANES)), axis=-1, keepdims=True)
    lg2 = jnp.where(lane == i1, neg, lg)
    m2 = jnp.max(lg2, axis=-1, keepdims=True)
    i2 = jnp.min(jnp.where(lg2 == m2, lane, float(LANES)), axis=-1, keepdims=True)
    e2 = jnp.exp(m2 - m1)
    denom = 1.0 + e2
    w_ref[...] = jnp.where(lane == 0.0, 1.0 / denom, e2 / denom)
    first, second = lane == i1, lane == i2
    chosen = jnp.logical_or(first, second).astype(F32)
    before = carry_ref[...] + jnp.dot(stri_ref[...], chosen.astype(BF16), preferred_element_type=F32)
    rank1 = jnp.sum(jnp.where(first, before, 0.0), axis=-1, keepdims=True)
    rank2 = jnp.sum(jnp.where(second, before, 0.0), axis=-1, keepdims=True)
    idx_ref[...] = jnp.where(lane == 0.0, i1, jnp.where(lane == 1.0, i2, jnp.where(lane == 2.0, rank1, rank2))
                             ).astype(jnp.int32)
    total = carry_ref[...] + jnp.sum(chosen, axis=0, keepdims=True)
    carry_ref[...] = total
    cnt_ref[...] = jnp.broadcast_to(total, cnt_ref.shape)


def _router(x, scale2, shift2, wr2, stri, tm):
    bsz, seq, d = x.shape
    assert d == SUBLANES * LANES and bsz * seq * TOP_K < 2 ** 24
    nj = seq // tm
    n_tiles = bsz * nj
    row = lambda i: (i // nj, 0, 0)
    tok = lambda i: (i, 0)
    return pl.pallas_call(
        functools.partial(_router_kernel, tm=tm),
        grid=(n_tiles,),
        in_specs=[pl.BlockSpec((1, tm, d), lambda i: (i // nj, i % nj, 0)),
                  pl.BlockSpec((1, 1, d), row), pl.BlockSpec((1, 1, d), row),
                  _const_spec((2, d, LANES)), _const_spec((tm, tm))],
        out_specs=[pl.BlockSpec((tm * SUBLANES, LANES), tok),
                   pl.BlockSpec((tm, LANES), tok), pl.BlockSpec((tm, LANES), tok),
                   pl.BlockSpec((SUBLANES, LANES), lambda i: (0, 0))],
        out_shape=[jax.ShapeDtypeStruct((n_tiles * tm * SUBLANES, LANES), F32),
                   jax.ShapeDtypeStruct((bsz * seq, LANES), jnp.int32),
                   jax.ShapeDtypeStruct((bsz * seq, LANES), F32),
                   jax.ShapeDtypeStruct((SUBLANES, LANES), F32)],
        scratch_shapes=[pltpu.VMEM((1, LANES), F32)],
        compiler_params=pltpu.CompilerParams(dimension_semantics=("arbitrary",),
                                             vmem_limit_bytes=VMEM_LIMIT_BYTES),
        name="router",
    )(x, scale2, shift2, wr2, stri)


_COPY_UNROLL = 8


def _tile_at(ref, t):
    return ref.at[pl.ds(pl.multiple_of(t * SUBLANES, SUBLANES), SUBLANES)]


def _issue_tile_copies(n, copy_of):
    def group(g, carry):
        copies = [copy_of(g, u) for u in range(_COPY_UNROLL)]
        for u, (src, dst, sem) in enumerate(copies):
            pltpu.make_async_copy(src, dst, sem).start(priority=u % 2)
        return carry

    assert n % _COPY_UNROLL == 0
    lax.fori_loop(0, n // _COPY_UNROLL, group, 0)


def _group_kernel(di_ref, fi_ref, src_ref, dst_hbm, zero_ref, sem, *, tb, nf):
    zero_ref[...] = jnp.zeros_like(zero_ref)
    per_group = _COPY_UNROLL // TOP_K
    _issue_tile_copies(TOP_K * tb, lambda g, u: (_tile_at(src_ref, g * per_group + u // TOP_K),
                                                 _tile_at(dst_hbm, di_ref[0, 0, g * _COPY_UNROLL + u]), sem))
    _issue_tile_copies(nf, lambda g, u: (zero_ref, _tile_at(dst_hbm, fi_ref[0, 0, g * _COPY_UNROLL + u]), sem))
    block = pl.ds(0, tb * SUBLANES)
    for _ in range(TOP_K):
        pltpu.make_async_copy(src_ref, dst_hbm.at[block], sem).wait()

    def wait_fill(r, carry):
        pltpu.make_async_copy(zero_ref, _tile_at(dst_hbm, 0), sem).wait()
        return carry

    lax.fori_loop(0, nf, wait_fill, 0)


def _group_rows(h_tiles, dest, fill_dst, n_rows, tb):
    n_asg = dest.shape[0]
    steps = n_asg // (TOP_K * tb)
    nf = fill_dst.shape[0] // steps
    assert steps * TOP_K * tb == n_asg and steps * nf == fill_dst.shape[0]
    return pl.pallas_call(
        functools.partial(_group_kernel, tb=tb, nf=nf),
        grid=(steps,),
        in_specs=[pl.BlockSpec((1, 1, TOP_K * tb), lambda i: (i, 0, 0), memory_space=pltpu.SMEM),
                  pl.BlockSpec((1, 1, nf), lambda i: (i, 0, 0), memory_space=pltpu.SMEM),
                  pl.BlockSpec((tb * SUBLANES, LANES), lambda i: (i, 0))],
        out_specs=pl.BlockSpec(memory_space=pl.ANY),
        out_shape=jax.ShapeDtypeStruct((n_rows * SUBLANES, LANES), h_tiles.dtype),
        scratch_shapes=[pltpu.VMEM((SUBLANES, LANES), h_tiles.dtype), pltpu.SemaphoreType.DMA(())],
        compiler_params=pltpu.CompilerParams(dimension_semantics=("arbitrary",), has_side_effects=True),
        name="moe_group",
    )(dest.astype(jnp.int32).reshape(steps, 1, TOP_K * tb), fill_dst.astype(jnp.int32).reshape(steps, 1, nf),
      h_tiles)


def _expert_kernel(be_ref, nu_ref, x_ref, wg_ref, wu_ref, wd_ref, o_ref, acc_ref, xb_ref, *, tm):
    i = pl.program_id(0)
    f = pl.program_id(1)
    used = i < nu_ref[0]

    @pl.when(jnp.logical_and(used, f == 0))
    def _():
        xb_ref[...] = _from_token_tiles(x_ref, tm).astype(BF16)
        acc_ref[...] = jnp.zeros_like(acc_ref)

    @pl.when(used)
    def _():
        xb = xb_ref[...]
        gt = jnp.dot(xb, wg_ref[0], preferred_element_type=F32)
        up = jnp.dot(xb, wu_ref[0], preferred_element_type=F32)
        acc_ref[...] += jnp.dot((_silu(gt) * up).astype(BF16), wd_ref[0], preferred_element_type=F32)

    last = f == pl.num_programs(1) - 1

    @pl.when(jnp.logical_and(used, last))
    def _():
        _to_token_tiles(o_ref, acc_ref[...], tm)

    @pl.when(jnp.logical_and(jnp.logical_not(used), last))
    def _():
        o_ref[...] = jnp.zeros_like(o_ref)


def _experts(x_tiles, nb, block_e, n_used, wg, wu, wd, tm, tf):
    n_exp, d, dff = wg.shape
    nf = dff // tf

    def xmap(i, f, be, nu):
        return (jnp.minimum(i, nu[0] - 1), 0)

    def fidx(i, f, nu):
        return jnp.where(i < nu[0], f, nf - 1)

    grid_spec = pltpu.PrefetchScalarGridSpec(
        num_scalar_prefetch=2,
        grid=(nb, nf),
        in_specs=[pl.BlockSpec((tm * SUBLANES, LANES), xmap),
                  pl.BlockSpec((1, d, tf), lambda i, f, be, nu: (be[i], 0, fidx(i, f, nu))),
                  pl.BlockSpec((1, d, tf), lambda i, f, be, nu: (be[i], 0, fidx(i, f, nu))),
                  pl.BlockSpec((1, tf, d), lambda i, f, be, nu: (be[i], fidx(i, f, nu), 0))],
        out_specs=pl.BlockSpec((tm * SUBLANES, LANES), lambda i, f, be, nu: (i, 0)),
        scratch_shapes=[pltpu.VMEM((tm, d), F32), pltpu.VMEM((tm, d), BF16)])
    return pl.pallas_call(
        functools.partial(_expert_kernel, tm=tm),
        grid_spec=grid_spec,
        out_shape=jax.ShapeDtypeStruct((nb * tm * SUBLANES, LANES), F32),
        compiler_params=pltpu.CompilerParams(dimension_semantics=("arbitrary", "arbitrary"),
                                             vmem_limit_bytes=VMEM_LIMIT_BYTES),
        name="moe_experts",
    )(block_e, n_used, x_tiles, wg, wu, wd)


def _combine_kernel(cur_ref, nxt_ref, y_hbm, w_ref, x_ref, gate_ref, g_ref, b_ref, o_ref, ybuf_ref, sems,
                    *, tc, alpha):
    i = pl.program_id(0)
    nt = TOP_K * tc

    def fetch(idx_ref, slot):
        buf, sem = ybuf_ref.at[slot], sems.at[slot]
        _issue_tile_copies(nt, lambda g, u: (_tile_at(y_hbm, idx_ref[0, 0, g * _COPY_UNROLL + u]),
                                             _tile_at(buf, g * _COPY_UNROLL + u), sem))

    @pl.when(i == 0)
    def _():
        fetch(cur_ref, 0)

    @pl.when(i + 1 < pl.num_programs(0))
    def _():
        fetch(nxt_ref, (i + 1) % 2)

    slot = i % 2
    yb = ybuf_ref.at[slot]
    pltpu.make_async_copy(y_hbm.at[pl.ds(0, nt * SUBLANES)], yb, sems.at[slot]).wait()
    w = w_ref[...]
    y = (w[:, 0:1] * _from_token_tiles(yb, tc, 0, TOP_K)
         + w[:, 1:2] * _from_token_tiles(yb, tc, 1, TOP_K))
    r = alpha * x_ref[0] + (1.0 + gate_ref[0]) * y
    o_ref[0] = _layer_norm_rows(r, g_ref[...], b_ref[...])


def _combine(y_tiles, dest, top_w, x, gate2, ln_g, ln_b, alpha, tc):
    bsz, seq, d = x.shape
    nj = seq // tc
    steps = bsz * nj
    nt = TOP_K * tc
    dest_blocks = dest.astype(jnp.int32).reshape(steps, 1, nt)
    return pl.pallas_call(
        functools.partial(_combine_kernel, tc=tc, alpha=alpha),
        grid=(steps,),
        in_specs=[pl.BlockSpec((1, 1, nt), lambda i: (i, 0, 0), memory_space=pltpu.SMEM),
                  pl.BlockSpec((1, 1, nt), lambda i: (jnp.minimum(i + 1, steps - 1), 0, 0),
                               memory_space=pltpu.SMEM),
                  pl.BlockSpec(memory_space=pl.ANY),
                  pl.BlockSpec((tc, LANES), lambda i: (i, 0)),
                  pl.BlockSpec((1, tc, d), lambda i: (i // nj, i % nj, 0)),
                  pl.BlockSpec((1, 1, d), lambda i: (i // nj, 0, 0)), _const_spec((1, d)), _const_spec((1, d))],
        out_specs=pl.BlockSpec((1, tc, d), lambda i: (i // nj, i % nj, 0)),
        out_shape=jax.ShapeDtypeStruct((bsz, seq, d), F32),
        scratch_shapes=[pltpu.VMEM((2, nt * SUBLANES, LANES), F32), pltpu.SemaphoreType.DMA((2,))],
        compiler_params=pltpu.CompilerParams(dimension_semantics=("arbitrary",),
                                             vmem_limit_bytes=VMEM_LIMIT_BYTES),
        name="moe_combine",
    )(dest_blocks, dest_blocks, y_tiles, top_w, x, gate2, ln_g, ln_b)


def _moe(x1, scale2, shift2, gate2, wr2, stri, wg, wu, wd, ln_g, ln_b, alpha, tm_tok, tm, tf, tc):
    bsz, seq, d = x1.shape
    n_tok = bsz * seq
    n_asg = n_tok * TOP_K
    h_tiles, idx128, w128, cnt = _router(x1, scale2, shift2, wr2, stri, tm_tok)
    counts = cnt[0, :N_EXPERTS].astype(jnp.int32)
    padded = (counts + tm - 1) // tm * tm
    pad_end = jnp.cumsum(padded)
    pad_start = pad_end - padded
    e_ids, ranks = idx128[:, 0:TOP_K], idx128[:, TOP_K:2 * TOP_K]
    start_of = jnp.sum(jnp.where(e_ids[:, :, None] == jnp.arange(N_EXPERTS, dtype=jnp.int32), pad_start, 0), axis=-1)
    dest = (start_of + ranks).reshape(n_asg)
    nb = n_asg // tm + N_EXPERTS
    n_rows = nb * tm
    n_fill = n_rows - n_asg
    fill_end = jnp.cumsum(padded - counts)
    slot = jnp.arange(n_fill, dtype=jnp.int32)
    slot_e = jnp.searchsorted(fill_end, slot, side="right").astype(jnp.int32)
    in_group = slot_e < N_EXPERTS
    ge = jnp.minimum(slot_e, N_EXPERTS - 1)
    fill_dst = jnp.where(in_group, (pad_start + counts)[ge] + slot - (fill_end - (padded - counts))[ge],
                         pad_end[-1] + slot - fill_end[-1])
    x_tiles = _group_rows(h_tiles, dest, fill_dst, n_rows, tm_tok)
    block_e = jnp.minimum(jnp.searchsorted(pad_end, jnp.arange(nb, dtype=jnp.int32) * tm, side="right"),
                          N_EXPERTS - 1).astype(jnp.int32)
    n_used = (pad_end[-1:] // tm).astype(jnp.int32)
    y_tiles = _experts(x_tiles, nb, block_e, n_used, wg, wu, wd, tm, tf)
    return _combine(y_tiles, dest, w128, x1, gate2, ln_g, ln_b, alpha, tc)


def _layer_weights(layer, w_in, b_fgate, w_conv, q_norm_g, kv_norm_g, w_uq, w_ukv, head_norm_g, w_o,
                   ln1_g, ln1_b):
    d = w_in.shape[1]
    sizes = [FOX_W, FOX_W, FOX_W, H_FOX, CONV_W, CONV_W, CONV_W, Q_LORA, KV_LORA, QK_ROPE]
    offs = np.concatenate([[0], np.cumsum(sizes)])
    wi = w_in[layer]
    fq, fk, fv, fl, bg, cg, hin, cq, ckv, kr = [wi[:, offs[i]:offs[i + 1]] for i in range(len(sizes))]
    half = QK_ROPE // 2
    z64 = jnp.zeros((d, QK_NOPE), F32)
    z32 = jnp.zeros((d, LANES - QK_NOPE - QK_ROPE), F32)
    wa = jnp.concatenate([
        fq, fk, fv,
        jnp.pad(jnp.repeat(fl, _N_SPLIT, axis=1), ((0, 0), (0, LANES - _N_SPLIT * H_FOX))), bg, cg, hin, cq, ckv,
        jnp.concatenate([z64, kr, z32], axis=1),
        jnp.concatenate([z64, kr[:, half:], kr[:, :half], z32], axis=1)], axis=1).astype(BF16)
    assert wa.shape[1] == _NA
    q3 = w_uq[layer].reshape(Q_LORA, H_MLA, QK_NOPE + QK_ROPE)
    q_main = jnp.pad(q3, ((0, 0), (0, 0), (0, LANES - QK_NOPE - QK_ROPE)))
    q_swap = jnp.concatenate([jnp.zeros((Q_LORA, H_MLA, QK_NOPE), F32), q3[:, :, QK_NOPE + half:],
                              q3[:, :, QK_NOPE:QK_NOPE + half],
                              jnp.zeros((Q_LORA, H_MLA, LANES - QK_NOPE - QK_ROPE), F32)], axis=2)
    wuq = jnp.concatenate([q_main.reshape(Q_LORA, H_MLA * LANES), q_swap.reshape(Q_LORA, H_MLA * LANES)],
                          axis=1).astype(BF16)
    kv3 = w_ukv[layer].reshape(KV_LORA, H_MLA, QK_NOPE + V_DIM)
    k_nope = jnp.pad(kv3[:, :, :QK_NOPE], ((0, 0), (0, 0), (0, LANES - QK_NOPE))).reshape(KV_LORA, H_MLA * LANES)
    wukv = jnp.concatenate([k_nope, kv3[:, :, QK_NOPE:].reshape(KV_LORA, MLA_W)], axis=1).astype(BF16)
    hg = head_norm_g[layer]
    wo = w_o[layer]
    c0, c1 = FOX_W, FOX_W + CONV_W
    return {
        "wa": wa, "wuq": wuq, "wukv": wukv,
        "qg": q_norm_g[layer].reshape(1, Q_LORA), "kvg": kv_norm_g[layer].reshape(1, KV_LORA),
        "bf": jnp.pad(jnp.repeat(b_fgate[layer], _N_SPLIT), (0, LANES - _N_SPLIT * H_FOX)).reshape(1, LANES),
        "wconv": jnp.pad(w_conv[layer], ((0, SUBLANES - CONV_K), (0, 0))),
        "hg_conv": hg[c0:c1].reshape(1, CONV_W),
        "hg_attn": jnp.concatenate([hg[:c0], hg[c1:]]).reshape(1, FOX_W + MLA_W),
        "wo_a": jnp.concatenate([wo[:c0], wo[c1:]], axis=0).astype(BF16),
        "wo_c": wo[c0:c1].astype(BF16),
        "ln1_g": ln1_g[layer].reshape(1, d), "ln1_b": ln1_b[layer].reshape(1, d),
    }


def _constants(seq, ts):
    pos = jnp.arange(seq, dtype=F32)
    inv_freq = ROPE_THETA ** (-jnp.arange(0, QK_ROPE, 2, dtype=F32) / QK_ROPE)
    ang = pos[:, None] * inv_freq[None, :]
    cos, sin = jnp.cos(ang), jnp.sin(ang)
    pad_r = jnp.zeros((seq, LANES - QK_NOPE - QK_ROPE), F32)
    ctab = jnp.concatenate([jnp.ones((seq, QK_NOPE), F32), cos, cos, pad_r], axis=1)
    stab = jnp.concatenate([jnp.zeros((seq, QK_NOPE), F32), -sin, sin, pad_r], axis=1)
    q_scale = (QK_NOPE + QK_ROPE) ** -0.5 * LOG2E
    lanes = np.arange(LANES)
    lm = np.zeros((SUBLANES, LANES), np.float32)
    for part in range(_N_SPLIT):
        lm[part] = (lanes % _N_SPLIT == part) & (lanes < _N_SPLIT * H_FOX)
    for row, base in ((3, HEAD_DIM), (5, 0)):
        lm[row] = (lanes >= base) & (lanes < base + _N_SPLIT)
        lm[row + 1] = (lanes >= base + _N_SPLIT) & (lanes < base + 2 * _N_SPLIT)

    def group_mean(n):
        gidx = np.arange(n) // HEAD_DIM
        return (gidx[:, None] == gidx[None, :]).astype(np.float32) / HEAD_DIM

    return {
        "cq": ctab * q_scale, "sq": stab * q_scale, "ck": ctab, "sk": stab,
        "tri": jnp.asarray(np.tril(np.ones((ts, ts), np.float32)), BF16),
        "stri": jnp.asarray(np.tril(np.ones((ts, ts), np.float32), k=-1), BF16),
        "lane_masks": jnp.asarray(lm),
        "gm256": jnp.asarray(group_mean(CONV_W), BF16), "gm128": jnp.asarray(group_mean(LANES), BF16),
    }


def _tile(n, pref):
    t = min(n, pref)
    assert n % t == 0, (n, pref)
    return t


def kernel(x, c, w_mod, b_mod, w_in, b_fgate, w_conv, q_norm_g, kv_norm_g, w_uq, w_ukv, head_norm_g, w_o, ln1_g, ln1_b, ln2_g, ln2_b, ffn_w_gate, ffn_w_up, ffn_w_down, router_w, exp_w_gate, exp_w_up, exp_w_down):
    bsz, seq, d = x.shape
    depth = w_mod.shape[0]
    assert d == D_MODEL and bsz <= SUBLANES
    alpha = (2 * depth) ** 0.25
    ts = _tile(seq, 512)
    tq = _tile(seq, 512)
    tm_e = 512
    tf_e = 512
    tc = _tile(seq, 256)
    assert (bsz * seq * TOP_K) % tm_e == 0 and exp_w_gate.shape[-1] % tf_e == 0

    consts = _constants(seq, ts)
    c_pad = jnp.pad(c, ((0, SUBLANES - bsz), (0, 0)))
    mod = _modulation(c_pad, w_mod, b_mod)[:, :bsz, :]

    for layer in range(depth):
        m6 = mod[layer].reshape(bsz, 6, 1, d)
        shift1, scale1, gate1, shift2, scale2, gate2 = [m6[:, i] for i in range(6)]
        lw = _layer_weights(layer, w_in, b_fgate, w_conv, q_norm_g, kv_norm_g, w_uq, w_ukv, head_norm_g,
                            w_o, ln1_g, ln1_b)
        qp, kp, vt, oc = _inproj(x, scale1, shift1, lw, consts, ts)
        oa = _attention(qp, kp, vt, lw["hg_attn"], consts["gm128"], tq, 6)
        x = _outproj(oa, oc, lw, x, gate1, alpha, ts)
        j = layer // 2
        g2, b2 = ln2_g[layer].reshape(1, d), ln2_b[layer].reshape(1, d)
        if layer % 2 == 0:
            x = _ffn_dense(x, scale2, shift2, gate2, ffn_w_gate[j].astype(BF16), ffn_w_up[j].astype(BF16),
                           ffn_w_down[j].astype(BF16), g2, b2, alpha, ts)
        else:
            wr = jnp.pad(router_w[j], ((0, 0), (0, LANES - N_EXPERTS)))
            wr_hi = wr.astype(BF16)
            wr_lo = (wr - wr_hi.astype(F32)).astype(BF16)
            x = _moe(x, scale2, shift2, gate2, jnp.stack([wr_hi, wr_lo]), consts["stri"],
                     exp_w_gate[j].astype(BF16), exp_w_up[j].astype(BF16), exp_w_down[j].astype(BF16),
                     g2, b2, alpha, ts, tm_e, tf_e, tc)
    return x
```

```python
import functools

import numpy as np
import jax
import jax.numpy as jnp
from jax import lax
from jax.experimental import pallas as pl
from jax.experimental.pallas import tpu as pltpu

F32 = jnp.float32
BF16 = jnp.bfloat16

D_MODEL = 1024
HEAD_DIM = 64
H_FOX = 6
FOX_W = H_FOX * HEAD_DIM
CONV_W = 256
CONV_K = 3
H_MLA = 6
Q_LORA = 256
KV_LORA = 256
QK_NOPE = 64
QK_ROPE = 32
V_DIM = 64
MLA_W = H_MLA * V_DIM
N_HEADS = H_FOX + H_MLA
ROPE_THETA = 10000.0
N_EXPERTS = 8
TOP_K = 2
LN_EPS = 1e-5
RMS_EPS = 1e-6

LANES = 128
SUBLANES = 8
VMEM_LIMIT_BYTES = 56 * 1024 * 1024

_HW = 6 * LANES
_QF0, _KF0, _VF0, _FL0 = 0, FOX_W, 2 * FOX_W, 3 * FOX_W
_BG0 = _FL0 + LANES
_CG0, _HI0, _CQ0, _CKV0 = _BG0 + CONV_W, _BG0 + 2 * CONV_W, _BG0 + 3 * CONV_W, _BG0 + 3 * CONV_W + Q_LORA
_KR0 = _CKV0 + KV_LORA
_KRS0 = _KR0 + LANES
_NA = _KRS0 + LANES
LOG2E = 1.4426950408889634
_N_SPLIT = 3


def _const_spec(shape):
    zeros = (0,) * len(shape)
    return pl.BlockSpec(shape, lambda *_: zeros, pipeline_mode=pl.Buffered(1))


def _silu(v):
    return v * (1.0 / (1.0 + jnp.exp(-v)))


def _split3(v):
    hi = v.astype(BF16)
    r1 = v - hi.astype(F32)
    mid = r1.astype(BF16)
    lo = (r1 - mid.astype(F32)).astype(BF16)
    return hi, mid, lo


def _group_mean_sq(v, gmat_ref):
    sq = v * v
    hi = sq.astype(BF16)
    lo = (sq - hi.astype(F32)).astype(BF16)
    g = gmat_ref[...]
    return (jnp.dot(hi, g, preferred_element_type=F32) + jnp.dot(lo, g, preferred_element_type=F32))


def _layer_norm_rows(r, g, b):
    mu = jnp.mean(r, axis=-1, keepdims=True)
    rc = r - mu
    var = jnp.mean(rc * rc, axis=-1, keepdims=True)
    return rc * lax.rsqrt(var + LN_EPS) * g + b


def _mod_kernel(c_ref, w_ref, b_ref, o_ref):
    act = _silu(c_ref[...]).astype(BF16)
    o_ref[0] = jnp.dot(act, w_ref[0].astype(BF16), preferred_element_type=F32) + b_ref[0]


def _modulation(c_pad, w_mod, b_mod):
    depth, d, n = w_mod.shape
    tn = 1024
    return pl.pallas_call(
        _mod_kernel,
        grid=(depth, n // tn),
        in_specs=[pl.BlockSpec((SUBLANES, d), lambda l, j: (0, 0)),
                  pl.BlockSpec((1, d, tn), lambda l, j: (l, 0, j)),
                  pl.BlockSpec((1, 1, tn), lambda l, j: (l, 0, j))],
        out_specs=pl.BlockSpec((1, SUBLANES, tn), lambda l, j: (l, 0, j)),
        out_shape=jax.ShapeDtypeStruct((depth, SUBLANES, n), F32),
        compiler_params=pltpu.CompilerParams(dimension_semantics=("arbitrary", "arbitrary")),
        name="modulation",
    )(c_pad, w_mod, b_mod.reshape(depth, 1, n))


def _inproj_kernel(x_ref, sc_ref, sh_ref, wa_ref, cq_ref, sq_ref, ck_ref, sk_ref, wuq_ref, wukv_ref,
                   qg_ref, kvg_ref, bf_ref, wconv_ref, hgc_ref, tri_ref, lm_ref, gm_ref,
                   qp_ref, kp_ref, vt_ref, oc_ref, fcarry_ref, ubuf_ref, wab_ref, *, ts):
    @pl.when(jnp.logical_and(pl.program_id(0) == 0, pl.program_id(1) == 0))
    def _():
        for c in range(_NA // LANES):
            sl = slice(c * LANES, (c + 1) * LANES)
            wab_ref[:, sl] = wa_ref[:, sl].astype(BF16)

    @pl.when(pl.program_id(1) == 0)
    def _():
        fcarry_ref[...] = jnp.zeros_like(fcarry_ref)
        ubuf_ref[pl.ds(0, SUBLANES), :] = jnp.zeros((SUBLANES, CONV_W), F32)

    hb = (x_ref[0] * (1.0 + sc_ref[0]) + sh_ref[0]).astype(BF16)

    def proj(lo, hi):
        return jnp.dot(hb, wab_ref[:, lo:hi], preferred_element_type=F32)

    def put_values_t(head, v_half_t):
        vt_ref[0, head, 0:V_DIM, :] = v_half_t.astype(BF16)
        vt_ref[0, head, V_DIM:LANES, :] = jnp.ones((LANES - V_DIM, ts), BF16)

    a = proj(_FL0, _FL0 + LANES) + bf_ref[...]
    logf = jnp.minimum(a, 0.0) - jnp.log1p(jnp.exp(-jnp.abs(a)))
    tri = tri_ref[...]
    csum = sum(jnp.dot(tri, part, preferred_element_type=F32) for part in _split3(logf))
    fcum = fcarry_ref[...] + csum
    fcarry_ref[...] = fcum[ts - 1:ts, :]
    f_hi, f_mid, f_lo = _split3(fcum * LOG2E)
    lm = lm_ref[...]
    fparts = (f_hi.astype(F32) * lm[0:1, :] + f_mid.astype(F32) * lm[1:2, :]
              + f_lo.astype(F32) * lm[2:3, :])
    lane = lax.broadcasted_iota(jnp.int32, (ts, LANES), 1)
    low_half = lane < HEAD_DIM
    zq = proj(_QF0, _QF0 + FOX_W) * (HEAD_DIM ** -0.5 * LOG2E)
    zk = proj(_KF0, _KF0 + FOX_W)
    zv = proj(_VF0, _VF0 + FOX_W)
    for h in range(H_FOX):
        blk = slice((h // 2) * LANES, (h // 2 + 1) * LANES)
        if h % 2 == 0:
            base, f_mask, one_mask, keep = HEAD_DIM, lm[3:4, :], lm[4:5, :], low_half
        else:
            base, f_mask, one_mask, keep = 0, lm[5:6, :], lm[6:7, :], jnp.logical_not(low_half)
        f_at_q = pltpu.roll(fparts, (base - _N_SPLIT * h) % LANES, axis=1)
        f_at_k = pltpu.roll(fparts, (base + _N_SPLIT - _N_SPLIT * h) % LANES, axis=1)
        qp_ref[0, h] = jnp.where(keep, zq[:, blk], f_at_q * f_mask + one_mask).astype(BF16)
        kp_ref[0, h] = jnp.where(keep, zk[:, blk], f_mask - f_at_k * one_mask).astype(BF16)
    for j in range(H_FOX // 2):
        vt = zv[:, j * LANES:(j + 1) * LANES].T
        put_values_t(2 * j, vt[0:V_DIM])
        put_values_t(2 * j + 1, vt[V_DIM:LANES])

    u = proj(_CG0, _CG0 + CONV_W) * proj(_HI0, _HI0 + CONV_W)
    ubuf_ref[pl.ds(SUBLANES, ts), :] = u
    u1 = ubuf_ref[pl.ds(SUBLANES - 1, ts), :]
    u2 = ubuf_ref[pl.ds(SUBLANES - 2, ts), :]
    ubuf_ref[pl.ds(0, SUBLANES), :] = u[ts - SUBLANES:ts, :]
    wc = wconv_ref[...]
    oc = proj(_BG0, _BG0 + CONV_W) * (wc[0:1, :] * u2 + wc[1:2, :] * u1 + wc[2:3, :] * u)
    ocn = oc * lax.rsqrt(_group_mean_sq(oc, gm_ref) + RMS_EPS) * hgc_ref[...]
    oc_ref[0] = ocn.astype(BF16)

    def rms(v, g):
        return (v * lax.rsqrt(jnp.mean(v * v, axis=-1, keepdims=True) + RMS_EPS) * g).astype(BF16)

    cqn = rms(proj(_CQ0, _CQ0 + Q_LORA), qg_ref[...])
    qm = jnp.dot(cqn, wuq_ref[:, 0:_HW], preferred_element_type=F32)
    qs = jnp.dot(cqn, wuq_ref[:, _HW:2 * _HW], preferred_element_type=F32)
    kvn = rms(proj(_CKV0, _CKV0 + KV_LORA), kvg_ref[...])
    kn = jnp.dot(kvn, wukv_ref[:, 0:_HW], preferred_element_type=F32)
    vm = jnp.dot(kvn, wukv_ref[:, _HW:_HW + MLA_W], preferred_element_type=F32)
    krr = proj(_KR0, _KR0 + LANES) * ck_ref[...] + proj(_KRS0, _KRS0 + LANES) * sk_ref[...]
    cq, sq = cq_ref[...], sq_ref[...]
    for h in range(H_MLA):
        sl = slice(h * LANES, (h + 1) * LANES)
        qp_ref[0, H_FOX + h] = (qm[:, sl] * cq + qs[:, sl] * sq).astype(BF16)
        kp_ref[0, H_FOX + h] = (kn[:, sl] + krr).astype(BF16)
    for j in range(H_MLA // 2):
        vt = vm[:, j * LANES:(j + 1) * LANES].T
        put_values_t(H_FOX + 2 * j, vt[0:V_DIM])
        put_values_t(H_FOX + 2 * j + 1, vt[V_DIM:LANES])


def _inproj(x, scale1, shift1, lw, consts, ts):
    bsz, seq, d = x.shape
    kern = functools.partial(_inproj_kernel, ts=ts)
    row = lambda b, j: (b, 0, 0)
    tab = pl.BlockSpec((ts, LANES), lambda b, j: (j, 0))
    return pl.pallas_call(
        kern,
        grid=(bsz, seq // ts),
        in_specs=[pl.BlockSpec((1, ts, d), lambda b, j: (b, j, 0)),
                  pl.BlockSpec((1, 1, d), row), pl.BlockSpec((1, 1, d), row),
                  _const_spec((d, _NA)), tab, tab, tab, tab,
                  _const_spec((Q_LORA, 2 * _HW)), _const_spec((KV_LORA, _HW + MLA_W)),
                  _const_spec((1, Q_LORA)), _const_spec((1, KV_LORA)), _const_spec((1, LANES)),
                  _const_spec((SUBLANES, CONV_W)), _const_spec((1, CONV_W)),
                  _const_spec((ts, ts)), _const_spec((SUBLANES, LANES)), _const_spec((CONV_W, CONV_W))],
        out_specs=[pl.BlockSpec((1, N_HEADS, ts, LANES), lambda b, j: (b, 0, j, 0)),
                   pl.BlockSpec((1, N_HEADS, ts, LANES), lambda b, j: (b, 0, j, 0)),
                   pl.BlockSpec((1, N_HEADS, LANES, ts), lambda b, j: (b, 0, 0, j)),
                   pl.BlockSpec((1, ts, CONV_W), lambda b, j: (b, j, 0))],
        out_shape=[jax.ShapeDtypeStruct((bsz, N_HEADS, seq, LANES), BF16),
                   jax.ShapeDtypeStruct((bsz, N_HEADS, seq, LANES), BF16),
                   jax.ShapeDtypeStruct((bsz, N_HEADS, LANES, seq), BF16),
                   jax.ShapeDtypeStruct((bsz, seq, CONV_W), BF16)],
        scratch_shapes=[pltpu.VMEM((1, LANES), F32), pltpu.VMEM((ts + SUBLANES, CONV_W), F32),
                        pltpu.VMEM((d, _NA), BF16)],
        compiler_params=pltpu.CompilerParams(dimension_semantics=("arbitrary", "arbitrary"),
                                             vmem_limit_bytes=VMEM_LIMIT_BYTES),
        name="inproj",
    )(x, scale1, shift1, lw["wa"], consts["cq"], consts["sq"], consts["ck"], consts["sk"],
      lw["wuq"], lw["wukv"], lw["qg"], lw["kvg"], lw["bf"], lw["wconv"], lw["hg_conv"],
      consts["tri"], consts["lane_masks"], consts["gm256"])


def _attn_kernel(qi_ref, kj_ref, q_ref, k_ref, vt_ref, g_ref, gm_ref, o_ref, m_ref, acc_ref, s_ref, *, nh):
    step = pl.program_id(2)
    qi = qi_ref[step]
    kj = kj_ref[step]

    @pl.when(kj == 0)
    def _():
        m_ref[...] = jnp.full(m_ref.shape, -jnp.inf, F32)
        acc_ref[...] = jnp.zeros_like(acc_ref)

    def update(masked):
        for h in range(nh):
            st = lax.dot_general(k_ref[0, h], q_ref[0, h], (((1,), (1,)), ((), ())),
                                 preferred_element_type=F32)
            if masked:
                key = lax.broadcasted_iota(jnp.int32, st.shape, 0)
                qry = lax.broadcasted_iota(jnp.int32, st.shape, 1)
                st = jnp.where(key <= qry, st, -jnp.inf)
            s_ref[h] = st
        pts, alphas = [], []
        for h in range(nh):
            m_prev = m_ref[h]
            m_new = jnp.maximum(m_prev, jnp.max(s_ref[h], axis=0, keepdims=True))
            pts.append(jnp.exp2(s_ref[h] - m_new).astype(BF16))
            alphas.append(jnp.exp2(m_prev - m_new))
            m_ref[h] = m_new
        for h in range(nh):
            acc_ref[h] = alphas[h] * acc_ref[h] + jnp.dot(vt_ref[0, h], pts[h], preferred_element_type=F32)

    @pl.when(kj < qi)
    def _():
        update(False)

    @pl.when(kj == qi)
    def _():
        update(True)
        for pair in range(nh // 2):
            ot = jnp.concatenate([acc_ref[h, 0:HEAD_DIM, :] / acc_ref[h, HEAD_DIM:2 * HEAD_DIM, :]
                                  for h in (2 * pair, 2 * pair + 1)], axis=0)
            o = ot.T
            sl = slice(pair * LANES, (pair + 1) * LANES)
            on = o * lax.rsqrt(_group_mean_sq(o, gm_ref) + RMS_EPS) * g_ref[:, sl]
            o_ref[0, :, sl] = on.astype(BF16)


def _attention(qp, kp, vt, g_attn, gm128, tq, nh):
    bsz, n_heads, seq, _ = qp.shape
    nq = seq // tq
    ow = (nh // 2) * LANES
    qi_tab = np.concatenate([np.full(i + 1, i) for i in range(nq)]).astype(np.int32)
    kj_tab = np.concatenate([np.arange(i + 1) for i in range(nq)]).astype(np.int32)
    grid_spec = pltpu.PrefetchScalarGridSpec(
        num_scalar_prefetch=2,
        grid=(bsz, n_heads // nh, len(qi_tab)),
        in_specs=[pl.BlockSpec((1, nh, tq, LANES), lambda b, p, s, qi, kj: (b, p, qi[s], 0)),
                  pl.BlockSpec((1, nh, tq, LANES), lambda b, p, s, qi, kj: (b, p, kj[s], 0)),
                  pl.BlockSpec((1, nh, LANES, tq), lambda b, p, s, qi, kj: (b, p, 0, kj[s])),
                  pl.BlockSpec((1, ow), lambda b, p, s, qi, kj: (0, p)),
                  _const_spec((LANES, LANES))],
        out_specs=pl.BlockSpec((1, tq, ow), lambda b, p, s, qi, kj: (b, qi[s], p)),
        scratch_shapes=[pltpu.VMEM((nh, 1, tq), F32), pltpu.VMEM((nh, LANES, tq), F32),
                        pltpu.VMEM((nh, tq, tq), F32)])
    return pl.pallas_call(
        functools.partial(_attn_kernel, nh=nh),
        grid_spec=grid_spec,
        out_shape=jax.ShapeDtypeStruct((bsz, seq, (n_heads // 2) * LANES), BF16),
        compiler_params=pltpu.CompilerParams(
            dimension_semantics=("arbitrary", "arbitrary", "arbitrary"),
            vmem_limit_bytes=VMEM_LIMIT_BYTES),
        name="attention",
    )(jnp.asarray(qi_tab), jnp.asarray(kj_tab), qp, kp, vt, g_attn, gm128)


def _outproj_kernel(oa_ref, oc_ref, wa_ref, wc_ref, x_ref, gate_ref, g_ref, b_ref, o_ref, *, alpha):
    y = (jnp.dot(oa_ref[0], wa_ref[...], preferred_element_type=F32)
         + jnp.dot(oc_ref[0], wc_ref[...], preferred_element_type=F32))
    r = alpha * x_ref[0] + (1.0 + gate_ref[0]) * y
    o_ref[0] = _layer_norm_rows(r, g_ref[...], b_ref[...])


def _outproj(oa, oc, lw, x, gate1, alpha, tm):
    bsz, seq, d = x.shape
    wa_rows = oa.shape[-1]
    row = lambda b, j: (b, 0, 0)
    return pl.pallas_call(
        functools.partial(_outproj_kernel, alpha=alpha),
        grid=(bsz, seq // tm),
        in_specs=[pl.BlockSpec((1, tm, wa_rows), lambda b, j: (b, j, 0)),
                  pl.BlockSpec((1, tm, CONV_W), lambda b, j: (b, j, 0)),
                  _const_spec((wa_rows, d)), _const_spec((CONV_W, d)),
                  pl.BlockSpec((1, tm, d), lambda b, j: (b, j, 0)),
                  pl.BlockSpec((1, 1, d), row), _const_spec((1, d)), _const_spec((1, d))],
        out_specs=pl.BlockSpec((1, tm, d), lambda b, j: (b, j, 0)),
        out_shape=jax.ShapeDtypeStruct((bsz, seq, d), F32),
        compiler_params=pltpu.CompilerParams(dimension_semantics=("arbitrary", "arbitrary"),
                                             vmem_limit_bytes=VMEM_LIMIT_BYTES),
        name="outproj",
    )(oa, oc, lw["wo_a"], lw["wo_c"], x, gate1, lw["ln1_g"], lw["ln1_b"])


def _ffn_kernel(x_ref, sc_ref, sh_ref, gate_ref, wg_ref, wu_ref, wd_ref, g_ref, b_ref, o_ref, *, alpha, tf):
    x = x_ref[0]
    hb = (x * (1.0 + sc_ref[0]) + sh_ref[0]).astype(BF16)
    acc = jnp.zeros(x.shape, F32)
    for c in range(wg_ref.shape[1] // tf):
        sl = slice(c * tf, (c + 1) * tf)
        gt = jnp.dot(hb, wg_ref[:, sl], preferred_element_type=F32)
        up = jnp.dot(hb, wu_ref[:, sl], preferred_element_type=F32)
        acc = acc + jnp.dot((_silu(gt) * up).astype(BF16), wd_ref[sl, :], preferred_element_type=F32)
    r = alpha * x + (1.0 + gate_ref[0]) * acc
    o_ref[0] = _layer_norm_rows(r, g_ref[...], b_ref[...])


def _ffn_dense(x, scale2, shift2, gate2, wg, wu, wd, ln_g, ln_b, alpha, tm):
    bsz, seq, d = x.shape
    dff = wg.shape[1]
    tf = dff // 2 if (dff // 2) % LANES == 0 else dff
    row = lambda b, j: (b, 0, 0)
    return pl.pallas_call(
        functools.partial(_ffn_kernel, alpha=alpha, tf=tf),
        grid=(bsz, seq // tm),
        in_specs=[pl.BlockSpec((1, tm, d), lambda b, j: (b, j, 0)),
                  pl.BlockSpec((1, 1, d), row), pl.BlockSpec((1, 1, d), row), pl.BlockSpec((1, 1, d), row),
                  _const_spec((d, dff)), _const_spec((d, dff)), _const_spec((dff, d)),
                  _const_spec((1, d)), _const_spec((1, d))],
        out_specs=pl.BlockSpec((1, tm, d), lambda b, j: (b, j, 0)),
        out_shape=jax.ShapeDtypeStruct((bsz, seq, d), F32),
        compiler_params=pltpu.CompilerParams(dimension_semantics=("arbitrary", "arbitrary"),
                                             vmem_limit_bytes=VMEM_LIMIT_BYTES),
        name="ffn_dense",
    )(x, scale2, shift2, gate2, wg, wu, wd, ln_g, ln_b)


def _to_token_tiles(ref, v, n):
    for c in range(v.shape[1] // LANES):
        ref[pl.ds(c, n, stride=SUBLANES), :] = v[:, c * LANES:(c + 1) * LANES]


def _from_token_tiles(ref, n, first=0, tiles_per_row=1):
    stride = tiles_per_row * SUBLANES
    return jnp.concatenate([ref[pl.ds(first * SUBLANES + c, n, stride=stride), :] for c in range(SUBLANES)],
                           axis=1)


def _router_kernel(x_ref, sc_ref, sh_ref, wr_ref, stri_ref, h_ref, idx_ref, w_ref, cnt_ref, carry_ref, *, tm):
    @pl.when(pl.program_id(0) == 0)
    def _():
        carry_ref[...] = jnp.zeros_like(carry_ref)

    h = x_ref[0] * (1.0 + sc_ref[0]) + sh_ref[0]
    _to_token_tiles(h_ref, h, tm)
    h_hi = h.astype(BF16)
    h_lo = (h - h_hi.astype(F32)).astype(BF16)
    w_hi, w_lo = wr_ref[0], wr_ref[1]
    logits = (jnp.dot(h_hi, w_hi, preferred_element_type=F32)
              + jnp.dot(h_hi, w_lo, preferred_element_type=F32)
              + jnp.dot(h_lo, w_hi, preferred_element_type=F32))
    lane = lax.broadcasted_iota(jnp.int32, logits.shape, 1).astype(F32)
    neg = -jnp.inf
    lg = jnp.where(lane < N_EXPERTS, logits, neg)
    m1 = jnp.max(lg, axis=-1, keepdims=True)
    i1 = jnp.min(jnp.where(lg == m1, lane, float(LANES)), axis=-1, keepdims=True)
    lg2 = jnp.where(lane == i1, neg, lg)
    m2 = jnp.max(lg2, axis=-1, keepdims=True)
    i2 = jnp.min(jnp.where(lg2 == m2, lane, float(LANES)), axis=-1, keepdims=True)
    e2 = jnp.exp(m2 - m1)
    denom = 1.0 + e2
    w_ref[...] = jnp.where(lane == 0.0, 1.0 / denom, e2 / denom)
    first, second = lane == i1, lane == i2
    chosen = jnp.logical_or(first, second).astype(F32)
    before = carry_ref[...] + jnp.dot(stri_ref[...], chosen.astype(BF16), preferred_element_type=F32)
    rank1 = jnp.sum(jnp.where(first, before, 0.0), axis=-1, keepdims=True)
    rank2 = jnp.sum(jnp.where(second, before, 0.0), axis=-1, keepdims=True)
    idx_ref[...] = jnp.where(lane == 0.0, i1, jnp.where(lane == 1.0, i2, jnp.where(lane == 2.0, rank1, rank2))
                             ).astype(jnp.int32)
    total = carry_ref[...] + jnp.sum(chosen, axis=0, keepdims=True)
    carry_ref[...] = total
    cnt_ref[...] = jnp.broadcast_to(total, cnt_ref.shape)


def _router(x, scale2, shift2, wr2, stri, tm):
    bsz, seq, d = x.shape
    assert d == SUBLANES * LANES and bsz * seq * TOP_K < 2 ** 24
    nj = seq // tm
    n_tiles = bsz * nj
    row = lambda i: (i // nj, 0, 0)
    tok = lambda i: (i, 0)
    return pl.pallas_call(
        functools.partial(_router_kernel, tm=tm),
        grid=(n_tiles,),
        in_specs=[pl.BlockSpec((1, tm, d), lambda i: (i // nj, i % nj, 0)),
                  pl.BlockSpec((1, 1, d), row), pl.BlockSpec((1, 1, d), row),
                  _const_spec((2, d, LANES)), _const_spec((tm, tm))],
        out_specs=[pl.BlockSpec((tm * SUBLANES, LANES), tok),
                   pl.BlockSpec((tm, LANES), tok), pl.BlockSpec((tm, LANES), tok),
                   pl.BlockSpec((SUBLANES, LANES), lambda i: (0, 0))],
        out_shape=[jax.ShapeDtypeStruct((n_tiles * tm * SUBLANES, LANES), F32),
                   jax.ShapeDtypeStruct((bsz * seq, LANES), jnp.int32),
                   jax.ShapeDtypeStruct((bsz * seq, LANES), F32),
                   jax.ShapeDtypeStruct((SUBLANES, LANES), F32)],
        scratch_shapes=[pltpu.VMEM((1, LANES), F32)],
        compiler_params=pltpu.CompilerParams(dimension_semantics=("arbitrary",),
                                             vmem_limit_bytes=VMEM_LIMIT_BYTES),
        name="router",
    )(x, scale2, shift2, wr2, stri)


_COPY_UNROLL = 8


def _tile_at(ref, t):
    return ref.at[pl.ds(pl.multiple_of(t * SUBLANES, SUBLANES), SUBLANES)]


def _issue_tile_copies(n, copy_of):
    def group(g, carry):
        copies = [copy_of(g, u) for u in range(_COPY_UNROLL)]
        for u, (src, dst, sem) in enumerate(copies):
            pltpu.make_async_copy(src, dst, sem).start(priority=u % 2)
        return carry

    assert n % _COPY_UNROLL == 0
    lax.fori_loop(0, n // _COPY_UNROLL, group, 0)


def _group_kernel(di_ref, fi_ref, src_ref, dst_hbm, zero_ref, sem, *, tb, nf):
    zero_ref[...] = jnp.zeros_like(zero_ref)
    per_group = _COPY_UNROLL // TOP_K
    _issue_tile_copies(TOP_K * tb, lambda g, u: (_tile_at(src_ref, g * per_group + u // TOP_K),
                                                 _tile_at(dst_hbm, di_ref[0, 0, g * _COPY_UNROLL + u]), sem))
    _issue_tile_copies(nf, lambda g, u: (zero_ref, _tile_at(dst_hbm, fi_ref[0, 0, g * _COPY_UNROLL + u]), sem))
    block = pl.ds(0, tb * SUBLANES)
    for _ in range(TOP_K):
        pltpu.make_async_copy(src_ref, dst_hbm.at[block], sem).wait()

    def wait_fill(r, carry):
        pltpu.make_async_copy(zero_ref, _tile_at(dst_hbm, 0), sem).wait()
        return carry

    lax.fori_loop(0, nf, wait_fill, 0)


def _group_rows(h_tiles, dest, fill_dst, n_rows, tb):
    n_asg = dest.shape[0]
    steps = n_asg // (TOP_K * tb)
    nf = fill_dst.shape[0] // steps
    assert steps * TOP_K * tb == n_asg and steps * nf == fill_dst.shape[0]
    return pl.pallas_call(
        functools.partial(_group_kernel, tb=tb, nf=nf),
        grid=(steps,),
        in_specs=[pl.BlockSpec((1, 1, TOP_K * tb), lambda i: (i, 0, 0), memory_space=pltpu.SMEM),
                  pl.BlockSpec((1, 1, nf), lambda i: (i, 0, 0), memory_space=pltpu.SMEM),
                  pl.BlockSpec((tb * SUBLANES, LANES), lambda i: (i, 0))],
        out_specs=pl.BlockSpec(memory_space=pl.ANY),
        out_shape=jax.ShapeDtypeStruct((n_rows * SUBLANES, LANES), h_tiles.dtype),
        scratch_shapes=[pltpu.VMEM((SUBLANES, LANES), h_tiles.dtype), pltpu.SemaphoreType.DMA(())],
        compiler_params=pltpu.CompilerParams(dimension_semantics=("arbitrary",), has_side_effects=True),
        name="moe_group",
    )(dest.astype(jnp.int32).reshape(steps, 1, TOP_K * tb), fill_dst.astype(jnp.int32).reshape(steps, 1, nf),
      h_tiles)


def _expert_kernel(be_ref, nu_ref, x_ref, wg_ref, wu_ref, wd_ref, o_ref, acc_ref, xb_ref, *, tm):
    i = pl.program_id(0)
    f = pl.program_id(1)
    used = i < nu_ref[0]

    @pl.when(jnp.logical_and(used, f == 0))
    def _():
        xb_ref[...] = _from_token_tiles(x_ref, tm).astype(BF16)
        acc_ref[...] = jnp.zeros_like(acc_ref)

    @pl.when(used)
    def _():
        xb = xb_ref[...]
        gt = jnp.dot(xb, wg_ref[0], preferred_element_type=F32)
        up = jnp.dot(xb, wu_ref[0], preferred_element_type=F32)
        acc_ref[...] += jnp.dot((_silu(gt) * up).astype(BF16), wd_ref[0], preferred_element_type=F32)

    last = f == pl.num_programs(1) - 1

    @pl.when(jnp.logical_and(used, last))
    def _():
        _to_token_tiles(o_ref, acc_ref[...], tm)

    @pl.when(jnp.logical_and(jnp.logical_not(used), last))
    def _():
        o_ref[...] = jnp.zeros_like(o_ref)


def _experts(x_tiles, nb, block_e, n_used, wg, wu, wd, tm, tf):
    n_exp, d, dff = wg.shape
    nf = dff // tf

    def xmap(i, f, be, nu):
        return (jnp.minimum(i, nu[0] - 1), 0)

    def fidx(i, f, nu):
        return jnp.where(i < nu[0], f, nf - 1)

    grid_spec = pltpu.PrefetchScalarGridSpec(
        num_scalar_prefetch=2,
        grid=(nb, nf),
        in_specs=[pl.BlockSpec((tm * SUBLANES, LANES), xmap),
                  pl.BlockSpec((1, d, tf), lambda i, f, be, nu: (be[i], 0, fidx(i, f, nu))),
                  pl.BlockSpec((1, d, tf), lambda i, f, be, nu: (be[i], 0, fidx(i, f, nu))),
                  pl.BlockSpec((1, tf, d), lambda i, f, be, nu: (be[i], fidx(i, f, nu), 0))],
        out_specs=pl.BlockSpec((tm * SUBLANES, LANES), lambda i, f, be, nu: (i, 0)),
        scratch_shapes=[pltpu.VMEM((tm, d), F32), pltpu.VMEM((tm, d), BF16)])
    return pl.pallas_call(
        functools.partial(_expert_kernel, tm=tm),
        grid_spec=grid_spec,
        out_shape=jax.ShapeDtypeStruct((nb * tm * SUBLANES, LANES), F32),
        compiler_params=pltpu.CompilerParams(dimension_semantics=("arbitrary", "arbitrary"),
                                             vmem_limit_bytes=VMEM_LIMIT_BYTES),
        name="moe_experts",
    )(block_e, n_used, x_tiles, wg, wu, wd)


def _combine_kernel(cur_ref, nxt_ref, y_hbm, w_ref, x_ref, gate_ref, g_ref, b_ref, o_ref, ybuf_ref, sems,
                    *, tc, alpha):
    i = pl.program_id(0)
    nt = TOP_K * tc

    def fetch(idx_ref, slot):
        buf, sem = ybuf_ref.at[slot], sems.at[slot]
        _issue_tile_copies(nt, lambda g, u: (_tile_at(y_hbm, idx_ref[0, 0, g * _COPY_UNROLL + u]),
                                             _tile_at(buf, g * _COPY_UNROLL + u), sem))

    @pl.when(i == 0)
    def _():
        fetch(cur_ref, 0)

    @pl.when(i + 1 < pl.num_programs(0))
    def _():
        fetch(nxt_ref, (i + 1) % 2)

    slot = i % 2
    yb = ybuf_ref.at[slot]
    pltpu.make_async_copy(y_hbm.at[pl.ds(0, nt * SUBLANES)], yb, sems.at[slot]).wait()
    w = w_ref[...]
    y = (w[:, 0:1] * _from_token_tiles(yb, tc, 0, TOP_K)
         + w[:, 1:2] * _from_token_tiles(yb, tc, 1, TOP_K))
    r = alpha * x_ref[0] + (1.0 + gate_ref[0]) * y
    o_ref[0] = _layer_norm_rows(r, g_ref[...], b_ref[...])


def _combine(y_tiles, dest, top_w, x, gate2, ln_g, ln_b, alpha, tc):
    bsz, seq, d = x.shape
    nj = seq // tc
    steps = bsz * nj
    nt = TOP_K * tc
    dest_blocks = dest.astype(jnp.int32).reshape(steps, 1, nt)
    return pl.pallas_call(
        functools.partial(_combine_kernel, tc=tc, alpha=alpha),
        grid=(steps,),
        in_specs=[pl.BlockSpec((1, 1, nt), lambda i: (i, 0, 0), memory_space=pltpu.SMEM),
                  pl.BlockSpec((1, 1, nt), lambda i: (jnp.minimum(i + 1, steps - 1), 0, 0),
                               memory_space=pltpu.SMEM),
                  pl.BlockSpec(memory_space=pl.ANY),
                  pl.BlockSpec((tc, LANES), lambda i: (i, 0)),
                  pl.BlockSpec((1, tc, d), lambda i: (i // nj, i % nj, 0)),
                  pl.BlockSpec((1, 1, d), lambda i: (i // nj, 0, 0)), _const_spec((1, d)), _const_spec((1, d))],
        out_specs=pl.BlockSpec((1, tc, d), lambda i: (i // nj, i % nj, 0)),
        out_shape=jax.ShapeDtypeStruct((bsz, seq, d), F32),
        scratch_shapes=[pltpu.VMEM((2, nt * SUBLANES, LANES), F32), pltpu.SemaphoreType.DMA((2,))],
        compiler_params=pltpu.CompilerParams(dimension_semantics=("arbitrary",),
                                             vmem_limit_bytes=VMEM_LIMIT_BYTES),
        name="moe_combine",
    )(dest_blocks, dest_blocks, y_tiles, top_w, x, gate2, ln_g, ln_b)


def _moe(x1, scale2, shift2, gate2, wr2, stri, wg, wu, wd, ln_g, ln_b, alpha, tm_tok, tm, tf, tc):
    bsz, seq, d = x1.shape
    n_tok = bsz * seq
    n_asg = n_tok * TOP_K
    h_tiles, idx128, w128, cnt = _router(x1, scale2, shift2, wr2, stri, tm_tok)
    counts = cnt[0, :N_EXPERTS].astype(jnp.int32)
    padded = (counts + tm - 1) // tm * tm
    pad_end = jnp.cumsum(padded)
    pad_start = pad_end - padded
    e_ids, ranks = idx128[:, 0:TOP_K], idx128[:, TOP_K:2 * TOP_K]
    start_of = jnp.sum(jnp.where(e_ids[:, :, None] == jnp.arange(N_EXPERTS, dtype=jnp.int32), pad_start, 0), axis=-1)
    dest = (start_of + ranks).reshape(n_asg)
    nb = n_asg // tm + N_EXPERTS
    n_rows = nb * tm
    n_fill = n_rows - n_asg
    fill_end = jnp.cumsum(padded - counts)
    slot = jnp.arange(n_fill, dtype=jnp.int32)
    slot_e = jnp.sum((slot[:, None] >= fill_end[None, :]).astype(jnp.int32), axis=1)
    in_group = slot_e < N_EXPERTS
    ge = jnp.minimum(slot_e, N_EXPERTS - 1)
    fill_dst = jnp.where(in_group, (pad_start + counts)[ge] + slot - (fill_end - (padded - counts))[ge],
                         pad_end[-1] + slot - fill_end[-1])
    x_tiles = _group_rows(h_tiles, dest, fill_dst, n_rows, tm_tok)
    block_row = jnp.arange(nb, dtype=jnp.int32) * tm
    block_e = jnp.minimum(jnp.sum((block_row[:, None] >= pad_end[None, :]).astype(jnp.int32), axis=1),
                          N_EXPERTS - 1)
    n_used = (pad_end[-1:] // tm).astype(jnp.int32)
    y_tiles = _experts(x_tiles, nb, block_e, n_used, wg, wu, wd, tm, tf)
    return _combine(y_tiles, dest, w128, x1, gate2, ln_g, ln_b, alpha, tc)


def _layer_weights(layer, w_in, b_fgate, w_conv, q_norm_g, kv_norm_g, w_uq, w_ukv, head_norm_g, w_o,
                   ln1_g, ln1_b):
    d = w_in.shape[1]
    sizes = [FOX_W, FOX_W, FOX_W, H_FOX, CONV_W, CONV_W, CONV_W, Q_LORA, KV_LORA, QK_ROPE]
    offs = np.concatenate([[0], np.cumsum(sizes)])
    wi = w_in[layer]
    fq, fk, fv, fl, bg, cg, hin, cq, ckv, kr = [wi[:, offs[i]:offs[i + 1]] for i in range(len(sizes))]
    half = QK_ROPE // 2
    z64 = jnp.zeros((d, QK_NOPE), F32)
    z32 = jnp.zeros((d, LANES - QK_NOPE - QK_ROPE), F32)
    wa = jnp.concatenate([
        fq, fk, fv,
        jnp.pad(jnp.repeat(fl, _N_SPLIT, axis=1), ((0, 0), (0, LANES - _N_SPLIT * H_FOX))), bg, cg, hin, cq, ckv,
        jnp.concatenate([z64, kr, z32], axis=1),
        jnp.concatenate([z64, kr[:, half:], kr[:, :half], z32], axis=1)], axis=1)
    assert wa.shape[1] == _NA
    q3 = w_uq[layer].reshape(Q_LORA, H_MLA, QK_NOPE + QK_ROPE)
    q_main = jnp.pad(q3, ((0, 0), (0, 0), (0, LANES - QK_NOPE - QK_ROPE)))
    q_swap = jnp.concatenate([jnp.zeros((Q_LORA, H_MLA, QK_NOPE), F32), q3[:, :, QK_NOPE + half:],
                              q3[:, :, QK_NOPE:QK_NOPE + half],
                              jnp.zeros((Q_LORA, H_MLA, LANES - QK_NOPE - QK_ROPE), F32)], axis=2)
    wuq = jnp.concatenate([q_main.reshape(Q_LORA, H_MLA * LANES), q_swap.reshape(Q_LORA, H_MLA * LANES)],
                          axis=1).astype(BF16)
    kv3 = w_ukv[layer].reshape(KV_LORA, H_MLA, QK_NOPE + V_DIM)
    k_nope = jnp.pad(kv3[:, :, :QK_NOPE], ((0, 0), (0, 0), (0, LANES - QK_NOPE))).reshape(KV_LORA, H_MLA * LANES)
    wukv = jnp.concatenate([k_nope, kv3[:, :, QK_NOPE:].reshape(KV_LORA, MLA_W)], axis=1).astype(BF16)
    hg = head_norm_g[layer]
    wo = w_o[layer]
    c0, c1 = FOX_W, FOX_W + CONV_W
    return {
        "wa": wa, "wuq": wuq, "wukv": wukv,
        "qg": q_norm_g[layer].reshape(1, Q_LORA), "kvg": kv_norm_g[layer].reshape(1, KV_LORA),
        "bf": jnp.pad(jnp.repeat(b_fgate[layer], _N_SPLIT), (0, LANES - _N_SPLIT * H_FOX)).reshape(1, LANES),
        "wconv": jnp.pad(w_conv[layer], ((0, SUBLANES - CONV_K), (0, 0))),
        "hg_conv": hg[c0:c1].reshape(1, CONV_W),
        "hg_attn": jnp.concatenate([hg[:c0], hg[c1:]]).reshape(1, FOX_W + MLA_W),
        "wo_a": jnp.concatenate([wo[:c0], wo[c1:]], axis=0).astype(BF16),
        "wo_c": wo[c0:c1].astype(BF16),
        "ln1_g": ln1_g[layer].reshape(1, d), "ln1_b": ln1_b[layer].reshape(1, d),
    }


def _constants(seq, ts):
    pos = jnp.arange(seq, dtype=F32)
    inv_freq = ROPE_THETA ** (-jnp.arange(0, QK_ROPE, 2, dtype=F32) / QK_ROPE)
    ang = pos[:, None] * inv_freq[None, :]
    cos, sin = jnp.cos(ang), jnp.sin(ang)
    pad_r = jnp.zeros((seq, LANES - QK_NOPE - QK_ROPE), F32)
    ctab = jnp.concatenate([jnp.ones((seq, QK_NOPE), F32), cos, cos, pad_r], axis=1)
    stab = jnp.concatenate([jnp.zeros((seq, QK_NOPE), F32), -sin, sin, pad_r], axis=1)
    q_scale = (QK_NOPE + QK_ROPE) ** -0.5 * LOG2E
    lanes = np.arange(LANES)
    lm = np.zeros((SUBLANES, LANES), np.float32)
    for part in range(_N_SPLIT):
        lm[part] = (lanes % _N_SPLIT == part) & (lanes < _N_SPLIT * H_FOX)
    for row, base in ((3, HEAD_DIM), (5, 0)):
        lm[row] = (lanes >= base) & (lanes < base + _N_SPLIT)
        lm[row + 1] = (lanes >= base + _N_SPLIT) & (lanes < base + 2 * _N_SPLIT)

    def group_mean(n):
        gidx = np.arange(n) // HEAD_DIM
        return (gidx[:, None] == gidx[None, :]).astype(np.float32) / HEAD_DIM

    return {
        "cq": ctab * q_scale, "sq": stab * q_scale, "ck": ctab, "sk": stab,
        "tri": jnp.asarray(np.tril(np.ones((ts, ts), np.float32)), BF16),
        "stri": jnp.asarray(np.tril(np.ones((ts, ts), np.float32), k=-1), BF16),
        "lane_masks": jnp.asarray(lm),
        "gm256": jnp.asarray(group_mean(CONV_W), BF16), "gm128": jnp.asarray(group_mean(LANES), BF16),
    }


def _tile(n, pref):
    t = min(n, pref)
    assert n % t == 0, (n, pref)
    return t


def kernel(x, c, w_mod, b_mod, w_in, b_fgate, w_conv, q_norm_g, kv_norm_g, w_uq, w_ukv, head_norm_g, w_o, ln1_g, ln1_b, ln2_g, ln2_b, ffn_w_gate, ffn_w_up, ffn_w_down, router_w, exp_w_gate, exp_w_up, exp_w_down):
    bsz, seq, d = x.shape
    depth = w_mod.shape[0]
    assert d == D_MODEL and bsz <= SUBLANES
    alpha = (2 * depth) ** 0.25
    ts = _tile(seq, 512)
    tq = _tile(seq, 512)
    tm_e = 512
    tf_e = 512
    tc = _tile(seq, 256)
    assert (bsz * seq * TOP_K) % tm_e == 0 and exp_w_gate.shape[-1] % tf_e == 0

    consts = _constants(seq, ts)
    c_pad = jnp.pad(c, ((0, SUBLANES - bsz), (0, 0)))
    mod = _modulation(c_pad, w_mod, b_mod)[:, :bsz, :]

    for layer in range(depth):
        m6 = mod[layer].reshape(bsz, 6, 1, d)
        shift1, scale1, gate1, shift2, scale2, gate2 = [m6[:, i] for i in range(6)]
        lw = _layer_weights(layer, w_in, b_fgate, w_conv, q_norm_g, kv_norm_g, w_uq, w_ukv, head_norm_g,
                            w_o, ln1_g, ln1_b)
        qp, kp, vt, oc = _inproj(x, scale1, shift1, lw, consts, ts)
        oa = _attention(qp, kp, vt, lw["hg_attn"], consts["gm128"], tq, 6)
        x = _outproj(oa, oc, lw, x, gate1, alpha, ts)
        j = layer // 2
        g2, b2 = ln2_g[layer].reshape(1, d), ln2_b[layer].reshape(1, d)
        if layer % 2 == 0:
            x = _ffn_dense(x, scale2, shift2, gate2, ffn_w_gate[j].astype(BF16), ffn_w_up[j].astype(BF16),
                           ffn_w_down[j].astype(BF16), g2, b2, alpha, ts)
        else:
            wr = jnp.pad(router_w[j], ((0, 0), (0, LANES - N_EXPERTS)))
            wr_hi = wr.astype(BF16)
            wr_lo = (wr - wr_hi.astype(F32)).astype(BF16)
            x = _moe(x, scale2, shift2, gate2, jnp.stack([wr_hi, wr_lo]), consts["stri"],
                     exp_w_gate[j].astype(BF16), exp_w_up[j].astype(BF16), exp_w_down[j].astype(BF16),
                     g2, b2, alpha, ts, tm_e, tf_e, tc)
    return x
```

```python
import functools

import numpy as np
import jax
import jax.numpy as jnp
from jax import lax
from jax.experimental import pallas as pl
from jax.experimental.pallas import tpu as pltpu

F32 = jnp.float32
BF16 = jnp.bfloat16

D_MODEL = 1024
HEAD_DIM = 64
H_FOX = 6
FOX_W = H_FOX * HEAD_DIM
CONV_W = 256
CONV_K = 3
H_MLA = 6
Q_LORA = 256
KV_LORA = 256
QK_NOPE = 64
QK_ROPE = 32
V_DIM = 64
MLA_W = H_MLA * V_DIM
N_HEADS = H_FOX + H_MLA
ROPE_THETA = 10000.0
N_EXPERTS = 8
TOP_K = 2
LN_EPS = 1e-5
RMS_EPS = 1e-6

LANES = 128
SUBLANES = 8
VMEM_LIMIT_BYTES = 56 * 1024 * 1024

_HW = 6 * LANES
_QF0, _KF0, _VF0, _FL0 = 0, FOX_W, 2 * FOX_W, 3 * FOX_W
_BG0 = _FL0 + LANES
_CG0, _HI0, _CQ0, _CKV0 = _BG0 + CONV_W, _BG0 + 2 * CONV_W, _BG0 + 3 * CONV_W, _BG0 + 3 * CONV_W + Q_LORA
_KR0 = _CKV0 + KV_LORA
_KRS0 = _KR0 + LANES
_NA = _KRS0 + LANES
LOG2E = 1.4426950408889634
VT_ROWS = V_DIM + 16
_N_SPLIT = 3


def _const_spec(shape):
    zeros = (0,) * len(shape)
    return pl.BlockSpec(shape, lambda *_: zeros, pipeline_mode=pl.Buffered(1))


def _silu(v):
    return v * (1.0 / (1.0 + jnp.exp(-v)))


def _split3(v):
    hi = v.astype(BF16)
    r1 = v - hi.astype(F32)
    mid = r1.astype(BF16)
    lo = (r1 - mid.astype(F32)).astype(BF16)
    return hi, mid, lo


def _group_mean_sq(v, gmat_ref):
    sq = v * v
    hi = sq.astype(BF16)
    lo = (sq - hi.astype(F32)).astype(BF16)
    g = gmat_ref[...]
    return (jnp.dot(hi, g, preferred_element_type=F32) + jnp.dot(lo, g, preferred_element_type=F32))


def _layer_norm_rows(r, g, b):
    mu = jnp.mean(r, axis=-1, keepdims=True)
    rc = r - mu
    var = jnp.mean(rc * rc, axis=-1, keepdims=True)
    return rc * lax.rsqrt(var + LN_EPS) * g + b


def _mod_kernel(c_ref, w_ref, b_ref, o_ref):
    act = _silu(c_ref[...]).astype(BF16)
    o_ref[0] = jnp.dot(act, w_ref[0].astype(BF16), preferred_element_type=F32) + b_ref[0]


def _modulation(c_pad, w_mod, b_mod):
    depth, d, n = w_mod.shape
    tn = 1024
    return pl.pallas_call(
        _mod_kernel,
        grid=(depth, n // tn),
        in_specs=[pl.BlockSpec((SUBLANES, d), lambda l, j: (0, 0)),
                  pl.BlockSpec((1, d, tn), lambda l, j: (l, 0, j)),
                  pl.BlockSpec((1, 1, tn), lambda l, j: (l, 0, j))],
        out_specs=pl.BlockSpec((1, SUBLANES, tn), lambda l, j: (l, 0, j)),
        out_shape=jax.ShapeDtypeStruct((depth, SUBLANES, n), F32),
        compiler_params=pltpu.CompilerParams(dimension_semantics=("arbitrary", "arbitrary")),
        name="modulation",
    )(c_pad, w_mod, b_mod.reshape(depth, 1, n))


def _inproj_kernel(x_ref, sc_ref, sh_ref, wa_ref, cq_ref, sq_ref, ck_ref, sk_ref, wuq_ref, wukv_ref,
                   qg_ref, kvg_ref, bf_ref, wconv_ref, hgc_ref, tri_ref, lm_ref, gm_ref,
                   qp_ref, kp_ref, vt_ref, oc_ref, fcarry_ref, ubuf_ref, wab_ref, *, ts):
    @pl.when(jnp.logical_and(pl.program_id(0) == 0, pl.program_id(1) == 0))
    def _():
        for c in range(_NA // LANES):
            sl = slice(c * LANES, (c + 1) * LANES)
            wab_ref[:, sl] = wa_ref[:, sl].astype(BF16)

    @pl.when(pl.program_id(1) == 0)
    def _():
        fcarry_ref[...] = jnp.zeros_like(fcarry_ref)
        ubuf_ref[pl.ds(0, SUBLANES), :] = jnp.zeros((SUBLANES, CONV_W), F32)

    hb = (x_ref[0] * (1.0 + sc_ref[0]) + sh_ref[0]).astype(BF16)

    def proj(lo, hi):
        return jnp.dot(hb, wab_ref[:, lo:hi], preferred_element_type=F32)

    def put_values_t(head, v_half_t):
        vt_ref[0, head, 0:V_DIM, :] = v_half_t.astype(BF16)
        vt_ref[0, head, V_DIM:VT_ROWS, :] = jnp.ones((VT_ROWS - V_DIM, ts), BF16)

    a = proj(_FL0, _FL0 + LANES) + bf_ref[...]
    logf = jnp.minimum(a, 0.0) - jnp.log1p(jnp.exp(-jnp.abs(a)))
    tri = tri_ref[...]
    csum = sum(jnp.dot(tri, part, preferred_element_type=F32) for part in _split3(logf))
    fcum = fcarry_ref[...] + csum
    fcarry_ref[...] = fcum[ts - 1:ts, :]
    f_hi, f_mid, f_lo = _split3(fcum * LOG2E)
    lm = lm_ref[...]
    fparts = (f_hi.astype(F32) * lm[0:1, :] + f_mid.astype(F32) * lm[1:2, :]
              + f_lo.astype(F32) * lm[2:3, :])
    lane = lax.broadcasted_iota(jnp.int32, (ts, LANES), 1)
    low_half = lane < HEAD_DIM
    zq = proj(_QF0, _QF0 + FOX_W) * (HEAD_DIM ** -0.5 * LOG2E)
    zk = proj(_KF0, _KF0 + FOX_W)
    zv = proj(_VF0, _VF0 + FOX_W)
    for h in range(H_FOX):
        blk = slice((h // 2) * LANES, (h // 2 + 1) * LANES)
        if h % 2 == 0:
            base, f_mask, one_mask, keep = HEAD_DIM, lm[3:4, :], lm[4:5, :], low_half
        else:
            base, f_mask, one_mask, keep = 0, lm[5:6, :], lm[6:7, :], jnp.logical_not(low_half)
        f_at_q = pltpu.roll(fparts, (base - _N_SPLIT * h) % LANES, axis=1)
        f_at_k = pltpu.roll(fparts, (base + _N_SPLIT - _N_SPLIT * h) % LANES, axis=1)
        qp_ref[0, h] = jnp.where(keep, zq[:, blk], f_at_q * f_mask + one_mask).astype(BF16)
        kp_ref[0, h] = jnp.where(keep, zk[:, blk], f_mask - f_at_k * one_mask).astype(BF16)
    for j in range(H_FOX // 2):
        vt = zv[:, j * LANES:(j + 1) * LANES].T
        put_values_t(2 * j, vt[0:V_DIM])
        put_values_t(2 * j + 1, vt[V_DIM:LANES])

    u = proj(_CG0, _CG0 + CONV_W) * proj(_HI0, _HI0 + CONV_W)
    ubuf_ref[pl.ds(SUBLANES, ts), :] = u
    u1 = ubuf_ref[pl.ds(SUBLANES - 1, ts), :]
    u2 = ubuf_ref[pl.ds(SUBLANES - 2, ts), :]
    ubuf_ref[pl.ds(0, SUBLANES), :] = u[ts - SUBLANES:ts, :]
    wc = wconv_ref[...]
    oc = proj(_BG0, _BG0 + CONV_W) * (wc[0:1, :] * u2 + wc[1:2, :] * u1 + wc[2:3, :] * u)
    ocn = oc * lax.rsqrt(_group_mean_sq(oc, gm_ref) + RMS_EPS) * hgc_ref[...]
    oc_ref[0] = ocn.astype(BF16)

    def rms(v, g):
        return (v * lax.rsqrt(jnp.mean(v * v, axis=-1, keepdims=True) + RMS_EPS) * g).astype(BF16)

    cqn = rms(proj(_CQ0, _CQ0 + Q_LORA), qg_ref[...])
    qm = jnp.dot(cqn, wuq_ref[:, 0:_HW], preferred_element_type=F32)
    qs = jnp.dot(cqn, wuq_ref[:, _HW:2 * _HW], preferred_element_type=F32)
    kvn = rms(proj(_CKV0, _CKV0 + KV_LORA), kvg_ref[...])
    kn = jnp.dot(kvn, wukv_ref[:, 0:_HW], preferred_element_type=F32)
    vm = jnp.dot(kvn, wukv_ref[:, _HW:_HW + MLA_W], preferred_element_type=F32)
    krr = proj(_KR0, _KR0 + LANES) * ck_ref[...] + proj(_KRS0, _KRS0 + LANES) * sk_ref[...]
    cq, sq = cq_ref[...], sq_ref[...]
    for h in range(H_MLA):
        sl = slice(h * LANES, (h + 1) * LANES)
        qp_ref[0, H_FOX + h] = (qm[:, sl] * cq + qs[:, sl] * sq).astype(BF16)
        kp_ref[0, H_FOX + h] = (kn[:, sl] + krr).astype(BF16)
    for j in range(H_MLA // 2):
        vt = vm[:, j * LANES:(j + 1) * LANES].T
        put_values_t(H_FOX + 2 * j, vt[0:V_DIM])
        put_values_t(H_FOX + 2 * j + 1, vt[V_DIM:LANES])


def _inproj(x, scale1, shift1, lw, consts, ts):
    bsz, seq, d = x.shape
    kern = functools.partial(_inproj_kernel, ts=ts)
    row = lambda b, j: (b, 0, 0)
    tab = pl.BlockSpec((ts, LANES), lambda b, j: (j, 0))
    return pl.pallas_call(
        kern,
        grid=(bsz, seq // ts),
        in_specs=[pl.BlockSpec((1, ts, d), lambda b, j: (b, j, 0)),
                  pl.BlockSpec((1, 1, d), row), pl.BlockSpec((1, 1, d), row),
                  _const_spec((d, _NA)), tab, tab, tab, tab,
                  _const_spec((Q_LORA, 2 * _HW)), _const_spec((KV_LORA, _HW + MLA_W)),
                  _const_spec((1, Q_LORA)), _const_spec((1, KV_LORA)), _const_spec((1, LANES)),
                  _const_spec((SUBLANES, CONV_W)), _const_spec((1, CONV_W)),
                  _const_spec((ts, ts)), _const_spec((SUBLANES, LANES)), _const_spec((CONV_W, CONV_W))],
        out_specs=[pl.BlockSpec((1, N_HEADS, ts, LANES), lambda b, j: (b, 0, j, 0)),
                   pl.BlockSpec((1, N_HEADS, ts, LANES), lambda b, j: (b, 0, j, 0)),
                   pl.BlockSpec((1, N_HEADS, VT_ROWS, ts), lambda b, j: (b, 0, 0, j)),
                   pl.BlockSpec((1, ts, CONV_W), lambda b, j: (b, j, 0))],
        out_shape=[jax.ShapeDtypeStruct((bsz, N_HEADS, seq, LANES), BF16),
                   jax.ShapeDtypeStruct((bsz, N_HEADS, seq, LANES), BF16),
                   jax.ShapeDtypeStruct((bsz, N_HEADS, VT_ROWS, seq), BF16),
                   jax.ShapeDtypeStruct((bsz, seq, CONV_W), BF16)],
        scratch_shapes=[pltpu.VMEM((1, LANES), F32), pltpu.VMEM((ts + SUBLANES, CONV_W), F32),
                        pltpu.VMEM((d, _NA), BF16)],
        compiler_params=pltpu.CompilerParams(dimension_semantics=("arbitrary", "arbitrary"),
                                             vmem_limit_bytes=VMEM_LIMIT_BYTES),
        name="inproj",
    )(x, scale1, shift1, lw["wa"], consts["cq"], consts["sq"], consts["ck"], consts["sk"],
      lw["wuq"], lw["wukv"], lw["qg"], lw["kvg"], lw["bf"], lw["wconv"], lw["hg_conv"],
      consts["tri"], consts["lane_masks"], consts["gm256"])


def _attn_kernel(qi_ref, kj_ref, q_ref, k_ref, vt_ref, g_ref, gm_ref, o_ref, m_ref, acc_ref, s_ref, *, nh):
    step = pl.program_id(2)
    qi = qi_ref[step]
    kj = kj_ref[step]

    @pl.when(kj == 0)
    def _():
        m_ref[...] = jnp.full(m_ref.shape, -jnp.inf, F32)
        acc_ref[...] = jnp.zeros_like(acc_ref)

    def update(masked):
        for h in range(nh):
            st = lax.dot_general(k_ref[0, h], q_ref[0, h], (((1,), (1,)), ((), ())),
                                 preferred_element_type=F32)
            if masked:
                key = lax.broadcasted_iota(jnp.int32, st.shape, 0)
                qry = lax.broadcasted_iota(jnp.int32, st.shape, 1)
                st = jnp.where(key <= qry, st, -jnp.inf)
            s_ref[h] = st
        pts, alphas = [], []
        for h in range(nh):
            m_prev = m_ref[h]
            m_new = jnp.maximum(m_prev, jnp.max(s_ref[h], axis=0, keepdims=True))
            pts.append(jnp.exp2(s_ref[h] - m_new).astype(BF16))
            alphas.append(jnp.exp2(m_prev - m_new))
            m_ref[h] = m_new
        for h in range(nh):
            acc_ref[h] = alphas[h] * acc_ref[h] + jnp.dot(vt_ref[0, h], pts[h], preferred_element_type=F32)

    @pl.when(kj < qi)
    def _():
        update(False)

    @pl.when(kj == qi)
    def _():
        update(True)
        for pair in range(nh // 2):
            ot = jnp.concatenate([acc_ref[h, 0:V_DIM, :] / acc_ref[h, V_DIM:V_DIM + 1, :]
                                  for h in (2 * pair, 2 * pair + 1)], axis=0)
            o = ot.T
            sl = slice(pair * LANES, (pair + 1) * LANES)
            on = o * lax.rsqrt(_group_mean_sq(o, gm_ref) + RMS_EPS) * g_ref[:, sl]
            o_ref[0, :, sl] = on.astype(BF16)


def _attention(qp, kp, vt, g_attn, gm128, tq, nh):
    bsz, n_heads, seq, _ = qp.shape
    nq = seq // tq
    ow = (nh // 2) * LANES
    qi_tab = np.concatenate([np.full(i + 1, i) for i in range(nq)]).astype(np.int32)
    kj_tab = np.concatenate([np.arange(i + 1) for i in range(nq)]).astype(np.int32)
    grid_spec = pltpu.PrefetchScalarGridSpec(
        num_scalar_prefetch=2,
        grid=(bsz, n_heads // nh, len(qi_tab)),
        in_specs=[pl.BlockSpec((1, nh, tq, LANES), lambda b, p, s, qi, kj: (b, p, qi[s], 0)),
                  pl.BlockSpec((1, nh, tq, LANES), lambda b, p, s, qi, kj: (b, p, kj[s], 0)),
                  pl.BlockSpec((1, nh, VT_ROWS, tq), lambda b, p, s, qi, kj: (b, p, 0, kj[s])),
                  pl.BlockSpec((1, ow), lambda b, p, s, qi, kj: (0, p)),
                  _const_spec((LANES, LANES))],
        out_specs=pl.BlockSpec((1, tq, ow), lambda b, p, s, qi, kj: (b, qi[s], p)),
        scratch_shapes=[pltpu.VMEM((nh, 1, tq), F32), pltpu.VMEM((nh, VT_ROWS, tq), F32),
                        pltpu.VMEM((nh, tq, tq), F32)])
    return pl.pallas_call(
        functools.partial(_attn_kernel, nh=nh),
        grid_spec=grid_spec,
        out_shape=jax.ShapeDtypeStruct((bsz, seq, (n_heads // 2) * LANES), BF16),
        compiler_params=pltpu.CompilerParams(
            dimension_semantics=("arbitrary", "arbitrary", "arbitrary"),
            vmem_limit_bytes=VMEM_LIMIT_BYTES),
        name="attention",
    )(jnp.asarray(qi_tab), jnp.asarray(kj_tab), qp, kp, vt, g_attn, gm128)


def _mixer_out_rows(oa_ref, oc_ref, woa_ref, woc_ref, x_ref, gate_ref, g_ref, b_ref, alpha):
    y = (jnp.dot(oa_ref[0], woa_ref[...], preferred_element_type=F32)
         + jnp.dot(oc_ref[0], woc_ref[...], preferred_element_type=F32))
    r = alpha * x_ref[0] + (1.0 + gate_ref[0]) * y
    return _layer_norm_rows(r, g_ref[...], b_ref[...])


def _mixer_out_specs(oa, d, tm, act, row):
    wa_rows = oa.shape[-1]
    return [pl.BlockSpec((1, tm, wa_rows), act), pl.BlockSpec((1, tm, CONV_W), act),
            _const_spec((wa_rows, d)), _const_spec((CONV_W, d)),
            pl.BlockSpec((1, tm, d), act), pl.BlockSpec((1, 1, d), row), _const_spec((1, d)), _const_spec((1, d))]


def _ffn_kernel(oa_ref, oc_ref, woa_ref, woc_ref, xin_ref, gate1_ref, g1_ref, b1_ref,
                sc_ref, sh_ref, gate_ref, wg_ref, wu_ref, wd_ref, g_ref, b_ref, o_ref, *, alpha, tf):
    x = _mixer_out_rows(oa_ref, oc_ref, woa_ref, woc_ref, xin_ref, gate1_ref, g1_ref, b1_ref, alpha)
    hb = (x * (1.0 + sc_ref[0]) + sh_ref[0]).astype(BF16)
    acc = jnp.zeros(x.shape, F32)
    for c in range(wg_ref.shape[1] // tf):
        sl = slice(c * tf, (c + 1) * tf)
        gt = jnp.dot(hb, wg_ref[:, sl], preferred_element_type=F32)
        up = jnp.dot(hb, wu_ref[:, sl], preferred_element_type=F32)
        acc = acc + jnp.dot((_silu(gt) * up).astype(BF16), wd_ref[sl, :], preferred_element_type=F32)
    r = alpha * x + (1.0 + gate_ref[0]) * acc
    o_ref[0] = _layer_norm_rows(r, g_ref[...], b_ref[...])


def _ffn_dense(oa, oc, lw, x, gate1, scale2, shift2, gate2, wg, wu, wd, ln_g, ln_b, alpha, tm):
    bsz, seq, d = x.shape
    dff = wg.shape[1]
    tf = dff // 2 if (dff // 2) % LANES == 0 else dff
    nj = seq // tm
    act = lambda b, j: (b, j, 0)
    row = lambda b, j: (b, 0, 0)
    return pl.pallas_call(
        functools.partial(_ffn_kernel, alpha=alpha, tf=tf),
        grid=(bsz, nj),
        in_specs=_mixer_out_specs(oa, d, tm, act, row) + [
            pl.BlockSpec((1, 1, d), row), pl.BlockSpec((1, 1, d), row), pl.BlockSpec((1, 1, d), row),
            _const_spec((d, dff)), _const_spec((d, dff)), _const_spec((dff, d)),
            _const_spec((1, d)), _const_spec((1, d))],
        out_specs=pl.BlockSpec((1, tm, d), act),
        out_shape=jax.ShapeDtypeStruct((bsz, seq, d), F32),
        compiler_params=pltpu.CompilerParams(dimension_semantics=("arbitrary", "arbitrary"),
                                             vmem_limit_bytes=VMEM_LIMIT_BYTES),
        name="ffn_dense",
    )(oa, oc, lw["wo_a"], lw["wo_c"], x, gate1, lw["ln1_g"], lw["ln1_b"],
      scale2, shift2, gate2, wg, wu, wd, ln_g, ln_b)


def _to_token_tiles(ref, v, n):
    for c in range(v.shape[1] // LANES):
        ref[pl.ds(c, n, stride=SUBLANES), :] = v[:, c * LANES:(c + 1) * LANES]


def _from_token_tiles(ref, n, first=0, tiles_per_row=1):
    stride = tiles_per_row * SUBLANES
    return jnp.concatenate([ref[pl.ds(first * SUBLANES + c, n, stride=stride), :] for c in range(SUBLANES)],
                           axis=1)


def _router_kernel(oa_ref, oc_ref, woa_ref, woc_ref, xin_ref, gate1_ref, g1_ref, b1_ref,
                   sc_ref, sh_ref, wr_ref, stri_ref, x_ref, h_ref, idx_ref, w_ref, cnt_ref, carry_ref,
                   *, tm, alpha):
    @pl.when(pl.program_id(0) == 0)
    def _():
        carry_ref[...] = jnp.zeros_like(carry_ref)

    x = _mixer_out_rows(oa_ref, oc_ref, woa_ref, woc_ref, xin_ref, gate1_ref, g1_ref, b1_ref, alpha)
    x_ref[0] = x
    h = x * (1.0 + sc_ref[0]) + sh_ref[0]
    _to_token_tiles(h_ref, h, tm)
    h_hi = h.astype(BF16)
    h_lo = (h - h_hi.astype(F32)).astype(BF16)
    w_hi, w_lo = wr_ref[0], wr_ref[1]
    logits = (jnp.dot(h_hi, w_hi, preferred_element_type=F32)
              + jnp.dot(h_hi, w_lo, preferred_element_type=F32)
              + jnp.dot(h_lo, w_hi, preferred_element_type=F32))
    lane = lax.broadcasted_iota(jnp.int32, logits.shape, 1).astype(F32)
    neg = -jnp.inf
    lg = jnp.where(lane < N_EXPERTS, logits, neg)
    m1 = jnp.max(lg, axis=-1, keepdims=True)
    i1 = jnp.min(jnp.where(lg == m1, lane, float(LANES)), axis=-1, keepdims=True)
    lg2 = jnp.where(lane == i1, neg, lg)
    m2 = jnp.max(lg2, axis=-1, keepdims=True)
    i2 = jnp.min(jnp.where(lg2 == m2, lane, float(LANES)), axis=-1, keepdims=True)
    e2 = jnp.exp(m2 - m1)
    denom = 1.0 + e2
    w_ref[...] = jnp.where(lane == 0.0, 1.0 / denom, e2 / denom)
    first, second = lane == i1, lane == i2
    chosen = jnp.logical_or(first, second).astype(F32)
    before = carry_ref[...] + jnp.dot(stri_ref[...], chosen.astype(BF16), preferred_element_type=F32)
    rank1 = jnp.sum(jnp.where(first, before, 0.0), axis=-1, keepdims=True)
    rank2 = jnp.sum(jnp.where(second, before, 0.0), axis=-1, keepdims=True)
    idx_ref[...] = jnp.where(lane == 0.0, i1, jnp.where(lane == 1.0, i2, jnp.where(lane == 2.0, rank1, rank2))
                             ).astype(jnp.int32)
    total = carry_ref[...] + jnp.sum(chosen, axis=0, keepdims=True)
    carry_ref[...] = total
    cnt_ref[...] = jnp.broadcast_to(total, cnt_ref.shape)


def _router(oa, oc, lw, x, gate1, scale2, shift2, wr2, stri, alpha, tm):
    bsz, seq, d = x.shape
    assert d == SUBLANES * LANES and bsz * seq * TOP_K < 2 ** 24
    nj = seq // tm
    n_tiles = bsz * nj
    act = lambda i: (i // nj, i % nj, 0)
    row = lambda i: (i // nj, 0, 0)
    tok = lambda i: (i, 0)
    return pl.pallas_call(
        functools.partial(_router_kernel, tm=tm, alpha=alpha),
        grid=(n_tiles,),
        in_specs=_mixer_out_specs(oa, d, tm, act, row) + [
            pl.BlockSpec((1, 1, d), row), pl.BlockSpec((1, 1, d), row),
            _const_spec((2, d, LANES)), _const_spec((tm, tm))],
        out_specs=[pl.BlockSpec((1, tm, d), act), pl.BlockSpec((tm * SUBLANES, LANES), tok),
                   pl.BlockSpec((tm, LANES), tok), pl.BlockSpec((tm, LANES), tok),
                   pl.BlockSpec((SUBLANES, LANES), lambda i: (0, 0))],
        out_shape=[jax.ShapeDtypeStruct((bsz, seq, d), F32),
                   jax.ShapeDtypeStruct((n_tiles * tm * SUBLANES, LANES), F32),
                   jax.ShapeDtypeStruct((bsz * seq, LANES), jnp.int32),
                   jax.ShapeDtypeStruct((bsz * seq, LANES), F32),
                   jax.ShapeDtypeStruct((SUBLANES, LANES), F32)],
        scratch_shapes=[pltpu.VMEM((1, LANES), F32)],
        compiler_params=pltpu.CompilerParams(dimension_semantics=("arbitrary",),
                                             vmem_limit_bytes=VMEM_LIMIT_BYTES),
        name="router",
    )(oa, oc, lw["wo_a"], lw["wo_c"], x, gate1, lw["ln1_g"], lw["ln1_b"], scale2, shift2, wr2, stri)


_COPY_UNROLL = 8


def _tile_at(ref, t):
    return ref.at[pl.ds(pl.multiple_of(t * SUBLANES, SUBLANES), SUBLANES)]


def _issue_tile_copies(n, copy_of):
    def group(g, carry):
        copies = [copy_of(g, u) for u in range(_COPY_UNROLL)]
        for u, (src, dst, sem) in enumerate(copies):
            pltpu.make_async_copy(src, dst, sem).start(priority=u % 2)
        return carry

    assert n % _COPY_UNROLL == 0
    lax.fori_loop(0, n // _COPY_UNROLL, group, 0)


def _group_kernel(di_ref, fi_ref, src_ref, dst_hbm, zero_ref, sem, *, tb, nf):
    zero_ref[...] = jnp.zeros_like(zero_ref)
    per_group = _COPY_UNROLL // TOP_K
    _issue_tile_copies(TOP_K * tb, lambda g, u: (_tile_at(src_ref, g * per_group + u // TOP_K),
                                                 _tile_at(dst_hbm, di_ref[0, 0, g * _COPY_UNROLL + u]), sem))
    _issue_tile_copies(nf, lambda g, u: (zero_ref, _tile_at(dst_hbm, fi_ref[0, 0, g * _COPY_UNROLL + u]), sem))
    block = pl.ds(0, tb * SUBLANES)
    for _ in range(TOP_K):
        pltpu.make_async_copy(src_ref, dst_hbm.at[block], sem).wait()

    def wait_fill(r, carry):
        pltpu.make_async_copy(zero_ref, _tile_at(dst_hbm, 0), sem).wait()
        return carry

    lax.fori_loop(0, nf, wait_fill, 0)


def _group_rows(h_tiles, dest, fill_dst, n_rows, tb):
    n_asg = dest.shape[0]
    steps = n_asg // (TOP_K * tb)
    nf = fill_dst.shape[0] // steps
    assert steps * TOP_K * tb == n_asg and steps * nf == fill_dst.shape[0]
    return pl.pallas_call(
        functools.partial(_group_kernel, tb=tb, nf=nf),
        grid=(steps,),
        in_specs=[pl.BlockSpec((1, 1, TOP_K * tb), lambda i: (i, 0, 0), memory_space=pltpu.SMEM),
                  pl.BlockSpec((1, 1, nf), lambda i: (i, 0, 0), memory_space=pltpu.SMEM),
                  pl.BlockSpec((tb * SUBLANES, LANES), lambda i: (i, 0))],
        out_specs=pl.BlockSpec(memory_space=pl.ANY),
        out_shape=jax.ShapeDtypeStruct((n_rows * SUBLANES, LANES), h_tiles.dtype),
        scratch_shapes=[pltpu.VMEM((SUBLANES, LANES), h_tiles.dtype), pltpu.SemaphoreType.DMA(())],
        compiler_params=pltpu.CompilerParams(dimension_semantics=("arbitrary",), has_side_effects=True),
        name="moe_group",
    )(dest.astype(jnp.int32).reshape(steps, 1, TOP_K * tb), fill_dst.astype(jnp.int32).reshape(steps, 1, nf),
      h_tiles)


def _expert_kernel(be_ref, nu_ref, x_ref, wg_ref, wu_ref, wd_ref, o_ref, acc_ref, xb_ref, *, tm):
    i = pl.program_id(0)
    f = pl.program_id(1)
    used = i < nu_ref[0]

    @pl.when(jnp.logical_and(used, f == 0))
    def _():
        xb_ref[...] = _from_token_tiles(x_ref, tm).astype(BF16)
        acc_ref[...] = jnp.zeros_like(acc_ref)

    @pl.when(used)
    def _():
        xb = xb_ref[...]
        gt = jnp.dot(xb, wg_ref[0], preferred_element_type=F32)
        up = jnp.dot(xb, wu_ref[0], preferred_element_type=F32)
        acc_ref[...] += jnp.dot((_silu(gt) * up).astype(BF16), wd_ref[0], preferred_element_type=F32)

    last = f == pl.num_programs(1) - 1

    @pl.when(jnp.logical_and(used, last))
    def _():
        _to_token_tiles(o_ref, acc_ref[...], tm)

    @pl.when(jnp.logical_and(jnp.logical_not(used), last))
    def _():
        o_ref[...] = jnp.zeros_like(o_ref)


def _experts(x_tiles, nb, block_e, n_used, wg, wu, wd, tm, tf):
    n_exp, d, dff = wg.shape
    nf = dff // tf

    def xmap(i, f, be, nu):
        return (jnp.minimum(i, nu[0] - 1), 0)

    def fidx(i, f, nu):
        return jnp.where(i < nu[0], f, nf - 1)

    grid_spec = pltpu.PrefetchScalarGridSpec(
        num_scalar_prefetch=2,
        grid=(nb, nf),
        in_specs=[pl.BlockSpec((tm * SUBLANES, LANES), xmap),
                  pl.BlockSpec((1, d, tf), lambda i, f, be, nu: (be[i], 0, fidx(i, f, nu))),
                  pl.BlockSpec((1, d, tf), lambda i, f, be, nu: (be[i], 0, fidx(i, f, nu))),
                  pl.BlockSpec((1, tf, d), lambda i, f, be, nu: (be[i], fidx(i, f, nu), 0))],
        out_specs=pl.BlockSpec((tm * SUBLANES, LANES), lambda i, f, be, nu: (i, 0)),
        scratch_shapes=[pltpu.VMEM((tm, d), F32), pltpu.VMEM((tm, d), BF16)])
    return pl.pallas_call(
        functools.partial(_expert_kernel, tm=tm),
        grid_spec=grid_spec,
        out_shape=jax.ShapeDtypeStruct((nb * tm * SUBLANES, LANES), F32),
        compiler_params=pltpu.CompilerParams(dimension_semantics=("arbitrary", "arbitrary"),
                                             vmem_limit_bytes=VMEM_LIMIT_BYTES),
        name="moe_experts",
    )(block_e, n_used, x_tiles, wg, wu, wd)


def _combine_kernel(cur_ref, nxt_ref, y_hbm, w_ref, x_ref, gate_ref, g_ref, b_ref, o_ref, ybuf_ref, sems,
                    *, tc, alpha):
    i = pl.program_id(0)
    nt = TOP_K * tc

    def fetch(idx_ref, slot):
        buf, sem = ybuf_ref.at[slot], sems.at[slot]
        _issue_tile_copies(nt, lambda g, u: (_tile_at(y_hbm, idx_ref[0, 0, g * _COPY_UNROLL + u]),
                                             _tile_at(buf, g * _COPY_UNROLL + u), sem))

    @pl.when(i == 0)
    def _():
        fetch(cur_ref, 0)

    @pl.when(i + 1 < pl.num_programs(0))
    def _():
        fetch(nxt_ref, (i + 1) % 2)

    slot = i % 2
    yb = ybuf_ref.at[slot]
    pltpu.make_async_copy(y_hbm.at[pl.ds(0, nt * SUBLANES)], yb, sems.at[slot]).wait()
    w = w_ref[...]
    y = (w[:, 0:1] * _from_token_tiles(yb, tc, 0, TOP_K)
         + w[:, 1:2] * _from_token_tiles(yb, tc, 1, TOP_K))
    r = alpha * x_ref[0] + (1.0 + gate_ref[0]) * y
    o_ref[0] = _layer_norm_rows(r, g_ref[...], b_ref[...])


def _combine(y_tiles, dest, top_w, x, gate2, ln_g, ln_b, alpha, tc):
    bsz, seq, d = x.shape
    nj = seq // tc
    steps = bsz * nj
    nt = TOP_K * tc
    dest_blocks = dest.astype(jnp.int32).reshape(steps, 1, nt)
    return pl.pallas_call(
        functools.partial(_combine_kernel, tc=tc, alpha=alpha),
        grid=(steps,),
        in_specs=[pl.BlockSpec((1, 1, nt), lambda i: (i, 0, 0), memory_space=pltpu.SMEM),
                  pl.BlockSpec((1, 1, nt), lambda i: (jnp.minimum(i + 1, steps - 1), 0, 0),
                               memory_space=pltpu.SMEM),
                  pl.BlockSpec(memory_space=pl.ANY),
                  pl.BlockSpec((tc, LANES), lambda i: (i, 0)),
                  pl.BlockSpec((1, tc, d), lambda i: (i // nj, i % nj, 0)),
                  pl.BlockSpec((1, 1, d), lambda i: (i // nj, 0, 0)), _const_spec((1, d)), _const_spec((1, d))],
        out_specs=pl.BlockSpec((1, tc, d), lambda i: (i // nj, i % nj, 0)),
        out_shape=jax.ShapeDtypeStruct((bsz, seq, d), F32),
        scratch_shapes=[pltpu.VMEM((2, nt * SUBLANES, LANES), F32), pltpu.SemaphoreType.DMA((2,))],
        compiler_params=pltpu.CompilerParams(dimension_semantics=("arbitrary",),
                                             vmem_limit_bytes=VMEM_LIMIT_BYTES),
        name="moe_combine",
    )(dest_blocks, dest_blocks, y_tiles, top_w, x, gate2, ln_g, ln_b)


def _moe(oa, oc, lw, x, gate1, scale2, shift2, gate2, wr2, stri, wg, wu, wd, ln_g, ln_b, alpha,
         tm_tok, tm, tf, tc):
    bsz, seq, d = x.shape
    n_tok = bsz * seq
    n_asg = n_tok * TOP_K
    x1, h_tiles, idx128, w128, cnt = _router(oa, oc, lw, x, gate1, scale2, shift2, wr2, stri, alpha, tm_tok)
    counts = cnt[0, :N_EXPERTS].astype(jnp.int32)
    padded = (counts + tm - 1) // tm * tm
    pad_end = jnp.cumsum(padded)
    pad_start = pad_end - padded
    e_ids, ranks = idx128[:, 0:TOP_K], idx128[:, TOP_K:2 * TOP_K]
    start_of = jnp.sum(jnp.where(e_ids[:, :, None] == jnp.arange(N_EXPERTS, dtype=jnp.int32), pad_start, 0), axis=-1)
    dest = (start_of + ranks).reshape(n_asg)
    nb = n_asg // tm + N_EXPERTS
    n_rows = nb * tm
    n_fill = n_rows - n_asg
    fill_end = jnp.cumsum(padded - counts)
    slot = jnp.arange(n_fill, dtype=jnp.int32)
    slot_e = jnp.sum((slot[:, None] >= fill_end[None, :]).astype(jnp.int32), axis=1)
    in_group = slot_e < N_EXPERTS
    ge = jnp.minimum(slot_e, N_EXPERTS - 1)
    fill_dst = jnp.where(in_group, (pad_start + counts)[ge] + slot - (fill_end - (padded - counts))[ge],
                         pad_end[-1] + slot - fill_end[-1])
    x_tiles = _group_rows(h_tiles, dest, fill_dst, n_rows, tm_tok)
    block_row = jnp.arange(nb, dtype=jnp.int32) * tm
    block_e = jnp.minimum(jnp.sum((block_row[:, None] >= pad_end[None, :]).astype(jnp.int32), axis=1),
                          N_EXPERTS - 1)
    n_used = (pad_end[-1:] // tm).astype(jnp.int32)
    y_tiles = _experts(x_tiles, nb, block_e, n_used, wg, wu, wd, tm, tf)
    return _combine(y_tiles, dest, w128, x1, gate2, ln_g, ln_b, alpha, tc)


def _layer_weights(layer, w_in, b_fgate, w_conv, q_norm_g, kv_norm_g, w_uq, w_ukv, head_norm_g, w_o,
                   ln1_g, ln1_b):
    d = w_in.shape[1]
    sizes = [FOX_W, FOX_W, FOX_W, H_FOX, CONV_W, CONV_W, CONV_W, Q_LORA, KV_LORA, QK_ROPE]
    offs = np.concatenate([[0], np.cumsum(sizes)])
    wi = w_in[layer]
    fq, fk, fv, fl, bg, cg, hin, cq, ckv, kr = [wi[:, offs[i]:offs[i + 1]] for i in range(len(sizes))]
    half = QK_ROPE // 2
    z64 = jnp.zeros((d, QK_NOPE), F32)
    z32 = jnp.zeros((d, LANES - QK_NOPE - QK_ROPE), F32)
    wa = jnp.concatenate([
        fq, fk, fv,
        jnp.pad(jnp.repeat(fl, _N_SPLIT, axis=1), ((0, 0), (0, LANES - _N_SPLIT * H_FOX))), bg, cg, hin, cq, ckv,
        jnp.concatenate([z64, kr, z32], axis=1),
        jnp.concatenate([z64, kr[:, half:], kr[:, :half], z32], axis=1)], axis=1)
    assert wa.shape[1] == _NA
    q3 = w_uq[layer].reshape(Q_LORA, H_MLA, QK_NOPE + QK_ROPE)
    q_main = jnp.pad(q3, ((0, 0), (0, 0), (0, LANES - QK_NOPE - QK_ROPE)))
    q_swap = jnp.concatenate([jnp.zeros((Q_LORA, H_MLA, QK_NOPE), F32), q3[:, :, QK_NOPE + half:],
                              q3[:, :, QK_NOPE:QK_NOPE + half],
                              jnp.zeros((Q_LORA, H_MLA, LANES - QK_NOPE - QK_ROPE), F32)], axis=2)
    wuq = jnp.concatenate([q_main.reshape(Q_LORA, H_MLA * LANES), q_swap.reshape(Q_LORA, H_MLA * LANES)],
                          axis=1).astype(BF16)
    kv3 = w_ukv[layer].reshape(KV_LORA, H_MLA, QK_NOPE + V_DIM)
    k_nope = jnp.pad(kv3[:, :, :QK_NOPE], ((0, 0), (0, 0), (0, LANES - QK_NOPE))).reshape(KV_LORA, H_MLA * LANES)
    wukv = jnp.concatenate([k_nope, kv3[:, :, QK_NOPE:].reshape(KV_LORA, MLA_W)], axis=1).astype(BF16)
    hg = head_norm_g[layer]
    wo = w_o[layer]
    c0, c1 = FOX_W, FOX_W + CONV_W
    return {
        "wa": wa, "wuq": wuq, "wukv": wukv,
        "qg": q_norm_g[layer].reshape(1, Q_LORA), "kvg": kv_norm_g[layer].reshape(1, KV_LORA),
        "bf": jnp.pad(jnp.repeat(b_fgate[layer], _N_SPLIT), (0, LANES - _N_SPLIT * H_FOX)).reshape(1, LANES),
        "wconv": jnp.pad(w_conv[layer], ((0, SUBLANES - CONV_K), (0, 0))),
        "hg_conv": hg[c0:c1].reshape(1, CONV_W),
        "hg_attn": jnp.concatenate([hg[:c0], hg[c1:]]).reshape(1, FOX_W + MLA_W),
        "wo_a": jnp.concatenate([wo[:c0], wo[c1:]], axis=0).astype(BF16),
        "wo_c": wo[c0:c1].astype(BF16),
        "ln1_g": ln1_g[layer].reshape(1, d), "ln1_b": ln1_b[layer].reshape(1, d),
    }


def _constants(seq, ts):
    pos = jnp.arange(seq, dtype=F32)
    inv_freq = ROPE_THETA ** (-jnp.arange(0, QK_ROPE, 2, dtype=F32) / QK_ROPE)
    ang = pos[:, None] * inv_freq[None, :]
    cos, sin = jnp.cos(ang), jnp.sin(ang)
    pad_r = jnp.zeros((seq, LANES - QK_NOPE - QK_ROPE), F32)
    ctab = jnp.concatenate([jnp.ones((seq, QK_NOPE), F32), cos, cos, pad_r], axis=1)
    stab = jnp.concatenate([jnp.zeros((seq, QK_NOPE), F32), -sin, sin, pad_r], axis=1)
    q_scale = (QK_NOPE + QK_ROPE) ** -0.5 * LOG2E
    lanes = np.arange(LANES)
    lm = np.zeros((SUBLANES, LANES), np.float32)
    for part in range(_N_SPLIT):
        lm[part] = (lanes % _N_SPLIT == part) & (lanes < _N_SPLIT * H_FOX)
    for row, base in ((3, HEAD_DIM), (5, 0)):
        lm[row] = (lanes >= base) & (lanes < base + _N_SPLIT)
        lm[row + 1] = (lanes >= base + _N_SPLIT) & (lanes < base + 2 * _N_SPLIT)

    def group_mean(n):
        gidx = np.arange(n) // HEAD_DIM
        return (gidx[:, None] == gidx[None, :]).astype(np.float32) / HEAD_DIM

    return {
        "cq": ctab * q_scale, "sq": stab * q_scale, "ck": ctab, "sk": stab,
        "tri": jnp.asarray(np.tril(np.ones((ts, ts), np.float32)), BF16),
        "stri": jnp.asarray(np.tril(np.ones((ts, ts), np.float32), k=-1), BF16),
        "lane_masks": jnp.asarray(lm),
        "gm256": jnp.asarray(group_mean(CONV_W), BF16), "gm128": jnp.asarray(group_mean(LANES), BF16),
    }


def _tile(n, pref):
    t = min(n, pref)
    assert n % t == 0, (n, pref)
    return t


def kernel(x, c, w_mod, b_mod, w_in, b_fgate, w_conv, q_norm_g, kv_norm_g, w_uq, w_ukv, head_norm_g, w_o, ln1_g, ln1_b, ln2_g, ln2_b, ffn_w_gate, ffn_w_up, ffn_w_down, router_w, exp_w_gate, exp_w_up, exp_w_down):
    bsz, seq, d = x.shape
    depth = w_mod.shape[0]
    assert d == D_MODEL and bsz <= SUBLANES
    alpha = (2 * depth) ** 0.25
    ts = _tile(seq, 512)
    tq = _tile(seq, 512)
    tm_e = 512
    tf_e = 512
    tc = _tile(seq, 256)
    assert (bsz * seq * TOP_K) % tm_e == 0 and exp_w_gate.shape[-1] % tf_e == 0

    consts = _constants(seq, ts)
    c_pad = jnp.pad(c, ((0, SUBLANES - bsz), (0, 0)))
    mod = _modulation(c_pad, w_mod, b_mod)[:, :bsz, :]

    for layer in range(depth):
        m6 = mod[layer].reshape(bsz, 6, 1, d)
        shift1, scale1, gate1, shift2, scale2, gate2 = [m6[:, i] for i in range(6)]
        lw = _layer_weights(layer, w_in, b_fgate, w_conv, q_norm_g, kv_norm_g, w_uq, w_ukv, head_norm_g,
                            w_o, ln1_g, ln1_b)
        qp, kp, vt, oc = _inproj(x, scale1, shift1, lw, consts, ts)
        oa = _attention(qp, kp, vt, lw["hg_attn"], consts["gm128"], tq, 6)
        j = layer // 2
        g2, b2 = ln2_g[layer].reshape(1, d), ln2_b[layer].reshape(1, d)
        if layer % 2 == 0:
            x = _ffn_dense(oa, oc, lw, x, gate1, scale2, shift2, gate2, ffn_w_gate[j].astype(BF16),
                           ffn_w_up[j].astype(BF16), ffn_w_down[j].astype(BF16), g2, b2, alpha, ts)
        else:
            wr = jnp.pad(router_w[j], ((0, 0), (0, LANES - N_EXPERTS)))
            wr_hi = wr.astype(BF16)
            wr_lo = (wr - wr_hi.astype(F32)).astype(BF16)
            x = _moe(oa, oc, lw, x, gate1, scale2, shift2, gate2, jnp.stack([wr_hi, wr_lo]), consts["stri"],
                     exp_w_gate[j].astype(BF16), exp_w_up[j].astype(BF16), exp_w_down[j].astype(BF16),
                     g2, b2, alpha, ts, tm_e, tf_e, tc)
    return x
```

```python
import functools

import numpy as np
import jax
import jax.numpy as jnp
from jax import lax
from jax.experimental import pallas as pl
from jax.experimental.pallas import tpu as pltpu

F32 = jnp.float32
BF16 = jnp.bfloat16

D_MODEL = 1024
HEAD_DIM = 64
H_FOX = 6
FOX_W = H_FOX * HEAD_DIM
CONV_W = 256
CONV_K = 3
H_MLA = 6
Q_LORA = 256
KV_LORA = 256
QK_NOPE = 64
QK_ROPE = 32
V_DIM = 64
MLA_W = H_MLA * V_DIM
N_HEADS = H_FOX + H_MLA
ROPE_THETA = 10000.0
N_EXPERTS = 8
TOP_K = 2
LN_EPS = 1e-5
RMS_EPS = 1e-6

LANES = 128
SUBLANES = 8
VMEM_LIMIT_BYTES = 56 * 1024 * 1024

_HW = 6 * LANES
_QF0, _KF0, _VF0, _FL0 = 0, FOX_W, 2 * FOX_W, 3 * FOX_W
_BG0 = _FL0 + LANES
_CG0, _HI0, _CQ0, _CKV0 = _BG0 + CONV_W, _BG0 + 2 * CONV_W, _BG0 + 3 * CONV_W, _BG0 + 3 * CONV_W + Q_LORA
_KR0 = _CKV0 + KV_LORA
_KRS0 = _KR0 + LANES
_NA = _KRS0 + LANES
LOG2E = 1.4426950408889634
VT_ROWS = LANES
_N_SPLIT = 3


def _const_spec(shape):
    zeros = (0,) * len(shape)
    return pl.BlockSpec(shape, lambda *_: zeros, pipeline_mode=pl.Buffered(1))


def _silu(v):
    return v * (1.0 / (1.0 + jnp.exp(-v)))


def _split3(v):
    hi = v.astype(BF16)
    r1 = v - hi.astype(F32)
    mid = r1.astype(BF16)
    lo = (r1 - mid.astype(F32)).astype(BF16)
    return hi, mid, lo


def _group_mean_sq(v, gmat_ref):
    sq = v * v
    hi = sq.astype(BF16)
    lo = (sq - hi.astype(F32)).astype(BF16)
    g = gmat_ref[...]
    return (jnp.dot(hi, g, preferred_element_type=F32) + jnp.dot(lo, g, preferred_element_type=F32))


def _layer_norm_rows(r, g, b):
    mu = jnp.mean(r, axis=-1, keepdims=True)
    rc = r - mu
    var = jnp.mean(rc * rc, axis=-1, keepdims=True)
    return rc * lax.rsqrt(var + LN_EPS) * g + b


def _mod_kernel(c_ref, w_ref, b_ref, o_ref):
    act = _silu(c_ref[...]).astype(BF16)
    o_ref[0] = jnp.dot(act, w_ref[0].astype(BF16), preferred_element_type=F32) + b_ref[0]


def _modulation(c_pad, w_mod, b_mod):
    depth, d, n = w_mod.shape
    tn = 1024
    return pl.pallas_call(
        _mod_kernel,
        grid=(depth, n // tn),
        in_specs=[pl.BlockSpec((SUBLANES, d), lambda l, j: (0, 0)),
                  pl.BlockSpec((1, d, tn), lambda l, j: (l, 0, j)),
                  pl.BlockSpec((1, 1, tn), lambda l, j: (l, 0, j))],
        out_specs=pl.BlockSpec((1, SUBLANES, tn), lambda l, j: (l, 0, j)),
        out_shape=jax.ShapeDtypeStruct((depth, SUBLANES, n), F32),
        compiler_params=pltpu.CompilerParams(dimension_semantics=("arbitrary", "arbitrary")),
        name="modulation",
    )(c_pad, w_mod, b_mod.reshape(depth, 1, n))


def _inproj_kernel(x_ref, sc_ref, sh_ref, wa_ref, cq_ref, sq_ref, ck_ref, sk_ref, wuq_ref, wukv_ref,
                   qg_ref, kvg_ref, bf_ref, wconv_ref, hgc_ref, tri_ref, lm_ref, gm_ref,
                   qp_ref, kp_ref, vt_ref, oc_ref, fcarry_ref, ubuf_ref, wab_ref, *, ts):
    @pl.when(jnp.logical_and(pl.program_id(0) == 0, pl.program_id(1) == 0))
    def _():
        for c in range(_NA // LANES):
            sl = slice(c * LANES, (c + 1) * LANES)
            wab_ref[:, sl] = wa_ref[:, sl].astype(BF16)

    @pl.when(pl.program_id(1) == 0)
    def _():
        fcarry_ref[...] = jnp.zeros_like(fcarry_ref)
        ubuf_ref[pl.ds(0, SUBLANES), :] = jnp.zeros((SUBLANES, CONV_W), F32)

    hb = (x_ref[0] * (1.0 + sc_ref[0]) + sh_ref[0]).astype(BF16)

    def proj(lo, hi):
        return jnp.dot(hb, wab_ref[:, lo:hi], preferred_element_type=F32)

    def put_values_t(head, v_half_t):
        vt_ref[0, head, 0:V_DIM, :] = v_half_t.astype(BF16)
        vt_ref[0, head, V_DIM:VT_ROWS, :] = jnp.ones((VT_ROWS - V_DIM, ts), BF16)

    z_fl = proj(_FL0, _FL0 + LANES)
    z_cq = proj(_CQ0, _CQ0 + Q_LORA)
    z_ckv = proj(_CKV0, _CKV0 + KV_LORA)
    z_cg = proj(_CG0, _CG0 + CONV_W)
    z_hi = proj(_HI0, _HI0 + CONV_W)
    z_bg = proj(_BG0, _BG0 + CONV_W)
    zq = proj(_QF0, _QF0 + FOX_W) * (HEAD_DIM ** -0.5 * LOG2E)
    zk = proj(_KF0, _KF0 + FOX_W)
    zv = proj(_VF0, _VF0 + FOX_W)
    z_kr = proj(_KR0, _KR0 + LANES)
    z_krs = proj(_KRS0, _KRS0 + LANES)

    a = z_fl + bf_ref[...]
    logf = jnp.minimum(a, 0.0) - jnp.log1p(jnp.exp(-jnp.abs(a)))
    tri = tri_ref[...]
    csum = sum(jnp.dot(tri, part, preferred_element_type=F32) for part in _split3(logf))
    fcum = fcarry_ref[...] + csum
    fcarry_ref[...] = fcum[ts - 1:ts, :]
    f_hi, f_mid, f_lo = _split3(fcum * LOG2E)
    lm = lm_ref[...]
    fparts = (f_hi.astype(F32) * lm[0:1, :] + f_mid.astype(F32) * lm[1:2, :]
              + f_lo.astype(F32) * lm[2:3, :])
    lane = lax.broadcasted_iota(jnp.int32, (ts, LANES), 1)
    low_half = lane < HEAD_DIM
    for h in range(H_FOX):
        blk = slice((h // 2) * LANES, (h // 2 + 1) * LANES)
        if h % 2 == 0:
            base, f_mask, one_mask, keep = HEAD_DIM, lm[3:4, :], lm[4:5, :], low_half
        else:
            base, f_mask, one_mask, keep = 0, lm[5:6, :], lm[6:7, :], jnp.logical_not(low_half)
        f_at_q = pltpu.roll(fparts, (base - _N_SPLIT * h) % LANES, axis=1)
        f_at_k = pltpu.roll(fparts, (base + _N_SPLIT - _N_SPLIT * h) % LANES, axis=1)
        qp_ref[0, h] = jnp.where(keep, zq[:, blk], f_at_q * f_mask + one_mask).astype(BF16)
        kp_ref[0, h] = jnp.where(keep, zk[:, blk], f_mask - f_at_k * one_mask).astype(BF16)
    for j in range(H_FOX // 2):
        vt = zv[:, j * LANES:(j + 1) * LANES].T
        put_values_t(2 * j, vt[0:V_DIM])
        put_values_t(2 * j + 1, vt[V_DIM:LANES])

    u = z_cg * z_hi
    ubuf_ref[pl.ds(SUBLANES, ts), :] = u
    u1 = ubuf_ref[pl.ds(SUBLANES - 1, ts), :]
    u2 = ubuf_ref[pl.ds(SUBLANES - 2, ts), :]
    ubuf_ref[pl.ds(0, SUBLANES), :] = u[ts - SUBLANES:ts, :]
    wc = wconv_ref[...]
    oc = z_bg * (wc[0:1, :] * u2 + wc[1:2, :] * u1 + wc[2:3, :] * u)
    ocn = oc * lax.rsqrt(_group_mean_sq(oc, gm_ref) + RMS_EPS) * hgc_ref[...]
    oc_ref[0] = ocn.astype(BF16)

    def rms(v, g):
        return (v * lax.rsqrt(jnp.mean(v * v, axis=-1, keepdims=True) + RMS_EPS) * g).astype(BF16)

    cqn = rms(z_cq, qg_ref[...])
    qm = jnp.dot(cqn, wuq_ref[:, 0:_HW], preferred_element_type=F32)
    qs = jnp.dot(cqn, wuq_ref[:, _HW:2 * _HW], preferred_element_type=F32)
    kvn = rms(z_ckv, kvg_ref[...])
    kn = jnp.dot(kvn, wukv_ref[:, 0:_HW], preferred_element_type=F32)
    vm = jnp.dot(kvn, wukv_ref[:, _HW:_HW + MLA_W], preferred_element_type=F32)
    krr = z_kr * ck_ref[...] + z_krs * sk_ref[...]
    cq, sq = cq_ref[...], sq_ref[...]
    for h in range(H_MLA):
        sl = slice(h * LANES, (h + 1) * LANES)
        qp_ref[0, H_FOX + h] = (qm[:, sl] * cq + qs[:, sl] * sq).astype(BF16)
        kp_ref[0, H_FOX + h] = (kn[:, sl] + krr).astype(BF16)
    for j in range(H_MLA // 2):
        vt = vm[:, j * LANES:(j + 1) * LANES].T
        put_values_t(H_FOX + 2 * j, vt[0:V_DIM])
        put_values_t(H_FOX + 2 * j + 1, vt[V_DIM:LANES])


def _inproj(x, scale1, shift1, lw, consts, ts):
    bsz, seq, d = x.shape
    kern = functools.partial(_inproj_kernel, ts=ts)
    row = lambda b, j: (b, 0, 0)
    tab = pl.BlockSpec((ts, LANES), lambda b, j: (j, 0))
    return pl.pallas_call(
        kern,
        grid=(bsz, seq // ts),
        in_specs=[pl.BlockSpec((1, ts, d), lambda b, j: (b, j, 0)),
                  pl.BlockSpec((1, 1, d), row), pl.BlockSpec((1, 1, d), row),
                  _const_spec((d, _NA)), tab, tab, tab, tab,
                  _const_spec((Q_LORA, 2 * _HW)), _const_spec((KV_LORA, _HW + MLA_W)),
                  _const_spec((1, Q_LORA)), _const_spec((1, KV_LORA)), _const_spec((1, LANES)),
                  _const_spec((SUBLANES, CONV_W)), _const_spec((1, CONV_W)),
                  _const_spec((ts, ts)), _const_spec((SUBLANES, LANES)), _const_spec((CONV_W, CONV_W))],
        out_specs=[pl.BlockSpec((1, N_HEADS, ts, LANES), lambda b, j: (b, 0, j, 0)),
                   pl.BlockSpec((1, N_HEADS, ts, LANES), lambda b, j: (b, 0, j, 0)),
                   pl.BlockSpec((1, N_HEADS, VT_ROWS, ts), lambda b, j: (b, 0, 0, j)),
                   pl.BlockSpec((1, ts, CONV_W), lambda b, j: (b, j, 0))],
        out_shape=[jax.ShapeDtypeStruct((bsz, N_HEADS, seq, LANES), BF16),
                   jax.ShapeDtypeStruct((bsz, N_HEADS, seq, LANES), BF16),
                   jax.ShapeDtypeStruct((bsz, N_HEADS, VT_ROWS, seq), BF16),
                   jax.ShapeDtypeStruct((bsz, seq, CONV_W), BF16)],
        scratch_shapes=[pltpu.VMEM((1, LANES), F32), pltpu.VMEM((ts + SUBLANES, CONV_W), F32),
                        pltpu.VMEM((d, _NA), BF16)],
        compiler_params=pltpu.CompilerParams(dimension_semantics=("arbitrary", "arbitrary"),
                                             vmem_limit_bytes=VMEM_LIMIT_BYTES),
        name="inproj",
    )(x, scale1, shift1, lw["wa"], consts["cq"], consts["sq"], consts["ck"], consts["sk"],
      lw["wuq"], lw["wukv"], lw["qg"], lw["kvg"], lw["bf"], lw["wconv"], lw["hg_conv"],
      consts["tri"], consts["lane_masks"], consts["gm256"])


def _attn_kernel(qi_ref, kj_ref, q_ref, k_ref, vt_ref, g_ref, gm_ref, o_ref, m_ref, acc_ref, s_ref, *, nh):
    step = pl.program_id(2)
    qi = qi_ref[step]
    kj = kj_ref[step]

    @pl.when(kj == 0)
    def _():
        m_ref[...] = jnp.full(m_ref.shape, -jnp.inf, F32)
        acc_ref[...] = jnp.zeros_like(acc_ref)

    def update(masked):
        if masked:
            tile = (q_ref.shape[2], q_ref.shape[2])
            causal = lax.broadcasted_iota(jnp.int32, tile, 0) <= lax.broadcasted_iota(jnp.int32, tile, 1)

        def scores(h):
            st = lax.dot_general(k_ref[0, h], q_ref[0, h], (((1,), (1,)), ((), ())),
                                 preferred_element_type=F32)
            if masked:
                st = jnp.where(causal, st, -jnp.inf)
            s_ref[h] = st

        for h in range(nh):
            scores(h)
        pts, alphas = [], []
        for h in range(nh):
            m_prev = m_ref[h]
            m_new = jnp.maximum(m_prev, jnp.max(s_ref[h], axis=0, keepdims=True))
            pts.append(jnp.exp2(s_ref[h] - m_new).astype(BF16))
            alphas.append(jnp.exp2(m_prev - m_new))
            m_ref[h] = m_new
        for h in range(nh):
            acc_ref[h] = alphas[h] * acc_ref[h] + jnp.dot(vt_ref[0, h], pts[h], preferred_element_type=F32)

    @pl.when(kj < qi)
    def _():
        update(False)

    @pl.when(kj == qi)
    def _():
        update(True)
        for pair in range(nh // 2):
            ot = jnp.concatenate([acc_ref[h, 0:V_DIM, :] / acc_ref[h, V_DIM:2 * V_DIM, :]
                                  for h in (2 * pair, 2 * pair + 1)], axis=0)
            o = ot.T
            sl = slice(pair * LANES, (pair + 1) * LANES)
            on = o * lax.rsqrt(_group_mean_sq(o, gm_ref) + RMS_EPS) * g_ref[:, sl]
            o_ref[0, :, sl] = on.astype(BF16)


def _attention(qp, kp, vt, g_attn, gm128, tq, nh):
    bsz, n_heads, seq, _ = qp.shape
    nq = seq // tq
    ow = (nh // 2) * LANES
    qi_tab = np.concatenate([np.full(i + 1, i) for i in range(nq)]).astype(np.int32)
    kj_tab = np.concatenate([np.arange(i + 1) for i in range(nq)]).astype(np.int32)
    grid_spec = pltpu.PrefetchScalarGridSpec(
        num_scalar_prefetch=2,
        grid=(bsz, n_heads // nh, len(qi_tab)),
        in_specs=[pl.BlockSpec((1, nh, tq, LANES), lambda b, p, s, qi, kj: (b, p, qi[s], 0)),
                  pl.BlockSpec((1, nh, tq, LANES), lambda b, p, s, qi, kj: (b, p, kj[s], 0)),
                  pl.BlockSpec((1, nh, VT_ROWS, tq), lambda b, p, s, qi, kj: (b, p, 0, kj[s])),
                  pl.BlockSpec((1, ow), lambda b, p, s, qi, kj: (0, p)),
                  _const_spec((LANES, LANES))],
        out_specs=pl.BlockSpec((1, tq, ow), lambda b, p, s, qi, kj: (b, qi[s], p)),
        scratch_shapes=[pltpu.VMEM((nh, 1, tq), F32), pltpu.VMEM((nh, VT_ROWS, tq), F32),
                        pltpu.VMEM((nh, tq, tq), F32)])
    return pl.pallas_call(
        functools.partial(_attn_kernel, nh=nh),
        grid_spec=grid_spec,
        out_shape=jax.ShapeDtypeStruct((bsz, seq, (n_heads // 2) * LANES), BF16),
        compiler_params=pltpu.CompilerParams(
            dimension_semantics=("arbitrary", "arbitrary", "arbitrary"),
            vmem_limit_bytes=VMEM_LIMIT_BYTES),
        name="attention",
    )(jnp.asarray(qi_tab), jnp.asarray(kj_tab), qp, kp, vt, g_attn, gm128)


def _mixer_out_rows(oa_ref, oc_ref, woa_ref, woc_ref, x_ref, gate_ref, g_ref, b_ref, alpha):
    y = (jnp.dot(oa_ref[0], woa_ref[...], preferred_element_type=F32)
         + jnp.dot(oc_ref[0], woc_ref[...], preferred_element_type=F32))
    r = alpha * x_ref[0] + (1.0 + gate_ref[0]) * y
    return _layer_norm_rows(r, g_ref[...], b_ref[...])


def _mixer_out_specs(oa, d, tm, act, row):
    wa_rows = oa.shape[-1]
    return [pl.BlockSpec((1, tm, wa_rows), act), pl.BlockSpec((1, tm, CONV_W), act),
            _const_spec((wa_rows, d)), _const_spec((CONV_W, d)),
            pl.BlockSpec((1, tm, d), act), pl.BlockSpec((1, 1, d), row), _const_spec((1, d)), _const_spec((1, d))]


def _outproj_kernel(oa_ref, oc_ref, woa_ref, woc_ref, x_ref, gate_ref, g_ref, b_ref, o_ref, *, alpha):
    o_ref[0] = _mixer_out_rows(oa_ref, oc_ref, woa_ref, woc_ref, x_ref, gate_ref, g_ref, b_ref, alpha)


def _outproj(oa, oc, lw, x, gate1, alpha, tm):
    bsz, seq, d = x.shape
    act = lambda b, j: (b, j, 0)
    return pl.pallas_call(
        functools.partial(_outproj_kernel, alpha=alpha),
        grid=(bsz, seq // tm),
        in_specs=_mixer_out_specs(oa, d, tm, act, lambda b, j: (b, 0, 0)),
        out_specs=pl.BlockSpec((1, tm, d), act),
        out_shape=jax.ShapeDtypeStruct((bsz, seq, d), F32),
        compiler_params=pltpu.CompilerParams(dimension_semantics=("arbitrary", "arbitrary"),
                                             vmem_limit_bytes=VMEM_LIMIT_BYTES),
        name="outproj",
    )(oa, oc, lw["wo_a"], lw["wo_c"], x, gate1, lw["ln1_g"], lw["ln1_b"])


def _ffn_kernel(oa_ref, oc_ref, woa_ref, woc_ref, xin_ref, gate1_ref, g1_ref, b1_ref,
                sc_ref, sh_ref, gate_ref, wg_ref, wu_ref, wd_ref, g_ref, b_ref, o_ref, *, alpha, tf):
    x = _mixer_out_rows(oa_ref, oc_ref, woa_ref, woc_ref, xin_ref, gate1_ref, g1_ref, b1_ref, alpha)
    hb = (x * (1.0 + sc_ref[0]) + sh_ref[0]).astype(BF16)
    chunks = [slice(c * tf, (c + 1) * tf) for c in range(wg_ref.shape[1] // tf)]
    acts = []
    for sl in chunks:
        gt = jnp.dot(hb, wg_ref[:, sl], preferred_element_type=F32)
        up = jnp.dot(hb, wu_ref[:, sl], preferred_element_type=F32)
        acts.append((_silu(gt) * up).astype(BF16))
    acc = sum(jnp.dot(act, wd_ref[sl, :], preferred_element_type=F32) for act, sl in zip(acts, chunks))
    r = alpha * x + (1.0 + gate_ref[0]) * acc
    o_ref[0] = _layer_norm_rows(r, g_ref[...], b_ref[...])


def _ffn_dense(oa, oc, lw, x, gate1, scale2, shift2, gate2, wg, wu, wd, ln_g, ln_b, alpha, tm):
    bsz, seq, d = x.shape
    dff = wg.shape[1]
    tf = dff // 2 if (dff // 2) % LANES == 0 else dff
    nj = seq // tm
    act = lambda b, j: (b, j, 0)
    row = lambda b, j: (b, 0, 0)
    return pl.pallas_call(
        functools.partial(_ffn_kernel, alpha=alpha, tf=tf),
        grid=(bsz, nj),
        in_specs=_mixer_out_specs(oa, d, tm, act, row) + [
            pl.BlockSpec((1, 1, d), row), pl.BlockSpec((1, 1, d), row), pl.BlockSpec((1, 1, d), row),
            _const_spec((d, dff)), _const_spec((d, dff)), _const_spec((dff, d)),
            _const_spec((1, d)), _const_spec((1, d))],
        out_specs=pl.BlockSpec((1, tm, d), act),
        out_shape=jax.ShapeDtypeStruct((bsz, seq, d), F32),
        compiler_params=pltpu.CompilerParams(dimension_semantics=("arbitrary", "arbitrary"),
                                             vmem_limit_bytes=VMEM_LIMIT_BYTES),
        name="ffn_dense",
    )(oa, oc, lw["wo_a"], lw["wo_c"], x, gate1, lw["ln1_g"], lw["ln1_b"],
      scale2, shift2, gate2, wg, wu, wd, ln_g, ln_b)


def _to_token_tiles(ref, v, n):
    for c in range(v.shape[1] // LANES):
        ref[pl.ds(c, n, stride=SUBLANES), :] = v[:, c * LANES:(c + 1) * LANES]


def _from_token_tiles(ref, n, first=0, tiles_per_row=1):
    stride = tiles_per_row * SUBLANES
    return jnp.concatenate([ref[pl.ds(first * SUBLANES + c, n, stride=stride), :] for c in range(SUBLANES)],
                           axis=1)


def _router_kernel(x_ref, sc_ref, sh_ref, wr_ref, stri_ref, h_ref, idx_ref, w_ref, cnt_ref, carry_ref, *, tm):
    @pl.when(pl.program_id(0) == 0)
    def _():
        carry_ref[...] = jnp.zeros_like(carry_ref)

    h = x_ref[0] * (1.0 + sc_ref[0]) + sh_ref[0]
    _to_token_tiles(h_ref, h, tm)
    h_hi = h.astype(BF16)
    h_lo = (h - h_hi.astype(F32)).astype(BF16)
    w_hi, w_lo = wr_ref[0], wr_ref[1]
    logits = (jnp.dot(h_hi, w_hi, preferred_element_type=F32)
              + jnp.dot(h_hi, w_lo, preferred_element_type=F32)
              + jnp.dot(h_lo, w_hi, preferred_element_type=F32))
    lane = lax.broadcasted_iota(jnp.int32, logits.shape, 1).astype(F32)
    neg = -jnp.inf
    lg = jnp.where(lane < N_EXPERTS, logits, neg)
    m1 = jnp.max(lg, axis=-1, keepdims=True)
    i1 = jnp.min(jnp.where(lg == m1, lane, float(LANES)), axis=-1, keepdims=True)
    lg2 = jnp.where(lane == i1, neg, lg)
    m2 = jnp.max(lg2, axis=-1, keepdims=True)
    i2 = jnp.min(jnp.where(lg2 == m2, lane, float(LANES)), axis=-1, keepdims=True)
    e2 = jnp.exp(m2 - m1)
    denom = 1.0 + e2
    w_ref[...] = jnp.where(lane == 0.0, 1.0 / denom, e2 / denom)
    first, second = lane == i1, lane == i2
    chosen = jnp.logical_or(first, second).astype(F32)
    before = carry_ref[...] + jnp.dot(stri_ref[...], chosen.astype(BF16), preferred_element_type=F32)
    rank1 = jnp.sum(jnp.where(first, before, 0.0), axis=-1, keepdims=True)
    rank2 = jnp.sum(jnp.where(second, before, 0.0), axis=-1, keepdims=True)
    idx_ref[...] = jnp.where(lane == 0.0, i1, jnp.where(lane == 1.0, i2, jnp.where(lane == 2.0, rank1, rank2))
                             ).astype(jnp.int32)
    total = carry_ref[...] + jnp.sum(chosen, axis=0, keepdims=True)
    carry_ref[...] = total
    cnt_ref[...] = jnp.broadcast_to(total, cnt_ref.shape)


def _router(x, scale2, shift2, wr2, stri, tm):
    bsz, seq, d = x.shape
    assert d == SUBLANES * LANES and bsz * seq * TOP_K < 2 ** 24
    nj = seq // tm
    n_tiles = bsz * nj
    row = lambda i: (i // nj, 0, 0)
    tok = lambda i: (i, 0)
    return pl.pallas_call(
        functools.partial(_router_kernel, tm=tm),
        grid=(n_tiles,),
        in_specs=[pl.BlockSpec((1, tm, d), lambda i: (i // nj, i % nj, 0)),
                  pl.BlockSpec((1, 1, d), row), pl.BlockSpec((1, 1, d), row),
                  _const_spec((2, d, LANES)), _const_spec((tm, tm))],
        out_specs=[pl.BlockSpec((tm * SUBLANES, LANES), tok),
                   pl.BlockSpec((tm, LANES), tok), pl.BlockSpec((tm, LANES), tok),
                   pl.BlockSpec((SUBLANES, LANES), lambda i: (0, 0))],
        out_shape=[jax.ShapeDtypeStruct((n_tiles * tm * SUBLANES, LANES), F32),
                   jax.ShapeDtypeStruct((bsz * seq, LANES), jnp.int32),
                   jax.ShapeDtypeStruct((bsz * seq, LANES), F32),
                   jax.ShapeDtypeStruct((SUBLANES, LANES), F32)],
        scratch_shapes=[pltpu.VMEM((1, LANES), F32)],
        compiler_params=pltpu.CompilerParams(dimension_semantics=("arbitrary",),
                                             vmem_limit_bytes=VMEM_LIMIT_BYTES),
        name="router",
    )(x, scale2, shift2, wr2, stri)


_COPY_UNROLL = 8


def _tile_at(ref, t):
    return ref.at[pl.ds(pl.multiple_of(t * SUBLANES, SUBLANES), SUBLANES)]


def _issue_tile_copies(n, copy_of):
    def group(g, carry):
        copies = [copy_of(g, u) for u in range(_COPY_UNROLL)]
        for u, (src, dst, sem) in enumerate(copies):
            pltpu.make_async_copy(src, dst, sem).start(priority=u % 2)
        return carry

    assert n % _COPY_UNROLL == 0
    lax.fori_loop(0, n // _COPY_UNROLL, group, 0)


def _group_kernel(di_ref, fi_ref, src_ref, dst_hbm, zero_ref, sem, *, tb, nf):
    zero_ref[...] = jnp.zeros_like(zero_ref)
    per_group = _COPY_UNROLL // TOP_K
    _issue_tile_copies(TOP_K * tb, lambda g, u: (_tile_at(src_ref, g * per_group + u // TOP_K),
                                                 _tile_at(dst_hbm, di_ref[0, 0, g * _COPY_UNROLL + u]), sem))
    _issue_tile_copies(nf, lambda g, u: (zero_ref, _tile_at(dst_hbm, fi_ref[0, 0, g * _COPY_UNROLL + u]), sem))
    block = pl.ds(0, tb * SUBLANES)
    for _ in range(TOP_K):
        pltpu.make_async_copy(src_ref, dst_hbm.at[block], sem).wait()

    def wait_fill(r, carry):
        pltpu.make_async_copy(zero_ref, _tile_at(dst_hbm, 0), sem).wait()
        return carry

    lax.fori_loop(0, nf, wait_fill, 0)


def _group_rows(h_tiles, dest, fill_dst, n_rows, tb):
    n_asg = dest.shape[0]
    steps = n_asg // (TOP_K * tb)
    nf = fill_dst.shape[0] // steps
    assert steps * TOP_K * tb == n_asg and steps * nf == fill_dst.shape[0]
    return pl.pallas_call(
        functools.partial(_group_kernel, tb=tb, nf=nf),
        grid=(steps,),
        in_specs=[pl.BlockSpec((1, 1, TOP_K * tb), lambda i: (i, 0, 0), memory_space=pltpu.SMEM),
                  pl.BlockSpec((1, 1, nf), lambda i: (i, 0, 0), memory_space=pltpu.SMEM),
                  pl.BlockSpec((tb * SUBLANES, LANES), lambda i: (i, 0))],
        out_specs=pl.BlockSpec(memory_space=pl.ANY),
        out_shape=jax.ShapeDtypeStruct((n_rows * SUBLANES, LANES), h_tiles.dtype),
        scratch_shapes=[pltpu.VMEM((SUBLANES, LANES), h_tiles.dtype), pltpu.SemaphoreType.DMA(())],
        compiler_params=pltpu.CompilerParams(dimension_semantics=("arbitrary",), has_side_effects=True),
        name="moe_group",
    )(dest.astype(jnp.int32).reshape(steps, 1, TOP_K * tb), fill_dst.astype(jnp.int32).reshape(steps, 1, nf),
      h_tiles)


def _expert_kernel(be_ref, nu_ref, x_ref, wg_ref, wu_ref, wd_ref, o_ref, acc_ref, xb_ref, *, tm):
    i = pl.program_id(0)
    f = pl.program_id(1)
    used = i < nu_ref[0]

    @pl.when(jnp.logical_and(used, f == 0))
    def _():
        xb_ref[...] = _from_token_tiles(x_ref, tm).astype(BF16)
        acc_ref[...] = jnp.zeros_like(acc_ref)

    @pl.when(used)
    def _():
        xb = xb_ref[...]
        tf = wg_ref.shape[2]
        halves = [slice(0, tf // 2), slice(tf // 2, tf)]
        acts = []
        for sl in halves:
            gt = jnp.dot(xb, wg_ref[0, :, sl], preferred_element_type=F32)
            up = jnp.dot(xb, wu_ref[0, :, sl], preferred_element_type=F32)
            acts.append((_silu(gt) * up).astype(BF16))
        acc_ref[...] += sum(jnp.dot(act, wd_ref[0, sl, :], preferred_element_type=F32)
                            for act, sl in zip(acts, halves))

    last = f == pl.num_programs(1) - 1

    @pl.when(jnp.logical_and(used, last))
    def _():
        _to_token_tiles(o_ref, acc_ref[...], tm)

    @pl.when(jnp.logical_and(jnp.logical_not(used), last))
    def _():
        o_ref[...] = jnp.zeros_like(o_ref)


def _experts(x_tiles, nb, block_e, n_used, wg, wu, wd, tm, tf):
    n_exp, d, dff = wg.shape
    nf = dff // tf

    def xmap(i, f, be, nu):
        return (jnp.minimum(i, nu[0] - 1), 0)

    def fidx(i, f, nu):
        return jnp.where(i < nu[0], f, nf - 1)

    grid_spec = pltpu.PrefetchScalarGridSpec(
        num_scalar_prefetch=2,
        grid=(nb, nf),
        in_specs=[pl.BlockSpec((tm * SUBLANES, LANES), xmap),
                  pl.BlockSpec((1, d, tf), lambda i, f, be, nu: (be[i], 0, fidx(i, f, nu))),
                  pl.BlockSpec((1, d, tf), lambda i, f, be, nu: (be[i], 0, fidx(i, f, nu))),
                  pl.BlockSpec((1, tf, d), lambda i, f, be, nu: (be[i], fidx(i, f, nu), 0))],
        out_specs=pl.BlockSpec((tm * SUBLANES, LANES), lambda i, f, be, nu: (i, 0)),
        scratch_shapes=[pltpu.VMEM((tm, d), F32), pltpu.VMEM((tm, d), BF16)])
    return pl.pallas_call(
        functools.partial(_expert_kernel, tm=tm),
        grid_spec=grid_spec,
        out_shape=jax.ShapeDtypeStruct((nb * tm * SUBLANES, LANES), F32),
        compiler_params=pltpu.CompilerParams(dimension_semantics=("arbitrary", "arbitrary"),
                                             vmem_limit_bytes=VMEM_LIMIT_BYTES),
        name="moe_experts",
    )(block_e, n_used, x_tiles, wg, wu, wd)


def _combine_kernel(cur_ref, nxt_ref, y_hbm, w_ref, x_ref, gate_ref, g_ref, b_ref, o_ref, ybuf_ref, sems,
                    *, tc, alpha):
    i = pl.program_id(0)
    nt = TOP_K * tc

    def fetch(idx_ref, slot):
        buf, sem = ybuf_ref.at[slot], sems.at[slot]
        _issue_tile_copies(nt, lambda g, u: (_tile_at(y_hbm, idx_ref[0, 0, g * _COPY_UNROLL + u]),
                                             _tile_at(buf, g * _COPY_UNROLL + u), sem))

    @pl.when(i == 0)
    def _():
        fetch(cur_ref, 0)

    @pl.when(i + 1 < pl.num_programs(0))
    def _():
        fetch(nxt_ref, (i + 1) % 2)

    slot = i % 2
    yb = ybuf_ref.at[slot]
    pltpu.make_async_copy(y_hbm.at[pl.ds(0, nt * SUBLANES)], yb, sems.at[slot]).wait()
    w = w_ref[...]
    y = (w[:, 0:1] * _from_token_tiles(yb, tc, 0, TOP_K)
         + w[:, 1:2] * _from_token_tiles(yb, tc, 1, TOP_K))
    r = alpha * x_ref[0] + (1.0 + gate_ref[0]) * y
    o_ref[0] = _layer_norm_rows(r, g_ref[...], b_ref[...])


def _combine(y_tiles, dest, top_w, x, gate2, ln_g, ln_b, alpha, tc):
    bsz, seq, d = x.shape
    nj = seq // tc
    steps = bsz * nj
    nt = TOP_K * tc
    dest_blocks = dest.astype(jnp.int32).reshape(steps, 1, nt)
    return pl.pallas_call(
        functools.partial(_combine_kernel, tc=tc, alpha=alpha),
        grid=(steps,),
        in_specs=[pl.BlockSpec((1, 1, nt), lambda i: (i, 0, 0), memory_space=pltpu.SMEM),
                  pl.BlockSpec((1, 1, nt), lambda i: (jnp.minimum(i + 1, steps - 1), 0, 0),
                               memory_space=pltpu.SMEM),
                  pl.BlockSpec(memory_space=pl.ANY),
                  pl.BlockSpec((tc, LANES), lambda i: (i, 0)),
                  pl.BlockSpec((1, tc, d), lambda i: (i // nj, i % nj, 0)),
                  pl.BlockSpec((1, 1, d), lambda i: (i // nj, 0, 0)), _const_spec((1, d)), _const_spec((1, d))],
        out_specs=pl.BlockSpec((1, tc, d), lambda i: (i // nj, i % nj, 0)),
        out_shape=jax.ShapeDtypeStruct((bsz, seq, d), F32),
        scratch_shapes=[pltpu.VMEM((2, nt * SUBLANES, LANES), F32), pltpu.SemaphoreType.DMA((2,))],
        compiler_params=pltpu.CompilerParams(dimension_semantics=("arbitrary",),
                                             vmem_limit_bytes=VMEM_LIMIT_BYTES),
        name="moe_combine",
    )(dest_blocks, dest_blocks, y_tiles, top_w, x, gate2, ln_g, ln_b)


def _moe(oa, oc, lw, x, gate1, scale2, shift2, gate2, wr2, stri, wg, wu, wd, ln_g, ln_b, alpha,
         tm_tok, tm, tf, tc):
    bsz, seq, d = x.shape
    n_tok = bsz * seq
    n_asg = n_tok * TOP_K
    x1 = _outproj(oa, oc, lw, x, gate1, alpha, tm_tok)
    h_tiles, idx128, w128, cnt = _router(x1, scale2, shift2, wr2, stri, tm_tok)
    counts = cnt[0, :N_EXPERTS].astype(jnp.int32)
    padded = (counts + tm - 1) // tm * tm
    pad_end = jnp.cumsum(padded)
    pad_start = pad_end - padded
    e_ids, ranks = idx128[:, 0:TOP_K], idx128[:, TOP_K:2 * TOP_K]
    start_of = jnp.sum(jnp.where(e_ids[:, :, None] == jnp.arange(N_EXPERTS, dtype=jnp.int32), pad_start, 0), axis=-1)
    dest = (start_of + ranks).reshape(n_asg)
    nb = n_asg // tm + N_EXPERTS
    n_rows = nb * tm
    n_fill = n_rows - n_asg
    fill_end = jnp.cumsum(padded - counts)
    slot = jnp.arange(n_fill, dtype=jnp.int32)
    slot_e = jnp.sum((slot[:, None] >= fill_end[None, :]).astype(jnp.int32), axis=1)
    in_group = slot_e < N_EXPERTS
    ge = jnp.minimum(slot_e, N_EXPERTS - 1)
    fill_dst = jnp.where(in_group, (pad_start + counts)[ge] + slot - (fill_end - (padded - counts))[ge],
                         pad_end[-1] + slot - fill_end[-1])
    x_tiles = _group_rows(h_tiles, dest, fill_dst, n_rows, tm_tok)
    block_row = jnp.arange(nb, dtype=jnp.int32) * tm
    block_e = jnp.minimum(jnp.sum((block_row[:, None] >= pad_end[None, :]).astype(jnp.int32), axis=1),
                          N_EXPERTS - 1)
    n_used = (pad_end[-1:] // tm).astype(jnp.int32)
    y_tiles = _experts(x_tiles, nb, block_e, n_used, wg, wu, wd, tm, tf)
    return _combine(y_tiles, dest, w128, x1, gate2, ln_g, ln_b, alpha, tc)


def _layer_weights(layer, w_in, b_fgate, w_conv, q_norm_g, kv_norm_g, w_uq, w_ukv, head_norm_g, w_o,
                   ln1_g, ln1_b):
    d = w_in.shape[1]
    sizes = [FOX_W, FOX_W, FOX_W, H_FOX, CONV_W, CONV_W, CONV_W, Q_LORA, KV_LORA, QK_ROPE]
    offs = np.concatenate([[0], np.cumsum(sizes)])
    wi = w_in[layer]
    fq, fk, fv, fl, bg, cg, hin, cq, ckv, kr = [wi[:, offs[i]:offs[i + 1]] for i in range(len(sizes))]
    half = QK_ROPE // 2
    z64 = jnp.zeros((d, QK_NOPE), F32)
    z32 = jnp.zeros((d, LANES - QK_NOPE - QK_ROPE), F32)
    wa = jnp.concatenate([
        fq, fk, fv,
        jnp.pad(jnp.repeat(fl, _N_SPLIT, axis=1), ((0, 0), (0, LANES - _N_SPLIT * H_FOX))), bg, cg, hin, cq, ckv,
        jnp.concatenate([z64, kr, z32], axis=1),
        jnp.concatenate([z64, kr[:, half:], kr[:, :half], z32], axis=1)], axis=1)
    assert wa.shape[1] == _NA
    q3 = w_uq[layer].reshape(Q_LORA, H_MLA, QK_NOPE + QK_ROPE)
    q_main = jnp.pad(q3, ((0, 0), (0, 0), (0, LANES - QK_NOPE - QK_ROPE)))
    q_swap = jnp.concatenate([jnp.zeros((Q_LORA, H_MLA, QK_NOPE), F32), q3[:, :, QK_NOPE + half:],
                              q3[:, :, QK_NOPE:QK_NOPE + half],
                              jnp.zeros((Q_LORA, H_MLA, LANES - QK_NOPE - QK_ROPE), F32)], axis=2)
    wuq = jnp.concatenate([q_main.reshape(Q_LORA, H_MLA * LANES), q_swap.reshape(Q_LORA, H_MLA * LANES)],
                          axis=1).astype(BF16)
    kv3 = w_ukv[layer].reshape(KV_LORA, H_MLA, QK_NOPE + V_DIM)
    k_nope = jnp.pad(kv3[:, :, :QK_NOPE], ((0, 0), (0, 0), (0, LANES - QK_NOPE))).reshape(KV_LORA, H_MLA * LANES)
    wukv = jnp.concatenate([k_nope, kv3[:, :, QK_NOPE:].reshape(KV_LORA, MLA_W)], axis=1).astype(BF16)
    hg = head_norm_g[layer]
    wo = w_o[layer]
    c0, c1 = FOX_W, FOX_W + CONV_W
    return {
        "wa": wa, "wuq": wuq, "wukv": wukv,
        "qg": q_norm_g[layer].reshape(1, Q_LORA), "kvg": kv_norm_g[layer].reshape(1, KV_LORA),
        "bf": jnp.pad(jnp.repeat(b_fgate[layer], _N_SPLIT), (0, LANES - _N_SPLIT * H_FOX)).reshape(1, LANES),
        "wconv": jnp.pad(w_conv[layer], ((0, SUBLANES - CONV_K), (0, 0))),
        "hg_conv": hg[c0:c1].reshape(1, CONV_W),
        "hg_attn": jnp.concatenate([hg[:c0], hg[c1:]]).reshape(1, FOX_W + MLA_W),
        "wo_a": jnp.concatenate([wo[:c0], wo[c1:]], axis=0).astype(BF16),
        "wo_c": wo[c0:c1].astype(BF16),
        "ln1_g": ln1_g[layer].reshape(1, d), "ln1_b": ln1_b[layer].reshape(1, d),
    }


def _constants(seq, ts):
    pos = jnp.arange(seq, dtype=F32)
    inv_freq = ROPE_THETA ** (-jnp.arange(0, QK_ROPE, 2, dtype=F32) / QK_ROPE)
    ang = pos[:, None] * inv_freq[None, :]
    cos, sin = jnp.cos(ang), jnp.sin(ang)
    pad_r = jnp.zeros((seq, LANES - QK_NOPE - QK_ROPE), F32)
    ctab = jnp.concatenate([jnp.ones((seq, QK_NOPE), F32), cos, cos, pad_r], axis=1)
    stab = jnp.concatenate([jnp.zeros((seq, QK_NOPE), F32), -sin, sin, pad_r], axis=1)
    q_scale = (QK_NOPE + QK_ROPE) ** -0.5 * LOG2E
    lanes = np.arange(LANES)
    lm = np.zeros((SUBLANES, LANES), np.float32)
    for part in range(_N_SPLIT):
        lm[part] = (lanes % _N_SPLIT == part) & (lanes < _N_SPLIT * H_FOX)
    for row, base in ((3, HEAD_DIM), (5, 0)):
        lm[row] = (lanes >= base) & (lanes < base + _N_SPLIT)
        lm[row + 1] = (lanes >= base + _N_SPLIT) & (lanes < base + 2 * _N_SPLIT)

    def group_mean(n):
        gidx = np.arange(n) // HEAD_DIM
        return (gidx[:, None] == gidx[None, :]).astype(np.float32) / HEAD_DIM

    return {
        "cq": ctab * q_scale, "sq": stab * q_scale, "ck": ctab, "sk": stab,
        "tri": jnp.asarray(np.tril(np.ones((ts, ts), np.float32)), BF16),
        "stri": jnp.asarray(np.tril(np.ones((ts, ts), np.float32), k=-1), BF16),
        "lane_masks": jnp.asarray(lm),
        "gm256": jnp.asarray(group_mean(CONV_W), BF16), "gm128": jnp.asarray(group_mean(LANES), BF16),
    }


def _tile(n, pref):
    t = min(n, pref)
    assert n % t == 0, (n, pref)
    return t


def kernel(x, c, w_mod, b_mod, w_in, b_fgate, w_conv, q_norm_g, kv_norm_g, w_uq, w_ukv, head_norm_g, w_o, ln1_g, ln1_b, ln2_g, ln2_b, ffn_w_gate, ffn_w_up, ffn_w_down, router_w, exp_w_gate, exp_w_up, exp_w_down):
    bsz, seq, d = x.shape
    depth = w_mod.shape[0]
    assert d == D_MODEL and bsz <= SUBLANES
    alpha = (2 * depth) ** 0.25
    ts = _tile(seq, 512)
    tq = _tile(seq, 512)
    tm_e = 512
    tf_e = 512
    tc = _tile(seq, 256)
    assert (bsz * seq * TOP_K) % tm_e == 0 and exp_w_gate.shape[-1] % tf_e == 0

    consts = _constants(seq, ts)
    c_pad = jnp.pad(c, ((0, SUBLANES - bsz), (0, 0)))
    mod = _modulation(c_pad, w_mod, b_mod)[:, :bsz, :]

    for layer in range(depth):
        m6 = mod[layer].reshape(bsz, 6, 1, d)
        shift1, scale1, gate1, shift2, scale2, gate2 = [m6[:, i] for i in range(6)]
        lw = _layer_weights(layer, w_in, b_fgate, w_conv, q_norm_g, kv_norm_g, w_uq, w_ukv, head_norm_g,
                            w_o, ln1_g, ln1_b)
        qp, kp, vt, oc = _inproj(x, scale1, shift1, lw, consts, ts)
        oa = _attention(qp, kp, vt, lw["hg_attn"], consts["gm128"], tq, 6)
        j = layer // 2
        g2, b2 = ln2_g[layer].reshape(1, d), ln2_b[layer].reshape(1, d)
        if layer % 2 == 0:
            x = _ffn_dense(oa, oc, lw, x, gate1, scale2, shift2, gate2, ffn_w_gate[j].astype(BF16),
                           ffn_w_up[j].astype(BF16), ffn_w_down[j].astype(BF16), g2, b2, alpha, ts)
        else:
            wr = jnp.pad(router_w[j], ((0, 0), (0, LANES - N_EXPERTS)))
            wr_hi = wr.astype(BF16)
            wr_lo = (wr - wr_hi.astype(F32)).astype(BF16)
            x = _moe(oa, oc, lw, x, gate1, scale2, shift2, gate2, jnp.stack([wr_hi, wr_lo]), consts["stri"],
                     exp_w_gate[j].astype(BF16), exp_w_up[j].astype(BF16), exp_w_down[j].astype(BF16),
                     g2, b2, alpha, ts, tm_e, tf_e, tc)
    return x
```

```python
import functools

import numpy as np
import jax
import jax.numpy as jnp
from jax import lax
from jax.experimental import pallas as pl
from jax.experimental.pallas import tpu as pltpu

F32 = jnp.float32
BF16 = jnp.bfloat16

D_MODEL = 1024
HEAD_DIM = 64
H_FOX = 6
FOX_W = H_FOX * HEAD_DIM
CONV_W = 256
CONV_K = 3
H_MLA = 6
Q_LORA = 256
KV_LORA = 256
QK_NOPE = 64
QK_ROPE = 32
V_DIM = 64
MLA_W = H_MLA * V_DIM
N_HEADS = H_FOX + H_MLA
ROPE_THETA = 10000.0
N_EXPERTS = 8
TOP_K = 2
LN_EPS = 1e-5
RMS_EPS = 1e-6

LANES = 128
SUBLANES = 8
VMEM_LIMIT_BYTES = 56 * 1024 * 1024

_HW = 6 * LANES
_QF0, _KF0, _VF0, _FL0 = 0, FOX_W, 2 * FOX_W, 3 * FOX_W
_BG0 = _FL0 + LANES
_CG0, _HI0, _CQ0, _CKV0 = _BG0 + CONV_W, _BG0 + 2 * CONV_W, _BG0 + 3 * CONV_W, _BG0 + 3 * CONV_W + Q_LORA
_KR0 = _CKV0 + KV_LORA
_KRS0 = _KR0 + LANES
_NA = _KRS0 + LANES
LOG2E = 1.4426950408889634
VT_ROWS = LANES
_N_SPLIT = 3


def _const_spec(shape):
    zeros = (0,) * len(shape)
    return pl.BlockSpec(shape, lambda *_: zeros, pipeline_mode=pl.Buffered(1))


def _silu(v):
    return v * (1.0 / (1.0 + jnp.exp(-v)))


def _split3(v):
    hi = v.astype(BF16)
    r1 = v - hi.astype(F32)
    mid = r1.astype(BF16)
    lo = (r1 - mid.astype(F32)).astype(BF16)
    return hi, mid, lo


def _group_mean_sq(v, gmat_ref):
    sq = v * v
    hi = sq.astype(BF16)
    lo = (sq - hi.astype(F32)).astype(BF16)
    g = gmat_ref[...]
    return (jnp.dot(hi, g, preferred_element_type=F32) + jnp.dot(lo, g, preferred_element_type=F32))


def _layer_norm_rows(r, g, b):
    mu = jnp.mean(r, axis=-1, keepdims=True)
    rc = r - mu
    var = jnp.mean(rc * rc, axis=-1, keepdims=True)
    return rc * lax.rsqrt(var + LN_EPS) * g + b


def _mod_kernel(c_ref, w_ref, b_ref, o_ref):
    act = _silu(c_ref[...]).astype(BF16)
    o_ref[0] = jnp.dot(act, w_ref[0].astype(BF16), preferred_element_type=F32) + b_ref[0]


def _modulation(c_pad, w_mod, b_mod):
    depth, d, n = w_mod.shape
    tn = 1024
    return pl.pallas_call(
        _mod_kernel,
        grid=(depth, n // tn),
        in_specs=[pl.BlockSpec((SUBLANES, d), lambda l, j: (0, 0)),
                  pl.BlockSpec((1, d, tn), lambda l, j: (l, 0, j)),
                  pl.BlockSpec((1, 1, tn), lambda l, j: (l, 0, j))],
        out_specs=pl.BlockSpec((1, SUBLANES, tn), lambda l, j: (l, 0, j)),
        out_shape=jax.ShapeDtypeStruct((depth, SUBLANES, n), F32),
        compiler_params=pltpu.CompilerParams(dimension_semantics=("arbitrary", "arbitrary")),
        name="modulation",
    )(c_pad, w_mod, b_mod.reshape(depth, 1, n))


def _inproj_kernel(x_ref, sc_ref, sh_ref, wa_ref, cq_ref, sq_ref, ck_ref, sk_ref, wuq_ref, wukv_ref,
                   qg_ref, kvg_ref, bf_ref, wconv_ref, hgc_ref, tri_ref, lm_ref, gm_ref,
                   qp_ref, kp_ref, vt_ref, oc_ref, fcarry_ref, ubuf_ref, wab_ref, *, ts):
    @pl.when(jnp.logical_and(pl.program_id(0) == 0, pl.program_id(1) == 0))
    def _():
        for c in range(_NA // LANES):
            sl = slice(c * LANES, (c + 1) * LANES)
            wab_ref[:, sl] = wa_ref[:, sl].astype(BF16)

    @pl.when(pl.program_id(1) == 0)
    def _():
        fcarry_ref[...] = jnp.zeros_like(fcarry_ref)
        ubuf_ref[pl.ds(0, SUBLANES), :] = jnp.zeros((SUBLANES, CONV_W), F32)

    hb = (x_ref[0] * (1.0 + sc_ref[0]) + sh_ref[0]).astype(BF16)

    def proj(lo, hi):
        return jnp.dot(hb, wab_ref[:, lo:hi], preferred_element_type=F32)

    def put_values_t(head, v_half_t):
        vt_ref[0, head, 0:V_DIM, :] = v_half_t.astype(BF16)
        vt_ref[0, head, V_DIM:VT_ROWS, :] = jnp.ones((VT_ROWS - V_DIM, ts), BF16)

    z_fl = proj(_FL0, _FL0 + LANES)
    z_cq = proj(_CQ0, _CQ0 + Q_LORA)
    z_ckv = proj(_CKV0, _CKV0 + KV_LORA)
    z_cg = proj(_CG0, _CG0 + CONV_W)
    z_hi = proj(_HI0, _HI0 + CONV_W)
    z_bg = proj(_BG0, _BG0 + CONV_W)
    zq = proj(_QF0, _QF0 + FOX_W) * (HEAD_DIM ** -0.5 * LOG2E)
    zk = proj(_KF0, _KF0 + FOX_W)
    zv = proj(_VF0, _VF0 + FOX_W)
    z_kr = proj(_KR0, _KR0 + LANES)
    z_krs = proj(_KRS0, _KRS0 + LANES)

    a = z_fl + bf_ref[...]
    logf = jnp.minimum(a, 0.0) - jnp.log1p(jnp.exp(-jnp.abs(a)))
    tri = tri_ref[...]
    csum = sum(jnp.dot(tri, part, preferred_element_type=F32) for part in _split3(logf))
    fcum = fcarry_ref[...] + csum
    fcarry_ref[...] = fcum[ts - 1:ts, :]
    f_hi, f_mid, f_lo = _split3(fcum * LOG2E)
    lm = lm_ref[...]
    fparts = (f_hi.astype(F32) * lm[0:1, :] + f_mid.astype(F32) * lm[1:2, :]
              + f_lo.astype(F32) * lm[2:3, :])
    lane = lax.broadcasted_iota(jnp.int32, (ts, LANES), 1)
    low_half = lane < HEAD_DIM
    for h in range(H_FOX):
        blk = slice((h // 2) * LANES, (h // 2 + 1) * LANES)
        if h % 2 == 0:
            base, f_mask, one_mask, keep = HEAD_DIM, lm[3:4, :], lm[4:5, :], low_half
        else:
            base, f_mask, one_mask, keep = 0, lm[5:6, :], lm[6:7, :], jnp.logical_not(low_half)
        f_at_q = pltpu.roll(fparts, (base - _N_SPLIT * h) % LANES, axis=1)
        f_at_k = pltpu.roll(fparts, (base + _N_SPLIT - _N_SPLIT * h) % LANES, axis=1)
        qp_ref[0, h] = jnp.where(keep, zq[:, blk], f_at_q * f_mask + one_mask).astype(BF16)
        kp_ref[0, h] = jnp.where(keep, zk[:, blk], f_mask - f_at_k * one_mask).astype(BF16)
    for j in range(H_FOX // 2):
        vt = zv[:, j * LANES:(j + 1) * LANES].T
        put_values_t(2 * j, vt[0:V_DIM])
        put_values_t(2 * j + 1, vt[V_DIM:LANES])

    u = z_cg * z_hi
    ubuf_ref[pl.ds(SUBLANES, ts), :] = u
    u1 = ubuf_ref[pl.ds(SUBLANES - 1, ts), :]
    u2 = ubuf_ref[pl.ds(SUBLANES - 2, ts), :]
    ubuf_ref[pl.ds(0, SUBLANES), :] = u[ts - SUBLANES:ts, :]
    wc = wconv_ref[...]
    oc = z_bg * (wc[0:1, :] * u2 + wc[1:2, :] * u1 + wc[2:3, :] * u)
    ocn = oc * lax.rsqrt(_group_mean_sq(oc, gm_ref) + RMS_EPS) * hgc_ref[...]
    oc_ref[0] = ocn.astype(BF16)

    def rms(v, g):
        return (v * lax.rsqrt(jnp.mean(v * v, axis=-1, keepdims=True) + RMS_EPS) * g).astype(BF16)

    cqn = rms(z_cq, qg_ref[...])
    qm = jnp.dot(cqn, wuq_ref[:, 0:_HW], preferred_element_type=F32)
    qs = jnp.dot(cqn, wuq_ref[:, _HW:2 * _HW], preferred_element_type=F32)
    kvn = rms(z_ckv, kvg_ref[...])
    kn = jnp.dot(kvn, wukv_ref[:, 0:_HW], preferred_element_type=F32)
    vm = jnp.dot(kvn, wukv_ref[:, _HW:_HW + MLA_W], preferred_element_type=F32)
    krr = z_kr * ck_ref[...] + z_krs * sk_ref[...]
    cq, sq = cq_ref[...], sq_ref[...]
    for h in range(H_MLA):
        sl = slice(h * LANES, (h + 1) * LANES)
        qp_ref[0, H_FOX + h] = (qm[:, sl] * cq + qs[:, sl] * sq).astype(BF16)
        kp_ref[0, H_FOX + h] = (kn[:, sl] + krr).astype(BF16)
    for j in range(H_MLA // 2):
        vt = vm[:, j * LANES:(j + 1) * LANES].T
        put_values_t(H_FOX + 2 * j, vt[0:V_DIM])
        put_values_t(H_FOX + 2 * j + 1, vt[V_DIM:LANES])


def _inproj(x, scale1, shift1, lw, consts, ts):
    bsz, seq, d = x.shape
    kern = functools.partial(_inproj_kernel, ts=ts)
    row = lambda b, j: (b, 0, 0)
    tab = pl.BlockSpec((ts, LANES), lambda b, j: (j, 0))
    return pl.pallas_call(
        kern,
        grid=(bsz, seq // ts),
        in_specs=[pl.BlockSpec((1, ts, d), lambda b, j: (b, j, 0)),
                  pl.BlockSpec((1, 1, d), row), pl.BlockSpec((1, 1, d), row),
                  _const_spec((d, _NA)), tab, tab, tab, tab,
                  _const_spec((Q_LORA, 2 * _HW)), _const_spec((KV_LORA, _HW + MLA_W)),
                  _const_spec((1, Q_LORA)), _const_spec((1, KV_LORA)), _const_spec((1, LANES)),
                  _const_spec((SUBLANES, CONV_W)), _const_spec((1, CONV_W)),
                  _const_spec((ts, ts)), _const_spec((SUBLANES, LANES)), _const_spec((CONV_W, CONV_W))],
        out_specs=[pl.BlockSpec((1, N_HEADS, ts, LANES), lambda b, j: (b, 0, j, 0)),
                   pl.BlockSpec((1, N_HEADS, ts, LANES), lambda b, j: (b, 0, j, 0)),
                   pl.BlockSpec((1, N_HEADS, VT_ROWS, ts), lambda b, j: (b, 0, 0, j)),
                   pl.BlockSpec((1, ts, CONV_W), lambda b, j: (b, j, 0))],
        out_shape=[jax.ShapeDtypeStruct((bsz, N_HEADS, seq, LANES), BF16),
                   jax.ShapeDtypeStruct((bsz, N_HEADS, seq, LANES), BF16),
                   jax.ShapeDtypeStruct((bsz, N_HEADS, VT_ROWS, seq), BF16),
                   jax.ShapeDtypeStruct((bsz, seq, CONV_W), BF16)],
        scratch_shapes=[pltpu.VMEM((1, LANES), F32), pltpu.VMEM((ts + SUBLANES, CONV_W), F32),
                        pltpu.VMEM((d, _NA), BF16)],
        compiler_params=pltpu.CompilerParams(dimension_semantics=("arbitrary", "arbitrary"),
                                             vmem_limit_bytes=VMEM_LIMIT_BYTES),
        name="inproj",
    )(x, scale1, shift1, lw["wa"], consts["cq"], consts["sq"], consts["ck"], consts["sk"],
      lw["wuq"], lw["wukv"], lw["qg"], lw["kvg"], lw["bf"], lw["wconv"], lw["hg_conv"],
      consts["tri"], consts["lane_masks"], consts["gm256"])


def _attn_kernel(qi_ref, kj_ref, q_ref, k_ref, vt_ref, g_ref, gm_ref, o_ref, m_ref, acc_ref, s_ref, *, nh):
    step = pl.program_id(2)
    qi = qi_ref[step]
    kj = kj_ref[step]

    @pl.when(kj == 0)
    def _():
        m_ref[...] = jnp.full(m_ref.shape, -jnp.inf, F32)
        acc_ref[...] = jnp.zeros_like(acc_ref)

    def update(masked):
        if masked:
            tile = (q_ref.shape[2], q_ref.shape[2])
            causal = lax.broadcasted_iota(jnp.int32, tile, 0) <= lax.broadcasted_iota(jnp.int32, tile, 1)

        def scores(h):
            st = lax.dot_general(k_ref[0, h], q_ref[0, h], (((1,), (1,)), ((), ())),
                                 preferred_element_type=F32)
            if masked:
                st = jnp.where(causal, st, -jnp.inf)
            s_ref[h] = st

        for h in range(nh):
            scores(h)
        pts, alphas = [], []
        for h in range(nh):
            m_prev = m_ref[h]
            m_new = jnp.maximum(m_prev, jnp.max(s_ref[h], axis=0, keepdims=True))
            pts.append(jnp.exp2(s_ref[h] - m_new).astype(BF16))
            alphas.append(jnp.exp2(m_prev - m_new))
            m_ref[h] = m_new
        for h in range(nh):
            acc_ref[h] = alphas[h] * acc_ref[h] + jnp.dot(vt_ref[0, h], pts[h], preferred_element_type=F32)

    @pl.when(kj < qi)
    def _():
        update(False)

    @pl.when(kj == qi)
    def _():
        update(True)
        for pair in range(nh // 2):
            ot = jnp.concatenate([acc_ref[h, 0:V_DIM, :] / acc_ref[h, V_DIM:2 * V_DIM, :]
                                  for h in (2 * pair, 2 * pair + 1)], axis=0)
            o = ot.T
            sl = slice(pair * LANES, (pair + 1) * LANES)
            on = o * lax.rsqrt(_group_mean_sq(o, gm_ref) + RMS_EPS) * g_ref[:, sl]
            o_ref[0, :, sl] = on.astype(BF16)


def _attention(qp, kp, vt, g_attn, gm128, tq, nh):
    bsz, n_heads, seq, _ = qp.shape
    nq = seq // tq
    ow = (nh // 2) * LANES
    qi_tab = np.concatenate([np.full(i + 1, i) for i in range(nq)]).astype(np.int32)
    kj_tab = np.concatenate([np.arange(i + 1) for i in range(nq)]).astype(np.int32)
    grid_spec = pltpu.PrefetchScalarGridSpec(
        num_scalar_prefetch=2,
        grid=(bsz, n_heads // nh, len(qi_tab)),
        in_specs=[pl.BlockSpec((1, nh, tq, LANES), lambda b, p, s, qi, kj: (b, p, qi[s], 0)),
                  pl.BlockSpec((1, nh, tq, LANES), lambda b, p, s, qi, kj: (b, p, kj[s], 0)),
                  pl.BlockSpec((1, nh, VT_ROWS, tq), lambda b, p, s, qi, kj: (b, p, 0, kj[s])),
                  pl.BlockSpec((1, ow), lambda b, p, s, qi, kj: (0, p)),
                  _const_spec((LANES, LANES))],
        out_specs=pl.BlockSpec((1, tq, ow), lambda b, p, s, qi, kj: (b, qi[s], p)),
        scratch_shapes=[pltpu.VMEM((nh, 1, tq), F32), pltpu.VMEM((nh, VT_ROWS, tq), F32),
                        pltpu.VMEM((nh, tq, tq), F32)])
    return pl.pallas_call(
        functools.partial(_attn_kernel, nh=nh),
        grid_spec=grid_spec,
        out_shape=jax.ShapeDtypeStruct((bsz, seq, (n_heads // 2) * LANES), BF16),
        compiler_params=pltpu.CompilerParams(
            dimension_semantics=("arbitrary", "arbitrary", "arbitrary"),
            vmem_limit_bytes=VMEM_LIMIT_BYTES),
        name="attention",
    )(jnp.asarray(qi_tab), jnp.asarray(kj_tab), qp, kp, vt, g_attn, gm128)


def _mixer_out_rows(oa_ref, oc_ref, woa_ref, woc_ref, x_ref, gate_ref, g_ref, b_ref, alpha):
    y = (jnp.dot(oa_ref[0], woa_ref[...], preferred_element_type=F32)
         + jnp.dot(oc_ref[0], woc_ref[...], preferred_element_type=F32))
    r = alpha * x_ref[0] + (1.0 + gate_ref[0]) * y
    return _layer_norm_rows(r, g_ref[...], b_ref[...])


def _mixer_out_specs(oa, d, tm, act, row):
    wa_rows = oa.shape[-1]
    return [pl.BlockSpec((1, tm, wa_rows), act), pl.BlockSpec((1, tm, CONV_W), act),
            _const_spec((wa_rows, d)), _const_spec((CONV_W, d)),
            pl.BlockSpec((1, tm, d), act), pl.BlockSpec((1, 1, d), row), _const_spec((1, d)), _const_spec((1, d))]


def _outproj_kernel(oa_ref, oc_ref, woa_ref, woc_ref, x_ref, gate_ref, g_ref, b_ref, o_ref, *, alpha):
    o_ref[0] = _mixer_out_rows(oa_ref, oc_ref, woa_ref, woc_ref, x_ref, gate_ref, g_ref, b_ref, alpha)


def _outproj(oa, oc, lw, x, gate1, alpha, tm):
    bsz, seq, d = x.shape
    act = lambda b, j: (b, j, 0)
    return pl.pallas_call(
        functools.partial(_outproj_kernel, alpha=alpha),
        grid=(bsz, seq // tm),
        in_specs=_mixer_out_specs(oa, d, tm, act, lambda b, j: (b, 0, 0)),
        out_specs=pl.BlockSpec((1, tm, d), act),
        out_shape=jax.ShapeDtypeStruct((bsz, seq, d), F32),
        compiler_params=pltpu.CompilerParams(dimension_semantics=("arbitrary", "arbitrary"),
                                             vmem_limit_bytes=VMEM_LIMIT_BYTES),
        name="outproj",
    )(oa, oc, lw["wo_a"], lw["wo_c"], x, gate1, lw["ln1_g"], lw["ln1_b"])


def _ffn_kernel(oa_ref, oc_ref, woa_ref, woc_ref, xin_ref, gate1_ref, g1_ref, b1_ref,
                sc_ref, sh_ref, gate_ref, wg_ref, wu_ref, wd_ref, g_ref, b_ref, o_ref, *, alpha, tf):
    x = _mixer_out_rows(oa_ref, oc_ref, woa_ref, woc_ref, xin_ref, gate1_ref, g1_ref, b1_ref, alpha)
    hb = (x * (1.0 + sc_ref[0]) + sh_ref[0]).astype(BF16)
    chunks = [slice(c * tf, (c + 1) * tf) for c in range(wg_ref.shape[1] // tf)]
    acts = []
    for sl in chunks:
        gt = jnp.dot(hb, wg_ref[:, sl], preferred_element_type=F32)
        up = jnp.dot(hb, wu_ref[:, sl], preferred_element_type=F32)
        acts.append((_silu(gt) * up).astype(BF16))
    acc = sum(jnp.dot(act, wd_ref[sl, :], preferred_element_type=F32) for act, sl in zip(acts, chunks))
    r = alpha * x + (1.0 + gate_ref[0]) * acc
    o_ref[0] = _layer_norm_rows(r, g_ref[...], b_ref[...])


def _ffn_dense(oa, oc, lw, x, gate1, scale2, shift2, gate2, wg, wu, wd, ln_g, ln_b, alpha, tm):
    bsz, seq, d = x.shape
    dff = wg.shape[1]
    tf = dff // 2 if (dff // 2) % LANES == 0 else dff
    nj = seq // tm
    act = lambda b, j: (b, j, 0)
    row = lambda b, j: (b, 0, 0)
    return pl.pallas_call(
        functools.partial(_ffn_kernel, alpha=alpha, tf=tf),
        grid=(bsz, nj),
        in_specs=_mixer_out_specs(oa, d, tm, act, row) + [
            pl.BlockSpec((1, 1, d), row), pl.BlockSpec((1, 1, d), row), pl.BlockSpec((1, 1, d), row),
            _const_spec((d, dff)), _const_spec((d, dff)), _const_spec((dff, d)),
            _const_spec((1, d)), _const_spec((1, d))],
        out_specs=pl.BlockSpec((1, tm, d), act),
        out_shape=jax.ShapeDtypeStruct((bsz, seq, d), F32),
        compiler_params=pltpu.CompilerParams(dimension_semantics=("arbitrary", "arbitrary"),
                                             vmem_limit_bytes=VMEM_LIMIT_BYTES),
        name="ffn_dense",
    )(oa, oc, lw["wo_a"], lw["wo_c"], x, gate1, lw["ln1_g"], lw["ln1_b"],
      scale2, shift2, gate2, wg, wu, wd, ln_g, ln_b)


def _to_token_tiles(ref, v, n):
    for c in range(v.shape[1] // LANES):
        ref[pl.ds(c, n, stride=SUBLANES), :] = v[:, c * LANES:(c + 1) * LANES]


def _from_token_tiles(ref, n, first=0, tiles_per_row=1):
    stride = tiles_per_row * SUBLANES
    return jnp.concatenate([ref[pl.ds(first * SUBLANES + c, n, stride=stride), :] for c in range(SUBLANES)],
                           axis=1)


def _router_kernel(x_ref, sc_ref, sh_ref, wr_ref, stri_ref, h_ref, idx_ref, w_ref, cnt_ref, carry_ref, *, tm):
    @pl.when(pl.program_id(0) == 0)
    def _():
        carry_ref[...] = jnp.zeros_like(carry_ref)

    h = x_ref[0] * (1.0 + sc_ref[0]) + sh_ref[0]
    _to_token_tiles(h_ref, h, tm)
    h_hi = h.astype(BF16)
    h_lo = (h - h_hi.astype(F32)).astype(BF16)
    w_hi, w_lo = wr_ref[0], wr_ref[1]
    logits = (jnp.dot(h_hi, w_hi, preferred_element_type=F32)
              + jnp.dot(h_hi, w_lo, preferred_element_type=F32)
              + jnp.dot(h_lo, w_hi, preferred_element_type=F32))
    lane = lax.broadcasted_iota(jnp.int32, logits.shape, 1).astype(F32)
    neg = -jnp.inf
    lg = jnp.where(lane < N_EXPERTS, logits, neg)
    m1 = jnp.max(lg, axis=-1, keepdims=True)
    i1 = jnp.min(jnp.where(lg == m1, lane, float(LANES)), axis=-1, keepdims=True)
    lg2 = jnp.where(lane == i1, neg, lg)
    m2 = jnp.max(lg2, axis=-1, keepdims=True)
    i2 = jnp.min(jnp.where(lg2 == m2, lane, float(LANES)), axis=-1, keepdims=True)
    e2 = jnp.exp(m2 - m1)
    denom = 1.0 + e2
    w_ref[...] = jnp.where(lane == 0.0, 1.0 / denom, e2 / denom)
    first, second = lane == i1, lane == i2
    chosen = jnp.logical_or(first, second).astype(F32)
    before = carry_ref[...] + jnp.dot(stri_ref[...], chosen.astype(BF16), preferred_element_type=F32)
    rank1 = jnp.sum(jnp.where(first, before, 0.0), axis=-1, keepdims=True)
    rank2 = jnp.sum(jnp.where(second, before, 0.0), axis=-1, keepdims=True)
    idx_ref[...] = jnp.where(lane == 0.0, i1, jnp.where(lane == 1.0, i2, jnp.where(lane == 2.0, rank1, rank2))
                             ).astype(jnp.int32)
    total = carry_ref[...] + jnp.sum(chosen, axis=0, keepdims=True)
    carry_ref[...] = total
    cnt_ref[...] = jnp.broadcast_to(total, cnt_ref.shape)


def _router(x, scale2, shift2, wr2, stri, tm):
    bsz, seq, d = x.shape
    assert d == SUBLANES * LANES and bsz * seq * TOP_K < 2 ** 24
    nj = seq // tm
    n_tiles = bsz * nj
    row = lambda i: (i // nj, 0, 0)
    tok = lambda i: (i, 0)
    return pl.pallas_call(
        functools.partial(_router_kernel, tm=tm),
        grid=(n_tiles,),
        in_specs=[pl.BlockSpec((1, tm, d), lambda i: (i // nj, i % nj, 0)),
                  pl.BlockSpec((1, 1, d), row), pl.BlockSpec((1, 1, d), row),
                  _const_spec((2, d, LANES)), _const_spec((tm, tm))],
        out_specs=[pl.BlockSpec((tm * SUBLANES, LANES), tok),
                   pl.BlockSpec((tm, LANES), tok), pl.BlockSpec((tm, LANES), tok),
                   pl.BlockSpec((SUBLANES, LANES), lambda i: (0, 0))],
        out_shape=[jax.ShapeDtypeStruct((n_tiles * tm * SUBLANES, LANES), F32),
                   jax.ShapeDtypeStruct((bsz * seq, LANES), jnp.int32),
                   jax.ShapeDtypeStruct((bsz * seq, LANES), F32),
                   jax.ShapeDtypeStruct((SUBLANES, LANES), F32)],
        scratch_shapes=[pltpu.VMEM((1, LANES), F32)],
        compiler_params=pltpu.CompilerParams(dimension_semantics=("arbitrary",),
                                             vmem_limit_bytes=VMEM_LIMIT_BYTES),
        name="router",
    )(x, scale2, shift2, wr2, stri)


_COPY_UNROLL = 8


def _tile_at(ref, t):
    return ref.at[pl.ds(pl.multiple_of(t * SUBLANES, SUBLANES), SUBLANES)]


def _issue_tile_copies(n, copy_of):
    def group(g, carry):
        copies = [copy_of(g, u) for u in range(_COPY_UNROLL)]
        for u, (src, dst, sem) in enumerate(copies):
            pltpu.make_async_copy(src, dst, sem).start(priority=u % 2)
        return carry

    assert n % _COPY_UNROLL == 0
    lax.fori_loop(0, n // _COPY_UNROLL, group, 0)


def _group_kernel(di_ref, fi_ref, src_ref, dst_hbm, zero_ref, sem, *, tb, nf):
    zero_ref[...] = jnp.zeros_like(zero_ref)
    per_group = _COPY_UNROLL // TOP_K
    _issue_tile_copies(TOP_K * tb, lambda g, u: (_tile_at(src_ref, g * per_group + u // TOP_K),
                                                 _tile_at(dst_hbm, di_ref[0, 0, g * _COPY_UNROLL + u]), sem))
    _issue_tile_copies(nf, lambda g, u: (zero_ref, _tile_at(dst_hbm, fi_ref[0, 0, g * _COPY_UNROLL + u]), sem))
    block = pl.ds(0, tb * SUBLANES)
    for _ in range(TOP_K):
        pltpu.make_async_copy(src_ref, dst_hbm.at[block], sem).wait()

    def wait_fill(r, carry):
        pltpu.make_async_copy(zero_ref, _tile_at(dst_hbm, 0), sem).wait()
        return carry

    lax.fori_loop(0, nf, wait_fill, 0)


def _group_rows(h_tiles, dest, fill_dst, n_rows, tb):
    n_asg = dest.shape[0]
    steps = n_asg // (TOP_K * tb)
    nf = fill_dst.shape[0] // steps
    assert steps * TOP_K * tb == n_asg and steps * nf == fill_dst.shape[0]
    return pl.pallas_call(
        functools.partial(_group_kernel, tb=tb, nf=nf),
        grid=(steps,),
        in_specs=[pl.BlockSpec((1, 1, TOP_K * tb), lambda i: (i, 0, 0), memory_space=pltpu.SMEM),
                  pl.BlockSpec((1, 1, nf), lambda i: (i, 0, 0), memory_space=pltpu.SMEM),
                  pl.BlockSpec((tb * SUBLANES, LANES), lambda i: (i, 0))],
        out_specs=pl.BlockSpec(memory_space=pl.ANY),
        out_shape=jax.ShapeDtypeStruct((n_rows * SUBLANES, LANES), h_tiles.dtype),
        scratch_shapes=[pltpu.VMEM((SUBLANES, LANES), h_tiles.dtype), pltpu.SemaphoreType.DMA(())],
        compiler_params=pltpu.CompilerParams(dimension_semantics=("arbitrary",), has_side_effects=True),
        name="moe_group",
    )(dest.astype(jnp.int32).reshape(steps, 1, TOP_K * tb), fill_dst.astype(jnp.int32).reshape(steps, 1, nf),
      h_tiles)


def _expert_kernel(be_ref, nu_ref, x_ref, wg_ref, wu_ref, wd_ref, o_ref, acc_ref, xb_ref, *, tm):
    i = pl.program_id(0)
    f = pl.program_id(1)
    used = i < nu_ref[0]

    @pl.when(jnp.logical_and(used, f == 0))
    def _():
        xb_ref[...] = _from_token_tiles(x_ref, tm).astype(BF16)
        acc_ref[...] = jnp.zeros_like(acc_ref)

    @pl.when(used)
    def _():
        xb = xb_ref[...]
        tf = wg_ref.shape[3]
        halves = [slice(0, tf // 2), slice(tf // 2, tf)]
        acts = []
        for sl in halves:
            gt = jnp.dot(xb, wg_ref[0, 0, :, sl], preferred_element_type=F32)
            up = jnp.dot(xb, wu_ref[0, 0, :, sl], preferred_element_type=F32)
            acts.append((_silu(gt) * up).astype(BF16))
        acc_ref[...] += sum(jnp.dot(act, wd_ref[0, sl, :], preferred_element_type=F32)
                            for act, sl in zip(acts, halves))

    last = f == pl.num_programs(1) - 1

    @pl.when(jnp.logical_and(used, last))
    def _():
        _to_token_tiles(o_ref, acc_ref[...], tm)

    @pl.when(jnp.logical_and(jnp.logical_not(used), last))
    def _():
        o_ref[...] = jnp.zeros_like(o_ref)


def _experts(x_tiles, nb, block_e, n_used, wg, wu, wd, tm):
    n_exp, nf, d, tf = wg.shape

    def xmap(i, f, be, nu):
        return (jnp.minimum(i, nu[0] - 1), 0)

    def fidx(i, f, nu):
        return jnp.where(i < nu[0], f, nf - 1)

    grid_spec = pltpu.PrefetchScalarGridSpec(
        num_scalar_prefetch=2,
        grid=(nb, nf),
        in_specs=[pl.BlockSpec((tm * SUBLANES, LANES), xmap),
                  pl.BlockSpec((1, 1, d, tf), lambda i, f, be, nu: (be[i], fidx(i, f, nu), 0, 0)),
                  pl.BlockSpec((1, 1, d, tf), lambda i, f, be, nu: (be[i], fidx(i, f, nu), 0, 0)),
                  pl.BlockSpec((1, tf, d), lambda i, f, be, nu: (be[i], fidx(i, f, nu), 0))],
        out_specs=pl.BlockSpec((tm * SUBLANES, LANES), lambda i, f, be, nu: (i, 0)),
        scratch_shapes=[pltpu.VMEM((tm, d), F32), pltpu.VMEM((tm, d), BF16)])
    return pl.pallas_call(
        functools.partial(_expert_kernel, tm=tm),
        grid_spec=grid_spec,
        out_shape=jax.ShapeDtypeStruct((nb * tm * SUBLANES, LANES), F32),
        compiler_params=pltpu.CompilerParams(dimension_semantics=("arbitrary", "arbitrary"),
                                             vmem_limit_bytes=VMEM_LIMIT_BYTES),
        name="moe_experts",
    )(block_e, n_used, x_tiles, wg, wu, wd)


def _combine_kernel(cur_ref, nxt_ref, y_hbm, w_ref, x_ref, gate_ref, g_ref, b_ref, o_ref, ybuf_ref, sems,
                    *, tc, alpha):
    i = pl.program_id(0)
    nt = TOP_K * tc

    def fetch(idx_ref, slot):
        buf, sem = ybuf_ref.at[slot], sems.at[slot]
        _issue_tile_copies(nt, lambda g, u: (_tile_at(y_hbm, idx_ref[0, 0, g * _COPY_UNROLL + u]),
                                             _tile_at(buf, g * _COPY_UNROLL + u), sem))

    @pl.when(i == 0)
    def _():
        fetch(cur_ref, 0)

    @pl.when(i + 1 < pl.num_programs(0))
    def _():
        fetch(nxt_ref, (i + 1) % 2)

    slot = i % 2
    yb = ybuf_ref.at[slot]
    pltpu.make_async_copy(y_hbm.at[pl.ds(0, nt * SUBLANES)], yb, sems.at[slot]).wait()
    w = w_ref[...]
    y = (w[:, 0:1] * _from_token_tiles(yb, tc, 0, TOP_K)
         + w[:, 1:2] * _from_token_tiles(yb, tc, 1, TOP_K))
    r = alpha * x_ref[0] + (1.0 + gate_ref[0]) * y
    o_ref[0] = _layer_norm_rows(r, g_ref[...], b_ref[...])


def _combine(y_tiles, dest, top_w, x, gate2, ln_g, ln_b, alpha, tc):
    bsz, seq, d = x.shape
    nj = seq // tc
    steps = bsz * nj
    nt = TOP_K * tc
    dest_blocks = dest.astype(jnp.int32).reshape(steps, 1, nt)
    return pl.pallas_call(
        functools.partial(_combine_kernel, tc=tc, alpha=alpha),
        grid=(steps,),
        in_specs=[pl.BlockSpec((1, 1, nt), lambda i: (i, 0, 0), memory_space=pltpu.SMEM),
                  pl.BlockSpec((1, 1, nt), lambda i: (jnp.minimum(i + 1, steps - 1), 0, 0),
                               memory_space=pltpu.SMEM),
                  pl.BlockSpec(memory_space=pl.ANY),
                  pl.BlockSpec((tc, LANES), lambda i: (i, 0)),
                  pl.BlockSpec((1, tc, d), lambda i: (i // nj, i % nj, 0)),
                  pl.BlockSpec((1, 1, d), lambda i: (i // nj, 0, 0)), _const_spec((1, d)), _const_spec((1, d))],
        out_specs=pl.BlockSpec((1, tc, d), lambda i: (i // nj, i % nj, 0)),
        out_shape=jax.ShapeDtypeStruct((bsz, seq, d), F32),
        scratch_shapes=[pltpu.VMEM((2, nt * SUBLANES, LANES), F32), pltpu.SemaphoreType.DMA((2,))],
        compiler_params=pltpu.CompilerParams(dimension_semantics=("arbitrary",),
                                             vmem_limit_bytes=VMEM_LIMIT_BYTES),
        name="moe_combine",
    )(dest_blocks, dest_blocks, y_tiles, top_w, x, gate2, ln_g, ln_b)


def _moe(oa, oc, lw, x, gate1, scale2, shift2, gate2, wr2, stri, wg, wu, wd, ln_g, ln_b, alpha,
         tm_tok, tm, tc):
    bsz, seq, d = x.shape
    n_tok = bsz * seq
    n_asg = n_tok * TOP_K
    x1 = _outproj(oa, oc, lw, x, gate1, alpha, tm_tok)
    h_tiles, idx128, w128, cnt = _router(x1, scale2, shift2, wr2, stri, tm_tok)
    counts = cnt[0, :N_EXPERTS].astype(jnp.int32)
    padded = (counts + tm - 1) // tm * tm
    pad_end = jnp.cumsum(padded)
    pad_start = pad_end - padded
    e_ids, ranks = idx128[:, 0:TOP_K], idx128[:, TOP_K:2 * TOP_K]
    start_of = jnp.sum(jnp.where(e_ids[:, :, None] == jnp.arange(N_EXPERTS, dtype=jnp.int32), pad_start, 0), axis=-1)
    dest = (start_of + ranks).reshape(n_asg)
    nb = n_asg // tm + N_EXPERTS
    n_rows = nb * tm
    n_fill = n_rows - n_asg
    fill_end = jnp.cumsum(padded - counts)
    slot = jnp.arange(n_fill, dtype=jnp.int32)
    slot_e = jnp.sum((slot[:, None] >= fill_end[None, :]).astype(jnp.int32), axis=1)
    in_group = slot_e < N_EXPERTS
    ge = jnp.minimum(slot_e, N_EXPERTS - 1)
    fill_dst = jnp.where(in_group, (pad_start + counts)[ge] + slot - (fill_end - (padded - counts))[ge],
                         pad_end[-1] + slot - fill_end[-1])
    x_tiles = _group_rows(h_tiles, dest, fill_dst, n_rows, tm_tok)
    block_row = jnp.arange(nb, dtype=jnp.int32) * tm
    block_e = jnp.minimum(jnp.sum((block_row[:, None] >= pad_end[None, :]).astype(jnp.int32), axis=1),
                          N_EXPERTS - 1)
    n_used = (pad_end[-1:] // tm).astype(jnp.int32)
    y_tiles = _experts(x_tiles, nb, block_e, n_used, wg, wu, wd, tm)
    return _combine(y_tiles, dest, w128, x1, gate2, ln_g, ln_b, alpha, tc)


def _layer_weights(layer, w_in, b_fgate, w_conv, q_norm_g, kv_norm_g, w_uq, w_ukv, head_norm_g, w_o,
                   ln1_g, ln1_b):
    d = w_in.shape[1]
    sizes = [FOX_W, FOX_W, FOX_W, H_FOX, CONV_W, CONV_W, CONV_W, Q_LORA, KV_LORA, QK_ROPE]
    offs = np.concatenate([[0], np.cumsum(sizes)])
    wi = w_in[layer]
    fq, fk, fv, fl, bg, cg, hin, cq, ckv, kr = [wi[:, offs[i]:offs[i + 1]] for i in range(len(sizes))]
    half = QK_ROPE // 2
    z64 = jnp.zeros((d, QK_NOPE), F32)
    z32 = jnp.zeros((d, LANES - QK_NOPE - QK_ROPE), F32)
    wa = jnp.concatenate([
        fq, fk, fv,
        jnp.pad(jnp.repeat(fl, _N_SPLIT, axis=1), ((0, 0), (0, LANES - _N_SPLIT * H_FOX))), bg, cg, hin, cq, ckv,
        jnp.concatenate([z64, kr, z32], axis=1),
        jnp.concatenate([z64, kr[:, half:], kr[:, :half], z32], axis=1)], axis=1)
    assert wa.shape[1] == _NA
    q3 = w_uq[layer].reshape(Q_LORA, H_MLA, QK_NOPE + QK_ROPE)
    q_main = jnp.pad(q3, ((0, 0), (0, 0), (0, LANES - QK_NOPE - QK_ROPE)))
    q_swap = jnp.concatenate([jnp.zeros((Q_LORA, H_MLA, QK_NOPE), F32), q3[:, :, QK_NOPE + half:],
                              q3[:, :, QK_NOPE:QK_NOPE + half],
                              jnp.zeros((Q_LORA, H_MLA, LANES - QK_NOPE - QK_ROPE), F32)], axis=2)
    wuq = jnp.concatenate([q_main.reshape(Q_LORA, H_MLA * LANES), q_swap.reshape(Q_LORA, H_MLA * LANES)],
                          axis=1).astype(BF16)
    kv3 = w_ukv[layer].reshape(KV_LORA, H_MLA, QK_NOPE + V_DIM)
    k_nope = jnp.pad(kv3[:, :, :QK_NOPE], ((0, 0), (0, 0), (0, LANES - QK_NOPE))).reshape(KV_LORA, H_MLA * LANES)
    wukv = jnp.concatenate([k_nope, kv3[:, :, QK_NOPE:].reshape(KV_LORA, MLA_W)], axis=1).astype(BF16)
    hg = head_norm_g[layer]
    wo = w_o[layer]
    c0, c1 = FOX_W, FOX_W + CONV_W
    return {
        "wa": wa, "wuq": wuq, "wukv": wukv,
        "qg": q_norm_g[layer].reshape(1, Q_LORA), "kvg": kv_norm_g[layer].reshape(1, KV_LORA),
        "bf": jnp.pad(jnp.repeat(b_fgate[layer], _N_SPLIT), (0, LANES - _N_SPLIT * H_FOX)).reshape(1, LANES),
        "wconv": jnp.pad(w_conv[layer], ((0, SUBLANES - CONV_K), (0, 0))),
        "hg_conv": hg[c0:c1].reshape(1, CONV_W),
        "hg_attn": jnp.concatenate([hg[:c0], hg[c1:]]).reshape(1, FOX_W + MLA_W),
        "wo_a": jnp.concatenate([wo[:c0], wo[c1:]], axis=0).astype(BF16),
        "wo_c": wo[c0:c1].astype(BF16),
        "ln1_g": ln1_g[layer].reshape(1, d), "ln1_b": ln1_b[layer].reshape(1, d),
    }


def _constants(seq, ts):
    pos = jnp.arange(seq, dtype=F32)
    inv_freq = ROPE_THETA ** (-jnp.arange(0, QK_ROPE, 2, dtype=F32) / QK_ROPE)
    ang = pos[:, None] * inv_freq[None, :]
    cos, sin = jnp.cos(ang), jnp.sin(ang)
    pad_r = jnp.zeros((seq, LANES - QK_NOPE - QK_ROPE), F32)
    ctab = jnp.concatenate([jnp.ones((seq, QK_NOPE), F32), cos, cos, pad_r], axis=1)
    stab = jnp.concatenate([jnp.zeros((seq, QK_NOPE), F32), -sin, sin, pad_r], axis=1)
    q_scale = (QK_NOPE + QK_ROPE) ** -0.5 * LOG2E
    lanes = np.arange(LANES)
    lm = np.zeros((SUBLANES, LANES), np.float32)
    for part in range(_N_SPLIT):
        lm[part] = (lanes % _N_SPLIT == part) & (lanes < _N_SPLIT * H_FOX)
    for row, base in ((3, HEAD_DIM), (5, 0)):
        lm[row] = (lanes >= base) & (lanes < base + _N_SPLIT)
        lm[row + 1] = (lanes >= base + _N_SPLIT) & (lanes < base + 2 * _N_SPLIT)

    def group_mean(n):
        gidx = np.arange(n) // HEAD_DIM
        return (gidx[:, None] == gidx[None, :]).astype(np.float32) / HEAD_DIM

    return {
        "cq": ctab * q_scale, "sq": stab * q_scale, "ck": ctab, "sk": stab,
        "tri": jnp.asarray(np.tril(np.ones((ts, ts), np.float32)), BF16),
        "stri": jnp.asarray(np.tril(np.ones((ts, ts), np.float32), k=-1), BF16),
        "lane_masks": jnp.asarray(lm),
        "gm256": jnp.asarray(group_mean(CONV_W), BF16), "gm128": jnp.asarray(group_mean(LANES), BF16),
    }


def _tile(n, pref):
    t = min(n, pref)
    assert n % t == 0, (n, pref)
    return t


def kernel(x, c, w_mod, b_mod, w_in, b_fgate, w_conv, q_norm_g, kv_norm_g, w_uq, w_ukv, head_norm_g, w_o, ln1_g, ln1_b, ln2_g, ln2_b, ffn_w_gate, ffn_w_up, ffn_w_down, router_w, exp_w_gate, exp_w_up, exp_w_down):
    bsz, seq, d = x.shape
    depth = w_mod.shape[0]
    assert d == D_MODEL and bsz <= SUBLANES
    alpha = (2 * depth) ** 0.25
    ts = _tile(seq, 512)
    tq = _tile(seq, 512)
    tm_e = 512
    tf_e = 512
    tc = _tile(seq, 512)
    assert (bsz * seq * TOP_K) % tm_e == 0 and exp_w_gate.shape[-1] % tf_e == 0

    consts = _constants(seq, ts)
    c_pad = jnp.pad(c, ((0, SUBLANES - bsz), (0, 0)))
    mod = _modulation(c_pad, w_mod, b_mod)[:, :bsz, :]

    for layer in range(depth):
        m6 = mod[layer].reshape(bsz, 6, 1, d)
        shift1, scale1, gate1, shift2, scale2, gate2 = [m6[:, i] for i in range(6)]
        lw = _layer_weights(layer, w_in, b_fgate, w_conv, q_norm_g, kv_norm_g, w_uq, w_ukv, head_norm_g,
                            w_o, ln1_g, ln1_b)
        qp, kp, vt, oc = _inproj(x, scale1, shift1, lw, consts, ts)
        oa = _attention(qp, kp, vt, lw["hg_attn"], consts["gm128"], tq, 6)
        j = layer // 2
        g2, b2 = ln2_g[layer].reshape(1, d), ln2_b[layer].reshape(1, d)
        if layer % 2 == 0:
            x = _ffn_dense(oa, oc, lw, x, gate1, scale2, shift2, gate2, ffn_w_gate[j].astype(BF16),
                           ffn_w_up[j].astype(BF16), ffn_w_down[j].astype(BF16), g2, b2, alpha, ts)
        else:
            wr = jnp.pad(router_w[j], ((0, 0), (0, LANES - N_EXPERTS)))
            wr_hi = wr.astype(BF16)
            wr_lo = (wr - wr_hi.astype(F32)).astype(BF16)

            def chunked(w):
                n_exp, _, dff = w.shape
                return w.astype(BF16).reshape(n_exp, d, dff // tf_e, tf_e).transpose(0, 2, 1, 3)

            x = _moe(oa, oc, lw, x, gate1, scale2, shift2, gate2, jnp.stack([wr_hi, wr_lo]), consts["stri"],
                     chunked(exp_w_gate[j]), chunked(exp_w_up[j]), exp_w_down[j].astype(BF16),
                     g2, b2, alpha, ts, tm_e, tc)
    return x
```

```python
import functools

import numpy as np
import jax
import jax.numpy as jnp
from jax import lax
from jax.experimental import pallas as pl
from jax.experimental.pallas import tpu as pltpu

F32 = jnp.float32
BF16 = jnp.bfloat16

D_MODEL = 1024
HEAD_DIM = 64
H_FOX = 6
FOX_W = H_FOX * HEAD_DIM
CONV_W = 256
CONV_K = 3
H_MLA = 6
Q_LORA = 256
KV_LORA = 256
QK_NOPE = 64
QK_ROPE = 32
V_DIM = 64
MLA_W = H_MLA * V_DIM
N_HEADS = H_FOX + H_MLA
ROPE_THETA = 10000.0
N_EXPERTS = 8
TOP_K = 2
LN_EPS = 1e-5
RMS_EPS = 1e-6

LANES = 128
SUBLANES = 8
VMEM_LIMIT_BYTES = 56 * 1024 * 1024

_HW = 6 * LANES
_QF0, _KF0, _VF0, _FL0 = 0, FOX_W, 2 * FOX_W, 3 * FOX_W
_BG0 = _FL0 + LANES
_CG0, _HI0, _CQ0, _CKV0 = _BG0 + CONV_W, _BG0 + 2 * CONV_W, _BG0 + 3 * CONV_W, _BG0 + 3 * CONV_W + Q_LORA
_KR0 = _CKV0 + KV_LORA
_KRS0 = _KR0 + LANES
_NA = _KRS0 + LANES
LOG2E = 1.4426950408889634
VT_ROWS = LANES
_N_SPLIT = 3


def _const_spec(shape):
    zeros = (0,) * len(shape)
    return pl.BlockSpec(shape, lambda *_: zeros, pipeline_mode=pl.Buffered(1))


def _silu(v):
    return v * (1.0 / (1.0 + jnp.exp(-v)))


def _split3(v):
    hi = v.astype(BF16)
    r1 = v - hi.astype(F32)
    mid = r1.astype(BF16)
    lo = (r1 - mid.astype(F32)).astype(BF16)
    return hi, mid, lo


def _group_mean_sq(v, gmat_ref):
    sq = v * v
    hi = sq.astype(BF16)
    lo = (sq - hi.astype(F32)).astype(BF16)
    g = gmat_ref[...]
    return (jnp.dot(hi, g, preferred_element_type=F32) + jnp.dot(lo, g, preferred_element_type=F32))


def _layer_norm_rows(r, g, b):
    mu = jnp.mean(r, axis=-1, keepdims=True)
    rc = r - mu
    var = jnp.mean(rc * rc, axis=-1, keepdims=True)
    return rc * lax.rsqrt(var + LN_EPS) * g + b


def _mod_kernel(c_ref, w_ref, b_ref, o_ref):
    act = _silu(c_ref[...]).astype(BF16)
    o_ref[0] = jnp.dot(act, w_ref[0].astype(BF16), preferred_element_type=F32) + b_ref[0]


def _modulation(c_pad, w_mod, b_mod):
    depth, d, n = w_mod.shape
    tn = 1024
    return pl.pallas_call(
        _mod_kernel,
        grid=(depth, n // tn),
        in_specs=[pl.BlockSpec((SUBLANES, d), lambda l, j: (0, 0)),
                  pl.BlockSpec((1, d, tn), lambda l, j: (l, 0, j)),
                  pl.BlockSpec((1, 1, tn), lambda l, j: (l, 0, j))],
        out_specs=pl.BlockSpec((1, SUBLANES, tn), lambda l, j: (l, 0, j)),
        out_shape=jax.ShapeDtypeStruct((depth, SUBLANES, n), F32),
        compiler_params=pltpu.CompilerParams(dimension_semantics=("arbitrary", "arbitrary")),
        name="modulation",
    )(c_pad, w_mod, b_mod.reshape(depth, 1, n))


def _inproj_kernel(x_ref, sc_ref, sh_ref, wa_ref, cq_ref, sq_ref, ck_ref, sk_ref, wuq_ref, wukv_ref,
                   qg_ref, kvg_ref, bf_ref, wconv_ref, hgc_ref, tri_ref, lm_ref, gm_ref,
                   qp_ref, kp_ref, vt_ref, oc_ref, fcarry_ref, ubuf_ref, wab_ref, *, ts):
    @pl.when(jnp.logical_and(pl.program_id(0) == 0, pl.program_id(1) == 0))
    def _():
        for c in range(_NA // LANES):
            sl = slice(c * LANES, (c + 1) * LANES)
            wab_ref[:, sl] = wa_ref[:, sl].astype(BF16)

    @pl.when(pl.program_id(1) == 0)
    def _():
        fcarry_ref[...] = jnp.zeros_like(fcarry_ref)
        ubuf_ref[pl.ds(0, SUBLANES), :] = jnp.zeros((SUBLANES, CONV_W), F32)

    hb = (x_ref[0] * (1.0 + sc_ref[0]) + sh_ref[0]).astype(BF16)

    def proj(lo, hi):
        return jnp.dot(hb, wab_ref[:, lo:hi], preferred_element_type=F32)

    def put_values_t(head, v_half_t):
        vt_ref[0, head, 0:V_DIM, :] = v_half_t.astype(BF16)
        vt_ref[0, head, V_DIM:VT_ROWS, :] = jnp.ones((VT_ROWS - V_DIM, ts), BF16)

    z_fl = proj(_FL0, _FL0 + LANES)
    z_cq = proj(_CQ0, _CQ0 + Q_LORA)
    z_ckv = proj(_CKV0, _CKV0 + KV_LORA)
    z_cg = proj(_CG0, _CG0 + CONV_W)
    z_hi = proj(_HI0, _HI0 + CONV_W)
    z_bg = proj(_BG0, _BG0 + CONV_W)
    zq = proj(_QF0, _QF0 + FOX_W) * (HEAD_DIM ** -0.5 * LOG2E)
    zk = proj(_KF0, _KF0 + FOX_W)
    zv = proj(_VF0, _VF0 + FOX_W)
    z_kr = proj(_KR0, _KR0 + LANES)
    z_krs = proj(_KRS0, _KRS0 + LANES)

    a = z_fl + bf_ref[...]
    logf = jnp.minimum(a, 0.0) - jnp.log1p(jnp.exp(-jnp.abs(a)))
    tri = tri_ref[...]
    csum = sum(jnp.dot(tri, part, preferred_element_type=F32) for part in _split3(logf))
    fcum = fcarry_ref[...] + csum
    fcarry_ref[...] = fcum[ts - 1:ts, :]
    f_hi, f_mid, f_lo = _split3(fcum * LOG2E)
    lm = lm_ref[...]
    fparts = (f_hi.astype(F32) * lm[0:1, :] + f_mid.astype(F32) * lm[1:2, :]
              + f_lo.astype(F32) * lm[2:3, :])
    lane = lax.broadcasted_iota(jnp.int32, (ts, LANES), 1)
    low_half = lane < HEAD_DIM
    for h in range(H_FOX):
        blk = slice((h // 2) * LANES, (h // 2 + 1) * LANES)
        if h % 2 == 0:
            base, f_mask, one_mask, keep = HEAD_DIM, lm[3:4, :], lm[4:5, :], low_half
        else:
            base, f_mask, one_mask, keep = 0, lm[5:6, :], lm[6:7, :], jnp.logical_not(low_half)
        f_at_q = pltpu.roll(fparts, (base - _N_SPLIT * h) % LANES, axis=1)
        f_at_k = pltpu.roll(fparts, (base + _N_SPLIT - _N_SPLIT * h) % LANES, axis=1)
        qp_ref[0, h] = jnp.where(keep, zq[:, blk], f_at_q * f_mask + one_mask).astype(BF16)
        kp_ref[0, h] = jnp.where(keep, zk[:, blk], f_mask - f_at_k * one_mask).astype(BF16)
    for j in range(H_FOX // 2):
        vt = zv[:, j * LANES:(j + 1) * LANES].T
        put_values_t(2 * j, vt[0:V_DIM])
        put_values_t(2 * j + 1, vt[V_DIM:LANES])

    u = z_cg * z_hi
    ubuf_ref[pl.ds(SUBLANES, ts), :] = u
    u1 = ubuf_ref[pl.ds(SUBLANES - 1, ts), :]
    u2 = ubuf_ref[pl.ds(SUBLANES - 2, ts), :]
    ubuf_ref[pl.ds(0, SUBLANES), :] = u[ts - SUBLANES:ts, :]
    wc = wconv_ref[...]
    oc = z_bg * (wc[0:1, :] * u2 + wc[1:2, :] * u1 + wc[2:3, :] * u)
    ocn = oc * lax.rsqrt(_group_mean_sq(oc, gm_ref) + RMS_EPS) * hgc_ref[...]
    oc_ref[0] = ocn.astype(BF16)

    def rms(v, g):
        return (v * lax.rsqrt(jnp.mean(v * v, axis=-1, keepdims=True) + RMS_EPS) * g).astype(BF16)

    cqn = rms(z_cq, qg_ref[...])
    qm = jnp.dot(cqn, wuq_ref[:, 0:_HW], preferred_element_type=F32)
    qs = jnp.dot(cqn, wuq_ref[:, _HW:2 * _HW], preferred_element_type=F32)
    kvn = rms(z_ckv, kvg_ref[...])
    kn = jnp.dot(kvn, wukv_ref[:, 0:_HW], preferred_element_type=F32)
    vm = jnp.dot(kvn, wukv_ref[:, _HW:_HW + MLA_W], preferred_element_type=F32)
    krr = z_kr * ck_ref[...] + z_krs * sk_ref[...]
    cq, sq = cq_ref[...], sq_ref[...]
    for h in range(H_MLA):
        sl = slice(h * LANES, (h + 1) * LANES)
        qp_ref[0, H_FOX + h] = (qm[:, sl] * cq + qs[:, sl] * sq).astype(BF16)
        kp_ref[0, H_FOX + h] = (kn[:, sl] + krr).astype(BF16)
    for j in range(H_MLA // 2):
        vt = vm[:, j * LANES:(j + 1) * LANES].T
        put_values_t(H_FOX + 2 * j, vt[0:V_DIM])
        put_values_t(H_FOX + 2 * j + 1, vt[V_DIM:LANES])


def _inproj(x, scale1, shift1, lw, consts, ts):
    bsz, seq, d = x.shape
    kern = functools.partial(_inproj_kernel, ts=ts)
    row = lambda b, j: (b, 0, 0)
    tab = pl.BlockSpec((ts, LANES), lambda b, j: (j, 0))
    return pl.pallas_call(
        kern,
        grid=(bsz, seq // ts),
        in_specs=[pl.BlockSpec((1, ts, d), lambda b, j: (b, j, 0)),
                  pl.BlockSpec((1, 1, d), row), pl.BlockSpec((1, 1, d), row),
                  _const_spec((d, _NA)), tab, tab, tab, tab,
                  _const_spec((Q_LORA, 2 * _HW)), _const_spec((KV_LORA, _HW + MLA_W)),
                  _const_spec((1, Q_LORA)), _const_spec((1, KV_LORA)), _const_spec((1, LANES)),
                  _const_spec((SUBLANES, CONV_W)), _const_spec((1, CONV_W)),
                  _const_spec((ts, ts)), _const_spec((SUBLANES, LANES)), _const_spec((CONV_W, CONV_W))],
        out_specs=[pl.BlockSpec((1, N_HEADS, ts, LANES), lambda b, j: (b, 0, j, 0)),
                   pl.BlockSpec((1, N_HEADS, ts, LANES), lambda b, j: (b, 0, j, 0)),
                   pl.BlockSpec((1, N_HEADS, VT_ROWS, ts), lambda b, j: (b, 0, 0, j)),
                   pl.BlockSpec((1, ts, CONV_W), lambda b, j: (b, j, 0))],
        out_shape=[jax.ShapeDtypeStruct((bsz, N_HEADS, seq, LANES), BF16),
                   jax.ShapeDtypeStruct((bsz, N_HEADS, seq, LANES), BF16),
                   jax.ShapeDtypeStruct((bsz, N_HEADS, VT_ROWS, seq), BF16),
                   jax.ShapeDtypeStruct((bsz, seq, CONV_W), BF16)],
        scratch_shapes=[pltpu.VMEM((1, LANES), F32), pltpu.VMEM((ts + SUBLANES, CONV_W), F32),
                        pltpu.VMEM((d, _NA), BF16)],
        compiler_params=pltpu.CompilerParams(dimension_semantics=("arbitrary", "arbitrary"),
                                             vmem_limit_bytes=VMEM_LIMIT_BYTES),
        name="inproj",
    )(x, scale1, shift1, lw["wa"], consts["cq"], consts["sq"], consts["ck"], consts["sk"],
      lw["wuq"], lw["wukv"], lw["qg"], lw["kvg"], lw["bf"], lw["wconv"], lw["hg_conv"],
      consts["tri"], consts["lane_masks"], consts["gm256"])


def _attn_kernel(qi_ref, kj_ref, q_ref, k_ref, vt_ref, g_ref, gm_ref, o_ref, m_ref, acc_ref, s_ref, *, nh):
    step = pl.program_id(2)
    qi = qi_ref[step]
    kj = kj_ref[step]

    @pl.when(kj == 0)
    def _():
        m_ref[...] = jnp.full(m_ref.shape, -jnp.inf, F32)
        acc_ref[...] = jnp.zeros_like(acc_ref)

    def update(masked):
        if masked:
            tile = (q_ref.shape[2], q_ref.shape[2])
            causal = lax.broadcasted_iota(jnp.int32, tile, 0) <= lax.broadcasted_iota(jnp.int32, tile, 1)

        def scores(h):
            st = lax.dot_general(k_ref[0, h], q_ref[0, h], (((1,), (1,)), ((), ())),
                                 preferred_element_type=F32)
            if masked:
                st = jnp.where(causal, st, -jnp.inf)
            s_ref[h] = st

        for h in range(nh):
            scores(h)
        pts, alphas = [], []
        for h in range(nh):
            m_prev = m_ref[h]
            m_new = jnp.maximum(m_prev, jnp.max(s_ref[h], axis=0, keepdims=True))
            pts.append(jnp.exp2(s_ref[h] - m_new).astype(BF16))
            alphas.append(jnp.exp2(m_prev - m_new))
            m_ref[h] = m_new
        for h in range(nh):
            acc_ref[h] = alphas[h] * acc_ref[h] + jnp.dot(vt_ref[0, h], pts[h], preferred_element_type=F32)

    @pl.when(kj < qi)
    def _():
        update(False)

    @pl.when(kj == qi)
    def _():
        update(True)
        for pair in range(nh // 2):
            ot = jnp.concatenate([acc_ref[h, 0:V_DIM, :] / acc_ref[h, V_DIM:2 * V_DIM, :]
                                  for h in (2 * pair, 2 * pair + 1)], axis=0)
            o = ot.T
            sl = slice(pair * LANES, (pair + 1) * LANES)
            on = o * lax.rsqrt(_group_mean_sq(o, gm_ref) + RMS_EPS) * g_ref[:, sl]
            o_ref[0, :, sl] = on.astype(BF16)


def _attention(qp, kp, vt, g_attn, gm128, tq, nh):
    bsz, n_heads, seq, _ = qp.shape
    nq = seq // tq
    ow = (nh // 2) * LANES
    qi_tab = np.concatenate([np.full(i + 1, i) for i in range(nq)]).astype(np.int32)
    kj_tab = np.concatenate([np.arange(i + 1) for i in range(nq)]).astype(np.int32)
    grid_spec = pltpu.PrefetchScalarGridSpec(
        num_scalar_prefetch=2,
        grid=(bsz, n_heads // nh, len(qi_tab)),
        in_specs=[pl.BlockSpec((1, nh, tq, LANES), lambda b, p, s, qi, kj: (b, p, qi[s], 0)),
                  pl.BlockSpec((1, nh, tq, LANES), lambda b, p, s, qi, kj: (b, p, kj[s], 0)),
                  pl.BlockSpec((1, nh, VT_ROWS, tq), lambda b, p, s, qi, kj: (b, p, 0, kj[s])),
                  pl.BlockSpec((1, ow), lambda b, p, s, qi, kj: (0, p)),
                  _const_spec((LANES, LANES))],
        out_specs=pl.BlockSpec((1, tq, ow), lambda b, p, s, qi, kj: (b, qi[s], p)),
        scratch_shapes=[pltpu.VMEM((nh, 1, tq), F32), pltpu.VMEM((nh, VT_ROWS, tq), F32),
                        pltpu.VMEM((nh, tq, tq), F32)])
    return pl.pallas_call(
        functools.partial(_attn_kernel, nh=nh),
        grid_spec=grid_spec,
        out_shape=jax.ShapeDtypeStruct((bsz, seq, (n_heads // 2) * LANES), BF16),
        compiler_params=pltpu.CompilerParams(
            dimension_semantics=("arbitrary", "arbitrary", "arbitrary"),
            vmem_limit_bytes=VMEM_LIMIT_BYTES),
        name="attention",
    )(jnp.asarray(qi_tab), jnp.asarray(kj_tab), qp, kp, vt, g_attn, gm128)


def _mixer_out_rows(oa_ref, oc_ref, woa_ref, woc_ref, x_ref, gate_ref, g_ref, b_ref, alpha):
    y = (jnp.dot(oa_ref[0], woa_ref[...], preferred_element_type=F32)
         + jnp.dot(oc_ref[0], woc_ref[...], preferred_element_type=F32))
    r = alpha * x_ref[0] + (1.0 + gate_ref[0]) * y
    return _layer_norm_rows(r, g_ref[...], b_ref[...])


def _mixer_out_specs(oa, d, tm, act, row):
    wa_rows = oa.shape[-1]
    return [pl.BlockSpec((1, tm, wa_rows), act), pl.BlockSpec((1, tm, CONV_W), act),
            _const_spec((wa_rows, d)), _const_spec((CONV_W, d)),
            pl.BlockSpec((1, tm, d), act), pl.BlockSpec((1, 1, d), row), _const_spec((1, d)), _const_spec((1, d))]


def _outproj_kernel(oa_ref, oc_ref, woa_ref, woc_ref, x_ref, gate_ref, g_ref, b_ref, o_ref, *, alpha):
    o_ref[0] = _mixer_out_rows(oa_ref, oc_ref, woa_ref, woc_ref, x_ref, gate_ref, g_ref, b_ref, alpha)


def _outproj(oa, oc, lw, x, gate1, alpha, tm):
    bsz, seq, d = x.shape
    act = lambda b, j: (b, j, 0)
    return pl.pallas_call(
        functools.partial(_outproj_kernel, alpha=alpha),
        grid=(bsz, seq // tm),
        in_specs=_mixer_out_specs(oa, d, tm, act, lambda b, j: (b, 0, 0)),
        out_specs=pl.BlockSpec((1, tm, d), act),
        out_shape=jax.ShapeDtypeStruct((bsz, seq, d), F32),
        compiler_params=pltpu.CompilerParams(dimension_semantics=("arbitrary", "arbitrary"),
                                             vmem_limit_bytes=VMEM_LIMIT_BYTES),
        name="outproj",
    )(oa, oc, lw["wo_a"], lw["wo_c"], x, gate1, lw["ln1_g"], lw["ln1_b"])


def _ffn_kernel(oa_ref, oc_ref, woa_ref, woc_ref, xin_ref, gate1_ref, g1_ref, b1_ref,
                sc_ref, sh_ref, gate_ref, wg_ref, wu_ref, wd_ref, g_ref, b_ref, o_ref, *, alpha, tf):
    x = _mixer_out_rows(oa_ref, oc_ref, woa_ref, woc_ref, xin_ref, gate1_ref, g1_ref, b1_ref, alpha)
    hb = (x * (1.0 + sc_ref[0]) + sh_ref[0]).astype(BF16)
    chunks = [slice(c * tf, (c + 1) * tf) for c in range(wg_ref.shape[1] // tf)]
    acts = []
    for sl in chunks:
        gt = jnp.dot(hb, wg_ref[:, sl], preferred_element_type=F32)
        up = jnp.dot(hb, wu_ref[:, sl], preferred_element_type=F32)
        acts.append((_silu(gt) * up).astype(BF16))
    acc = sum(jnp.dot(act, wd_ref[sl, :], preferred_element_type=F32) for act, sl in zip(acts, chunks))
    r = alpha * x + (1.0 + gate_ref[0]) * acc
    o_ref[0] = _layer_norm_rows(r, g_ref[...], b_ref[...])


def _ffn_dense(oa, oc, lw, x, gate1, scale2, shift2, gate2, wg, wu, wd, ln_g, ln_b, alpha, tm):
    bsz, seq, d = x.shape
    dff = wg.shape[1]
    tf = dff // 2 if (dff // 2) % LANES == 0 else dff
    nj = seq // tm
    act = lambda b, j: (b, j, 0)
    row = lambda b, j: (b, 0, 0)
    return pl.pallas_call(
        functools.partial(_ffn_kernel, alpha=alpha, tf=tf),
        grid=(bsz, nj),
        in_specs=_mixer_out_specs(oa, d, tm, act, row) + [
            pl.BlockSpec((1, 1, d), row), pl.BlockSpec((1, 1, d), row), pl.BlockSpec((1, 1, d), row),
            _const_spec((d, dff)), _const_spec((d, dff)), _const_spec((dff, d)),
            _const_spec((1, d)), _const_spec((1, d))],
        out_specs=pl.BlockSpec((1, tm, d), act),
        out_shape=jax.ShapeDtypeStruct((bsz, seq, d), F32),
        compiler_params=pltpu.CompilerParams(dimension_semantics=("arbitrary", "arbitrary"),
                                             vmem_limit_bytes=VMEM_LIMIT_BYTES),
        name="ffn_dense",
    )(oa, oc, lw["wo_a"], lw["wo_c"], x, gate1, lw["ln1_g"], lw["ln1_b"],
      scale2, shift2, gate2, wg, wu, wd, ln_g, ln_b)


def _to_token_tiles(ref, v, n):
    for c in range(v.shape[1] // LANES):
        ref[pl.ds(c, n, stride=SUBLANES), :] = v[:, c * LANES:(c + 1) * LANES]


def _from_token_tiles(ref, n, first=0, tiles_per_row=1):
    stride = tiles_per_row * SUBLANES
    return jnp.concatenate([ref[pl.ds(first * SUBLANES + c, n, stride=stride), :] for c in range(SUBLANES)],
                           axis=1)


def _router_kernel(x_ref, sc_ref, sh_ref, wr_ref, stri_ref, h_ref, idx_ref, w_ref, cnt_ref, carry_ref, *, tm):
    @pl.when(pl.program_id(0) == 0)
    def _():
        carry_ref[...] = jnp.zeros_like(carry_ref)

    h = x_ref[0] * (1.0 + sc_ref[0]) + sh_ref[0]
    _to_token_tiles(h_ref, h, tm)
    h_hi = h.astype(BF16)
    h_lo = (h - h_hi.astype(F32)).astype(BF16)
    w_hi, w_lo = wr_ref[0], wr_ref[1]
    logits = (jnp.dot(h_hi, w_hi, preferred_element_type=F32)
              + jnp.dot(h_hi, w_lo, preferred_element_type=F32)
              + jnp.dot(h_lo, w_hi, preferred_element_type=F32))
    lane = lax.broadcasted_iota(jnp.int32, logits.shape, 1).astype(F32)
    neg = -jnp.inf
    lg = jnp.where(lane < N_EXPERTS, logits, neg)
    m1 = jnp.max(lg, axis=-1, keepdims=True)
    i1 = jnp.min(jnp.where(lg == m1, lane, float(LANES)), axis=-1, keepdims=True)
    lg2 = jnp.where(lane == i1, neg, lg)
    m2 = jnp.max(lg2, axis=-1, keepdims=True)
    i2 = jnp.min(jnp.where(lg2 == m2, lane, float(LANES)), axis=-1, keepdims=True)
    e2 = jnp.exp(m2 - m1)
    denom = 1.0 + e2
    w_ref[...] = jnp.where(lane == 0.0, 1.0 / denom, e2 / denom)
    first, second = lane == i1, lane == i2
    chosen = jnp.logical_or(first, second).astype(F32)
    before = carry_ref[...] + jnp.dot(stri_ref[...], chosen.astype(BF16), preferred_element_type=F32)
    rank1 = jnp.sum(jnp.where(first, before, 0.0), axis=-1, keepdims=True)
    rank2 = jnp.sum(jnp.where(second, before, 0.0), axis=-1, keepdims=True)
    idx_ref[...] = jnp.where(lane == 0.0, i1, jnp.where(lane == 1.0, i2, jnp.where(lane == 2.0, rank1, rank2))
                             ).astype(jnp.int32)
    total = carry_ref[...] + jnp.sum(chosen, axis=0, keepdims=True)
    carry_ref[...] = total
    cnt_ref[...] = jnp.broadcast_to(total, cnt_ref.shape)


def _router(x, scale2, shift2, wr2, stri, tm):
    bsz, seq, d = x.shape
    assert d == SUBLANES * LANES and bsz * seq * TOP_K < 2 ** 24
    nj = seq // tm
    n_tiles = bsz * nj
    row = lambda i: (i // nj, 0, 0)
    tok = lambda i: (i, 0)
    return pl.pallas_call(
        functools.partial(_router_kernel, tm=tm),
        grid=(n_tiles,),
        in_specs=[pl.BlockSpec((1, tm, d), lambda i: (i // nj, i % nj, 0)),
                  pl.BlockSpec((1, 1, d), row), pl.BlockSpec((1, 1, d), row),
                  _const_spec((2, d, LANES)), _const_spec((tm, tm))],
        out_specs=[pl.BlockSpec((tm * SUBLANES, LANES), tok),
                   pl.BlockSpec((tm, LANES), tok), pl.BlockSpec((tm, LANES), tok),
                   pl.BlockSpec((SUBLANES, LANES), lambda i: (0, 0))],
        out_shape=[jax.ShapeDtypeStruct((n_tiles * tm * SUBLANES, LANES), F32),
                   jax.ShapeDtypeStruct((bsz * seq, LANES), jnp.int32),
                   jax.ShapeDtypeStruct((bsz * seq, LANES), F32),
                   jax.ShapeDtypeStruct((SUBLANES, LANES), F32)],
        scratch_shapes=[pltpu.VMEM((1, LANES), F32)],
        compiler_params=pltpu.CompilerParams(dimension_semantics=("arbitrary",),
                                             vmem_limit_bytes=VMEM_LIMIT_BYTES),
        name="router",
    )(x, scale2, shift2, wr2, stri)


_COPY_UNROLL = 8


def _tile_at(ref, t):
    return ref.at[pl.ds(pl.multiple_of(t * SUBLANES, SUBLANES), SUBLANES)]


def _issue_tile_copies(n, copy_of):
    def group(g, carry):
        copies = [copy_of(g, u) for u in range(_COPY_UNROLL)]
        for u, (src, dst, sem) in enumerate(copies):
            pltpu.make_async_copy(src, dst, sem).start(priority=u % 2)
        return carry

    assert n % _COPY_UNROLL == 0
    lax.fori_loop(0, n // _COPY_UNROLL, group, 0)


def _group_kernel(di_ref, fi_ref, src_ref, dst_hbm, zero_ref, sem, *, tb, nf):
    zero_ref[...] = jnp.zeros_like(zero_ref)
    per_group = _COPY_UNROLL // TOP_K
    _issue_tile_copies(TOP_K * tb, lambda g, u: (_tile_at(src_ref, g * per_group + u // TOP_K),
                                                 _tile_at(dst_hbm, di_ref[0, 0, g * _COPY_UNROLL + u]), sem))
    _issue_tile_copies(nf, lambda g, u: (zero_ref, _tile_at(dst_hbm, fi_ref[0, 0, g * _COPY_UNROLL + u]), sem))
    block = pl.ds(0, tb * SUBLANES)
    for _ in range(TOP_K):
        pltpu.make_async_copy(src_ref, dst_hbm.at[block], sem).wait()

    def wait_fill(r, carry):
        pltpu.make_async_copy(zero_ref, _tile_at(dst_hbm, 0), sem).wait()
        return carry

    lax.fori_loop(0, nf, wait_fill, 0)


def _group_rows(h_tiles, dest, fill_dst, n_rows, tb):
    n_asg = dest.shape[0]
    steps = n_asg // (TOP_K * tb)
    nf = fill_dst.shape[0] // steps
    assert steps * TOP_K * tb == n_asg and steps * nf == fill_dst.shape[0]
    return pl.pallas_call(
        functools.partial(_group_kernel, tb=tb, nf=nf),
        grid=(steps,),
        in_specs=[pl.BlockSpec((1, 1, TOP_K * tb), lambda i: (i, 0, 0), memory_space=pltpu.SMEM),
                  pl.BlockSpec((1, 1, nf), lambda i: (i, 0, 0), memory_space=pltpu.SMEM),
                  pl.BlockSpec((tb * SUBLANES, LANES), lambda i: (i, 0))],
        out_specs=pl.BlockSpec(memory_space=pl.ANY),
        out_shape=jax.ShapeDtypeStruct((n_rows * SUBLANES, LANES), h_tiles.dtype),
        scratch_shapes=[pltpu.VMEM((SUBLANES, LANES), h_tiles.dtype), pltpu.SemaphoreType.DMA(())],
        compiler_params=pltpu.CompilerParams(dimension_semantics=("arbitrary",), has_side_effects=True),
        name="moe_group",
    )(dest.astype(jnp.int32).reshape(steps, 1, TOP_K * tb), fill_dst.astype(jnp.int32).reshape(steps, 1, nf),
      h_tiles)


def _expert_kernel(be_ref, nu_ref, nh_ref, x_ref, wg_ref, wu_ref, wd_ref, o_ref, acc_ref, xb_ref, *, tm):
    i = pl.program_id(0)
    f = pl.program_id(1)
    used = i < nu_ref[0]
    whole = jnp.logical_and(used, nh_ref[i] == 2)
    first_half_only = jnp.logical_and(used, nh_ref[i] == 1)

    @pl.when(jnp.logical_and(used, f == 0))
    def _():
        xb_ref[...] = _from_token_tiles(x_ref, tm).astype(BF16)
        acc_ref[...] = jnp.zeros_like(acc_ref)

    def swiglu_step(rows):
        xb = xb_ref[0:rows, :]
        tf = wg_ref.shape[2]
        halves = [slice(0, tf // 2), slice(tf // 2, tf)]
        acts = []
        for sl in halves:
            gt = jnp.dot(xb, wg_ref[0, :, sl], preferred_element_type=F32)
            up = jnp.dot(xb, wu_ref[0, :, sl], preferred_element_type=F32)
            acts.append((_silu(gt) * up).astype(BF16))
        acc_ref[0:rows, :] += sum(jnp.dot(act, wd_ref[0, sl, :], preferred_element_type=F32)
                                  for act, sl in zip(acts, halves))

    pl.when(whole)(functools.partial(swiglu_step, tm))
    pl.when(first_half_only)(functools.partial(swiglu_step, tm // 2))

    last = f == pl.num_programs(1) - 1

    @pl.when(jnp.logical_and(used, last))
    def _():
        _to_token_tiles(o_ref, acc_ref[...], tm)

    @pl.when(jnp.logical_and(jnp.logical_not(used), last))
    def _():
        o_ref[...] = jnp.zeros_like(o_ref)


def _experts(x_tiles, nb, block_e, n_used, n_halves, wg, wu, wd, tm, tf):
    n_exp, d, dff = wg.shape
    nf = dff // tf

    def xmap(i, f, be, nu, nh):
        return (jnp.minimum(i, nu[0] - 1), 0)

    def fidx(i, f, nu):
        return jnp.where(i < nu[0], f, nf - 1)

    grid_spec = pltpu.PrefetchScalarGridSpec(
        num_scalar_prefetch=3,
        grid=(nb, nf),
        in_specs=[pl.BlockSpec((tm * SUBLANES, LANES), xmap),
                  pl.BlockSpec((1, d, tf), lambda i, f, be, nu, nh: (be[i], 0, fidx(i, f, nu))),
                  pl.BlockSpec((1, d, tf), lambda i, f, be, nu, nh: (be[i], 0, fidx(i, f, nu))),
                  pl.BlockSpec((1, tf, d), lambda i, f, be, nu, nh: (be[i], fidx(i, f, nu), 0))],
        out_specs=pl.BlockSpec((tm * SUBLANES, LANES), lambda i, f, be, nu, nh: (i, 0)),
        scratch_shapes=[pltpu.VMEM((tm, d), F32), pltpu.VMEM((tm, d), BF16)])
    return pl.pallas_call(
        functools.partial(_expert_kernel, tm=tm),
        grid_spec=grid_spec,
        out_shape=jax.ShapeDtypeStruct((nb * tm * SUBLANES, LANES), F32),
        compiler_params=pltpu.CompilerParams(dimension_semantics=("arbitrary", "arbitrary"),
                                             vmem_limit_bytes=VMEM_LIMIT_BYTES),
        name="moe_experts",
    )(block_e, n_used, n_halves, x_tiles, wg, wu, wd)


def _combine_kernel(cur_ref, nxt_ref, y_hbm, w_ref, x_ref, gate_ref, g_ref, b_ref, o_ref, ybuf_ref, sems,
                    *, tc, alpha):
    i = pl.program_id(0)
    nt = TOP_K * tc

    def fetch(idx_ref, slot):
        buf, sem = ybuf_ref.at[slot], sems.at[slot]
        _issue_tile_copies(nt, lambda g, u: (_tile_at(y_hbm, idx_ref[0, 0, g * _COPY_UNROLL + u]),
                                             _tile_at(buf, g * _COPY_UNROLL + u), sem))

    @pl.when(i == 0)
    def _():
        fetch(cur_ref, 0)

    @pl.when(i + 1 < pl.num_programs(0))
    def _():
        fetch(nxt_ref, (i + 1) % 2)

    slot = i % 2
    yb = ybuf_ref.at[slot]
    pltpu.make_async_copy(y_hbm.at[pl.ds(0, nt * SUBLANES)], yb, sems.at[slot]).wait()
    w = w_ref[...]
    y = (w[:, 0:1] * _from_token_tiles(yb, tc, 0, TOP_K)
         + w[:, 1:2] * _from_token_tiles(yb, tc, 1, TOP_K))
    r = alpha * x_ref[0] + (1.0 + gate_ref[0]) * y
    o_ref[0] = _layer_norm_rows(r, g_ref[...], b_ref[...])


def _combine(y_tiles, dest, top_w, x, gate2, ln_g, ln_b, alpha, tc):
    bsz, seq, d = x.shape
    nj = seq // tc
    steps = bsz * nj
    nt = TOP_K * tc
    dest_blocks = dest.astype(jnp.int32).reshape(steps, 1, nt)
    return pl.pallas_call(
        functools.partial(_combine_kernel, tc=tc, alpha=alpha),
        grid=(steps,),
        in_specs=[pl.BlockSpec((1, 1, nt), lambda i: (i, 0, 0), memory_space=pltpu.SMEM),
                  pl.BlockSpec((1, 1, nt), lambda i: (jnp.minimum(i + 1, steps - 1), 0, 0),
                               memory_space=pltpu.SMEM),
                  pl.BlockSpec(memory_space=pl.ANY),
                  pl.BlockSpec((tc, LANES), lambda i: (i, 0)),
                  pl.BlockSpec((1, tc, d), lambda i: (i // nj, i % nj, 0)),
                  pl.BlockSpec((1, 1, d), lambda i: (i // nj, 0, 0)), _const_spec((1, d)), _const_spec((1, d))],
        out_specs=pl.BlockSpec((1, tc, d), lambda i: (i // nj, i % nj, 0)),
        out_shape=jax.ShapeDtypeStruct((bsz, seq, d), F32),
        scratch_shapes=[pltpu.VMEM((2, nt * SUBLANES, LANES), F32), pltpu.SemaphoreType.DMA((2,))],
        compiler_params=pltpu.CompilerParams(dimension_semantics=("arbitrary",),
                                             vmem_limit_bytes=VMEM_LIMIT_BYTES),
        name="moe_combine",
    )(dest_blocks, dest_blocks, y_tiles, top_w, x, gate2, ln_g, ln_b)


def _moe(oa, oc, lw, x, gate1, scale2, shift2, gate2, wr2, stri, wg, wu, wd, ln_g, ln_b, alpha,
         tm_tok, tm, tf, tc):
    bsz, seq, d = x.shape
    n_tok = bsz * seq
    n_asg = n_tok * TOP_K
    x1 = _outproj(oa, oc, lw, x, gate1, alpha, tm_tok)
    h_tiles, idx128, w128, cnt = _router(x1, scale2, shift2, wr2, stri, tm_tok)
    counts = cnt[0, :N_EXPERTS].astype(jnp.int32)
    padded = (counts + tm - 1) // tm * tm
    pad_end = jnp.cumsum(padded)
    pad_start = pad_end - padded
    e_ids, ranks = idx128[:, 0:TOP_K], idx128[:, TOP_K:2 * TOP_K]
    start_of = jnp.sum(jnp.where(e_ids[:, :, None] == jnp.arange(N_EXPERTS, dtype=jnp.int32), pad_start, 0), axis=-1)
    dest = (start_of + ranks).reshape(n_asg)
    nb = n_asg // tm + N_EXPERTS
    n_rows = nb * tm
    n_fill = n_rows - n_asg
    fill_end = jnp.cumsum(padded - counts)
    slot = jnp.arange(n_fill, dtype=jnp.int32)
    slot_e = jnp.sum((slot[:, None] >= fill_end[None, :]).astype(jnp.int32), axis=1)
    in_group = slot_e < N_EXPERTS
    ge = jnp.minimum(slot_e, N_EXPERTS - 1)
    fill_dst = jnp.where(in_group, (pad_start + counts)[ge] + slot - (fill_end - (padded - counts))[ge],
                         pad_end[-1] + slot - fill_end[-1])
    x_tiles = _group_rows(h_tiles, dest, fill_dst, n_rows, tm_tok)
    block_row = jnp.arange(nb, dtype=jnp.int32) * tm
    block_e = jnp.minimum(jnp.sum((block_row[:, None] >= pad_end[None, :]).astype(jnp.int32), axis=1),
                          N_EXPERTS - 1)
    n_used = (pad_end[-1:] // tm).astype(jnp.int32)
    rows_in_block = (pad_start + counts)[block_e] - block_row
    n_halves = jnp.where(rows_in_block > tm // 2, 2, 1).astype(jnp.int32)
    y_tiles = _experts(x_tiles, nb, block_e, n_used, n_halves, wg, wu, wd, tm, tf)
    return _combine(y_tiles, dest, w128, x1, gate2, ln_g, ln_b, alpha, tc)


def _layer_weights(layer, w_in, b_fgate, w_conv, q_norm_g, kv_norm_g, w_uq, w_ukv, head_norm_g, w_o,
                   ln1_g, ln1_b):
    d = w_in.shape[1]
    sizes = [FOX_W, FOX_W, FOX_W, H_FOX, CONV_W, CONV_W, CONV_W, Q_LORA, KV_LORA, QK_ROPE]
    offs = np.concatenate([[0], np.cumsum(sizes)])
    wi = w_in[layer]
    fq, fk, fv, fl, bg, cg, hin, cq, ckv, kr = [wi[:, offs[i]:offs[i + 1]] for i in range(len(sizes))]
    half = QK_ROPE // 2
    z64 = jnp.zeros((d, QK_NOPE), F32)
    z32 = jnp.zeros((d, LANES - QK_NOPE - QK_ROPE), F32)
    wa = jnp.concatenate([
        fq, fk, fv,
        jnp.pad(jnp.repeat(fl, _N_SPLIT, axis=1), ((0, 0), (0, LANES - _N_SPLIT * H_FOX))), bg, cg, hin, cq, ckv,
        jnp.concatenate([z64, kr, z32], axis=1),
        jnp.concatenate([z64, kr[:, half:], kr[:, :half], z32], axis=1)], axis=1)
    assert wa.shape[1] == _NA
    q3 = w_uq[layer].reshape(Q_LORA, H_MLA, QK_NOPE + QK_ROPE)
    q_main = jnp.pad(q3, ((0, 0), (0, 0), (0, LANES - QK_NOPE - QK_ROPE)))
    q_swap = jnp.concatenate([jnp.zeros((Q_LORA, H_MLA, QK_NOPE), F32), q3[:, :, QK_NOPE + half:],
                              q3[:, :, QK_NOPE:QK_NOPE + half],
                              jnp.zeros((Q_LORA, H_MLA, LANES - QK_NOPE - QK_ROPE), F32)], axis=2)
    wuq = jnp.concatenate([q_main.reshape(Q_LORA, H_MLA * LANES), q_swap.reshape(Q_LORA, H_MLA * LANES)],
                          axis=1).astype(BF16)
    kv3 = w_ukv[layer].reshape(KV_LORA, H_MLA, QK_NOPE + V_DIM)
    k_nope = jnp.pad(kv3[:, :, :QK_NOPE], ((0, 0), (0, 0), (0, LANES - QK_NOPE))).reshape(KV_LORA, H_MLA * LANES)
    wukv = jnp.concatenate([k_nope, kv3[:, :, QK_NOPE:].reshape(KV_LORA, MLA_W)], axis=1).astype(BF16)
    hg = head_norm_g[layer]
    wo = w_o[layer]
    c0, c1 = FOX_W, FOX_W + CONV_W
    return {
        "wa": wa, "wuq": wuq, "wukv": wukv,
        "qg": q_norm_g[layer].reshape(1, Q_LORA), "kvg": kv_norm_g[layer].reshape(1, KV_LORA),
        "bf": jnp.pad(jnp.repeat(b_fgate[layer], _N_SPLIT), (0, LANES - _N_SPLIT * H_FOX)).reshape(1, LANES),
        "wconv": jnp.pad(w_conv[layer], ((0, SUBLANES - CONV_K), (0, 0))),
        "hg_conv": hg[c0:c1].reshape(1, CONV_W),
        "hg_attn": jnp.concatenate([hg[:c0], hg[c1:]]).reshape(1, FOX_W + MLA_W),
        "wo_a": jnp.concatenate([wo[:c0], wo[c1:]], axis=0).astype(BF16),
        "wo_c": wo[c0:c1].astype(BF16),
        "ln1_g": ln1_g[layer].reshape(1, d), "ln1_b": ln1_b[layer].reshape(1, d),
    }


def _constants(seq, ts):
    pos = jnp.arange(seq, dtype=F32)
    inv_freq = ROPE_THETA ** (-jnp.arange(0, QK_ROPE, 2, dtype=F32) / QK_ROPE)
    ang = pos[:, None] * inv_freq[None, :]
    cos, sin = jnp.cos(ang), jnp.sin(ang)
    pad_r = jnp.zeros((seq, LANES - QK_NOPE - QK_ROPE), F32)
    ctab = jnp.concatenate([jnp.ones((seq, QK_NOPE), F32), cos, cos, pad_r], axis=1)
    stab = jnp.concatenate([jnp.zeros((seq, QK_NOPE), F32), -sin, sin, pad_r], axis=1)
    q_scale = (QK_NOPE + QK_ROPE) ** -0.5 * LOG2E
    lanes = np.arange(LANES)
    lm = np.zeros((SUBLANES, LANES), np.float32)
    for part in range(_N_SPLIT):
        lm[part] = (lanes % _N_SPLIT == part) & (lanes < _N_SPLIT * H_FOX)
    for row, base in ((3, HEAD_DIM), (5, 0)):
        lm[row] = (lanes >= base) & (lanes < base + _N_SPLIT)
        lm[row + 1] = (lanes >= base + _N_SPLIT) & (lanes < base + 2 * _N_SPLIT)

    def group_mean(n):
        gidx = np.arange(n) // HEAD_DIM
        return (gidx[:, None] == gidx[None, :]).astype(np.float32) / HEAD_DIM

    return {
        "cq": ctab * q_scale, "sq": stab * q_scale, "ck": ctab, "sk": stab,
        "tri": jnp.asarray(np.tril(np.ones((ts, ts), np.float32)), BF16),
        "stri": jnp.asarray(np.tril(np.ones((ts, ts), np.float32), k=-1), BF16),
        "lane_masks": jnp.asarray(lm),
        "gm256": jnp.asarray(group_mean(CONV_W), BF16), "gm128": jnp.asarray(group_mean(LANES), BF16),
    }


def _tile(n, pref):
    t = min(n, pref)
    assert n % t == 0, (n, pref)
    return t


def kernel(x, c, w_mod, b_mod, w_in, b_fgate, w_conv, q_norm_g, kv_norm_g, w_uq, w_ukv, head_norm_g, w_o, ln1_g, ln1_b, ln2_g, ln2_b, ffn_w_gate, ffn_w_up, ffn_w_down, router_w, exp_w_gate, exp_w_up, exp_w_down):
    bsz, seq, d = x.shape
    depth = w_mod.shape[0]
    assert d == D_MODEL and bsz <= SUBLANES
    alpha = (2 * depth) ** 0.25
    ts = _tile(seq, 512)
    tq = _tile(seq, 512)
    tm_e = 1024
    tf_e = 512
    tc = _tile(seq, 256)
    assert (bsz * seq * TOP_K) % tm_e == 0 and exp_w_gate.shape[-1] % tf_e == 0

    consts = _constants(seq, ts)
    c_pad = jnp.pad(c, ((0, SUBLANES - bsz), (0, 0)))
    mod = _modulation(c_pad, w_mod, b_mod)[:, :bsz, :]

    for layer in range(depth):
        m6 = mod[layer].reshape(bsz, 6, 1, d)
        shift1, scale1, gate1, shift2, scale2, gate2 = [m6[:, i] for i in range(6)]
        lw = _layer_weights(layer, w_in, b_fgate, w_conv, q_norm_g, kv_norm_g, w_uq, w_ukv, head_norm_g,
                            w_o, ln1_g, ln1_b)
        qp, kp, vt, oc = _inproj(x, scale1, shift1, lw, consts, ts)
        oa = _attention(qp, kp, vt, lw["hg_attn"], consts["gm128"], tq, 12)
        j = layer // 2
        g2, b2 = ln2_g[layer].reshape(1, d), ln2_b[layer].reshape(1, d)
        if layer % 2 == 0:
            x = _ffn_dense(oa, oc, lw, x, gate1, scale2, shift2, gate2, ffn_w_gate[j].astype(BF16),
                           ffn_w_up[j].astype(BF16), ffn_w_down[j].astype(BF16), g2, b2, alpha, ts)
        else:
            wr = jnp.pad(router_w[j], ((0, 0), (0, LANES - N_EXPERTS)))
            wr_hi = wr.astype(BF16)
            wr_lo = (wr - wr_hi.astype(F32)).astype(BF16)
            x = _moe(oa, oc, lw, x, gate1, scale2, shift2, gate2, jnp.stack([wr_hi, wr_lo]), consts["stri"],
                     exp_w_gate[j].astype(BF16), exp_w_up[j].astype(BF16), exp_w_down[j].astype(BF16),
                     g2, b2, alpha, ts, tm_e, tf_e, tc)
    return x
```

```python
import functools

import numpy as np
import jax
import jax.numpy as jnp
from jax import lax
from jax.experimental import pallas as pl
from jax.experimental.pallas import tpu as pltpu

F32 = jnp.float32
BF16 = jnp.bfloat16

D_MODEL = 1024
HEAD_DIM = 64
H_FOX = 6
FOX_W = H_FOX * HEAD_DIM
CONV_W = 256
CONV_K = 3
H_MLA = 6
Q_LORA = 256
KV_LORA = 256
QK_NOPE = 64
QK_ROPE = 32
V_DIM = 64
MLA_W = H_MLA * V_DIM
N_HEADS = H_FOX + H_MLA
ROPE_THETA = 10000.0
N_EXPERTS = 8
TOP_K = 2
LN_EPS = 1e-5
RMS_EPS = 1e-6

LANES = 128
SUBLANES = 8
VMEM_LIMIT_BYTES = 56 * 1024 * 1024

_HW = 6 * LANES
_QF0, _KF0, _VF0, _FL0 = 0, FOX_W, 2 * FOX_W, 3 * FOX_W
_BG0 = _FL0 + LANES
_CG0, _HI0, _CQ0, _CKV0 = _BG0 + CONV_W, _BG0 + 2 * CONV_W, _BG0 + 3 * CONV_W, _BG0 + 3 * CONV_W + Q_LORA
_KR0 = _CKV0 + KV_LORA
_KRS0 = _KR0 + LANES
_NA = _KRS0 + LANES
LOG2E = 1.4426950408889634
VT_ROWS = LANES
_N_SPLIT = 3


def _const_spec(shape):
    zeros = (0,) * len(shape)
    return pl.BlockSpec(shape, lambda *_: zeros, pipeline_mode=pl.Buffered(1))


def _silu(v):
    return v * (1.0 / (1.0 + jnp.exp(-v)))


def _split3(v):
    hi = v.astype(BF16)
    r1 = v - hi.astype(F32)
    mid = r1.astype(BF16)
    lo = (r1 - mid.astype(F32)).astype(BF16)
    return hi, mid, lo


def _group_mean_sq(v, gmat_ref):
    sq = v * v
    hi = sq.astype(BF16)
    lo = (sq - hi.astype(F32)).astype(BF16)
    g = gmat_ref[...]
    return (jnp.dot(hi, g, preferred_element_type=F32) + jnp.dot(lo, g, preferred_element_type=F32))


def _layer_norm_rows(r, g, b):
    mu = jnp.mean(r, axis=-1, keepdims=True)
    rc = r - mu
    var = jnp.mean(rc * rc, axis=-1, keepdims=True)
    return rc * lax.rsqrt(var + LN_EPS) * g + b


def _mod_kernel(c_ref, w_ref, b_ref, o_ref):
    act = _silu(c_ref[...]).astype(BF16)
    o_ref[0] = jnp.dot(act, w_ref[0].astype(BF16), preferred_element_type=F32) + b_ref[0]


def _modulation(c_pad, w_mod, b_mod):
    depth, d, n = w_mod.shape
    tn = 1024
    return pl.pallas_call(
        _mod_kernel,
        grid=(depth, n // tn),
        in_specs=[pl.BlockSpec((SUBLANES, d), lambda l, j: (0, 0)),
                  pl.BlockSpec((1, d, tn), lambda l, j: (l, 0, j)),
                  pl.BlockSpec((1, 1, tn), lambda l, j: (l, 0, j))],
        out_specs=pl.BlockSpec((1, SUBLANES, tn), lambda l, j: (l, 0, j)),
        out_shape=jax.ShapeDtypeStruct((depth, SUBLANES, n), F32),
        compiler_params=pltpu.CompilerParams(dimension_semantics=("arbitrary", "arbitrary")),
        name="modulation",
    )(c_pad, w_mod, b_mod.reshape(depth, 1, n))


def _inproj_kernel(x_ref, sc_ref, sh_ref, wa_ref, cq_ref, sq_ref, ck_ref, sk_ref, wuq_ref, wukv_ref,
                   qg_ref, kvg_ref, bf_ref, wconv_ref, hgc_ref, tri_ref, lm_ref, gm_ref,
                   qp_ref, kp_ref, vt_ref, oc_ref, fcarry_ref, ubuf_ref, wab_ref, *, ts):
    @pl.when(jnp.logical_and(pl.program_id(0) == 0, pl.program_id(1) == 0))
    def _():
        for c in range(_NA // LANES):
            sl = slice(c * LANES, (c + 1) * LANES)
            wab_ref[:, sl] = wa_ref[:, sl].astype(BF16)

    @pl.when(pl.program_id(1) == 0)
    def _():
        fcarry_ref[...] = jnp.zeros_like(fcarry_ref)
        ubuf_ref[pl.ds(0, SUBLANES), :] = jnp.zeros((SUBLANES, CONV_W), F32)

    hb = (x_ref[0] * (1.0 + sc_ref[0]) + sh_ref[0]).astype(BF16)

    def proj(lo, hi):
        return jnp.dot(hb, wab_ref[:, lo:hi], preferred_element_type=F32)

    def put_values_t(head, v_half_t):
        vt_ref[0, head, 0:V_DIM, :] = v_half_t.astype(BF16)
        vt_ref[0, head, V_DIM:VT_ROWS, :] = jnp.ones((VT_ROWS - V_DIM, ts), BF16)

    z_fl = proj(_FL0, _FL0 + LANES)
    z_cq = proj(_CQ0, _CQ0 + Q_LORA)
    z_ckv = proj(_CKV0, _CKV0 + KV_LORA)
    z_cg = proj(_CG0, _CG0 + CONV_W)
    z_hi = proj(_HI0, _HI0 + CONV_W)
    z_bg = proj(_BG0, _BG0 + CONV_W)
    zq = proj(_QF0, _QF0 + FOX_W) * (HEAD_DIM ** -0.5 * LOG2E)
    zk = proj(_KF0, _KF0 + FOX_W)
    zv = proj(_VF0, _VF0 + FOX_W)
    z_kr = proj(_KR0, _KR0 + LANES)
    z_krs = proj(_KRS0, _KRS0 + LANES)

    a = z_fl + bf_ref[...]
    logf = jnp.minimum(a, 0.0) - jnp.log1p(jnp.exp(-jnp.abs(a)))
    tri = tri_ref[...]
    csum = sum(jnp.dot(tri, part, preferred_element_type=F32) for part in _split3(logf))
    fcum = fcarry_ref[...] + csum
    fcarry_ref[...] = fcum[ts - 1:ts, :]
    f_hi, f_mid, f_lo = _split3(fcum * LOG2E)
    lm = lm_ref[...]
    fparts = (f_hi.astype(F32) * lm[0:1, :] + f_mid.astype(F32) * lm[1:2, :]
              + f_lo.astype(F32) * lm[2:3, :])
    lane = lax.broadcasted_iota(jnp.int32, (ts, LANES), 1)
    low_half = lane < HEAD_DIM
    for h in range(H_FOX):
        blk = slice((h // 2) * LANES, (h // 2 + 1) * LANES)
        if h % 2 == 0:
            base, f_mask, one_mask, keep = HEAD_DIM, lm[3:4, :], lm[4:5, :], low_half
        else:
            base, f_mask, one_mask, keep = 0, lm[5:6, :], lm[6:7, :], jnp.logical_not(low_half)
        f_at_q = pltpu.roll(fparts, (base - _N_SPLIT * h) % LANES, axis=1)
        f_at_k = pltpu.roll(fparts, (base + _N_SPLIT - _N_SPLIT * h) % LANES, axis=1)
        qp_ref[0, h] = jnp.where(keep, zq[:, blk], f_at_q * f_mask + one_mask).astype(BF16)
        kp_ref[0, h] = jnp.where(keep, zk[:, blk], f_mask - f_at_k * one_mask).astype(BF16)
    for j in range(H_FOX // 2):
        vt = zv[:, j * LANES:(j + 1) * LANES].T
        put_values_t(2 * j, vt[0:V_DIM])
        put_values_t(2 * j + 1, vt[V_DIM:LANES])

    u = z_cg * z_hi
    ubuf_ref[pl.ds(SUBLANES, ts), :] = u
    u1 = ubuf_ref[pl.ds(SUBLANES - 1, ts), :]
    u2 = ubuf_ref[pl.ds(SUBLANES - 2, ts), :]
    ubuf_ref[pl.ds(0, SUBLANES), :] = u[ts - SUBLANES:ts, :]
    wc = wconv_ref[...]
    oc = z_bg * (wc[0:1, :] * u2 + wc[1:2, :] * u1 + wc[2:3, :] * u)
    ocn = oc * lax.rsqrt(_group_mean_sq(oc, gm_ref) + RMS_EPS) * hgc_ref[...]
    oc_ref[0] = ocn.astype(BF16)

    def rms(v, g):
        return (v * lax.rsqrt(jnp.mean(v * v, axis=-1, keepdims=True) + RMS_EPS) * g).astype(BF16)

    cqn = rms(z_cq, qg_ref[...])
    qm = jnp.dot(cqn, wuq_ref[:, 0:_HW], preferred_element_type=F32)
    qs = jnp.dot(cqn, wuq_ref[:, _HW:2 * _HW], preferred_element_type=F32)
    kvn = rms(z_ckv, kvg_ref[...])
    kn = jnp.dot(kvn, wukv_ref[:, 0:_HW], preferred_element_type=F32)
    vm = jnp.dot(kvn, wukv_ref[:, _HW:_HW + MLA_W], preferred_element_type=F32)
    krr = z_kr * ck_ref[...] + z_krs * sk_ref[...]
    cq, sq = cq_ref[...], sq_ref[...]
    for h in range(H_MLA):
        sl = slice(h * LANES, (h + 1) * LANES)
        qp_ref[0, H_FOX + h] = (qm[:, sl] * cq + qs[:, sl] * sq).astype(BF16)
        kp_ref[0, H_FOX + h] = (kn[:, sl] + krr).astype(BF16)
    for j in range(H_MLA // 2):
        vt = vm[:, j * LANES:(j + 1) * LANES].T
        put_values_t(H_FOX + 2 * j, vt[0:V_DIM])
        put_values_t(H_FOX + 2 * j + 1, vt[V_DIM:LANES])


def _inproj(x, scale1, shift1, lw, consts, ts):
    bsz, seq, d = x.shape
    kern = functools.partial(_inproj_kernel, ts=ts)
    row = lambda b, j: (b, 0, 0)
    tab = pl.BlockSpec((ts, LANES), lambda b, j: (j, 0))
    return pl.pallas_call(
        kern,
        grid=(bsz, seq // ts),
        in_specs=[pl.BlockSpec((1, ts, d), lambda b, j: (b, j, 0)),
                  pl.BlockSpec((1, 1, d), row), pl.BlockSpec((1, 1, d), row),
                  _const_spec((d, _NA)), tab, tab, tab, tab,
                  _const_spec((Q_LORA, 2 * _HW)), _const_spec((KV_LORA, _HW + MLA_W)),
                  _const_spec((1, Q_LORA)), _const_spec((1, KV_LORA)), _const_spec((1, LANES)),
                  _const_spec((SUBLANES, CONV_W)), _const_spec((1, CONV_W)),
                  _const_spec((ts, ts)), _const_spec((SUBLANES, LANES)), _const_spec((CONV_W, CONV_W))],
        out_specs=[pl.BlockSpec((1, N_HEADS, ts, LANES), lambda b, j: (b, 0, j, 0)),
                   pl.BlockSpec((1, N_HEADS, ts, LANES), lambda b, j: (b, 0, j, 0)),
                   pl.BlockSpec((1, N_HEADS, VT_ROWS, ts), lambda b, j: (b, 0, 0, j)),
                   pl.BlockSpec((1, ts, CONV_W), lambda b, j: (b, j, 0))],
        out_shape=[jax.ShapeDtypeStruct((bsz, N_HEADS, seq, LANES), BF16),
                   jax.ShapeDtypeStruct((bsz, N_HEADS, seq, LANES), BF16),
                   jax.ShapeDtypeStruct((bsz, N_HEADS, VT_ROWS, seq), BF16),
                   jax.ShapeDtypeStruct((bsz, seq, CONV_W), BF16)],
        scratch_shapes=[pltpu.VMEM((1, LANES), F32), pltpu.VMEM((ts + SUBLANES, CONV_W), F32),
                        pltpu.VMEM((d, _NA), BF16)],
        compiler_params=pltpu.CompilerParams(dimension_semantics=("arbitrary", "arbitrary"),
                                             vmem_limit_bytes=VMEM_LIMIT_BYTES),
        name="inproj",
    )(x, scale1, shift1, lw["wa"], consts["cq"], consts["sq"], consts["ck"], consts["sk"],
      lw["wuq"], lw["wukv"], lw["qg"], lw["kvg"], lw["bf"], lw["wconv"], lw["hg_conv"],
      consts["tri"], consts["lane_masks"], consts["gm256"])


def _attn_kernel(qi_ref, kj_ref, q_ref, k_ref, vt_ref, g_ref, gm_ref, o_ref, m_ref, acc_ref, s_ref, *, nh):
    step = pl.program_id(2)
    qi = qi_ref[step]
    kj = kj_ref[step]

    @pl.when(kj == 0)
    def _():
        m_ref[...] = jnp.full(m_ref.shape, -jnp.inf, F32)
        acc_ref[...] = jnp.zeros_like(acc_ref)

    def update(masked):
        if masked:
            tile = (q_ref.shape[2], q_ref.shape[2])
            causal = lax.broadcasted_iota(jnp.int32, tile, 0) <= lax.broadcasted_iota(jnp.int32, tile, 1)

        def scores(h):
            st = lax.dot_general(k_ref[0, h], q_ref[0, h], (((1,), (1,)), ((), ())),
                                 preferred_element_type=F32)
            if masked:
                st = jnp.where(causal, st, -jnp.inf)
            s_ref[h] = st

        for h in range(nh):
            scores(h)
        pts, alphas = [], []
        for h in range(nh):
            m_prev = m_ref[h]
            m_new = jnp.maximum(m_prev, jnp.max(s_ref[h], axis=0, keepdims=True))
            pts.append(jnp.exp2(s_ref[h] - m_new).astype(BF16))
            alphas.append(jnp.exp2(m_prev - m_new))
            m_ref[h] = m_new
        for h in range(nh):
            acc_ref[h] = alphas[h] * acc_ref[h] + jnp.dot(vt_ref[0, h], pts[h], preferred_element_type=F32)

    @pl.when(kj < qi)
    def _():
        update(False)

    @pl.when(kj == qi)
    def _():
        update(True)
        for pair in range(nh // 2):
            ot = jnp.concatenate([acc_ref[h, 0:V_DIM, :] / acc_ref[h, V_DIM:2 * V_DIM, :]
                                  for h in (2 * pair, 2 * pair + 1)], axis=0)
            o = ot.T
            sl = slice(pair * LANES, (pair + 1) * LANES)
            on = o * lax.rsqrt(_group_mean_sq(o, gm_ref) + RMS_EPS) * g_ref[:, sl]
            o_ref[0, :, sl] = on.astype(BF16)


def _attention(qp, kp, vt, g_attn, gm128, tq, nh):
    bsz, n_heads, seq, _ = qp.shape
    nq = seq // tq
    ow = (nh // 2) * LANES
    qi_tab = np.concatenate([np.full(i + 1, i) for i in range(nq)]).astype(np.int32)
    kj_tab = np.concatenate([np.arange(i + 1) for i in range(nq)]).astype(np.int32)
    grid_spec = pltpu.PrefetchScalarGridSpec(
        num_scalar_prefetch=2,
        grid=(bsz, n_heads // nh, len(qi_tab)),
        in_specs=[pl.BlockSpec((1, nh, tq, LANES), lambda b, p, s, qi, kj: (b, p, qi[s], 0)),
                  pl.BlockSpec((1, nh, tq, LANES), lambda b, p, s, qi, kj: (b, p, kj[s], 0)),
                  pl.BlockSpec((1, nh, VT_ROWS, tq), lambda b, p, s, qi, kj: (b, p, 0, kj[s])),
                  pl.BlockSpec((1, ow), lambda b, p, s, qi, kj: (0, p)),
                  _const_spec((LANES, LANES))],
        out_specs=pl.BlockSpec((1, tq, ow), lambda b, p, s, qi, kj: (b, qi[s], p)),
        scratch_shapes=[pltpu.VMEM((nh, 1, tq), F32), pltpu.VMEM((nh, VT_ROWS, tq), F32),
                        pltpu.VMEM((nh, tq, tq), F32)])
    return pl.pallas_call(
        functools.partial(_attn_kernel, nh=nh),
        grid_spec=grid_spec,
        out_shape=jax.ShapeDtypeStruct((bsz, seq, (n_heads // 2) * LANES), BF16),
        compiler_params=pltpu.CompilerParams(
            dimension_semantics=("arbitrary", "arbitrary", "arbitrary"),
            vmem_limit_bytes=VMEM_LIMIT_BYTES),
        name="attention",
    )(jnp.asarray(qi_tab), jnp.asarray(kj_tab), qp, kp, vt, g_attn, gm128)


def _mixer_out_rows(oa_ref, oc_ref, woa_ref, woc_ref, x_ref, gate_ref, g_ref, b_ref, alpha):
    y = (jnp.dot(oa_ref[0], woa_ref[...], preferred_element_type=F32)
         + jnp.dot(oc_ref[0], woc_ref[...], preferred_element_type=F32))
    r = alpha * x_ref[0] + (1.0 + gate_ref[0]) * y
    return _layer_norm_rows(r, g_ref[...], b_ref[...])


def _mixer_out_specs(oa, d, tm, act, row):
    wa_rows = oa.shape[-1]
    return [pl.BlockSpec((1, tm, wa_rows), act), pl.BlockSpec((1, tm, CONV_W), act),
            _const_spec((wa_rows, d)), _const_spec((CONV_W, d)),
            pl.BlockSpec((1, tm, d), act), pl.BlockSpec((1, 1, d), row), _const_spec((1, d)), _const_spec((1, d))]


def _outproj_kernel(oa_ref, oc_ref, woa_ref, woc_ref, x_ref, gate_ref, g_ref, b_ref, o_ref, *, alpha):
    o_ref[0] = _mixer_out_rows(oa_ref, oc_ref, woa_ref, woc_ref, x_ref, gate_ref, g_ref, b_ref, alpha)


def _outproj(oa, oc, lw, x, gate1, alpha, tm):
    bsz, seq, d = x.shape
    act = lambda b, j: (b, j, 0)
    return pl.pallas_call(
        functools.partial(_outproj_kernel, alpha=alpha),
        grid=(bsz, seq // tm),
        in_specs=_mixer_out_specs(oa, d, tm, act, lambda b, j: (b, 0, 0)),
        out_specs=pl.BlockSpec((1, tm, d), act),
        out_shape=jax.ShapeDtypeStruct((bsz, seq, d), F32),
        compiler_params=pltpu.CompilerParams(dimension_semantics=("arbitrary", "arbitrary"),
                                             vmem_limit_bytes=VMEM_LIMIT_BYTES),
        name="outproj",
    )(oa, oc, lw["wo_a"], lw["wo_c"], x, gate1, lw["ln1_g"], lw["ln1_b"])


def _ffn_kernel(oa_ref, oc_ref, woa_ref, woc_ref, xin_ref, gate1_ref, g1_ref, b1_ref,
                sc_ref, sh_ref, gate_ref, wg_ref, wu_ref, wd_ref, g_ref, b_ref, o_ref, *, alpha, tf):
    x = _mixer_out_rows(oa_ref, oc_ref, woa_ref, woc_ref, xin_ref, gate1_ref, g1_ref, b1_ref, alpha)
    hb = (x * (1.0 + sc_ref[0]) + sh_ref[0]).astype(BF16)
    chunks = [slice(c * tf, (c + 1) * tf) for c in range(wg_ref.shape[1] // tf)]
    acts = []
    for sl in chunks:
        gt = jnp.dot(hb, wg_ref[:, sl], preferred_element_type=F32)
        up = jnp.dot(hb, wu_ref[:, sl], preferred_element_type=F32)
        acts.append((_silu(gt) * up).astype(BF16))
    acc = sum(jnp.dot(act, wd_ref[sl, :], preferred_element_type=F32) for act, sl in zip(acts, chunks))
    r = alpha * x + (1.0 + gate_ref[0]) * acc
    o_ref[0] = _layer_norm_rows(r, g_ref[...], b_ref[...])


def _ffn_dense(oa, oc, lw, x, gate1, scale2, shift2, gate2, wg, wu, wd, ln_g, ln_b, alpha, tm):
    bsz, seq, d = x.shape
    dff = wg.shape[1]
    tf = dff // 2 if (dff // 2) % LANES == 0 else dff
    nj = seq // tm
    act = lambda b, j: (b, j, 0)
    row = lambda b, j: (b, 0, 0)
    return pl.pallas_call(
        functools.partial(_ffn_kernel, alpha=alpha, tf=tf),
        grid=(bsz, nj),
        in_specs=_mixer_out_specs(oa, d, tm, act, row) + [
            pl.BlockSpec((1, 1, d), row), pl.BlockSpec((1, 1, d), row), pl.BlockSpec((1, 1, d), row),
            _const_spec((d, dff)), _const_spec((d, dff)), _const_spec((dff, d)),
            _const_spec((1, d)), _const_spec((1, d))],
        out_specs=pl.BlockSpec((1, tm, d), act),
        out_shape=jax.ShapeDtypeStruct((bsz, seq, d), F32),
        compiler_params=pltpu.CompilerParams(dimension_semantics=("arbitrary", "arbitrary"),
                                             vmem_limit_bytes=VMEM_LIMIT_BYTES),
        name="ffn_dense",
    )(oa, oc, lw["wo_a"], lw["wo_c"], x, gate1, lw["ln1_g"], lw["ln1_b"],
      scale2, shift2, gate2, wg, wu, wd, ln_g, ln_b)


def _to_token_tiles(ref, v, n):
    for c in range(v.shape[1] // LANES):
        ref[pl.ds(c, n, stride=SUBLANES), :] = v[:, c * LANES:(c + 1) * LANES]


def _from_token_tiles(ref, n, first=0):
    return jnp.concatenate([ref[pl.ds(first * SUBLANES + c, n, stride=SUBLANES), :] for c in range(SUBLANES)],
                           axis=1)


def _router_kernel(x_ref, sc_ref, sh_ref, wr_ref, stri_ref, h_ref, idx_ref, w_ref, cnt_ref, carry_ref, *, tm):
    @pl.when(pl.program_id(0) == 0)
    def _():
        carry_ref[...] = jnp.zeros_like(carry_ref)

    h = x_ref[0] * (1.0 + sc_ref[0]) + sh_ref[0]
    _to_token_tiles(h_ref, h, tm)
    h_hi = h.astype(BF16)
    h_lo = (h - h_hi.astype(F32)).astype(BF16)
    w_hi, w_lo = wr_ref[0], wr_ref[1]
    logits = (jnp.dot(h_hi, w_hi, preferred_element_type=F32)
              + jnp.dot(h_hi, w_lo, preferred_element_type=F32)
              + jnp.dot(h_lo, w_hi, preferred_element_type=F32))
    lane = lax.broadcasted_iota(jnp.int32, logits.shape, 1).astype(F32)
    neg = -jnp.inf
    lg = jnp.where(lane < N_EXPERTS, logits, neg)
    m1 = jnp.max(lg, axis=-1, keepdims=True)
    i1 = jnp.min(jnp.where(lg == m1, lane, float(LANES)), axis=-1, keepdims=True)
    lg2 = jnp.where(lane == i1, neg, lg)
    m2 = jnp.max(lg2, axis=-1, keepdims=True)
    i2 = jnp.min(jnp.where(lg2 == m2, lane, float(LANES)), axis=-1, keepdims=True)
    e2 = jnp.exp(m2 - m1)
    denom = 1.0 + e2
    w_ref[...] = jnp.where(lane == 0.0, 1.0 / denom, e2 / denom)
    first, second = lane == i1, lane == i2
    chosen = jnp.logical_or(first, second).astype(F32)
    before = carry_ref[...] + jnp.dot(stri_ref[...], chosen.astype(BF16), preferred_element_type=F32)
    rank1 = jnp.sum(jnp.where(first, before, 0.0), axis=-1, keepdims=True)
    rank2 = jnp.sum(jnp.where(second, before, 0.0), axis=-1, keepdims=True)
    packed = jnp.where(lane == 0.0, i1, jnp.where(lane == 1.0, i2, jnp.where(lane == 2.0, rank1, rank2)))
    idx_ref[...] = packed.T[0:SUBLANES, :].astype(jnp.int32)
    total = carry_ref[...] + jnp.sum(chosen, axis=0, keepdims=True)
    carry_ref[...] = total
    cnt_ref[...] = jnp.broadcast_to(total, cnt_ref.shape)


def _router(x, scale2, shift2, wr2, stri, tm):
    bsz, seq, d = x.shape
    assert d == SUBLANES * LANES and bsz * seq * TOP_K < 2 ** 24
    nj = seq // tm
    n_tiles = bsz * nj
    row = lambda i: (i // nj, 0, 0)
    tok = lambda i: (i, 0)
    return pl.pallas_call(
        functools.partial(_router_kernel, tm=tm),
        grid=(n_tiles,),
        in_specs=[pl.BlockSpec((1, tm, d), lambda i: (i // nj, i % nj, 0)),
                  pl.BlockSpec((1, 1, d), row), pl.BlockSpec((1, 1, d), row),
                  _const_spec((2, d, LANES)), _const_spec((tm, tm))],
        out_specs=[pl.BlockSpec((tm * SUBLANES, LANES), tok),
                   pl.BlockSpec((SUBLANES, tm), tok), pl.BlockSpec((tm, LANES), tok),
                   pl.BlockSpec((SUBLANES, LANES), lambda i: (0, 0))],
        out_shape=[jax.ShapeDtypeStruct((n_tiles * tm * SUBLANES, LANES), F32),
                   jax.ShapeDtypeStruct((n_tiles * SUBLANES, tm), jnp.int32),
                   jax.ShapeDtypeStruct((bsz * seq, LANES), F32),
                   jax.ShapeDtypeStruct((SUBLANES, LANES), F32)],
        scratch_shapes=[pltpu.VMEM((1, LANES), F32)],
        compiler_params=pltpu.CompilerParams(dimension_semantics=("arbitrary",),
                                             vmem_limit_bytes=VMEM_LIMIT_BYTES),
        name="router",
    )(x, scale2, shift2, wr2, stri)


_COPY_UNROLL = 8


def _tile_at(ref, t):
    return ref.at[pl.ds(pl.multiple_of(t * SUBLANES, SUBLANES), SUBLANES)]


def _issue_tile_copies(n, copy_of):
    def group(g, carry):
        copies = [copy_of(g, u) for u in range(_COPY_UNROLL)]
        for u, (src, dst, sem) in enumerate(copies):
            pltpu.make_async_copy(src, dst, sem).start(priority=u % 2)
        return carry

    assert n % _COPY_UNROLL == 0
    lax.fori_loop(0, n // _COPY_UNROLL, group, 0)


def _group_kernel(di_ref, fi_ref, src_ref, dst_hbm, zero_ref, sem, *, tb, nf):
    zero_ref[...] = jnp.zeros_like(zero_ref)
    for k in range(TOP_K):
        _issue_tile_copies(tb, lambda g, u: (_tile_at(src_ref, g * _COPY_UNROLL + u),
                                             _tile_at(dst_hbm, di_ref[0, k, g * _COPY_UNROLL + u]), sem))
    _issue_tile_copies(nf, lambda g, u: (zero_ref, _tile_at(dst_hbm, fi_ref[0, 0, g * _COPY_UNROLL + u]), sem))
    block = pl.ds(0, tb * SUBLANES)
    for _ in range(TOP_K):
        pltpu.make_async_copy(src_ref, dst_hbm.at[block], sem).wait()

    def wait_fill(r, carry):
        pltpu.make_async_copy(zero_ref, _tile_at(dst_hbm, 0), sem).wait()
        return carry

    lax.fori_loop(0, nf, wait_fill, 0)


def _group_rows(h_tiles, dest, fill_dst, n_rows):
    steps, _, tb = dest.shape
    nf = fill_dst.shape[0] // steps
    assert steps * nf == fill_dst.shape[0]
    return pl.pallas_call(
        functools.partial(_group_kernel, tb=tb, nf=nf),
        grid=(steps,),
        in_specs=[pl.BlockSpec((1, TOP_K, tb), lambda i: (i, 0, 0), memory_space=pltpu.SMEM),
                  pl.BlockSpec((1, 1, nf), lambda i: (i, 0, 0), memory_space=pltpu.SMEM),
                  pl.BlockSpec((tb * SUBLANES, LANES), lambda i: (i, 0))],
        out_specs=pl.BlockSpec(memory_space=pl.ANY),
        out_shape=jax.ShapeDtypeStruct((n_rows * SUBLANES, LANES), h_tiles.dtype),
        scratch_shapes=[pltpu.VMEM((SUBLANES, LANES), h_tiles.dtype), pltpu.SemaphoreType.DMA(())],
        compiler_params=pltpu.CompilerParams(dimension_semantics=("arbitrary",), has_side_effects=True),
        name="moe_group",
    )(dest, fill_dst.astype(jnp.int32).reshape(steps, 1, nf), h_tiles)


def _expert_kernel(be_ref, nu_ref, nh_ref, x_ref, wg_ref, wu_ref, wd_ref, o_ref, acc_ref, xb_ref, *, tm):
    i = pl.program_id(0)
    f = pl.program_id(1)
    used = i < nu_ref[0]
    whole = jnp.logical_and(used, nh_ref[i] == 2)
    first_half_only = jnp.logical_and(used, nh_ref[i] == 1)

    @pl.when(jnp.logical_and(used, f == 0))
    def _():
        xb_ref[...] = _from_token_tiles(x_ref, tm).astype(BF16)
        acc_ref[...] = jnp.zeros_like(acc_ref)

    def swiglu_step(rows):
        xb = xb_ref[0:rows, :]
        tf = wg_ref.shape[2]
        halves = [slice(0, tf // 2), slice(tf // 2, tf)]
        acts = []
        for sl in halves:
            gt = jnp.dot(xb, wg_ref[0, :, sl], preferred_element_type=F32)
            up = jnp.dot(xb, wu_ref[0, :, sl], preferred_element_type=F32)
            acts.append((_silu(gt) * up).astype(BF16))
        acc_ref[0:rows, :] += sum(jnp.dot(act, wd_ref[0, sl, :], preferred_element_type=F32)
                                  for act, sl in zip(acts, halves))

    pl.when(whole)(functools.partial(swiglu_step, tm))
    pl.when(first_half_only)(functools.partial(swiglu_step, tm // 2))

    last = f == pl.num_programs(1) - 1

    @pl.when(jnp.logical_and(used, last))
    def _():
        _to_token_tiles(o_ref, acc_ref[...], tm)

    @pl.when(jnp.logical_and(jnp.logical_not(used), last))
    def _():
        o_ref[...] = jnp.zeros_like(o_ref)


def _experts(x_tiles, nb, block_e, n_used, n_halves, wg, wu, wd, tm, tf):
    n_exp, d, dff = wg.shape
    nf = dff // tf

    def xmap(i, f, be, nu, nh):
        return (jnp.minimum(i, nu[0] - 1), 0)

    def fidx(i, f, nu):
        return jnp.where(i < nu[0], f, nf - 1)

    grid_spec = pltpu.PrefetchScalarGridSpec(
        num_scalar_prefetch=3,
        grid=(nb, nf),
        in_specs=[pl.BlockSpec((tm * SUBLANES, LANES), xmap),
                  pl.BlockSpec((1, d, tf), lambda i, f, be, nu, nh: (be[i], 0, fidx(i, f, nu))),
                  pl.BlockSpec((1, d, tf), lambda i, f, be, nu, nh: (be[i], 0, fidx(i, f, nu))),
                  pl.BlockSpec((1, tf, d), lambda i, f, be, nu, nh: (be[i], fidx(i, f, nu), 0))],
        out_specs=pl.BlockSpec((tm * SUBLANES, LANES), lambda i, f, be, nu, nh: (i, 0)),
        scratch_shapes=[pltpu.VMEM((tm, d), F32), pltpu.VMEM((tm, d), BF16)])
    return pl.pallas_call(
        functools.partial(_expert_kernel, tm=tm),
        grid_spec=grid_spec,
        out_shape=jax.ShapeDtypeStruct((nb * tm * SUBLANES, LANES), F32),
        compiler_params=pltpu.CompilerParams(dimension_semantics=("arbitrary", "arbitrary"),
                                             vmem_limit_bytes=VMEM_LIMIT_BYTES),
        name="moe_experts",
    )(block_e, n_used, n_halves, x_tiles, wg, wu, wd)


def _combine_kernel(cur_ref, nxt_ref, y_hbm, w_ref, x_ref, gate_ref, g_ref, b_ref, o_ref, ybuf_ref, sems,
                    *, tc, alpha):
    i = pl.program_id(0)
    nt = TOP_K * tc

    def fetch(idx_ref, slot):
        buf, sem = ybuf_ref.at[slot], sems.at[slot]
        for k in range(TOP_K):
            _issue_tile_copies(tc, lambda g, u: (_tile_at(y_hbm, idx_ref[0, k, g * _COPY_UNROLL + u]),
                                                 _tile_at(buf, k * tc + g * _COPY_UNROLL + u), sem))

    @pl.when(i == 0)
    def _():
        fetch(cur_ref, 0)

    @pl.when(i + 1 < pl.num_programs(0))
    def _():
        fetch(nxt_ref, (i + 1) % 2)

    slot = i % 2
    yb = ybuf_ref.at[slot]
    pltpu.make_async_copy(y_hbm.at[pl.ds(0, nt * SUBLANES)], yb, sems.at[slot]).wait()
    w = w_ref[...]
    y = w[:, 0:1] * _from_token_tiles(yb, tc, 0) + w[:, 1:2] * _from_token_tiles(yb, tc, tc)
    r = alpha * x_ref[0] + (1.0 + gate_ref[0]) * y
    o_ref[0] = _layer_norm_rows(r, g_ref[...], b_ref[...])


def _combine(y_tiles, dest, top_w, x, gate2, ln_g, ln_b, alpha, tc):
    bsz, seq, d = x.shape
    nj = seq // tc
    steps = bsz * nj
    nt = TOP_K * tc
    per_tile = dest.shape[2] // tc
    assert per_tile * tc == dest.shape[2]
    blk = lambda i: (i // per_tile, 0, i % per_tile)
    return pl.pallas_call(
        functools.partial(_combine_kernel, tc=tc, alpha=alpha),
        grid=(steps,),
        in_specs=[pl.BlockSpec((1, TOP_K, tc), blk, memory_space=pltpu.SMEM),
                  pl.BlockSpec((1, TOP_K, tc), lambda i: blk(jnp.minimum(i + 1, steps - 1)),
                               memory_space=pltpu.SMEM),
                  pl.BlockSpec(memory_space=pl.ANY),
                  pl.BlockSpec((tc, LANES), lambda i: (i, 0)),
                  pl.BlockSpec((1, tc, d), lambda i: (i // nj, i % nj, 0)),
                  pl.BlockSpec((1, 1, d), lambda i: (i // nj, 0, 0)), _const_spec((1, d)), _const_spec((1, d))],
        out_specs=pl.BlockSpec((1, tc, d), lambda i: (i // nj, i % nj, 0)),
        out_shape=jax.ShapeDtypeStruct((bsz, seq, d), F32),
        scratch_shapes=[pltpu.VMEM((2, nt * SUBLANES, LANES), F32), pltpu.SemaphoreType.DMA((2,))],
        compiler_params=pltpu.CompilerParams(dimension_semantics=("arbitrary",),
                                             vmem_limit_bytes=VMEM_LIMIT_BYTES),
        name="moe_combine",
    )(dest, dest, y_tiles, top_w, x, gate2, ln_g, ln_b)


def _moe(oa, oc, lw, x, gate1, scale2, shift2, gate2, wr2, stri, wg, wu, wd, ln_g, ln_b, alpha,
         tm_tok, tm, tf, tc):
    bsz, seq, d = x.shape
    n_tok = bsz * seq
    n_asg = n_tok * TOP_K
    x1 = _outproj(oa, oc, lw, x, gate1, alpha, tm_tok)
    h_tiles, ids_t, w128, cnt = _router(x1, scale2, shift2, wr2, stri, tm_tok)
    counts = cnt[0, :N_EXPERTS].astype(jnp.int32)
    padded = (counts + tm - 1) // tm * tm
    pad_end = jnp.cumsum(padded)
    pad_start = pad_end - padded
    ids = ids_t.reshape(n_tok // tm_tok, SUBLANES, tm_tok)
    e_ids, ranks = ids[:, 0:TOP_K, :], ids[:, TOP_K:2 * TOP_K, :]
    start_of = sum(jnp.where(e_ids == e, pad_start[e], 0) for e in range(N_EXPERTS))
    dest = start_of + ranks
    nb = n_asg // tm + N_EXPERTS
    n_rows = nb * tm
    n_fill = n_rows - n_asg
    fill_end = jnp.cumsum(padded - counts)
    slot = jnp.arange(n_fill, dtype=jnp.int32)
    slot_e = jnp.sum((slot[:, None] >= fill_end[None, :]).astype(jnp.int32), axis=1)
    in_group = slot_e < N_EXPERTS
    ge = jnp.minimum(slot_e, N_EXPERTS - 1)
    fill_dst = jnp.where(in_group, (pad_start + counts)[ge] + slot - (fill_end - (padded - counts))[ge],
                         pad_end[-1] + slot - fill_end[-1])
    x_tiles = _group_rows(h_tiles, dest, fill_dst, n_rows)
    block_row = jnp.arange(nb, dtype=jnp.int32) * tm
    block_e = jnp.minimum(jnp.sum((block_row[:, None] >= pad_end[None, :]).astype(jnp.int32), axis=1),
                          N_EXPERTS - 1)
    n_used = (pad_end[-1:] // tm).astype(jnp.int32)
    rows_in_block = (pad_start + counts)[block_e] - block_row
    n_halves = jnp.where(rows_in_block > tm // 2, 2, 1).astype(jnp.int32)
    y_tiles = _experts(x_tiles, nb, block_e, n_used, n_halves, wg, wu, wd, tm, tf)
    return _combine(y_tiles, dest, w128, x1, gate2, ln_g, ln_b, alpha, tc)


def _layer_weights(layer, w_in, b_fgate, w_conv, q_norm_g, kv_norm_g, w_uq, w_ukv, head_norm_g, w_o,
                   ln1_g, ln1_b):
    d = w_in.shape[1]
    sizes = [FOX_W, FOX_W, FOX_W, H_FOX, CONV_W, CONV_W, CONV_W, Q_LORA, KV_LORA, QK_ROPE]
    offs = np.concatenate([[0], np.cumsum(sizes)])
    wi = w_in[layer]
    fq, fk, fv, fl, bg, cg, hin, cq, ckv, kr = [wi[:, offs[i]:offs[i + 1]] for i in range(len(sizes))]
    half = QK_ROPE // 2
    z64 = jnp.zeros((d, QK_NOPE), F32)
    z32 = jnp.zeros((d, LANES - QK_NOPE - QK_ROPE), F32)
    wa = jnp.concatenate([
        fq, fk, fv,
        jnp.pad(jnp.repeat(fl, _N_SPLIT, axis=1), ((0, 0), (0, LANES - _N_SPLIT * H_FOX))), bg, cg, hin, cq, ckv,
        jnp.concatenate([z64, kr, z32], axis=1),
        jnp.concatenate([z64, kr[:, half:], kr[:, :half], z32], axis=1)], axis=1)
    assert wa.shape[1] == _NA
    q3 = w_uq[layer].reshape(Q_LORA, H_MLA, QK_NOPE + QK_ROPE)
    q_main = jnp.pad(q3, ((0, 0), (0, 0), (0, LANES - QK_NOPE - QK_ROPE)))
    q_swap = jnp.concatenate([jnp.zeros((Q_LORA, H_MLA, QK_NOPE), F32), q3[:, :, QK_NOPE + half:],
                              q3[:, :, QK_NOPE:QK_NOPE + half],
                              jnp.zeros((Q_LORA, H_MLA, LANES - QK_NOPE - QK_ROPE), F32)], axis=2)
    wuq = jnp.concatenate([q_main.reshape(Q_LORA, H_MLA * LANES), q_swap.reshape(Q_LORA, H_MLA * LANES)],
                          axis=1).astype(BF16)
    kv3 = w_ukv[layer].reshape(KV_LORA, H_MLA, QK_NOPE + V_DIM)
    k_nope = jnp.pad(kv3[:, :, :QK_NOPE], ((0, 0), (0, 0), (0, LANES - QK_NOPE))).reshape(KV_LORA, H_MLA * LANES)
    wukv = jnp.concatenate([k_nope, kv3[:, :, QK_NOPE:].reshape(KV_LORA, MLA_W)], axis=1).astype(BF16)
    hg = head_norm_g[layer]
    wo = w_o[layer]
    c0, c1 = FOX_W, FOX_W + CONV_W
    return {
        "wa": wa, "wuq": wuq, "wukv": wukv,
        "qg": q_norm_g[layer].reshape(1, Q_LORA), "kvg": kv_norm_g[layer].reshape(1, KV_LORA),
        "bf": jnp.pad(jnp.repeat(b_fgate[layer], _N_SPLIT), (0, LANES - _N_SPLIT * H_FOX)).reshape(1, LANES),
        "wconv": jnp.pad(w_conv[layer], ((0, SUBLANES - CONV_K), (0, 0))),
        "hg_conv": hg[c0:c1].reshape(1, CONV_W),
        "hg_attn": jnp.concatenate([hg[:c0], hg[c1:]]).reshape(1, FOX_W + MLA_W),
        "wo_a": jnp.concatenate([wo[:c0], wo[c1:]], axis=0).astype(BF16),
        "wo_c": wo[c0:c1].astype(BF16),
        "ln1_g": ln1_g[layer].reshape(1, d), "ln1_b": ln1_b[layer].reshape(1, d),
    }


def _constants(seq, ts):
    pos = jnp.arange(seq, dtype=F32)
    inv_freq = ROPE_THETA ** (-jnp.arange(0, QK_ROPE, 2, dtype=F32) / QK_ROPE)
    ang = pos[:, None] * inv_freq[None, :]
    cos, sin = jnp.cos(ang), jnp.sin(ang)
    pad_r = jnp.zeros((seq, LANES - QK_NOPE - QK_ROPE), F32)
    ctab = jnp.concatenate([jnp.ones((seq, QK_NOPE), F32), cos, cos, pad_r], axis=1)
    stab = jnp.concatenate([jnp.zeros((seq, QK_NOPE), F32), -sin, sin, pad_r], axis=1)
    q_scale = (QK_NOPE + QK_ROPE) ** -0.5 * LOG2E
    lanes = np.arange(LANES)
    lm = np.zeros((SUBLANES, LANES), np.float32)
    for part in range(_N_SPLIT):
        lm[part] = (lanes % _N_SPLIT == part) & (lanes < _N_SPLIT * H_FOX)
    for row, base in ((3, HEAD_DIM), (5, 0)):
        lm[row] = (lanes >= base) & (lanes < base + _N_SPLIT)
        lm[row + 1] = (lanes >= base + _N_SPLIT) & (lanes < base + 2 * _N_SPLIT)

    def group_mean(n):
        gidx = np.arange(n) // HEAD_DIM
        return (gidx[:, None] == gidx[None, :]).astype(np.float32) / HEAD_DIM

    return {
        "cq": ctab * q_scale, "sq": stab * q_scale, "ck": ctab, "sk": stab,
        "tri": jnp.asarray(np.tril(np.ones((ts, ts), np.float32)), BF16),
        "stri": jnp.asarray(np.tril(np.ones((ts, ts), np.float32), k=-1), BF16),
        "lane_masks": jnp.asarray(lm),
        "gm256": jnp.asarray(group_mean(CONV_W), BF16), "gm128": jnp.asarray(group_mean(LANES), BF16),
    }


def _tile(n, pref):
    t = min(n, pref)
    assert n % t == 0, (n, pref)
    return t


def kernel(x, c, w_mod, b_mod, w_in, b_fgate, w_conv, q_norm_g, kv_norm_g, w_uq, w_ukv, head_norm_g, w_o, ln1_g, ln1_b, ln2_g, ln2_b, ffn_w_gate, ffn_w_up, ffn_w_down, router_w, exp_w_gate, exp_w_up, exp_w_down):
    bsz, seq, d = x.shape
    depth = w_mod.shape[0]
    assert d == D_MODEL and bsz <= SUBLANES
    alpha = (2 * depth) ** 0.25
    ts = _tile(seq, 512)
    tq = _tile(seq, 512)
    tm_e = 1024
    tf_e = 512
    tc = _tile(seq, 256)
    assert (bsz * seq * TOP_K) % tm_e == 0 and exp_w_gate.shape[-1] % tf_e == 0

    consts = _constants(seq, ts)
    c_pad = jnp.pad(c, ((0, SUBLANES - bsz), (0, 0)))
    mod = _modulation(c_pad, w_mod, b_mod)[:, :bsz, :]

    for layer in range(depth):
        m6 = mod[layer].reshape(bsz, 6, 1, d)
        shift1, scale1, gate1, shift2, scale2, gate2 = [m6[:, i] for i in range(6)]
        lw = _layer_weights(layer, w_in, b_fgate, w_conv, q_norm_g, kv_norm_g, w_uq, w_ukv, head_norm_g,
                            w_o, ln1_g, ln1_b)
        qp, kp, vt, oc = _inproj(x, scale1, shift1, lw, consts, ts)
        oa = _attention(qp, kp, vt, lw["hg_attn"], consts["gm128"], tq, 12)
        j = layer // 2
        g2, b2 = ln2_g[layer].reshape(1, d), ln2_b[layer].reshape(1, d)
        if layer % 2 == 0:
            x = _ffn_dense(oa, oc, lw, x, gate1, scale2, shift2, gate2, ffn_w_gate[j].astype(BF16),
                           ffn_w_up[j].astype(BF16), ffn_w_down[j].astype(BF16), g2, b2, alpha, ts)
        else:
            wr = jnp.pad(router_w[j], ((0, 0), (0, LANES - N_EXPERTS)))
            wr_hi = wr.astype(BF16)
            wr_lo = (wr - wr_hi.astype(F32)).astype(BF16)
            x = _moe(oa, oc, lw, x, gate1, scale2, shift2, gate2, jnp.stack([wr_hi, wr_lo]), consts["stri"],
                     exp_w_gate[j].astype(BF16), exp_w_up[j].astype(BF16), exp_w_down[j].astype(BF16),
                     g2, b2, alpha, ts, tm_e, tf_e, tc)
    return x
```

```python
import functools

import numpy as np
import jax
import jax.numpy as jnp
from jax import lax
from jax.experimental import pallas as pl
from jax.experimental.pallas import tpu as pltpu

F32 = jnp.float32
BF16 = jnp.bfloat16

D_MODEL = 1024
HEAD_DIM = 64
H_FOX = 6
FOX_W = H_FOX * HEAD_DIM
CONV_W = 256
CONV_K = 3
H_MLA = 6
Q_LORA = 256
KV_LORA = 256
QK_NOPE = 64
QK_ROPE = 32
V_DIM = 64
MLA_W = H_MLA * V_DIM
N_HEADS = H_FOX + H_MLA
ROPE_THETA = 10000.0
N_EXPERTS = 8
TOP_K = 2
LN_EPS = 1e-5
RMS_EPS = 1e-6

LANES = 128
SUBLANES = 8
VMEM_LIMIT_BYTES = 56 * 1024 * 1024

_HW = 6 * LANES
_QF0, _KF0, _VF0, _FL0 = 0, FOX_W, 2 * FOX_W, 3 * FOX_W
_BG0 = _FL0 + LANES
_CG0, _HI0, _CQ0, _CKV0 = _BG0 + CONV_W, _BG0 + 2 * CONV_W, _BG0 + 3 * CONV_W, _BG0 + 3 * CONV_W + Q_LORA
_KR0 = _CKV0 + KV_LORA
_KRS0 = _KR0 + LANES
_NA = _KRS0 + LANES
LOG2E = 1.4426950408889634
ONES_ROWS = 16
VT_ROWS = V_DIM + ONES_ROWS
_N_SPLIT = 3


def _const_spec(shape):
    zeros = (0,) * len(shape)
    return pl.BlockSpec(shape, lambda *_: zeros, pipeline_mode=pl.Buffered(1))


def _silu(v):
    return v * (1.0 / (1.0 + jnp.exp(-v)))


def _split3(v):
    hi = v.astype(BF16)
    r1 = v - hi.astype(F32)
    mid = r1.astype(BF16)
    lo = (r1 - mid.astype(F32)).astype(BF16)
    return hi, mid, lo


def _group_mean_sq(v, gmat_ref):
    sq = v * v
    hi = sq.astype(BF16)
    lo = (sq - hi.astype(F32)).astype(BF16)
    g = gmat_ref[...]
    return (jnp.dot(hi, g, preferred_element_type=F32) + jnp.dot(lo, g, preferred_element_type=F32))


def _layer_norm_rows(r, g, b):
    mu = jnp.mean(r, axis=-1, keepdims=True)
    rc = r - mu
    var = jnp.mean(rc * rc, axis=-1, keepdims=True)
    return rc * lax.rsqrt(var + LN_EPS) * g + b


def _mod_kernel(c_ref, w_ref, b_ref, o_ref):
    act = _silu(c_ref[...]).astype(BF16)
    o_ref[0] = jnp.dot(act, w_ref[0].astype(BF16), preferred_element_type=F32) + b_ref[0]


def _modulation(c_pad, w_mod, b_mod):
    depth, d, n = w_mod.shape
    tn = 1024
    return pl.pallas_call(
        _mod_kernel,
        grid=(depth, n // tn),
        in_specs=[pl.BlockSpec((SUBLANES, d), lambda l, j: (0, 0)),
                  pl.BlockSpec((1, d, tn), lambda l, j: (l, 0, j)),
                  pl.BlockSpec((1, 1, tn), lambda l, j: (l, 0, j))],
        out_specs=pl.BlockSpec((1, SUBLANES, tn), lambda l, j: (l, 0, j)),
        out_shape=jax.ShapeDtypeStruct((depth, SUBLANES, n), F32),
        compiler_params=pltpu.CompilerParams(dimension_semantics=("arbitrary", "arbitrary")),
        name="modulation",
    )(c_pad, w_mod, b_mod.reshape(depth, 1, n))


def _inproj_kernel(x_ref, sc_ref, sh_ref, wa_ref, cq_ref, sq_ref, ck_ref, sk_ref, wuq_ref, wukv_ref,
                   qg_ref, kvg_ref, bf_ref, wconv_ref, hgc_ref, tri_ref, lm_ref, gm_ref,
                   qp_ref, kp_ref, vt_ref, oc_ref, fcarry_ref, ubuf_ref, wab_ref, *, ts):
    @pl.when(jnp.logical_and(pl.program_id(0) == 0, pl.program_id(1) == 0))
    def _():
        for c in range(_NA // LANES):
            sl = slice(c * LANES, (c + 1) * LANES)
            wab_ref[:, sl] = wa_ref[:, sl].astype(BF16)

    @pl.when(pl.program_id(1) == 0)
    def _():
        fcarry_ref[...] = jnp.zeros_like(fcarry_ref)
        ubuf_ref[pl.ds(0, SUBLANES), :] = jnp.zeros((SUBLANES, CONV_W), F32)

    hb = (x_ref[0] * (1.0 + sc_ref[0]) + sh_ref[0]).astype(BF16)

    def proj(lo, hi):
        return jnp.dot(hb, wab_ref[:, lo:hi], preferred_element_type=F32)

    def put_values_t(head, v_half_t):
        vt_ref[0, head, 0:V_DIM, :] = v_half_t.astype(BF16)
        vt_ref[0, head, V_DIM:VT_ROWS, :] = jnp.ones((VT_ROWS - V_DIM, ts), BF16)

    z_fl = proj(_FL0, _FL0 + LANES)
    z_cq = proj(_CQ0, _CQ0 + Q_LORA)
    z_ckv = proj(_CKV0, _CKV0 + KV_LORA)
    z_cg = proj(_CG0, _CG0 + CONV_W)
    z_hi = proj(_HI0, _HI0 + CONV_W)
    z_bg = proj(_BG0, _BG0 + CONV_W)
    zq = proj(_QF0, _QF0 + FOX_W) * (HEAD_DIM ** -0.5 * LOG2E)
    zk = proj(_KF0, _KF0 + FOX_W)
    zv = proj(_VF0, _VF0 + FOX_W)
    z_kr = proj(_KR0, _KR0 + LANES)
    z_krs = proj(_KRS0, _KRS0 + LANES)

    a = z_fl + bf_ref[...]
    logf = jnp.minimum(a, 0.0) - jnp.log1p(jnp.exp(-jnp.abs(a)))
    tri = tri_ref[...]
    csum = sum(jnp.dot(tri, part, preferred_element_type=F32) for part in _split3(logf))
    fcum = fcarry_ref[...] + csum
    fcarry_ref[...] = fcum[ts - 1:ts, :]
    f_hi, f_mid, f_lo = _split3(fcum * LOG2E)
    lm = lm_ref[...]
    fparts = (f_hi.astype(F32) * lm[0:1, :] + f_mid.astype(F32) * lm[1:2, :]
              + f_lo.astype(F32) * lm[2:3, :])
    lane = lax.broadcasted_iota(jnp.int32, (ts, LANES), 1)
    low_half = lane < HEAD_DIM
    for h in range(H_FOX):
        blk = slice((h // 2) * LANES, (h // 2 + 1) * LANES)
        if h % 2 == 0:
            base, f_mask, one_mask, keep = HEAD_DIM, lm[3:4, :], lm[4:5, :], low_half
        else:
            base, f_mask, one_mask, keep = 0, lm[5:6, :], lm[6:7, :], jnp.logical_not(low_half)
        f_at_q = pltpu.roll(fparts, (base - _N_SPLIT * h) % LANES, axis=1)
        f_at_k = pltpu.roll(fparts, (base + _N_SPLIT - _N_SPLIT * h) % LANES, axis=1)
        qp_ref[0, h] = jnp.where(keep, zq[:, blk], f_at_q * f_mask + one_mask).astype(BF16)
        kp_ref[0, h] = jnp.where(keep, zk[:, blk], f_mask - f_at_k * one_mask).astype(BF16)
    for j in range(H_FOX // 2):
        vt = zv[:, j * LANES:(j + 1) * LANES].T
        put_values_t(2 * j, vt[0:V_DIM])
        put_values_t(2 * j + 1, vt[V_DIM:LANES])

    u = z_cg * z_hi
    ubuf_ref[pl.ds(SUBLANES, ts), :] = u
    u1 = ubuf_ref[pl.ds(SUBLANES - 1, ts), :]
    u2 = ubuf_ref[pl.ds(SUBLANES - 2, ts), :]
    ubuf_ref[pl.ds(0, SUBLANES), :] = u[ts - SUBLANES:ts, :]
    wc = wconv_ref[...]
    oc = z_bg * (wc[0:1, :] * u2 + wc[1:2, :] * u1 + wc[2:3, :] * u)
    ocn = oc * lax.rsqrt(_group_mean_sq(oc, gm_ref) + RMS_EPS) * hgc_ref[...]
    oc_ref[0] = ocn.astype(BF16)

    def rms(v, g):
        return (v * lax.rsqrt(jnp.mean(v * v, axis=-1, keepdims=True) + RMS_EPS) * g).astype(BF16)

    cqn = rms(z_cq, qg_ref[...])
    qm = jnp.dot(cqn, wuq_ref[:, 0:_HW], preferred_element_type=F32)
    qs = jnp.dot(cqn, wuq_ref[:, _HW:2 * _HW], preferred_element_type=F32)
    kvn = rms(z_ckv, kvg_ref[...])
    kn = jnp.dot(kvn, wukv_ref[:, 0:_HW], preferred_element_type=F32)
    vm = jnp.dot(kvn, wukv_ref[:, _HW:_HW + MLA_W], preferred_element_type=F32)
    krr = z_kr * ck_ref[...] + z_krs * sk_ref[...]
    cq, sq = cq_ref[...], sq_ref[...]
    for h in range(H_MLA):
        sl = slice(h * LANES, (h + 1) * LANES)
        qp_ref[0, H_FOX + h] = (qm[:, sl] * cq + qs[:, sl] * sq).astype(BF16)
        kp_ref[0, H_FOX + h] = (kn[:, sl] + krr).astype(BF16)
    for j in range(H_MLA // 2):
        vt = vm[:, j * LANES:(j + 1) * LANES].T
        put_values_t(H_FOX + 2 * j, vt[0:V_DIM])
        put_values_t(H_FOX + 2 * j + 1, vt[V_DIM:LANES])


def _inproj(x, scale1, shift1, lw, consts, ts):
    bsz, seq, d = x.shape
    kern = functools.partial(_inproj_kernel, ts=ts)
    row = lambda b, j: (b, 0, 0)
    tab = pl.BlockSpec((ts, LANES), lambda b, j: (j, 0))
    return pl.pallas_call(
        kern,
        grid=(bsz, seq // ts),
        in_specs=[pl.BlockSpec((1, ts, d), lambda b, j: (b, j, 0)),
                  pl.BlockSpec((1, 1, d), row), pl.BlockSpec((1, 1, d), row),
                  _const_spec((d, _NA)), tab, tab, tab, tab,
                  _const_spec((Q_LORA, 2 * _HW)), _const_spec((KV_LORA, _HW + MLA_W)),
                  _const_spec((1, Q_LORA)), _const_spec((1, KV_LORA)), _const_spec((1, LANES)),
                  _const_spec((SUBLANES, CONV_W)), _const_spec((1, CONV_W)),
                  _const_spec((ts, ts)), _const_spec((SUBLANES, LANES)), _const_spec((CONV_W, CONV_W))],
        out_specs=[pl.BlockSpec((1, N_HEADS, ts, LANES), lambda b, j: (b, 0, j, 0)),
                   pl.BlockSpec((1, N_HEADS, ts, LANES), lambda b, j: (b, 0, j, 0)),
                   pl.BlockSpec((1, N_HEADS, VT_ROWS, ts), lambda b, j: (b, 0, 0, j)),
                   pl.BlockSpec((1, ts, CONV_W), lambda b, j: (b, j, 0))],
        out_shape=[jax.ShapeDtypeStruct((bsz, N_HEADS, seq, LANES), BF16),
                   jax.ShapeDtypeStruct((bsz, N_HEADS, seq, LANES), BF16),
                   jax.ShapeDtypeStruct((bsz, N_HEADS, VT_ROWS, seq), BF16),
                   jax.ShapeDtypeStruct((bsz, seq, CONV_W), BF16)],
        scratch_shapes=[pltpu.VMEM((1, LANES), F32), pltpu.VMEM((ts + SUBLANES, CONV_W), F32),
                        pltpu.VMEM((d, _NA), BF16)],
        compiler_params=pltpu.CompilerParams(dimension_semantics=("arbitrary", "arbitrary"),
                                             vmem_limit_bytes=VMEM_LIMIT_BYTES),
        name="inproj",
    )(x, scale1, shift1, lw["wa"], consts["cq"], consts["sq"], consts["ck"], consts["sk"],
      lw["wuq"], lw["wukv"], lw["qg"], lw["kvg"], lw["bf"], lw["wconv"], lw["hg_conv"],
      consts["tri"], consts["lane_masks"], consts["gm256"])


def _attn_kernel(qi_ref, kj_ref, q_ref, k_ref, vt_ref, g_ref, gm_ref, o_ref, m_ref, acc_ref, s_ref, *, nh):
    step = pl.program_id(2)
    qi = qi_ref[step]
    kj = kj_ref[step]

    @pl.when(kj == 0)
    def _():
        m_ref[...] = jnp.full(m_ref.shape, -jnp.inf, F32)
        acc_ref[...] = jnp.zeros_like(acc_ref)

    def update(masked):
        if masked:
            tile = (q_ref.shape[2], q_ref.shape[2])
            causal = lax.broadcasted_iota(jnp.int32, tile, 0) <= lax.broadcasted_iota(jnp.int32, tile, 1)

        def scores(h):
            st = lax.dot_general(k_ref[0, h], q_ref[0, h], (((1,), (1,)), ((), ())),
                                 preferred_element_type=F32)
            if masked:
                st = jnp.where(causal, st, -jnp.inf)
            s_ref[h] = st

        for h in range(nh):
            scores(h)
        pts, alphas = [], []
        for h in range(nh):
            m_prev = m_ref[h]
            m_new = jnp.maximum(m_prev, jnp.max(s_ref[h], axis=0, keepdims=True))
            pts.append(jnp.exp2(s_ref[h] - m_new).astype(BF16))
            alphas.append(jnp.exp2(m_prev - m_new))
            m_ref[h] = m_new
        for h in range(nh):
            acc_ref[h] = alphas[h] * acc_ref[h] + jnp.dot(vt_ref[0, h], pts[h], preferred_element_type=F32)

    @pl.when(kj < qi)
    def _():
        update(False)

    @pl.when(kj == qi)
    def _():
        update(True)
        for pair in range(nh // 2):
            ot = jnp.concatenate(
                [acc_ref[h, 0:V_DIM, :] / jnp.concatenate([acc_ref[h, V_DIM:VT_ROWS, :]] * (V_DIM // ONES_ROWS), axis=0)
                 for h in (2 * pair, 2 * pair + 1)], axis=0)
            o = ot.T
            sl = slice(pair * LANES, (pair + 1) * LANES)
            on = o * lax.rsqrt(_group_mean_sq(o, gm_ref) + RMS_EPS) * g_ref[:, sl]
            o_ref[0, :, sl] = on.astype(BF16)


def _attention(qp, kp, vt, g_attn, gm128, tq, nh):
    bsz, n_heads, seq, _ = qp.shape
    nq = seq // tq
    ow = (nh // 2) * LANES
    qi_tab = np.concatenate([np.full(i + 1, i) for i in range(nq)]).astype(np.int32)
    kj_tab = np.concatenate([np.arange(i + 1) for i in range(nq)]).astype(np.int32)
    grid_spec = pltpu.PrefetchScalarGridSpec(
        num_scalar_prefetch=2,
        grid=(bsz, n_heads // nh, len(qi_tab)),
        in_specs=[pl.BlockSpec((1, nh, tq, LANES), lambda b, p, s, qi, kj: (b, p, qi[s], 0)),
                  pl.BlockSpec((1, nh, tq, LANES), lambda b, p, s, qi, kj: (b, p, kj[s], 0)),
                  pl.BlockSpec((1, nh, VT_ROWS, tq), lambda b, p, s, qi, kj: (b, p, 0, kj[s])),
                  pl.BlockSpec((1, ow), lambda b, p, s, qi, kj: (0, p)),
                  _const_spec((LANES, LANES))],
        out_specs=pl.BlockSpec((1, tq, ow), lambda b, p, s, qi, kj: (b, qi[s], p)),
        scratch_shapes=[pltpu.VMEM((nh, 1, tq), F32), pltpu.VMEM((nh, VT_ROWS, tq), F32),
                        pltpu.VMEM((nh, tq, tq), F32)])
    return pl.pallas_call(
        functools.partial(_attn_kernel, nh=nh),
        grid_spec=grid_spec,
        out_shape=jax.ShapeDtypeStruct((bsz, seq, (n_heads // 2) * LANES), BF16),
        compiler_params=pltpu.CompilerParams(
            dimension_semantics=("arbitrary", "arbitrary", "arbitrary"),
            vmem_limit_bytes=VMEM_LIMIT_BYTES),
        name="attention",
    )(jnp.asarray(qi_tab), jnp.asarray(kj_tab), qp, kp, vt, g_attn, gm128)


def _mixer_out_rows(oa_ref, oc_ref, woa_ref, woc_ref, x_ref, gate_ref, g_ref, b_ref, alpha):
    y = (jnp.dot(oa_ref[0], woa_ref[...], preferred_element_type=F32)
         + jnp.dot(oc_ref[0], woc_ref[...], preferred_element_type=F32))
    r = alpha * x_ref[0] + (1.0 + gate_ref[0]) * y
    return _layer_norm_rows(r, g_ref[...], b_ref[...])


def _mixer_out_specs(oa, d, tm, act, row):
    wa_rows = oa.shape[-1]
    return [pl.BlockSpec((1, tm, wa_rows), act), pl.BlockSpec((1, tm, CONV_W), act),
            _const_spec((wa_rows, d)), _const_spec((CONV_W, d)),
            pl.BlockSpec((1, tm, d), act), pl.BlockSpec((1, 1, d), row), _const_spec((1, d)), _const_spec((1, d))]


def _outproj_kernel(oa_ref, oc_ref, woa_ref, woc_ref, x_ref, gate_ref, g_ref, b_ref, o_ref, *, alpha):
    o_ref[0] = _mixer_out_rows(oa_ref, oc_ref, woa_ref, woc_ref, x_ref, gate_ref, g_ref, b_ref, alpha)


def _outproj(oa, oc, lw, x, gate1, alpha, tm):
    bsz, seq, d = x.shape
    act = lambda b, j: (b, j, 0)
    return pl.pallas_call(
        functools.partial(_outproj_kernel, alpha=alpha),
        grid=(bsz, seq // tm),
        in_specs=_mixer_out_specs(oa, d, tm, act, lambda b, j: (b, 0, 0)),
        out_specs=pl.BlockSpec((1, tm, d), act),
        out_shape=jax.ShapeDtypeStruct((bsz, seq, d), F32),
        compiler_params=pltpu.CompilerParams(dimension_semantics=("arbitrary", "arbitrary"),
                                             vmem_limit_bytes=VMEM_LIMIT_BYTES),
        name="outproj",
    )(oa, oc, lw["wo_a"], lw["wo_c"], x, gate1, lw["ln1_g"], lw["ln1_b"])


def _ffn_kernel(oa_ref, oc_ref, woa_ref, woc_ref, xin_ref, gate1_ref, g1_ref, b1_ref,
                sc_ref, sh_ref, gate_ref, wg_ref, wu_ref, wd_ref, g_ref, b_ref, o_ref, *, alpha, tf):
    x = _mixer_out_rows(oa_ref, oc_ref, woa_ref, woc_ref, xin_ref, gate1_ref, g1_ref, b1_ref, alpha)
    hb = (x * (1.0 + sc_ref[0]) + sh_ref[0]).astype(BF16)
    chunks = [slice(c * tf, (c + 1) * tf) for c in range(wg_ref.shape[1] // tf)]
    acts = []
    for sl in chunks:
        gt = jnp.dot(hb, wg_ref[:, sl], preferred_element_type=F32)
        up = jnp.dot(hb, wu_ref[:, sl], preferred_element_type=F32)
        acts.append((_silu(gt) * up).astype(BF16))
    acc = sum(jnp.dot(act, wd_ref[sl, :], preferred_element_type=F32) for act, sl in zip(acts, chunks))
    r = alpha * x + (1.0 + gate_ref[0]) * acc
    o_ref[0] = _layer_norm_rows(r, g_ref[...], b_ref[...])


def _ffn_dense(oa, oc, lw, x, gate1, scale2, shift2, gate2, wg, wu, wd, ln_g, ln_b, alpha, tm):
    bsz, seq, d = x.shape
    dff = wg.shape[1]
    tf = next(t for t in (704, 512, 256, 128, dff) if dff % t == 0 and t % LANES == 0)
    nj = seq // tm
    act = lambda b, j: (b, j, 0)
    row = lambda b, j: (b, 0, 0)
    return pl.pallas_call(
        functools.partial(_ffn_kernel, alpha=alpha, tf=tf),
        grid=(bsz, nj),
        in_specs=_mixer_out_specs(oa, d, tm, act, row) + [
            pl.BlockSpec((1, 1, d), row), pl.BlockSpec((1, 1, d), row), pl.BlockSpec((1, 1, d), row),
            _const_spec((d, dff)), _const_spec((d, dff)), _const_spec((dff, d)),
            _const_spec((1, d)), _const_spec((1, d))],
        out_specs=pl.BlockSpec((1, tm, d), act),
        out_shape=jax.ShapeDtypeStruct((bsz, seq, d), F32),
        compiler_params=pltpu.CompilerParams(dimension_semantics=("arbitrary", "arbitrary"),
                                             vmem_limit_bytes=VMEM_LIMIT_BYTES),
        name="ffn_dense",
    )(oa, oc, lw["wo_a"], lw["wo_c"], x, gate1, lw["ln1_g"], lw["ln1_b"],
      scale2, shift2, gate2, wg, wu, wd, ln_g, ln_b)


def _to_token_tiles(ref, v, n):
    for c in range(v.shape[1] // LANES):
        ref[pl.ds(c, n, stride=SUBLANES), :] = v[:, c * LANES:(c + 1) * LANES]


def _from_token_tiles(ref, n, first=0, tiles_per_row=1):
    stride = tiles_per_row * SUBLANES
    return jnp.concatenate([ref[pl.ds(first * SUBLANES + c, n, stride=stride), :] for c in range(SUBLANES)],
                           axis=1)


def _router_kernel(x_ref, sc_ref, sh_ref, wr_ref, stri_ref, h_ref, idx_ref, w_ref, cnt_ref, carry_ref, *, tm):
    @pl.when(pl.program_id(0) == 0)
    def _():
        carry_ref[...] = jnp.zeros_like(carry_ref)

    h = x_ref[0] * (1.0 + sc_ref[0]) + sh_ref[0]
    _to_token_tiles(h_ref, h, tm)
    h_hi = h.astype(BF16)
    h_lo = (h - h_hi.astype(F32)).astype(BF16)
    w_hi, w_lo = wr_ref[0], wr_ref[1]
    logits = (jnp.dot(h_hi, w_hi, preferred_element_type=F32)
              + jnp.dot(h_hi, w_lo, preferred_element_type=F32)
              + jnp.dot(h_lo, w_hi, preferred_element_type=F32))
    lane = lax.broadcasted_iota(jnp.int32, logits.shape, 1).astype(F32)
    neg = -jnp.inf
    lg = jnp.where(lane < N_EXPERTS, logits, neg)
    m1 = jnp.max(lg, axis=-1, keepdims=True)
    i1 = jnp.min(jnp.where(lg == m1, lane, float(LANES)), axis=-1, keepdims=True)
    lg2 = jnp.where(lane == i1, neg, lg)
    m2 = jnp.max(lg2, axis=-1, keepdims=True)
    i2 = jnp.min(jnp.where(lg2 == m2, lane, float(LANES)), axis=-1, keepdims=True)
    e2 = jnp.exp(m2 - m1)
    denom = 1.0 + e2
    w_ref[...] = jnp.where(lane == 0.0, 1.0 / denom, e2 / denom)
    first, second = lane == i1, lane == i2
    chosen = jnp.logical_or(first, second).astype(F32)
    before = carry_ref[...] + jnp.dot(stri_ref[...], chosen.astype(BF16), preferred_element_type=F32)
    rank1 = jnp.sum(jnp.where(first, before, 0.0), axis=-1, keepdims=True)
    rank2 = jnp.sum(jnp.where(second, before, 0.0), axis=-1, keepdims=True)
    idx_ref[...] = jnp.where(lane == 0.0, i1, jnp.where(lane == 1.0, i2, jnp.where(lane == 2.0, rank1, rank2))
                             ).astype(jnp.int32)
    total = carry_ref[...] + jnp.sum(chosen, axis=0, keepdims=True)
    carry_ref[...] = total
    cnt_ref[...] = jnp.broadcast_to(total, cnt_ref.shape)


def _router(x, scale2, shift2, wr2, stri, tm):
    bsz, seq, d = x.shape
    assert d == SUBLANES * LANES and bsz * seq * TOP_K < 2 ** 24
    nj = seq // tm
    n_tiles = bsz * nj
    row = lambda i: (i // nj, 0, 0)
    tok = lambda i: (i, 0)
    return pl.pallas_call(
        functools.partial(_router_kernel, tm=tm),
        grid=(n_tiles,),
        in_specs=[pl.BlockSpec((1, tm, d), lambda i: (i // nj, i % nj, 0)),
                  pl.BlockSpec((1, 1, d), row), pl.BlockSpec((1, 1, d), row),
                  _const_spec((2, d, LANES)), _const_spec((tm, tm))],
        out_specs=[pl.BlockSpec((tm * SUBLANES, LANES), tok),
                   pl.BlockSpec((tm, LANES), tok), pl.BlockSpec((tm, LANES), tok),
                   pl.BlockSpec((SUBLANES, LANES), lambda i: (0, 0))],
        out_shape=[jax.ShapeDtypeStruct((n_tiles * tm * SUBLANES, LANES), F32),
                   jax.ShapeDtypeStruct((bsz * seq, LANES), jnp.int32),
                   jax.ShapeDtypeStruct((bsz * seq, LANES), F32),
                   jax.ShapeDtypeStruct((SUBLANES, LANES), F32)],
        scratch_shapes=[pltpu.VMEM((1, LANES), F32)],
        compiler_params=pltpu.CompilerParams(dimension_semantics=("arbitrary",),
                                             vmem_limit_bytes=VMEM_LIMIT_BYTES),
        name="router",
    )(x, scale2, shift2, wr2, stri)


_COPY_UNROLL = 8


def _tile_at(ref, t):
    return ref.at[pl.ds(pl.multiple_of(t * SUBLANES, SUBLANES), SUBLANES)]


def _issue_tile_copies(n, copy_of):
    def group(g, carry):
        copies = [copy_of(g, u) for u in range(_COPY_UNROLL)]
        for u, (src, dst, sem) in enumerate(copies):
            pltpu.make_async_copy(src, dst, sem).start(priority=u % 2)
        return carry

    assert n % _COPY_UNROLL == 0
    lax.fori_loop(0, n // _COPY_UNROLL, group, 0)


def _group_kernel(di_ref, fi_ref, src_ref, dst_hbm, zero_ref, sem, *, tb, nf):
    zero_ref[...] = jnp.zeros_like(zero_ref)
    per_group = _COPY_UNROLL // TOP_K
    _issue_tile_copies(TOP_K * tb, lambda g, u: (_tile_at(src_ref, g * per_group + u // TOP_K),
                                                 _tile_at(dst_hbm, di_ref[0, 0, g * _COPY_UNROLL + u]), sem))
    _issue_tile_copies(nf, lambda g, u: (zero_ref, _tile_at(dst_hbm, fi_ref[0, 0, g * _COPY_UNROLL + u]), sem))
    block = pl.ds(0, tb * SUBLANES)
    for _ in range(TOP_K):
        pltpu.make_async_copy(src_ref, dst_hbm.at[block], sem).wait()

    def wait_fill(r, carry):
        pltpu.make_async_copy(zero_ref, _tile_at(dst_hbm, 0), sem).wait()
        return carry

    lax.fori_loop(0, nf, wait_fill, 0)


def _group_rows(h_tiles, dest, fill_dst, n_rows, tb):
    n_asg = dest.shape[0]
    steps = n_asg // (TOP_K * tb)
    nf = fill_dst.shape[0] // steps
    assert steps * TOP_K * tb == n_asg and steps * nf == fill_dst.shape[0]
    return pl.pallas_call(
        functools.partial(_group_kernel, tb=tb, nf=nf),
        grid=(steps,),
        in_specs=[pl.BlockSpec((1, 1, TOP_K * tb), lambda i: (i, 0, 0), memory_space=pltpu.SMEM),
                  pl.BlockSpec((1, 1, nf), lambda i: (i, 0, 0), memory_space=pltpu.SMEM),
                  pl.BlockSpec((tb * SUBLANES, LANES), lambda i: (i, 0))],
        out_specs=pl.BlockSpec(memory_space=pl.ANY),
        out_shape=jax.ShapeDtypeStruct((n_rows * SUBLANES, LANES), h_tiles.dtype),
        scratch_shapes=[pltpu.VMEM((SUBLANES, LANES), h_tiles.dtype), pltpu.SemaphoreType.DMA(())],
        compiler_params=pltpu.CompilerParams(dimension_semantics=("arbitrary",), has_side_effects=True),
        name="moe_group",
    )(dest.astype(jnp.int32).reshape(steps, 1, TOP_K * tb), fill_dst.astype(jnp.int32).reshape(steps, 1, nf),
      h_tiles)


def _expert_kernel(be_ref, nu_ref, nh_ref, x_ref, wg_ref, wu_ref, wd_ref, o_ref, acc_ref, xb_ref, *, tm):
    i = pl.program_id(0)
    f = pl.program_id(1)
    used = i < nu_ref[0]
    whole = jnp.logical_and(used, nh_ref[i] == 2)
    first_half_only = jnp.logical_and(used, nh_ref[i] == 1)

    @pl.when(jnp.logical_and(used, f == 0))
    def _():
        xb_ref[...] = _from_token_tiles(x_ref, tm).astype(BF16)
        acc_ref[...] = jnp.zeros_like(acc_ref)

    def swiglu_step(rows):
        xb = xb_ref[0:rows, :]
        tf = wg_ref.shape[2]
        halves = [slice(0, tf // 2), slice(tf // 2, tf)]
        acts = []
        for sl in halves:
            gt = jnp.dot(xb, wg_ref[0, :, sl], preferred_element_type=F32)
            up = jnp.dot(xb, wu_ref[0, :, sl], preferred_element_type=F32)
            acts.append((_silu(gt) * up).astype(BF16))
        acc_ref[0:rows, :] += sum(jnp.dot(act, wd_ref[0, sl, :], preferred_element_type=F32)
                                  for act, sl in zip(acts, halves))

    pl.when(whole)(functools.partial(swiglu_step, tm))
    pl.when(first_half_only)(functools.partial(swiglu_step, tm // 2))

    last = f == pl.num_programs(1) - 1

    @pl.when(jnp.logical_and(used, last))
    def _():
        _to_token_tiles(o_ref, acc_ref[...], tm)

    @pl.when(jnp.logical_and(jnp.logical_not(used), last))
    def _():
        o_ref[...] = jnp.zeros_like(o_ref)


def _experts(x_tiles, nb, block_e, n_used, n_halves, wg, wu, wd, tm, tf):
    n_exp, d, dff = wg.shape
    nf = dff // tf

    def xmap(i, f, be, nu, nh):
        return (jnp.minimum(i, nu[0] - 1), 0)

    def fidx(i, f, nu):
        return jnp.where(i < nu[0], f, nf - 1)

    grid_spec = pltpu.PrefetchScalarGridSpec(
        num_scalar_prefetch=3,
        grid=(nb, nf),
        in_specs=[pl.BlockSpec((tm * SUBLANES, LANES), xmap),
                  pl.BlockSpec((1, d, tf), lambda i, f, be, nu, nh: (be[i], 0, fidx(i, f, nu))),
                  pl.BlockSpec((1, d, tf), lambda i, f, be, nu, nh: (be[i], 0, fidx(i, f, nu))),
                  pl.BlockSpec((1, tf, d), lambda i, f, be, nu, nh: (be[i], fidx(i, f, nu), 0))],
        out_specs=pl.BlockSpec((tm * SUBLANES, LANES), lambda i, f, be, nu, nh: (i, 0)),
        scratch_shapes=[pltpu.VMEM((tm, d), F32), pltpu.VMEM((tm, d), BF16)])
    return pl.pallas_call(
        functools.partial(_expert_kernel, tm=tm),
        grid_spec=grid_spec,
        out_shape=jax.ShapeDtypeStruct((nb * tm * SUBLANES, LANES), F32),
        compiler_params=pltpu.CompilerParams(dimension_semantics=("arbitrary", "arbitrary"),
                                             vmem_limit_bytes=VMEM_LIMIT_BYTES),
        name="moe_experts",
    )(block_e, n_used, n_halves, x_tiles, wg, wu, wd)


def _combine_kernel(cur_ref, nxt_ref, y_hbm, w_ref, x_ref, gate_ref, g_ref, b_ref, o_ref, ybuf_ref, sems,
                    *, tc, alpha):
    i = pl.program_id(0)
    nt = TOP_K * tc

    def fetch(idx_ref, slot):
        buf, sem = ybuf_ref.at[slot], sems.at[slot]
        _issue_tile_copies(nt, lambda g, u: (_tile_at(y_hbm, idx_ref[0, 0, g * _COPY_UNROLL + u]),
                                             _tile_at(buf, g * _COPY_UNROLL + u), sem))

    @pl.when(i == 0)
    def _():
        fetch(cur_ref, 0)

    @pl.when(i + 1 < pl.num_programs(0))
    def _():
        fetch(nxt_ref, (i + 1) % 2)

    slot = i % 2
    yb = ybuf_ref.at[slot]
    pltpu.make_async_copy(y_hbm.at[pl.ds(0, nt * SUBLANES)], yb, sems.at[slot]).wait()
    w = w_ref[...]
    y = (w[:, 0:1] * _from_token_tiles(yb, tc, 0, TOP_K)
         + w[:, 1:2] * _from_token_tiles(yb, tc, 1, TOP_K))
    r = alpha * x_ref[0] + (1.0 + gate_ref[0]) * y
    o_ref[0] = _layer_norm_rows(r, g_ref[...], b_ref[...])


def _combine(y_tiles, dest, top_w, x, gate2, ln_g, ln_b, alpha, tc):
    bsz, seq, d = x.shape
    nj = seq // tc
    steps = bsz * nj
    nt = TOP_K * tc
    dest_blocks = dest.astype(jnp.int32).reshape(steps, 1, nt)
    return pl.pallas_call(
        functools.partial(_combine_kernel, tc=tc, alpha=alpha),
        grid=(steps,),
        in_specs=[pl.BlockSpec((1, 1, nt), lambda i: (i, 0, 0), memory_space=pltpu.SMEM),
                  pl.BlockSpec((1, 1, nt), lambda i: (jnp.minimum(i + 1, steps - 1), 0, 0),
                               memory_space=pltpu.SMEM),
                  pl.BlockSpec(memory_space=pl.ANY),
                  pl.BlockSpec((tc, LANES), lambda i: (i, 0)),
                  pl.BlockSpec((1, tc, d), lambda i: (i // nj, i % nj, 0)),
                  pl.BlockSpec((1, 1, d), lambda i: (i // nj, 0, 0)), _const_spec((1, d)), _const_spec((1, d))],
        out_specs=pl.BlockSpec((1, tc, d), lambda i: (i // nj, i % nj, 0)),
        out_shape=jax.ShapeDtypeStruct((bsz, seq, d), F32),
        scratch_shapes=[pltpu.VMEM((2, nt * SUBLANES, LANES), F32), pltpu.SemaphoreType.DMA((2,))],
        compiler_params=pltpu.CompilerParams(dimension_semantics=("arbitrary",),
                                             vmem_limit_bytes=VMEM_LIMIT_BYTES),
        name="moe_combine",
    )(dest_blocks, dest_blocks, y_tiles, top_w, x, gate2, ln_g, ln_b)


def _moe(oa, oc, lw, x, gate1, scale2, shift2, gate2, wr2, stri, wg, wu, wd, ln_g, ln_b, alpha,
         tm_tok, tm, tf, tc):
    bsz, seq, d = x.shape
    n_tok = bsz * seq
    n_asg = n_tok * TOP_K
    x1 = _outproj(oa, oc, lw, x, gate1, alpha, tm_tok)
    h_tiles, idx128, w128, cnt = _router(x1, scale2, shift2, wr2, stri, tm_tok)
    counts = cnt[0, :N_EXPERTS].astype(jnp.int32)
    padded = (counts + tm - 1) // tm * tm
    pad_end = jnp.cumsum(padded)
    pad_start = pad_end - padded
    e_ids, ranks = idx128[:, 0:TOP_K], idx128[:, TOP_K:2 * TOP_K]
    start_of = jnp.sum(jnp.where(e_ids[:, :, None] == jnp.arange(N_EXPERTS, dtype=jnp.int32), pad_start, 0), axis=-1)
    dest = (start_of + ranks).reshape(n_asg)
    nb = n_asg // tm + N_EXPERTS
    n_rows = nb * tm
    n_fill = n_rows - n_asg
    fill_end = jnp.cumsum(padded - counts)
    slot = jnp.arange(n_fill, dtype=jnp.int32)
    slot_e = jnp.sum((slot[:, None] >= fill_end[None, :]).astype(jnp.int32), axis=1)
    in_group = slot_e < N_EXPERTS
    ge = jnp.minimum(slot_e, N_EXPERTS - 1)
    fill_dst = jnp.where(in_group, (pad_start + counts)[ge] + slot - (fill_end - (padded - counts))[ge],
                         pad_end[-1] + slot - fill_end[-1])
    x_tiles = _group_rows(h_tiles, dest, fill_dst, n_rows, _tile(n_tok, 1024))
    block_row = jnp.arange(nb, dtype=jnp.int32) * tm
    block_e = jnp.minimum(jnp.sum((block_row[:, None] >= pad_end[None, :]).astype(jnp.int32), axis=1),
                          N_EXPERTS - 1)
    n_used = (pad_end[-1:] // tm).astype(jnp.int32)
    rows_in_block = (pad_start + counts)[block_e] - block_row
    n_halves = jnp.where(rows_in_block > tm // 2, 2, 1).astype(jnp.int32)
    y_tiles = _experts(x_tiles, nb, block_e, n_used, n_halves, wg, wu, wd, tm, tf)
    return _combine(y_tiles, dest, w128, x1, gate2, ln_g, ln_b, alpha, tc)


def _layer_weights(layer, w_in, b_fgate, w_conv, q_norm_g, kv_norm_g, w_uq, w_ukv, head_norm_g, w_o,
                   ln1_g, ln1_b):
    d = w_in.shape[1]
    sizes = [FOX_W, FOX_W, FOX_W, H_FOX, CONV_W, CONV_W, CONV_W, Q_LORA, KV_LORA, QK_ROPE]
    offs = np.concatenate([[0], np.cumsum(sizes)])
    wi = w_in[layer]
    fq, fk, fv, fl, bg, cg, hin, cq, ckv, kr = [wi[:, offs[i]:offs[i + 1]] for i in range(len(sizes))]
    half = QK_ROPE // 2
    z64 = jnp.zeros((d, QK_NOPE), F32)
    z32 = jnp.zeros((d, LANES - QK_NOPE - QK_ROPE), F32)
    wa = jnp.concatenate([
        fq, fk, fv,
        jnp.pad(jnp.repeat(fl, _N_SPLIT, axis=1), ((0, 0), (0, LANES - _N_SPLIT * H_FOX))), bg, cg, hin, cq, ckv,
        jnp.concatenate([z64, kr, z32], axis=1),
        jnp.concatenate([z64, kr[:, half:], kr[:, :half], z32], axis=1)], axis=1)
    assert wa.shape[1] == _NA
    q3 = w_uq[layer].reshape(Q_LORA, H_MLA, QK_NOPE + QK_ROPE)
    q_main = jnp.pad(q3, ((0, 0), (0, 0), (0, LANES - QK_NOPE - QK_ROPE)))
    q_swap = jnp.concatenate([jnp.zeros((Q_LORA, H_MLA, QK_NOPE), F32), q3[:, :, QK_NOPE + half:],
                              q3[:, :, QK_NOPE:QK_NOPE + half],
                              jnp.zeros((Q_LORA, H_MLA, LANES - QK_NOPE - QK_ROPE), F32)], axis=2)
    wuq = jnp.concatenate([q_main.reshape(Q_LORA, H_MLA * LANES), q_swap.reshape(Q_LORA, H_MLA * LANES)],
                          axis=1).astype(BF16)
    kv3 = w_ukv[layer].reshape(KV_LORA, H_MLA, QK_NOPE + V_DIM)
    k_nope = jnp.pad(kv3[:, :, :QK_NOPE], ((0, 0), (0, 0), (0, LANES - QK_NOPE))).reshape(KV_LORA, H_MLA * LANES)
    wukv = jnp.concatenate([k_nope, kv3[:, :, QK_NOPE:].reshape(KV_LORA, MLA_W)], axis=1).astype(BF16)
    hg = head_norm_g[layer]
    wo = w_o[layer]
    c0, c1 = FOX_W, FOX_W + CONV_W
    return {
        "wa": wa, "wuq": wuq, "wukv": wukv,
        "qg": q_norm_g[layer].reshape(1, Q_LORA), "kvg": kv_norm_g[layer].reshape(1, KV_LORA),
        "bf": jnp.pad(jnp.repeat(b_fgate[layer], _N_SPLIT), (0, LANES - _N_SPLIT * H_FOX)).reshape(1, LANES),
        "wconv": jnp.pad(w_conv[layer], ((0, SUBLANES - CONV_K), (0, 0))),
        "hg_conv": hg[c0:c1].reshape(1, CONV_W),
        "hg_attn": jnp.concatenate([hg[:c0], hg[c1:]]).reshape(1, FOX_W + MLA_W),
        "wo_a": jnp.concatenate([wo[:c0], wo[c1:]], axis=0).astype(BF16),
        "wo_c": wo[c0:c1].astype(BF16),
        "ln1_g": ln1_g[layer].reshape(1, d), "ln1_b": ln1_b[layer].reshape(1, d),
    }


def _constants(seq, ts):
    pos = jnp.arange(seq, dtype=F32)
    inv_freq = ROPE_THETA ** (-jnp.arange(0, QK_ROPE, 2, dtype=F32) / QK_ROPE)
    ang = pos[:, None] * inv_freq[None, :]
    cos, sin = jnp.cos(ang), jnp.sin(ang)
    pad_r = jnp.zeros((seq, LANES - QK_NOPE - QK_ROPE), F32)
    ctab = jnp.concatenate([jnp.ones((seq, QK_NOPE), F32), cos, cos, pad_r], axis=1)
    stab = jnp.concatenate([jnp.zeros((seq, QK_NOPE), F32), -sin, sin, pad_r], axis=1)
    q_scale = (QK_NOPE + QK_ROPE) ** -0.5 * LOG2E
    lanes = np.arange(LANES)
    lm = np.zeros((SUBLANES, LANES), np.float32)
    for part in range(_N_SPLIT):
        lm[part] = (lanes % _N_SPLIT == part) & (lanes < _N_SPLIT * H_FOX)
    for row, base in ((3, HEAD_DIM), (5, 0)):
        lm[row] = (lanes >= base) & (lanes < base + _N_SPLIT)
        lm[row + 1] = (lanes >= base + _N_SPLIT) & (lanes < base + 2 * _N_SPLIT)

    def group_mean(n):
        gidx = np.arange(n) // HEAD_DIM
        return (gidx[:, None] == gidx[None, :]).astype(np.float32) / HEAD_DIM

    return {
        "cq": ctab * q_scale, "sq": stab * q_scale, "ck": ctab, "sk": stab,
        "tri": jnp.asarray(np.tril(np.ones((ts, ts), np.float32)), BF16),
        "stri": jnp.asarray(np.tril(np.ones((ts, ts), np.float32), k=-1), BF16),
        "lane_masks": jnp.asarray(lm),
        "gm256": jnp.asarray(group_mean(CONV_W), BF16), "gm128": jnp.asarray(group_mean(LANES), BF16),
    }


def _tile(n, pref):
    t = min(n, pref)
    assert n % t == 0, (n, pref)
    return t


def kernel(x, c, w_mod, b_mod, w_in, b_fgate, w_conv, q_norm_g, kv_norm_g, w_uq, w_ukv, head_norm_g, w_o, ln1_g, ln1_b, ln2_g, ln2_b, ffn_w_gate, ffn_w_up, ffn_w_down, router_w, exp_w_gate, exp_w_up, exp_w_down):
    bsz, seq, d = x.shape
    depth = w_mod.shape[0]
    assert d == D_MODEL and bsz <= SUBLANES
    alpha = (2 * depth) ** 0.25
    ts = _tile(seq, 512)
    tq = _tile(seq, 512)
    tm_e = 1024
    tf_e = 512
    tc = _tile(seq, 256)
    assert (bsz * seq * TOP_K) % tm_e == 0 and exp_w_gate.shape[-1] % tf_e == 0

    consts = _constants(seq, ts)
    c_pad = jnp.pad(c, ((0, SUBLANES - bsz), (0, 0)))
    mod = _modulation(c_pad, w_mod, b_mod)[:, :bsz, :]

    for layer in range(depth):
        m6 = mod[layer].reshape(bsz, 6, 1, d)
        shift1, scale1, gate1, shift2, scale2, gate2 = [m6[:, i] for i in range(6)]
        lw = _layer_weights(layer, w_in, b_fgate, w_conv, q_norm_g, kv_norm_g, w_uq, w_ukv, head_norm_g,
                            w_o, ln1_g, ln1_b)
        qp, kp, vt, oc = _inproj(x, scale1, shift1, lw, consts, ts)
        oa = _attention(qp, kp, vt, lw["hg_attn"], consts["gm128"], tq, 12)
        j = layer // 2
        g2, b2 = ln2_g[layer].reshape(1, d), ln2_b[layer].reshape(1, d)
        if layer % 2 == 0:
            x = _ffn_dense(oa, oc, lw, x, gate1, scale2, shift2, gate2, ffn_w_gate[j].astype(BF16),
                           ffn_w_up[j].astype(BF16), ffn_w_down[j].astype(BF16), g2, b2, alpha, ts)
        else:
            wr = jnp.pad(router_w[j], ((0, 0), (0, LANES - N_EXPERTS)))
            wr_hi = wr.astype(BF16)
            wr_lo = (wr - wr_hi.astype(F32)).astype(BF16)
            x = _moe(oa, oc, lw, x, gate1, scale2, shift2, gate2, jnp.stack([wr_hi, wr_lo]), consts["stri"],
                     exp_w_gate[j].astype(BF16), exp_w_up[j].astype(BF16), exp_w_down[j].astype(BF16),
                     g2, b2, alpha, ts, tm_e, tf_e, tc)
    return x
```

```python
import functools

import numpy as np
import jax
import jax.numpy as jnp
from jax import lax
from jax.experimental import pallas as pl
from jax.experimental.pallas import tpu as pltpu

F32 = jnp.float32
BF16 = jnp.bfloat16

D_MODEL = 1024
HEAD_DIM = 64
H_FOX = 6
FOX_W = H_FOX * HEAD_DIM
CONV_W = 256
CONV_K = 3
H_MLA = 6
Q_LORA = 256
KV_LORA = 256
QK_NOPE = 64
QK_ROPE = 32
V_DIM = 64
MLA_W = H_MLA * V_DIM
N_HEADS = H_FOX + H_MLA
ROPE_THETA = 10000.0
N_EXPERTS = 8
TOP_K = 2
LN_EPS = 1e-5
RMS_EPS = 1e-6

LANES = 128
SUBLANES = 8
VMEM_LIMIT_BYTES = 56 * 1024 * 1024

_HW = 6 * LANES
_QF0, _KF0, _VF0, _FL0 = 0, FOX_W, 2 * FOX_W, 3 * FOX_W
_BG0 = _FL0 + LANES
_CG0, _HI0, _CQ0, _CKV0 = _BG0 + CONV_W, _BG0 + 2 * CONV_W, _BG0 + 3 * CONV_W, _BG0 + 3 * CONV_W + Q_LORA
_KR0 = _CKV0 + KV_LORA
_KRS0 = _KR0 + LANES
_NA = _KRS0 + LANES
LOG2E = 1.4426950408889634
VT_ROWS = LANES
_N_SPLIT = 3


def _const_spec(shape):
    zeros = (0,) * len(shape)
    return pl.BlockSpec(shape, lambda *_: zeros, pipeline_mode=pl.Buffered(1))


def _silu(v):
    return v * (1.0 / (1.0 + jnp.exp(-v)))


def _split3(v):
    hi = v.astype(BF16)
    r1 = v - hi.astype(F32)
    mid = r1.astype(BF16)
    lo = (r1 - mid.astype(F32)).astype(BF16)
    return hi, mid, lo


def _group_mean_sq(v, gmat_ref):
    sq = v * v
    hi = sq.astype(BF16)
    lo = (sq - hi.astype(F32)).astype(BF16)
    g = gmat_ref[...]
    return (jnp.dot(hi, g, preferred_element_type=F32) + jnp.dot(lo, g, preferred_element_type=F32))


def _layer_norm_rows(r, g, b):
    mu = jnp.mean(r, axis=-1, keepdims=True)
    rc = r - mu
    var = jnp.mean(rc * rc, axis=-1, keepdims=True)
    return rc * lax.rsqrt(var + LN_EPS) * g + b


def _mod_kernel(c_ref, w_ref, b_ref, o_ref):
    act = _silu(c_ref[...]).astype(BF16)
    o_ref[0] = jnp.dot(act, w_ref[0].astype(BF16), preferred_element_type=F32) + b_ref[0]


def _modulation(c_pad, w_mod, b_mod):
    depth, d, n = w_mod.shape
    tn = 1024
    return pl.pallas_call(
        _mod_kernel,
        grid=(depth, n // tn),
        in_specs=[pl.BlockSpec((SUBLANES, d), lambda l, j: (0, 0)),
                  pl.BlockSpec((1, d, tn), lambda l, j: (l, 0, j)),
                  pl.BlockSpec((1, 1, tn), lambda l, j: (l, 0, j))],
        out_specs=pl.BlockSpec((1, SUBLANES, tn), lambda l, j: (l, 0, j)),
        out_shape=jax.ShapeDtypeStruct((depth, SUBLANES, n), F32),
        compiler_params=pltpu.CompilerParams(dimension_semantics=("arbitrary", "arbitrary")),
        name="modulation",
    )(c_pad, w_mod, b_mod.reshape(depth, 1, n))


def _inproj_kernel(x_ref, sc_ref, sh_ref, wa_ref, cq_ref, sq_ref, ck_ref, sk_ref, wuq_ref, wukv_ref,
                   qg_ref, kvg_ref, bf_ref, wconv_ref, hgc_ref, tri_ref, lm_ref, gm_ref,
                   qp_ref, kp_ref, vt_ref, oc_ref, fcarry_ref, ubuf_ref, wab_ref, *, ts):
    @pl.when(jnp.logical_and(pl.program_id(0) == 0, pl.program_id(1) == 0))
    def _():
        for c in range(_NA // LANES):
            sl = slice(c * LANES, (c + 1) * LANES)
            wab_ref[:, sl] = wa_ref[:, sl].astype(BF16)

    @pl.when(pl.program_id(1) == 0)
    def _():
        fcarry_ref[...] = jnp.zeros_like(fcarry_ref)
        ubuf_ref[pl.ds(0, SUBLANES), :] = jnp.zeros((SUBLANES, CONV_W), F32)

    hb = (x_ref[0] * (1.0 + sc_ref[0]) + sh_ref[0]).astype(BF16)

    def proj(lo, hi):
        return jnp.dot(hb, wab_ref[:, lo:hi], preferred_element_type=F32)

    def put_values_t(head, v_half_t):
        vt_ref[0, head, 0:V_DIM, :] = v_half_t.astype(BF16)
        vt_ref[0, head, V_DIM:VT_ROWS, :] = jnp.ones((VT_ROWS - V_DIM, ts), BF16)

    z_fl = proj(_FL0, _FL0 + LANES)
    z_cq = proj(_CQ0, _CQ0 + Q_LORA)
    z_ckv = proj(_CKV0, _CKV0 + KV_LORA)
    z_cg = proj(_CG0, _CG0 + CONV_W)
    z_hi = proj(_HI0, _HI0 + CONV_W)
    z_bg = proj(_BG0, _BG0 + CONV_W)
    zq = proj(_QF0, _QF0 + FOX_W) * (HEAD_DIM ** -0.5 * LOG2E)
    zk = proj(_KF0, _KF0 + FOX_W)
    zv = proj(_VF0, _VF0 + FOX_W)
    z_kr = proj(_KR0, _KR0 + LANES)
    z_krs = proj(_KRS0, _KRS0 + LANES)

    a = z_fl + bf_ref[...]
    logf = jnp.minimum(a, 0.0) - jnp.log1p(jnp.exp(-jnp.abs(a)))
    tri = tri_ref[...]
    csum = sum(jnp.dot(tri, part, preferred_element_type=F32) for part in _split3(logf))
    fcum = fcarry_ref[...] + csum
    fcarry_ref[...] = fcum[ts - 1:ts, :]
    f_hi, f_mid, f_lo = _split3(fcum * LOG2E)
    lm = lm_ref[...]
    fparts = (f_hi.astype(F32) * lm[0:1, :] + f_mid.astype(F32) * lm[1:2, :]
              + f_lo.astype(F32) * lm[2:3, :])
    lane = lax.broadcasted_iota(jnp.int32, (ts, LANES), 1)
    low_half = lane < HEAD_DIM
    for h in range(H_FOX):
        blk = slice((h // 2) * LANES, (h // 2 + 1) * LANES)
        if h % 2 == 0:
            base, f_mask, one_mask, keep = HEAD_DIM, lm[3:4, :], lm[4:5, :], low_half
        else:
            base, f_mask, one_mask, keep = 0, lm[5:6, :], lm[6:7, :], jnp.logical_not(low_half)
        f_at_q = pltpu.roll(fparts, (base - _N_SPLIT * h) % LANES, axis=1)
        f_at_k = pltpu.roll(fparts, (base + _N_SPLIT - _N_SPLIT * h) % LANES, axis=1)
        qp_ref[0, h] = jnp.where(keep, zq[:, blk], f_at_q * f_mask + one_mask).astype(BF16)
        kp_ref[0, h] = jnp.where(keep, zk[:, blk], f_mask - f_at_k * one_mask).astype(BF16)
    for j in range(H_FOX // 2):
        vt = zv[:, j * LANES:(j + 1) * LANES].T
        put_values_t(2 * j, vt[0:V_DIM])
        put_values_t(2 * j + 1, vt[V_DIM:LANES])

    u = z_cg * z_hi
    ubuf_ref[pl.ds(SUBLANES, ts), :] = u
    u1 = ubuf_ref[pl.ds(SUBLANES - 1, ts), :]
    u2 = ubuf_ref[pl.ds(SUBLANES - 2, ts), :]
    ubuf_ref[pl.ds(0, SUBLANES), :] = u[ts - SUBLANES:ts, :]
    wc = wconv_ref[...]
    oc = z_bg * (wc[0:1, :] * u2 + wc[1:2, :] * u1 + wc[2:3, :] * u)
    ocn = oc * lax.rsqrt(_group_mean_sq(oc, gm_ref) + RMS_EPS) * hgc_ref[...]
    oc_ref[0] = ocn.astype(BF16)

    def rms(v, g):
        return (v * lax.rsqrt(jnp.mean(v * v, axis=-1, keepdims=True) + RMS_EPS) * g).astype(BF16)

    cqn = rms(z_cq, qg_ref[...])
    qm = jnp.dot(cqn, wuq_ref[:, 0:_HW], preferred_element_type=F32)
    qs = jnp.dot(cqn, wuq_ref[:, _HW:2 * _HW], preferred_element_type=F32)
    kvn = rms(z_ckv, kvg_ref[...])
    kn = jnp.dot(kvn, wukv_ref[:, 0:_HW], preferred_element_type=F32)
    vm = jnp.dot(kvn, wukv_ref[:, _HW:_HW + MLA_W], preferred_element_type=F32)
    krr = z_kr * ck_ref[...] + z_krs * sk_ref[...]
    cq, sq = cq_ref[...], sq_ref[...]
    for h in range(H_MLA):
        sl = slice(h * LANES, (h + 1) * LANES)
        qp_ref[0, H_FOX + h] = (qm[:, sl] * cq + qs[:, sl] * sq).astype(BF16)
        kp_ref[0, H_FOX + h] = (kn[:, sl] + krr).astype(BF16)
    for j in range(H_MLA // 2):
        vt = vm[:, j * LANES:(j + 1) * LANES].T
        put_values_t(H_FOX + 2 * j, vt[0:V_DIM])
        put_values_t(H_FOX + 2 * j + 1, vt[V_DIM:LANES])


def _inproj(x, scale1, shift1, lw, consts, ts):
    bsz, seq, d = x.shape
    kern = functools.partial(_inproj_kernel, ts=ts)
    row = lambda b, j: (b, 0, 0)
    tab = pl.BlockSpec((ts, LANES), lambda b, j: (j, 0))
    return pl.pallas_call(
        kern,
        grid=(bsz, seq // ts),
        in_specs=[pl.BlockSpec((1, ts, d), lambda b, j: (b, j, 0)),
                  pl.BlockSpec((1, 1, d), row), pl.BlockSpec((1, 1, d), row),
                  _const_spec((d, _NA)), tab, tab, tab, tab,
                  _const_spec((Q_LORA, 2 * _HW)), _const_spec((KV_LORA, _HW + MLA_W)),
                  _const_spec((1, Q_LORA)), _const_spec((1, KV_LORA)), _const_spec((1, LANES)),
                  _const_spec((SUBLANES, CONV_W)), _const_spec((1, CONV_W)),
                  _const_spec((ts, ts)), _const_spec((SUBLANES, LANES)), _const_spec((CONV_W, CONV_W))],
        out_specs=[pl.BlockSpec((1, N_HEADS, ts, LANES), lambda b, j: (b, 0, j, 0)),
                   pl.BlockSpec((1, N_HEADS, ts, LANES), lambda b, j: (b, 0, j, 0)),
                   pl.BlockSpec((1, N_HEADS, VT_ROWS, ts), lambda b, j: (b, 0, 0, j)),
                   pl.BlockSpec((1, ts, CONV_W), lambda b, j: (b, j, 0))],
        out_shape=[jax.ShapeDtypeStruct((bsz, N_HEADS, seq, LANES), BF16),
                   jax.ShapeDtypeStruct((bsz, N_HEADS, seq, LANES), BF16),
                   jax.ShapeDtypeStruct((bsz, N_HEADS, VT_ROWS, seq), BF16),
                   jax.ShapeDtypeStruct((bsz, seq, CONV_W), BF16)],
        scratch_shapes=[pltpu.VMEM((1, LANES), F32), pltpu.VMEM((ts + SUBLANES, CONV_W), F32),
                        pltpu.VMEM((d, _NA), BF16)],
        compiler_params=pltpu.CompilerParams(dimension_semantics=("arbitrary", "arbitrary"),
                                             vmem_limit_bytes=VMEM_LIMIT_BYTES),
        name="inproj",
    )(x, scale1, shift1, lw["wa"], consts["cq"], consts["sq"], consts["ck"], consts["sk"],
      lw["wuq"], lw["wukv"], lw["qg"], lw["kvg"], lw["bf"], lw["wconv"], lw["hg_conv"],
      consts["tri"], consts["lane_masks"], consts["gm256"])


def _attn_kernel(qi_ref, kj_ref, q_ref, k_ref, vt_ref, g_ref, gm_ref, o_ref, m_ref, acc_ref, s_ref, *, nh):
    step = pl.program_id(2)
    qi = qi_ref[step]
    kj = kj_ref[step]

    @pl.when(kj == 0)
    def _():
        m_ref[...] = jnp.full(m_ref.shape, -jnp.inf, F32)
        acc_ref[...] = jnp.zeros_like(acc_ref)

    def update(masked):
        if masked:
            tile = (q_ref.shape[2], q_ref.shape[2])
            causal = lax.broadcasted_iota(jnp.int32, tile, 0) <= lax.broadcasted_iota(jnp.int32, tile, 1)

        def scores(h):
            st = lax.dot_general(k_ref[0, h], q_ref[0, h], (((1,), (1,)), ((), ())),
                                 preferred_element_type=F32)
            if masked:
                st = jnp.where(causal, st, -jnp.inf)
            s_ref[h] = st

        for h in range(nh):
            scores(h)
        pts, alphas = [], []
        for h in range(nh):
            m_prev = m_ref[h]
            m_new = jnp.maximum(m_prev, jnp.max(s_ref[h], axis=0, keepdims=True))
            pts.append(jnp.exp2(s_ref[h] - m_new).astype(BF16))
            alphas.append(jnp.exp2(m_prev - m_new))
            m_ref[h] = m_new
        for h in range(nh):
            acc_ref[h] = alphas[h] * acc_ref[h] + jnp.dot(vt_ref[0, h], pts[h], preferred_element_type=F32)

    @pl.when(kj < qi)
    def _():
        update(False)

    @pl.when(kj == qi)
    def _():
        update(True)
        for pair in range(nh // 2):
            ot = jnp.concatenate([acc_ref[h, 0:V_DIM, :] / acc_ref[h, V_DIM:2 * V_DIM, :]
                                  for h in (2 * pair, 2 * pair + 1)], axis=0)
            o = ot.T
            sl = slice(pair * LANES, (pair + 1) * LANES)
            on = o * lax.rsqrt(_group_mean_sq(o, gm_ref) + RMS_EPS) * g_ref[:, sl]
            o_ref[0, :, sl] = on.astype(BF16)


def _attention(qp, kp, vt, g_attn, gm128, tq, nh):
    bsz, n_heads, seq, _ = qp.shape
    nq = seq // tq
    ow = (nh // 2) * LANES
    qi_tab = np.concatenate([np.full(i + 1, i) for i in range(nq)]).astype(np.int32)
    kj_tab = np.concatenate([np.arange(i + 1) for i in range(nq)]).astype(np.int32)
    grid_spec = pltpu.PrefetchScalarGridSpec(
        num_scalar_prefetch=2,
        grid=(bsz, n_heads // nh, len(qi_tab)),
        in_specs=[pl.BlockSpec((1, nh, tq, LANES), lambda b, p, s, qi, kj: (b, p, qi[s], 0)),
                  pl.BlockSpec((1, nh, tq, LANES), lambda b, p, s, qi, kj: (b, p, kj[s], 0)),
                  pl.BlockSpec((1, nh, VT_ROWS, tq), lambda b, p, s, qi, kj: (b, p, 0, kj[s])),
                  pl.BlockSpec((1, ow), lambda b, p, s, qi, kj: (0, p)),
                  _const_spec((LANES, LANES))],
        out_specs=pl.BlockSpec((1, tq, ow), lambda b, p, s, qi, kj: (b, qi[s], p)),
        scratch_shapes=[pltpu.VMEM((nh, 1, tq), F32), pltpu.VMEM((nh, VT_ROWS, tq), F32),
                        pltpu.VMEM((nh, tq, tq), F32)])
    return pl.pallas_call(
        functools.partial(_attn_kernel, nh=nh),
        grid_spec=grid_spec,
        out_shape=jax.ShapeDtypeStruct((bsz, seq, (n_heads // 2) * LANES), BF16),
        compiler_params=pltpu.CompilerParams(
            dimension_semantics=("arbitrary", "arbitrary", "arbitrary"),
            vmem_limit_bytes=VMEM_LIMIT_BYTES),
        name="attention",
    )(jnp.asarray(qi_tab), jnp.asarray(kj_tab), qp, kp, vt, g_attn, gm128)


def _mixer_out_rows(oa_ref, oc_ref, woa_ref, woc_ref, x_ref, gate_ref, g_ref, b_ref, alpha):
    y = (jnp.dot(oa_ref[0], woa_ref[...], preferred_element_type=F32)
         + jnp.dot(oc_ref[0], woc_ref[...], preferred_element_type=F32))
    r = alpha * x_ref[0] + (1.0 + gate_ref[0]) * y
    return _layer_norm_rows(r, g_ref[...], b_ref[...])


def _mixer_out_specs(oa, d, tm, act, row):
    wa_rows = oa.shape[-1]
    return [pl.BlockSpec((1, tm, wa_rows), act), pl.BlockSpec((1, tm, CONV_W), act),
            _const_spec((wa_rows, d)), _const_spec((CONV_W, d)),
            pl.BlockSpec((1, tm, d), act), pl.BlockSpec((1, 1, d), row), _const_spec((1, d)), _const_spec((1, d))]


def _outproj_kernel(oa_ref, oc_ref, woa_ref, woc_ref, x_ref, gate_ref, g_ref, b_ref, o_ref, *, alpha):
    o_ref[0] = _mixer_out_rows(oa_ref, oc_ref, woa_ref, woc_ref, x_ref, gate_ref, g_ref, b_ref, alpha)


def _outproj(oa, oc, lw, x, gate1, alpha, tm):
    bsz, seq, d = x.shape
    act = lambda b, j: (b, j, 0)
    return pl.pallas_call(
        functools.partial(_outproj_kernel, alpha=alpha),
        grid=(bsz, seq // tm),
        in_specs=_mixer_out_specs(oa, d, tm, act, lambda b, j: (b, 0, 0)),
        out_specs=pl.BlockSpec((1, tm, d), act),
        out_shape=jax.ShapeDtypeStruct((bsz, seq, d), F32),
        compiler_params=pltpu.CompilerParams(dimension_semantics=("arbitrary", "arbitrary"),
                                             vmem_limit_bytes=VMEM_LIMIT_BYTES),
        name="outproj",
    )(oa, oc, lw["wo_a"], lw["wo_c"], x, gate1, lw["ln1_g"], lw["ln1_b"])


def _ffn_kernel(oa_ref, oc_ref, woa_ref, woc_ref, xin_ref, gate1_ref, g1_ref, b1_ref,
                sc_ref, sh_ref, gate_ref, wg_ref, wu_ref, wd_ref, g_ref, b_ref, o_ref, *, alpha, tf):
    x = _mixer_out_rows(oa_ref, oc_ref, woa_ref, woc_ref, xin_ref, gate1_ref, g1_ref, b1_ref, alpha)
    hb = (x * (1.0 + sc_ref[0]) + sh_ref[0]).astype(BF16)
    chunks = [slice(c * tf, (c + 1) * tf) for c in range(wg_ref.shape[1] // tf)]
    acts = []
    for sl in chunks:
        gt = jnp.dot(hb, wg_ref[:, sl], preferred_element_type=F32)
        up = jnp.dot(hb, wu_ref[:, sl], preferred_element_type=F32)
        acts.append((_silu(gt) * up).astype(BF16))
    acc = sum(jnp.dot(act, wd_ref[sl, :], preferred_element_type=F32) for act, sl in zip(acts, chunks))
    r = alpha * x + (1.0 + gate_ref[0]) * acc
    o_ref[0] = _layer_norm_rows(r, g_ref[...], b_ref[...])


def _ffn_dense(oa, oc, lw, x, gate1, scale2, shift2, gate2, wg, wu, wd, ln_g, ln_b, alpha, tm):
    bsz, seq, d = x.shape
    dff = wg.shape[1]
    tf = next(t for t in (704, 512, 256, 128, dff) if dff % t == 0 and t % LANES == 0)
    nj = seq // tm
    act = lambda b, j: (b, j, 0)
    row = lambda b, j: (b, 0, 0)
    return pl.pallas_call(
        functools.partial(_ffn_kernel, alpha=alpha, tf=tf),
        grid=(bsz, nj),
        in_specs=_mixer_out_specs(oa, d, tm, act, row) + [
            pl.BlockSpec((1, 1, d), row), pl.BlockSpec((1, 1, d), row), pl.BlockSpec((1, 1, d), row),
            _const_spec((d, dff)), _const_spec((d, dff)), _const_spec((dff, d)),
            _const_spec((1, d)), _const_spec((1, d))],
        out_specs=pl.BlockSpec((1, tm, d), act),
        out_shape=jax.ShapeDtypeStruct((bsz, seq, d), F32),
        compiler_params=pltpu.CompilerParams(dimension_semantics=("arbitrary", "arbitrary"),
                                             vmem_limit_bytes=VMEM_LIMIT_BYTES),
        name="ffn_dense",
    )(oa, oc, lw["wo_a"], lw["wo_c"], x, gate1, lw["ln1_g"], lw["ln1_b"],
      scale2, shift2, gate2, wg, wu, wd, ln_g, ln_b)


def _to_token_tiles(ref, v, n):
    for c in range(v.shape[1] // LANES):
        ref[pl.ds(c, n, stride=SUBLANES), :] = v[:, c * LANES:(c + 1) * LANES]


def _from_token_tiles(ref, n, first=0, tiles_per_row=1):
    stride = tiles_per_row * SUBLANES
    return jnp.concatenate([ref[pl.ds(first * SUBLANES + c, n, stride=stride), :] for c in range(SUBLANES)],
                           axis=1)


def _router_kernel(x_ref, sc_ref, sh_ref, wr_ref, stri_ref, h_ref, idx_ref, w_ref, cnt_ref, carry_ref, *, tm):
    @pl.when(pl.program_id(0) == 0)
    def _():
        carry_ref[...] = jnp.zeros_like(carry_ref)

    h = x_ref[0] * (1.0 + sc_ref[0]) + sh_ref[0]
    _to_token_tiles(h_ref, h, tm)
    h_hi = h.astype(BF16)
    h_lo = (h - h_hi.astype(F32)).astype(BF16)
    w_hi, w_lo = wr_ref[0], wr_ref[1]
    logits = (jnp.dot(h_hi, w_hi, preferred_element_type=F32)
              + jnp.dot(h_hi, w_lo, preferred_element_type=F32)
              + jnp.dot(h_lo, w_hi, preferred_element_type=F32))
    lane = lax.broadcasted_iota(jnp.int32, logits.shape, 1).astype(F32)
    neg = -jnp.inf
    lg = jnp.where(lane < N_EXPERTS, logits, neg)
    m1 = jnp.max(lg, axis=-1, keepdims=True)
    i1 = jnp.min(jnp.where(lg == m1, lane, float(LANES)), axis=-1, keepdims=True)
    lg2 = jnp.where(lane == i1, neg, lg)
    m2 = jnp.max(lg2, axis=-1, keepdims=True)
    i2 = jnp.min(jnp.where(lg2 == m2, lane, float(LANES)), axis=-1, keepdims=True)
    e2 = jnp.exp(m2 - m1)
    denom = 1.0 + e2
    w_ref[...] = jnp.where(lane == 0.0, 1.0 / denom, e2 / denom)
    first, second = lane == i1, lane == i2
    chosen = jnp.logical_or(first, second).astype(F32)
    before = carry_ref[...] + jnp.dot(stri_ref[...], chosen.astype(BF16), preferred_element_type=F32)
    rank1 = jnp.sum(jnp.where(first, before, 0.0), axis=-1, keepdims=True)
    rank2 = jnp.sum(jnp.where(second, before, 0.0), axis=-1, keepdims=True)
    idx_ref[...] = jnp.where(lane == 0.0, i1, jnp.where(lane == 1.0, i2, jnp.where(lane == 2.0, rank1, rank2))
                             ).astype(jnp.int32)
    total = carry_ref[...] + jnp.sum(chosen, axis=0, keepdims=True)
    carry_ref[...] = total
    cnt_ref[...] = jnp.broadcast_to(total, cnt_ref.shape)


def _router(x, scale2, shift2, wr2, stri, tm):
    bsz, seq, d = x.shape
    assert d == SUBLANES * LANES and bsz * seq * TOP_K < 2 ** 24
    nj = seq // tm
    n_tiles = bsz * nj
    row = lambda i: (i // nj, 0, 0)
    tok = lambda i: (i, 0)
    return pl.pallas_call(
        functools.partial(_router_kernel, tm=tm),
        grid=(n_tiles,),
        in_specs=[pl.BlockSpec((1, tm, d), lambda i: (i // nj, i % nj, 0)),
                  pl.BlockSpec((1, 1, d), row), pl.BlockSpec((1, 1, d), row),
                  _const_spec((2, d, LANES)), _const_spec((tm, tm))],
        out_specs=[pl.BlockSpec((tm * SUBLANES, LANES), tok),
                   pl.BlockSpec((tm, LANES), tok), pl.BlockSpec((tm, LANES), tok),
                   pl.BlockSpec((SUBLANES, LANES), lambda i: (0, 0))],
        out_shape=[jax.ShapeDtypeStruct((n_tiles * tm * SUBLANES, LANES), F32),
                   jax.ShapeDtypeStruct((bsz * seq, LANES), jnp.int32),
                   jax.ShapeDtypeStruct((bsz * seq, LANES), F32),
                   jax.ShapeDtypeStruct((SUBLANES, LANES), F32)],
        scratch_shapes=[pltpu.VMEM((1, LANES), F32)],
        compiler_params=pltpu.CompilerParams(dimension_semantics=("arbitrary",),
                                             vmem_limit_bytes=VMEM_LIMIT_BYTES),
        name="router",
    )(x, scale2, shift2, wr2, stri)


_COPY_UNROLL = 8


def _tile_at(ref, t):
    return ref.at[pl.ds(pl.multiple_of(t * SUBLANES, SUBLANES), SUBLANES)]


def _issue_tile_copies(n, copy_of):
    def group(g, carry):
        copies = [copy_of(g, u) for u in range(_COPY_UNROLL)]
        for u, (src, dst, sem) in enumerate(copies):
            pltpu.make_async_copy(src, dst, sem).start(priority=u % 2)
        return carry

    assert n % _COPY_UNROLL == 0
    lax.fori_loop(0, n // _COPY_UNROLL, group, 0)


def _group_kernel(di_ref, fi_ref, src_ref, dst_hbm, zero_ref, sem, *, tb, nf):
    zero_ref[...] = jnp.zeros_like(zero_ref)
    per_group = _COPY_UNROLL // TOP_K
    _issue_tile_copies(TOP_K * tb, lambda g, u: (_tile_at(src_ref, g * per_group + u // TOP_K),
                                                 _tile_at(dst_hbm, di_ref[0, 0, g * _COPY_UNROLL + u]), sem))
    _issue_tile_copies(nf, lambda g, u: (zero_ref, _tile_at(dst_hbm, fi_ref[0, 0, g * _COPY_UNROLL + u]), sem))
    block = pl.ds(0, tb * SUBLANES)
    for _ in range(TOP_K):
        pltpu.make_async_copy(src_ref, dst_hbm.at[block], sem).wait()

    def wait_fill(r, carry):
        pltpu.make_async_copy(zero_ref, _tile_at(dst_hbm, 0), sem).wait()
        return carry

    lax.fori_loop(0, nf, wait_fill, 0)


def _group_rows(h_tiles, dest, fill_dst, n_rows, tb):
    n_asg = dest.shape[0]
    steps = n_asg // (TOP_K * tb)
    nf = fill_dst.shape[0] // steps
    assert steps * TOP_K * tb == n_asg and steps * nf == fill_dst.shape[0]
    return pl.pallas_call(
        functools.partial(_group_kernel, tb=tb, nf=nf),
        grid=(steps,),
        in_specs=[pl.BlockSpec((1, 1, TOP_K * tb), lambda i: (i, 0, 0), memory_space=pltpu.SMEM),
                  pl.BlockSpec((1, 1, nf), lambda i: (i, 0, 0), memory_space=pltpu.SMEM),
                  pl.BlockSpec((tb * SUBLANES, LANES), lambda i: (i, 0))],
        out_specs=pl.BlockSpec(memory_space=pl.ANY),
        out_shape=jax.ShapeDtypeStruct((n_rows * SUBLANES, LANES), h_tiles.dtype),
        scratch_shapes=[pltpu.VMEM((SUBLANES, LANES), h_tiles.dtype), pltpu.SemaphoreType.DMA(())],
        compiler_params=pltpu.CompilerParams(dimension_semantics=("arbitrary",), has_side_effects=True),
        name="moe_group",
    )(dest.astype(jnp.int32).reshape(steps, 1, TOP_K * tb), fill_dst.astype(jnp.int32).reshape(steps, 1, nf),
      h_tiles)


def _expert_kernel(be_ref, nu_ref, nh_ref, x_ref, wg_ref, wu_ref, wd_ref, o_ref, acc_ref, xb_ref, *, tm):
    i = pl.program_id(0)
    f = pl.program_id(1)
    used = i < nu_ref[0]
    whole = jnp.logical_and(used, nh_ref[i] == 2)
    first_half_only = jnp.logical_and(used, nh_ref[i] == 1)

    @pl.when(jnp.logical_and(used, f == 0))
    def _():
        xb_ref[...] = _from_token_tiles(x_ref, tm).astype(BF16)
        acc_ref[...] = jnp.zeros_like(acc_ref)

    def swiglu_step(rows):
        xb = xb_ref[0:rows, :]
        tf = wg_ref.shape[2]
        halves = [slice(0, tf // 2), slice(tf // 2, tf)]
        acts = []
        for sl in halves:
            gt = jnp.dot(xb, wg_ref[0, :, sl], preferred_element_type=F32)
            up = jnp.dot(xb, wu_ref[0, :, sl], preferred_element_type=F32)
            acts.append((_silu(gt) * up).astype(BF16))
        acc_ref[0:rows, :] += sum(jnp.dot(act, wd_ref[0, sl, :], preferred_element_type=F32)
                                  for act, sl in zip(acts, halves))

    pl.when(whole)(functools.partial(swiglu_step, tm))
    pl.when(first_half_only)(functools.partial(swiglu_step, tm // 2))

    last = f == pl.num_programs(1) - 1

    @pl.when(jnp.logical_and(used, last))
    def _():
        _to_token_tiles(o_ref, acc_ref[...], tm)

    @pl.when(jnp.logical_and(jnp.logical_not(used), last))
    def _():
        o_ref[...] = jnp.zeros_like(o_ref)


def _experts(x_tiles, nb, block_e, n_used, n_halves, wg, wu, wd, tm, tf):
    n_exp, d, dff = wg.shape
    nf = dff // tf

    def xmap(i, f, be, nu, nh):
        return (jnp.minimum(i, nu[0] - 1), 0)

    def fidx(i, f, nu):
        return jnp.where(i < nu[0], f, nf - 1)

    grid_spec = pltpu.PrefetchScalarGridSpec(
        num_scalar_prefetch=3,
        grid=(nb, nf),
        in_specs=[pl.BlockSpec((tm * SUBLANES, LANES), xmap),
                  pl.BlockSpec((1, d, tf), lambda i, f, be, nu, nh: (be[i], 0, fidx(i, f, nu))),
                  pl.BlockSpec((1, d, tf), lambda i, f, be, nu, nh: (be[i], 0, fidx(i, f, nu))),
                  pl.BlockSpec((1, tf, d), lambda i, f, be, nu, nh: (be[i], fidx(i, f, nu), 0))],
        out_specs=pl.BlockSpec((tm * SUBLANES, LANES), lambda i, f, be, nu, nh: (i, 0)),
        scratch_shapes=[pltpu.VMEM((tm, d), F32), pltpu.VMEM((tm, d), BF16)])
    return pl.pallas_call(
        functools.partial(_expert_kernel, tm=tm),
        grid_spec=grid_spec,
        out_shape=jax.ShapeDtypeStruct((nb * tm * SUBLANES, LANES), F32),
        compiler_params=pltpu.CompilerParams(dimension_semantics=("arbitrary", "arbitrary"),
                                             vmem_limit_bytes=VMEM_LIMIT_BYTES),
        name="moe_experts",
    )(block_e, n_used, n_halves, x_tiles, wg, wu, wd)


def _combine_kernel(cur_ref, nxt_ref, y_hbm, w_ref, x_ref, gate_ref, g_ref, b_ref, o_ref, ybuf_ref, sems,
                    *, tc, alpha):
    i = pl.program_id(0)
    nt = TOP_K * tc

    def fetch(idx_ref, slot):
        buf, sem = ybuf_ref.at[slot], sems.at[slot]
        _issue_tile_copies(nt, lambda g, u: (_tile_at(y_hbm, idx_ref[0, 0, g * _COPY_UNROLL + u]),
                                             _tile_at(buf, g * _COPY_UNROLL + u), sem))

    @pl.when(i == 0)
    def _():
        fetch(cur_ref, 0)

    @pl.when(i + 1 < pl.num_programs(0))
    def _():
        fetch(nxt_ref, (i + 1) % 2)

    slot = i % 2
    yb = ybuf_ref.at[slot]
    pltpu.make_async_copy(y_hbm.at[pl.ds(0, nt * SUBLANES)], yb, sems.at[slot]).wait()
    w = w_ref[...]
    y = (w[:, 0:1] * _from_token_tiles(yb, tc, 0, TOP_K)
         + w[:, 1:2] * _from_token_tiles(yb, tc, 1, TOP_K))
    r = alpha * x_ref[0] + (1.0 + gate_ref[0]) * y
    o_ref[0] = _layer_norm_rows(r, g_ref[...], b_ref[...])


def _combine(y_tiles, dest, top_w, x, gate2, ln_g, ln_b, alpha, tc):
    bsz, seq, d = x.shape
    nj = seq // tc
    steps = bsz * nj
    nt = TOP_K * tc
    dest_blocks = dest.astype(jnp.int32).reshape(steps, 1, nt)
    return pl.pallas_call(
        functools.partial(_combine_kernel, tc=tc, alpha=alpha),
        grid=(steps,),
        in_specs=[pl.BlockSpec((1, 1, nt), lambda i: (i, 0, 0), memory_space=pltpu.SMEM),
                  pl.BlockSpec((1, 1, nt), lambda i: (jnp.minimum(i + 1, steps - 1), 0, 0),
                               memory_space=pltpu.SMEM),
                  pl.BlockSpec(memory_space=pl.ANY),
                  pl.BlockSpec((tc, LANES), lambda i: (i, 0)),
                  pl.BlockSpec((1, tc, d), lambda i: (i // nj, i % nj, 0)),
                  pl.BlockSpec((1, 1, d), lambda i: (i // nj, 0, 0)), _const_spec((1, d)), _const_spec((1, d))],
        out_specs=pl.BlockSpec((1, tc, d), lambda i: (i // nj, i % nj, 0)),
        out_shape=jax.ShapeDtypeStruct((bsz, seq, d), F32),
        scratch_shapes=[pltpu.VMEM((2, nt * SUBLANES, LANES), F32), pltpu.SemaphoreType.DMA((2,))],
        compiler_params=pltpu.CompilerParams(dimension_semantics=("arbitrary",),
                                             vmem_limit_bytes=VMEM_LIMIT_BYTES),
        name="moe_combine",
    )(dest_blocks, dest_blocks, y_tiles, top_w, x, gate2, ln_g, ln_b)


def _moe(oa, oc, lw, x, gate1, scale2, shift2, gate2, wr2, stri, wg, wu, wd, ln_g, ln_b, alpha,
         tm_tok, tm, tf, tc):
    bsz, seq, d = x.shape
    n_tok = bsz * seq
    n_asg = n_tok * TOP_K
    x1 = _outproj(oa, oc, lw, x, gate1, alpha, tm_tok)
    h_tiles, idx128, w128, cnt = _router(x1, scale2, shift2, wr2, stri, tm_tok)
    counts = cnt[0, :N_EXPERTS].astype(jnp.int32)
    padded = (counts + tm - 1) // tm * tm
    pad_end = jnp.cumsum(padded)
    pad_start = pad_end - padded
    e_ids, ranks = idx128[:, 0:TOP_K], idx128[:, TOP_K:2 * TOP_K]
    start_of = jnp.sum(jnp.where(e_ids[:, :, None] == jnp.arange(N_EXPERTS, dtype=jnp.int32), pad_start, 0), axis=-1)
    dest = (start_of + ranks).reshape(n_asg)
    nb = n_asg // tm + N_EXPERTS
    n_rows = nb * tm
    n_fill = n_rows - n_asg
    fill_end = jnp.cumsum(padded - counts)
    slot = jnp.arange(n_fill, dtype=jnp.int32)
    slot_e = jnp.sum((slot[:, None] >= fill_end[None, :]).astype(jnp.int32), axis=1)
    in_group = slot_e < N_EXPERTS
    ge = jnp.minimum(slot_e, N_EXPERTS - 1)
    fill_dst = jnp.where(in_group, (pad_start + counts)[ge] + slot - (fill_end - (padded - counts))[ge],
                         pad_end[-1] + slot - fill_end[-1])
    x_tiles = _group_rows(h_tiles, dest, fill_dst, n_rows, _tile(n_tok, 2048))
    block_row = jnp.arange(nb, dtype=jnp.int32) * tm
    block_e = jnp.minimum(jnp.sum((block_row[:, None] >= pad_end[None, :]).astype(jnp.int32), axis=1),
                          N_EXPERTS - 1)
    n_used = (pad_end[-1:] // tm).astype(jnp.int32)
    rows_in_block = (pad_start + counts)[block_e] - block_row
    n_halves = jnp.where(rows_in_block > tm // 2, 2, 1).astype(jnp.int32)
    y_tiles = _experts(x_tiles, nb, block_e, n_used, n_halves, wg, wu, wd, tm, tf)
    return _combine(y_tiles, dest, w128, x1, gate2, ln_g, ln_b, alpha, tc)


def _layer_weights(layer, w_in, b_fgate, w_conv, q_norm_g, kv_norm_g, w_uq, w_ukv, head_norm_g, w_o,
                   ln1_g, ln1_b):
    d = w_in.shape[1]
    sizes = [FOX_W, FOX_W, FOX_W, H_FOX, CONV_W, CONV_W, CONV_W, Q_LORA, KV_LORA, QK_ROPE]
    offs = np.concatenate([[0], np.cumsum(sizes)])
    wi = w_in[layer]
    fq, fk, fv, fl, bg, cg, hin, cq, ckv, kr = [wi[:, offs[i]:offs[i + 1]] for i in range(len(sizes))]
    half = QK_ROPE // 2
    z64 = jnp.zeros((d, QK_NOPE), F32)
    z32 = jnp.zeros((d, LANES - QK_NOPE - QK_ROPE), F32)
    wa = jnp.concatenate([
        fq, fk, fv,
        jnp.pad(jnp.repeat(fl, _N_SPLIT, axis=1), ((0, 0), (0, LANES - _N_SPLIT * H_FOX))), bg, cg, hin, cq, ckv,
        jnp.concatenate([z64, kr, z32], axis=1),
        jnp.concatenate([z64, kr[:, half:], kr[:, :half], z32], axis=1)], axis=1)
    assert wa.shape[1] == _NA
    q3 = w_uq[layer].reshape(Q_LORA, H_MLA, QK_NOPE + QK_ROPE)
    q_main = jnp.pad(q3, ((0, 0), (0, 0), (0, LANES - QK_NOPE - QK_ROPE)))
    q_swap = jnp.concatenate([jnp.zeros((Q_LORA, H_MLA, QK_NOPE), F32), q3[:, :, QK_NOPE + half:],
                              q3[:, :, QK_NOPE:QK_NOPE + half],
                              jnp.zeros((Q_LORA, H_MLA, LANES - QK_NOPE - QK_ROPE), F32)], axis=2)
    wuq = jnp.concatenate([q_main.reshape(Q_LORA, H_MLA * LANES), q_swap.reshape(Q_LORA, H_MLA * LANES)],
                          axis=1).astype(BF16)
    kv3 = w_ukv[layer].reshape(KV_LORA, H_MLA, QK_NOPE + V_DIM)
    k_nope = jnp.pad(kv3[:, :, :QK_NOPE], ((0, 0), (0, 0), (0, LANES - QK_NOPE))).reshape(KV_LORA, H_MLA * LANES)
    wukv = jnp.concatenate([k_nope, kv3[:, :, QK_NOPE:].reshape(KV_LORA, MLA_W)], axis=1).astype(BF16)
    hg = head_norm_g[layer]
    wo = w_o[layer]
    c0, c1 = FOX_W, FOX_W + CONV_W
    return {
        "wa": wa, "wuq": wuq, "wukv": wukv,
        "qg": q_norm_g[layer].reshape(1, Q_LORA), "kvg": kv_norm_g[layer].reshape(1, KV_LORA),
        "bf": jnp.pad(jnp.repeat(b_fgate[layer], _N_SPLIT), (0, LANES - _N_SPLIT * H_FOX)).reshape(1, LANES),
        "wconv": jnp.pad(w_conv[layer], ((0, SUBLANES - CONV_K), (0, 0))),
        "hg_conv": hg[c0:c1].reshape(1, CONV_W),
        "hg_attn": jnp.concatenate([hg[:c0], hg[c1:]]).reshape(1, FOX_W + MLA_W),
        "wo_a": jnp.concatenate([wo[:c0], wo[c1:]], axis=0).astype(BF16),
        "wo_c": wo[c0:c1].astype(BF16),
        "ln1_g": ln1_g[layer].reshape(1, d), "ln1_b": ln1_b[layer].reshape(1, d),
    }


def _constants(seq, ts):
    pos = jnp.arange(seq, dtype=F32)
    inv_freq = ROPE_THETA ** (-jnp.arange(0, QK_ROPE, 2, dtype=F32) / QK_ROPE)
    ang = pos[:, None] * inv_freq[None, :]
    cos, sin = jnp.cos(ang), jnp.sin(ang)
    pad_r = jnp.zeros((seq, LANES - QK_NOPE - QK_ROPE), F32)
    ctab = jnp.concatenate([jnp.ones((seq, QK_NOPE), F32), cos, cos, pad_r], axis=1)
    stab = jnp.concatenate([jnp.zeros((seq, QK_NOPE), F32), -sin, sin, pad_r], axis=1)
    q_scale = (QK_NOPE + QK_ROPE) ** -0.5 * LOG2E
    lanes = np.arange(LANES)
    lm = np.zeros((SUBLANES, LANES), np.float32)
    for part in range(_N_SPLIT):
        lm[part] = (lanes % _N_SPLIT == part) & (lanes < _N_SPLIT * H_FOX)
    for row, base in ((3, HEAD_DIM), (5, 0)):
        lm[row] = (lanes >= base) & (lanes < base + _N_SPLIT)
        lm[row + 1] = (lanes >= base + _N_SPLIT) & (lanes < base + 2 * _N_SPLIT)

    def group_mean(n):
        gidx = np.arange(n) // HEAD_DIM
        return (gidx[:, None] == gidx[None, :]).astype(np.float32) / HEAD_DIM

    return {
        "cq": ctab * q_scale, "sq": stab * q_scale, "ck": ctab, "sk": stab,
        "tri": jnp.asarray(np.tril(np.ones((ts, ts), np.float32)), BF16),
        "stri": jnp.asarray(np.tril(np.ones((ts, ts), np.float32), k=-1), BF16),
        "lane_masks": jnp.asarray(lm),
        "gm256": jnp.asarray(group_mean(CONV_W), BF16), "gm128": jnp.asarray(group_mean(LANES), BF16),
    }


def _tile(n, pref):
    t = min(n, pref)
    assert n % t == 0, (n, pref)
    return t


def kernel(x, c, w_mod, b_mod, w_in, b_fgate, w_conv, q_norm_g, kv_norm_g, w_uq, w_ukv, head_norm_g, w_o, ln1_g, ln1_b, ln2_g, ln2_b, ffn_w_gate, ffn_w_up, ffn_w_down, router_w, exp_w_gate, exp_w_up, exp_w_down):
    bsz, seq, d = x.shape
    depth = w_mod.shape[0]
    assert d == D_MODEL and bsz <= SUBLANES
    alpha = (2 * depth) ** 0.25
    ts = _tile(seq, 512)
    tq = _tile(seq, 512)
    tm_e = 1024
    tf_e = 512
    tc = _tile(seq, 256)
    assert (bsz * seq * TOP_K) % tm_e == 0 and exp_w_gate.shape[-1] % tf_e == 0

    consts = _constants(seq, ts)
    c_pad = jnp.pad(c, ((0, SUBLANES - bsz), (0, 0)))
    mod = _modulation(c_pad, w_mod, b_mod)[:, :bsz, :]

    for layer in range(depth):
        m6 = mod[layer].reshape(bsz, 6, 1, d)
        shift1, scale1, gate1, shift2, scale2, gate2 = [m6[:, i] for i in range(6)]
        lw = _layer_weights(layer, w_in, b_fgate, w_conv, q_norm_g, kv_norm_g, w_uq, w_ukv, head_norm_g,
                            w_o, ln1_g, ln1_b)
        qp, kp, vt, oc = _inproj(x, scale1, shift1, lw, consts, ts)
        oa = _attention(qp, kp, vt, lw["hg_attn"], consts["gm128"], tq, 12)
        j = layer // 2
        g2, b2 = ln2_g[layer].reshape(1, d), ln2_b[layer].reshape(1, d)
        if layer % 2 == 0:
            x = _ffn_dense(oa, oc, lw, x, gate1, scale2, shift2, gate2, ffn_w_gate[j].astype(BF16),
                           ffn_w_up[j].astype(BF16), ffn_w_down[j].astype(BF16), g2, b2, alpha, ts)
        else:
            wr = jnp.pad(router_w[j], ((0, 0), (0, LANES - N_EXPERTS)))
            wr_hi = wr.astype(BF16)
            wr_lo = (wr - wr_hi.astype(F32)).astype(BF16)
            x = _moe(oa, oc, lw, x, gate1, scale2, shift2, gate2, jnp.stack([wr_hi, wr_lo]), consts["stri"],
                     exp_w_gate[j].astype(BF16), exp_w_up[j].astype(BF16), exp_w_down[j].astype(BF16),
                     g2, b2, alpha, ts, tm_e, tf_e, tc)
    return x
```

```python
import functools

import numpy as np
import jax
import jax.numpy as jnp
from jax import lax
from jax.experimental import pallas as pl
from jax.experimental.pallas import tpu as pltpu

F32 = jnp.float32
BF16 = jnp.bfloat16

D_MODEL = 1024
HEAD_DIM = 64
H_FOX = 6
FOX_W = H_FOX * HEAD_DIM
CONV_W = 256
CONV_K = 3
H_MLA = 6
Q_LORA = 256
KV_LORA = 256
QK_NOPE = 64
QK_ROPE = 32
V_DIM = 64
MLA_W = H_MLA * V_DIM
N_HEADS = H_FOX + H_MLA
ROPE_THETA = 10000.0
N_EXPERTS = 8
TOP_K = 2
LN_EPS = 1e-5
RMS_EPS = 1e-6

LANES = 128
SUBLANES = 8
VMEM_LIMIT_BYTES = 56 * 1024 * 1024

_HW = 6 * LANES
_QF0, _KF0, _VF0, _FL0 = 0, FOX_W, 2 * FOX_W, 3 * FOX_W
_BG0 = _FL0 + LANES
_CG0, _HI0, _CQ0, _CKV0 = _BG0 + CONV_W, _BG0 + 2 * CONV_W, _BG0 + 3 * CONV_W, _BG0 + 3 * CONV_W + Q_LORA
_KR0 = _CKV0 + KV_LORA
_KRS0 = _KR0 + LANES
_NA = _KRS0 + LANES
LOG2E = 1.4426950408889634
VT_ROWS = LANES
_N_SPLIT = 3


def _const_spec(shape):
    zeros = (0,) * len(shape)
    return pl.BlockSpec(shape, lambda *_: zeros, pipeline_mode=pl.Buffered(1))


def _silu(v):
    return v * (1.0 / (1.0 + jnp.exp(-v)))


def _split3(v):
    hi = v.astype(BF16)
    r1 = v - hi.astype(F32)
    mid = r1.astype(BF16)
    lo = (r1 - mid.astype(F32)).astype(BF16)
    return hi, mid, lo


def _group_mean_sq(v, gmat_ref):
    sq = v * v
    hi = sq.astype(BF16)
    lo = (sq - hi.astype(F32)).astype(BF16)
    g = gmat_ref[...]
    return (jnp.dot(hi, g, preferred_element_type=F32) + jnp.dot(lo, g, preferred_element_type=F32))


def _layer_norm_rows(r, g, b):
    mu = jnp.mean(r, axis=-1, keepdims=True)
    rc = r - mu
    var = jnp.mean(rc * rc, axis=-1, keepdims=True)
    return rc * lax.rsqrt(var + LN_EPS) * g + b


def _mod_kernel(c_ref, w_ref, b_ref, o_ref):
    act = _silu(c_ref[...]).astype(BF16)
    o_ref[0] = jnp.dot(act, w_ref[0].astype(BF16), preferred_element_type=F32) + b_ref[0]


def _modulation(c_pad, w_mod, b_mod):
    depth, d, n = w_mod.shape
    tn = 1024
    return pl.pallas_call(
        _mod_kernel,
        grid=(depth, n // tn),
        in_specs=[pl.BlockSpec((SUBLANES, d), lambda l, j: (0, 0)),
                  pl.BlockSpec((1, d, tn), lambda l, j: (l, 0, j)),
                  pl.BlockSpec((1, 1, tn), lambda l, j: (l, 0, j))],
        out_specs=pl.BlockSpec((1, SUBLANES, tn), lambda l, j: (l, 0, j)),
        out_shape=jax.ShapeDtypeStruct((depth, SUBLANES, n), F32),
        compiler_params=pltpu.CompilerParams(dimension_semantics=("arbitrary", "arbitrary")),
        name="modulation",
    )(c_pad, w_mod, b_mod.reshape(depth, 1, n))


def _inproj_kernel(x_ref, sc_ref, sh_ref, wa_ref, cq_ref, sq_ref, ck_ref, sk_ref, wuq_ref, wukv_ref,
                   qg_ref, kvg_ref, bf_ref, wconv_ref, hgc_ref, tri_ref, lm_ref, gm_ref,
                   qp_ref, kp_ref, vt_ref, oc_ref, fcarry_ref, ubuf_ref, wab_ref, *, ts):
    @pl.when(jnp.logical_and(pl.program_id(0) == 0, pl.program_id(1) == 0))
    def _():
        for c in range(_NA // LANES):
            sl = slice(c * LANES, (c + 1) * LANES)
            wab_ref[:, sl] = wa_ref[:, sl].astype(BF16)

    @pl.when(pl.program_id(1) == 0)
    def _():
        fcarry_ref[...] = jnp.zeros_like(fcarry_ref)
        ubuf_ref[pl.ds(0, SUBLANES), :] = jnp.zeros((SUBLANES, CONV_W), F32)

    hb = (x_ref[0] * (1.0 + sc_ref[0]) + sh_ref[0]).astype(BF16)

    def proj(lo, hi):
        return jnp.dot(hb, wab_ref[:, lo:hi], preferred_element_type=F32)

    def put_values_t(head, v_half_t):
        vt_ref[0, head, 0:V_DIM, :] = v_half_t.astype(BF16)
        vt_ref[0, head, V_DIM:VT_ROWS, :] = jnp.ones((VT_ROWS - V_DIM, ts), BF16)

    z_fl = proj(_FL0, _FL0 + LANES)
    z_cq = proj(_CQ0, _CQ0 + Q_LORA)
    z_ckv = proj(_CKV0, _CKV0 + KV_LORA)
    z_cg = proj(_CG0, _CG0 + CONV_W)
    z_hi = proj(_HI0, _HI0 + CONV_W)
    z_bg = proj(_BG0, _BG0 + CONV_W)
    zq = proj(_QF0, _QF0 + FOX_W) * (HEAD_DIM ** -0.5 * LOG2E)
    zk = proj(_KF0, _KF0 + FOX_W)
    zv = proj(_VF0, _VF0 + FOX_W)
    z_kr = proj(_KR0, _KR0 + LANES)
    z_krs = proj(_KRS0, _KRS0 + LANES)

    a = z_fl + bf_ref[...]
    logf = jnp.minimum(a, 0.0) - jnp.log1p(jnp.exp(-jnp.abs(a)))
    tri = tri_ref[...]
    csum = sum(jnp.dot(tri, part, preferred_element_type=F32) for part in _split3(logf))
    fcum = fcarry_ref[...] + csum
    fcarry_ref[...] = fcum[ts - 1:ts, :]
    f_hi, f_mid, f_lo = _split3(fcum * LOG2E)
    lm = lm_ref[...]
    fparts = (f_hi.astype(F32) * lm[0:1, :] + f_mid.astype(F32) * lm[1:2, :]
              + f_lo.astype(F32) * lm[2:3, :])
    lane = lax.broadcasted_iota(jnp.int32, (ts, LANES), 1)
    low_half = lane < HEAD_DIM
    for h in range(H_FOX):
        blk = slice((h // 2) * LANES, (h // 2 + 1) * LANES)
        if h % 2 == 0:
            base, f_mask, one_mask, keep = HEAD_DIM, lm[3:4, :], lm[4:5, :], low_half
        else:
            base, f_mask, one_mask, keep = 0, lm[5:6, :], lm[6:7, :], jnp.logical_not(low_half)
        f_at_q = pltpu.roll(fparts, (base - _N_SPLIT * h) % LANES, axis=1)
        f_at_k = pltpu.roll(fparts, (base + _N_SPLIT - _N_SPLIT * h) % LANES, axis=1)
        qp_ref[0, h] = jnp.where(keep, zq[:, blk], f_at_q * f_mask + one_mask).astype(BF16)
        kp_ref[0, h] = jnp.where(keep, zk[:, blk], f_mask - f_at_k * one_mask).astype(BF16)
    for j in range(H_FOX // 2):
        vt = zv[:, j * LANES:(j + 1) * LANES].T
        put_values_t(2 * j, vt[0:V_DIM])
        put_values_t(2 * j + 1, vt[V_DIM:LANES])

    u = z_cg * z_hi
    ubuf_ref[pl.ds(SUBLANES, ts), :] = u
    u1 = ubuf_ref[pl.ds(SUBLANES - 1, ts), :]
    u2 = ubuf_ref[pl.ds(SUBLANES - 2, ts), :]
    ubuf_ref[pl.ds(0, SUBLANES), :] = u[ts - SUBLANES:ts, :]
    wc = wconv_ref[...]
    oc = z_bg * (wc[0:1, :] * u2 + wc[1:2, :] * u1 + wc[2:3, :] * u)
    ocn = oc * lax.rsqrt(_group_mean_sq(oc, gm_ref) + RMS_EPS) * hgc_ref[...]
    oc_ref[0] = ocn.astype(BF16)

    def rms(v, g):
        return (v * lax.rsqrt(jnp.mean(v * v, axis=-1, keepdims=True) + RMS_EPS) * g).astype(BF16)

    cqn = rms(z_cq, qg_ref[...])
    qm = jnp.dot(cqn, wuq_ref[:, 0:_HW], preferred_element_type=F32)
    qs = jnp.dot(cqn, wuq_ref[:, _HW:2 * _HW], preferred_element_type=F32)
    kvn = rms(z_ckv, kvg_ref[...])
    kn = jnp.dot(kvn, wukv_ref[:, 0:_HW], preferred_element_type=F32)
    vm = jnp.dot(kvn, wukv_ref[:, _HW:_HW + MLA_W], preferred_element_type=F32)
    krr = z_kr * ck_ref[...] + z_krs * sk_ref[...]
    cq, sq = cq_ref[...], sq_ref[...]
    for h in range(H_MLA):
        sl = slice(h * LANES, (h + 1) * LANES)
        qp_ref[0, H_FOX + h] = (qm[:, sl] * cq + qs[:, sl] * sq).astype(BF16)
        kp_ref[0, H_FOX + h] = (kn[:, sl] + krr).astype(BF16)
    for j in range(H_MLA // 2):
        vt = vm[:, j * LANES:(j + 1) * LANES].T
        put_values_t(H_FOX + 2 * j, vt[0:V_DIM])
        put_values_t(H_FOX + 2 * j + 1, vt[V_DIM:LANES])


def _inproj(x, scale1, shift1, lw, consts, ts):
    bsz, seq, d = x.shape
    kern = functools.partial(_inproj_kernel, ts=ts)
    row = lambda b, j: (b, 0, 0)
    tab = pl.BlockSpec((ts, LANES), lambda b, j: (j, 0))
    return pl.pallas_call(
        kern,
        grid=(bsz, seq // ts),
        in_specs=[pl.BlockSpec((1, ts, d), lambda b, j: (b, j, 0)),
                  pl.BlockSpec((1, 1, d), row), pl.BlockSpec((1, 1, d), row),
                  _const_spec((d, _NA)), tab, tab, tab, tab,
                  _const_spec((Q_LORA, 2 * _HW)), _const_spec((KV_LORA, _HW + MLA_W)),
                  _const_spec((1, Q_LORA)), _const_spec((1, KV_LORA)), _const_spec((1, LANES)),
                  _const_spec((SUBLANES, CONV_W)), _const_spec((1, CONV_W)),
                  _const_spec((ts, ts)), _const_spec((SUBLANES, LANES)), _const_spec((CONV_W, CONV_W))],
        out_specs=[pl.BlockSpec((1, N_HEADS, ts, LANES), lambda b, j: (b, 0, j, 0)),
                   pl.BlockSpec((1, N_HEADS, ts, LANES), lambda b, j: (b, 0, j, 0)),
                   pl.BlockSpec((1, N_HEADS, VT_ROWS, ts), lambda b, j: (b, 0, 0, j)),
                   pl.BlockSpec((1, ts, CONV_W), lambda b, j: (b, j, 0))],
        out_shape=[jax.ShapeDtypeStruct((bsz, N_HEADS, seq, LANES), BF16),
                   jax.ShapeDtypeStruct((bsz, N_HEADS, seq, LANES), BF16),
                   jax.ShapeDtypeStruct((bsz, N_HEADS, VT_ROWS, seq), BF16),
                   jax.ShapeDtypeStruct((bsz, seq, CONV_W), BF16)],
        scratch_shapes=[pltpu.VMEM((1, LANES), F32), pltpu.VMEM((ts + SUBLANES, CONV_W), F32),
                        pltpu.VMEM((d, _NA), BF16)],
        compiler_params=pltpu.CompilerParams(dimension_semantics=("arbitrary", "arbitrary"),
                                             vmem_limit_bytes=VMEM_LIMIT_BYTES),
        name="inproj",
    )(x, scale1, shift1, lw["wa"], consts["cq"], consts["sq"], consts["ck"], consts["sk"],
      lw["wuq"], lw["wukv"], lw["qg"], lw["kvg"], lw["bf"], lw["wconv"], lw["hg_conv"],
      consts["tri"], consts["lane_masks"], consts["gm256"])


def _attn_kernel(qi_ref, kj_ref, q_ref, k_ref, vt_ref, g_ref, gm_ref, o_ref, m_ref, acc_ref, s_ref, *, nh):
    step = pl.program_id(2)
    qi = qi_ref[step]
    kj = kj_ref[step]

    @pl.when(kj == 0)
    def _():
        m_ref[...] = jnp.full(m_ref.shape, -jnp.inf, F32)
        acc_ref[...] = jnp.zeros_like(acc_ref)

    def update(masked):
        if masked:
            tile = (q_ref.shape[2], q_ref.shape[2])
            causal = lax.broadcasted_iota(jnp.int32, tile, 0) <= lax.broadcasted_iota(jnp.int32, tile, 1)

        def scores(h):
            st = lax.dot_general(k_ref[0, h], q_ref[0, h], (((1,), (1,)), ((), ())),
                                 preferred_element_type=F32)
            if masked:
                st = jnp.where(causal, st, -jnp.inf)
            s_ref[h] = st

        for h in range(nh):
            scores(h)
        pts, alphas = [], []
        for h in range(nh):
            m_prev = m_ref[h]
            m_new = jnp.maximum(m_prev, jnp.max(s_ref[h], axis=0, keepdims=True))
            pts.append(jnp.exp2(s_ref[h] - m_new).astype(BF16))
            alphas.append(jnp.exp2(m_prev - m_new))
            m_ref[h] = m_new
        for h in range(nh):
            acc_ref[h] = alphas[h] * acc_ref[h] + jnp.dot(vt_ref[0, h], pts[h], preferred_element_type=F32)

    @pl.when(kj < qi)
    def _():
        update(False)

    @pl.when(kj == qi)
    def _():
        update(True)
        for pair in range(nh // 2):
            ot = jnp.concatenate([acc_ref[h, 0:V_DIM, :] / acc_ref[h, V_DIM:2 * V_DIM, :]
                                  for h in (2 * pair, 2 * pair + 1)], axis=0)
            o = ot.T
            sl = slice(pair * LANES, (pair + 1) * LANES)
            on = o * lax.rsqrt(_group_mean_sq(o, gm_ref) + RMS_EPS) * g_ref[:, sl]
            o_ref[0, :, sl] = on.astype(BF16)


def _attention(qp, kp, vt, g_attn, gm128, tq, nh):
    bsz, n_heads, seq, _ = qp.shape
    nq = seq // tq
    ow = (nh // 2) * LANES
    qi_tab = np.concatenate([np.full(i + 1, i) for i in range(nq)]).astype(np.int32)
    kj_tab = np.concatenate([np.arange(i + 1) for i in range(nq)]).astype(np.int32)
    grid_spec = pltpu.PrefetchScalarGridSpec(
        num_scalar_prefetch=2,
        grid=(bsz, n_heads // nh, len(qi_tab)),
        in_specs=[pl.BlockSpec((1, nh, tq, LANES), lambda b, p, s, qi, kj: (b, p, qi[s], 0)),
                  pl.BlockSpec((1, nh, tq, LANES), lambda b, p, s, qi, kj: (b, p, kj[s], 0)),
                  pl.BlockSpec((1, nh, VT_ROWS, tq), lambda b, p, s, qi, kj: (b, p, 0, kj[s])),
                  pl.BlockSpec((1, ow), lambda b, p, s, qi, kj: (0, p)),
                  _const_spec((LANES, LANES))],
        out_specs=pl.BlockSpec((1, tq, ow), lambda b, p, s, qi, kj: (b, qi[s], p)),
        scratch_shapes=[pltpu.VMEM((nh, 1, tq), F32), pltpu.VMEM((nh, VT_ROWS, tq), F32),
                        pltpu.VMEM((nh, tq, tq), F32)])
    return pl.pallas_call(
        functools.partial(_attn_kernel, nh=nh),
        grid_spec=grid_spec,
        out_shape=jax.ShapeDtypeStruct((bsz, seq, (n_heads // 2) * LANES), BF16),
        compiler_params=pltpu.CompilerParams(
            dimension_semantics=("arbitrary", "arbitrary", "arbitrary"),
            vmem_limit_bytes=VMEM_LIMIT_BYTES),
        name="attention",
    )(jnp.asarray(qi_tab), jnp.asarray(kj_tab), qp, kp, vt, g_attn, gm128)


def _mixer_out_rows(oa_ref, oc_ref, woa_ref, woc_ref, x_ref, gate_ref, g_ref, b_ref, alpha):
    y = (jnp.dot(oa_ref[0], woa_ref[...], preferred_element_type=F32)
         + jnp.dot(oc_ref[0], woc_ref[...], preferred_element_type=F32))
    r = alpha * x_ref[0] + (1.0 + gate_ref[0]) * y
    return _layer_norm_rows(r, g_ref[...], b_ref[...])


def _mixer_out_specs(oa, d, tm, act, row):
    wa_rows = oa.shape[-1]
    return [pl.BlockSpec((1, tm, wa_rows), act), pl.BlockSpec((1, tm, CONV_W), act),
            _const_spec((wa_rows, d)), _const_spec((CONV_W, d)),
            pl.BlockSpec((1, tm, d), act), pl.BlockSpec((1, 1, d), row), _const_spec((1, d)), _const_spec((1, d))]


def _outproj_kernel(oa_ref, oc_ref, woa_ref, woc_ref, x_ref, gate_ref, g_ref, b_ref, o_ref, *, alpha):
    o_ref[0] = _mixer_out_rows(oa_ref, oc_ref, woa_ref, woc_ref, x_ref, gate_ref, g_ref, b_ref, alpha)


def _outproj(oa, oc, lw, x, gate1, alpha, tm):
    bsz, seq, d = x.shape
    act = lambda b, j: (b, j, 0)
    return pl.pallas_call(
        functools.partial(_outproj_kernel, alpha=alpha),
        grid=(bsz, seq // tm),
        in_specs=_mixer_out_specs(oa, d, tm, act, lambda b, j: (b, 0, 0)),
        out_specs=pl.BlockSpec((1, tm, d), act),
        out_shape=jax.ShapeDtypeStruct((bsz, seq, d), F32),
        compiler_params=pltpu.CompilerParams(dimension_semantics=("arbitrary", "arbitrary"),
                                             vmem_limit_bytes=VMEM_LIMIT_BYTES),
        name="outproj",
    )(oa, oc, lw["wo_a"], lw["wo_c"], x, gate1, lw["ln1_g"], lw["ln1_b"])


def _ffn_kernel(oa_ref, oc_ref, woa_ref, woc_ref, xin_ref, gate1_ref, g1_ref, b1_ref,
                sc_ref, sh_ref, gate_ref, wg_ref, wu_ref, wd_ref, g_ref, b_ref, o_ref, *, alpha, tf):
    x = _mixer_out_rows(oa_ref, oc_ref, woa_ref, woc_ref, xin_ref, gate1_ref, g1_ref, b1_ref, alpha)
    hb = (x * (1.0 + sc_ref[0]) + sh_ref[0]).astype(BF16)
    chunks = [slice(c * tf, (c + 1) * tf) for c in range(wg_ref.shape[1] // tf)]
    acts = []
    for sl in chunks:
        gt = jnp.dot(hb, wg_ref[:, sl], preferred_element_type=F32)
        up = jnp.dot(hb, wu_ref[:, sl], preferred_element_type=F32)
        acts.append((_silu(gt) * up).astype(BF16))
    acc = sum(jnp.dot(act, wd_ref[sl, :], preferred_element_type=F32) for act, sl in zip(acts, chunks))
    r = alpha * x + (1.0 + gate_ref[0]) * acc
    o_ref[0] = _layer_norm_rows(r, g_ref[...], b_ref[...])


def _ffn_dense(oa, oc, lw, x, gate1, scale2, shift2, gate2, wg, wu, wd, ln_g, ln_b, alpha, tm):
    bsz, seq, d = x.shape
    dff = wg.shape[1]
    tf = next(t for t in (704, 512, 256, 128, dff) if dff % t == 0 and t % LANES == 0)
    nj = seq // tm
    act = lambda b, j: (b, j, 0)
    row = lambda b, j: (b, 0, 0)
    return pl.pallas_call(
        functools.partial(_ffn_kernel, alpha=alpha, tf=tf),
        grid=(bsz, nj),
        in_specs=_mixer_out_specs(oa, d, tm, act, row) + [
            pl.BlockSpec((1, 1, d), row), pl.BlockSpec((1, 1, d), row), pl.BlockSpec((1, 1, d), row),
            _const_spec((d, dff)), _const_spec((d, dff)), _const_spec((dff, d)),
            _const_spec((1, d)), _const_spec((1, d))],
        out_specs=pl.BlockSpec((1, tm, d), act),
        out_shape=jax.ShapeDtypeStruct((bsz, seq, d), F32),
        compiler_params=pltpu.CompilerParams(dimension_semantics=("arbitrary", "arbitrary"),
                                             vmem_limit_bytes=VMEM_LIMIT_BYTES),
        name="ffn_dense",
    )(oa, oc, lw["wo_a"], lw["wo_c"], x, gate1, lw["ln1_g"], lw["ln1_b"],
      scale2, shift2, gate2, wg, wu, wd, ln_g, ln_b)


def _to_token_tiles(ref, v, n):
    for c in range(v.shape[1] // LANES):
        ref[pl.ds(c, n, stride=SUBLANES), :] = v[:, c * LANES:(c + 1) * LANES]


def _from_token_tiles(ref, n, first=0, tiles_per_row=1):
    stride = tiles_per_row * SUBLANES
    return jnp.concatenate([ref[pl.ds(first * SUBLANES + c, n, stride=stride), :] for c in range(SUBLANES)],
                           axis=1)


def _router_kernel(x_ref, sc_ref, sh_ref, wr_ref, stri_ref, h_ref, idx_ref, w_ref, cnt_ref, carry_ref, *, tm):
    @pl.when(pl.program_id(0) == 0)
    def _():
        carry_ref[...] = jnp.zeros_like(carry_ref)

    h = x_ref[0] * (1.0 + sc_ref[0]) + sh_ref[0]
    _to_token_tiles(h_ref, h, tm)
    h_hi = h.astype(BF16)
    h_lo = (h - h_hi.astype(F32)).astype(BF16)
    w_hi, w_lo = wr_ref[0], wr_ref[1]
    logits = (jnp.dot(h_hi, w_hi, preferred_element_type=F32)
              + jnp.dot(h_hi, w_lo, preferred_element_type=F32)
              + jnp.dot(h_lo, w_hi, preferred_element_type=F32))
    lane = lax.broadcasted_iota(jnp.int32, logits.shape, 1).astype(F32)
    neg = -jnp.inf
    lg = jnp.where(lane < N_EXPERTS, logits, neg)
    m1 = jnp.max(lg, axis=-1, keepdims=True)
    i1 = jnp.min(jnp.where(lg == m1, lane, float(LANES)), axis=-1, keepdims=True)
    lg2 = jnp.where(lane == i1, neg, lg)
    m2 = jnp.max(lg2, axis=-1, keepdims=True)
    i2 = jnp.min(jnp.where(lg2 == m2, lane, float(LANES)), axis=-1, keepdims=True)
    e2 = jnp.exp(m2 - m1)
    denom = 1.0 + e2
    w_ref[...] = jnp.where(lane == 0.0, 1.0 / denom, e2 / denom)
    first, second = lane == i1, lane == i2
    chosen = jnp.logical_or(first, second).astype(F32)
    before = carry_ref[...] + jnp.dot(stri_ref[...], chosen.astype(BF16), preferred_element_type=F32)
    rank1 = jnp.sum(jnp.where(first, before, 0.0), axis=-1, keepdims=True)
    rank2 = jnp.sum(jnp.where(second, before, 0.0), axis=-1, keepdims=True)
    idx_ref[...] = jnp.where(lane == 0.0, i1, jnp.where(lane == 1.0, i2, jnp.where(lane == 2.0, rank1, rank2))
                             ).astype(jnp.int32)
    total = carry_ref[...] + jnp.sum(chosen, axis=0, keepdims=True)
    carry_ref[...] = total
    cnt_ref[...] = jnp.broadcast_to(total, cnt_ref.shape)


def _router(x, scale2, shift2, wr2, stri, tm):
    bsz, seq, d = x.shape
    assert d == SUBLANES * LANES and bsz * seq * TOP_K < 2 ** 24
    nj = seq // tm
    n_tiles = bsz * nj
    row = lambda i: (i // nj, 0, 0)
    tok = lambda i: (i, 0)
    return pl.pallas_call(
        functools.partial(_router_kernel, tm=tm),
        grid=(n_tiles,),
        in_specs=[pl.BlockSpec((1, tm, d), lambda i: (i // nj, i % nj, 0)),
                  pl.BlockSpec((1, 1, d), row), pl.BlockSpec((1, 1, d), row),
                  _const_spec((2, d, LANES)), _const_spec((tm, tm))],
        out_specs=[pl.BlockSpec((tm * SUBLANES, LANES), tok),
                   pl.BlockSpec((tm, LANES), tok), pl.BlockSpec((tm, LANES), tok),
                   pl.BlockSpec((SUBLANES, LANES), lambda i: (0, 0))],
        out_shape=[jax.ShapeDtypeStruct((n_tiles * tm * SUBLANES, LANES), F32),
                   jax.ShapeDtypeStruct((bsz * seq, LANES), jnp.int32),
                   jax.ShapeDtypeStruct((bsz * seq, LANES), F32),
                   jax.ShapeDtypeStruct((SUBLANES, LANES), F32)],
        scratch_shapes=[pltpu.VMEM((1, LANES), F32)],
        compiler_params=pltpu.CompilerParams(dimension_semantics=("arbitrary",),
                                             vmem_limit_bytes=VMEM_LIMIT_BYTES),
        name="router",
    )(x, scale2, shift2, wr2, stri)


_COPY_UNROLL = 8


def _tile_at(ref, t):
    return ref.at[pl.ds(pl.multiple_of(t * SUBLANES, SUBLANES), SUBLANES)]


def _issue_tile_copies(n, copy_of):
    def group(g, carry):
        copies = [copy_of(g, u) for u in range(_COPY_UNROLL)]
        for u, (src, dst, sem) in enumerate(copies):
            pltpu.make_async_copy(src, dst, sem).start(priority=u % 2)
        return carry

    assert n % _COPY_UNROLL == 0
    lax.fori_loop(0, n // _COPY_UNROLL, group, 0)


def _group_kernel(di_ref, fi_ref, src_ref, dst_hbm, zero_ref, sem, *, tb, nf):
    zero_ref[...] = jnp.zeros_like(zero_ref)
    per_group = _COPY_UNROLL // TOP_K
    _issue_tile_copies(TOP_K * tb, lambda g, u: (_tile_at(src_ref, g * per_group + u // TOP_K),
                                                 _tile_at(dst_hbm, di_ref[0, 0, g * _COPY_UNROLL + u]), sem))
    _issue_tile_copies(nf, lambda g, u: (zero_ref, _tile_at(dst_hbm, fi_ref[0, 0, g * _COPY_UNROLL + u]), sem))
    block = pl.ds(0, tb * SUBLANES)
    for _ in range(TOP_K):
        pltpu.make_async_copy(src_ref, dst_hbm.at[block], sem).wait()

    def wait_fill(r, carry):
        pltpu.make_async_copy(zero_ref, _tile_at(dst_hbm, 0), sem).wait()
        return carry

    lax.fori_loop(0, nf, wait_fill, 0)


def _group_rows(h_tiles, dest, fill_dst, n_rows, tb):
    n_asg = dest.shape[0]
    steps = n_asg // (TOP_K * tb)
    nf = fill_dst.shape[0] // steps
    assert steps * TOP_K * tb == n_asg and steps * nf == fill_dst.shape[0]
    return pl.pallas_call(
        functools.partial(_group_kernel, tb=tb, nf=nf),
        grid=(steps,),
        in_specs=[pl.BlockSpec((1, 1, TOP_K * tb), lambda i: (i, 0, 0), memory_space=pltpu.SMEM),
                  pl.BlockSpec((1, 1, nf), lambda i: (i, 0, 0), memory_space=pltpu.SMEM),
                  pl.BlockSpec((tb * SUBLANES, LANES), lambda i: (i, 0))],
        out_specs=pl.BlockSpec(memory_space=pl.ANY),
        out_shape=jax.ShapeDtypeStruct((n_rows * SUBLANES, LANES), h_tiles.dtype),
        scratch_shapes=[pltpu.VMEM((SUBLANES, LANES), h_tiles.dtype), pltpu.SemaphoreType.DMA(())],
        compiler_params=pltpu.CompilerParams(dimension_semantics=("arbitrary",), has_side_effects=True),
        name="moe_group",
    )(dest.astype(jnp.int32).reshape(steps, 1, TOP_K * tb), fill_dst.astype(jnp.int32).reshape(steps, 1, nf),
      h_tiles)


def _expert_kernel(be_ref, nu_ref, nh_ref, x_ref, wg_ref, wu_ref, wd_ref, o_ref, acc_ref, xb_ref, *, tm):
    i = pl.program_id(0)
    f = pl.program_id(1)
    used = i < nu_ref[0]
    whole = jnp.logical_and(used, nh_ref[i] == 2)
    first_half_only = jnp.logical_and(used, nh_ref[i] == 1)

    @pl.when(jnp.logical_and(used, f == 0))
    def _():
        xb_ref[...] = _from_token_tiles(x_ref, tm).astype(BF16)
        acc_ref[...] = jnp.zeros_like(acc_ref)

    def swiglu_step(rows):
        xb = xb_ref[0:rows, :]
        tf = wg_ref.shape[2]
        halves = [slice(c, c + 256) for c in range(0, tf, 256)]
        acts = []
        for sl in halves:
            gt = jnp.dot(xb, wg_ref[0, :, sl], preferred_element_type=F32)
            up = jnp.dot(xb, wu_ref[0, :, sl], preferred_element_type=F32)
            acts.append((_silu(gt) * up).astype(BF16))
        acc_ref[0:rows, :] += sum(jnp.dot(act, wd_ref[0, sl, :], preferred_element_type=F32)
                                  for act, sl in zip(acts, halves))

    pl.when(whole)(functools.partial(swiglu_step, tm))
    pl.when(first_half_only)(functools.partial(swiglu_step, tm // 2))

    last = f == pl.num_programs(1) - 1

    @pl.when(jnp.logical_and(used, last))
    def _():
        _to_token_tiles(o_ref, acc_ref[...], tm)

    @pl.when(jnp.logical_and(jnp.logical_not(used), last))
    def _():
        o_ref[...] = jnp.zeros_like(o_ref)


def _experts(x_tiles, nb, block_e, n_used, n_halves, wg, wu, wd, tm, tf):
    n_exp, d, dff = wg.shape
    nf = dff // tf

    def xmap(i, f, be, nu, nh):
        return (jnp.minimum(i, nu[0] - 1), 0)

    def fidx(i, f, nu):
        return jnp.where(i < nu[0], f, nf - 1)

    grid_spec = pltpu.PrefetchScalarGridSpec(
        num_scalar_prefetch=3,
        grid=(nb, nf),
        in_specs=[pl.BlockSpec((tm * SUBLANES, LANES), xmap),
                  pl.BlockSpec((1, d, tf), lambda i, f, be, nu, nh: (be[i], 0, fidx(i, f, nu))),
                  pl.BlockSpec((1, d, tf), lambda i, f, be, nu, nh: (be[i], 0, fidx(i, f, nu))),
                  pl.BlockSpec((1, tf, d), lambda i, f, be, nu, nh: (be[i], fidx(i, f, nu), 0))],
        out_specs=pl.BlockSpec((tm * SUBLANES, LANES), lambda i, f, be, nu, nh: (i, 0)),
        scratch_shapes=[pltpu.VMEM((tm, d), F32), pltpu.VMEM((tm, d), BF16)])
    return pl.pallas_call(
        functools.partial(_expert_kernel, tm=tm),
        grid_spec=grid_spec,
        out_shape=jax.ShapeDtypeStruct((nb * tm * SUBLANES, LANES), F32),
        compiler_params=pltpu.CompilerParams(dimension_semantics=("arbitrary", "arbitrary"),
                                             vmem_limit_bytes=VMEM_LIMIT_BYTES),
        name="moe_experts",
    )(block_e, n_used, n_halves, x_tiles, wg, wu, wd)


def _combine_kernel(cur_ref, nxt_ref, y_hbm, w_ref, x_ref, gate_ref, g_ref, b_ref, o_ref, ybuf_ref, sems,
                    *, tc, alpha):
    i = pl.program_id(0)
    nt = TOP_K * tc

    def fetch(idx_ref, slot):
        buf, sem = ybuf_ref.at[slot], sems.at[slot]
        _issue_tile_copies(nt, lambda g, u: (_tile_at(y_hbm, idx_ref[0, 0, g * _COPY_UNROLL + u]),
                                             _tile_at(buf, g * _COPY_UNROLL + u), sem))

    @pl.when(i == 0)
    def _():
        fetch(cur_ref, 0)

    @pl.when(i + 1 < pl.num_programs(0))
    def _():
        fetch(nxt_ref, (i + 1) % 2)

    slot = i % 2
    yb = ybuf_ref.at[slot]
    pltpu.make_async_copy(y_hbm.at[pl.ds(0, nt * SUBLANES)], yb, sems.at[slot]).wait()
    w = w_ref[...]
    y = (w[:, 0:1] * _from_token_tiles(yb, tc, 0, TOP_K)
         + w[:, 1:2] * _from_token_tiles(yb, tc, 1, TOP_K))
    r = alpha * x_ref[0] + (1.0 + gate_ref[0]) * y
    o_ref[0] = _layer_norm_rows(r, g_ref[...], b_ref[...])


def _combine(y_tiles, dest, top_w, x, gate2, ln_g, ln_b, alpha, tc):
    bsz, seq, d = x.shape
    nj = seq // tc
    steps = bsz * nj
    nt = TOP_K * tc
    dest_blocks = dest.astype(jnp.int32).reshape(steps, 1, nt)
    return pl.pallas_call(
        functools.partial(_combine_kernel, tc=tc, alpha=alpha),
        grid=(steps,),
        in_specs=[pl.BlockSpec((1, 1, nt), lambda i: (i, 0, 0), memory_space=pltpu.SMEM),
                  pl.BlockSpec((1, 1, nt), lambda i: (jnp.minimum(i + 1, steps - 1), 0, 0),
                               memory_space=pltpu.SMEM),
                  pl.BlockSpec(memory_space=pl.ANY),
                  pl.BlockSpec((tc, LANES), lambda i: (i, 0)),
                  pl.BlockSpec((1, tc, d), lambda i: (i // nj, i % nj, 0)),
                  pl.BlockSpec((1, 1, d), lambda i: (i // nj, 0, 0)), _const_spec((1, d)), _const_spec((1, d))],
        out_specs=pl.BlockSpec((1, tc, d), lambda i: (i // nj, i % nj, 0)),
        out_shape=jax.ShapeDtypeStruct((bsz, seq, d), F32),
        scratch_shapes=[pltpu.VMEM((2, nt * SUBLANES, LANES), F32), pltpu.SemaphoreType.DMA((2,))],
        compiler_params=pltpu.CompilerParams(dimension_semantics=("arbitrary",),
                                             vmem_limit_bytes=VMEM_LIMIT_BYTES),
        name="moe_combine",
    )(dest_blocks, dest_blocks, y_tiles, top_w, x, gate2, ln_g, ln_b)


def _moe(oa, oc, lw, x, gate1, scale2, shift2, gate2, wr2, stri, wg, wu, wd, ln_g, ln_b, alpha,
         tm_tok, tm, tf, tc):
    bsz, seq, d = x.shape
    n_tok = bsz * seq
    n_asg = n_tok * TOP_K
    x1 = _outproj(oa, oc, lw, x, gate1, alpha, tm_tok)
    h_tiles, idx128, w128, cnt = _router(x1, scale2, shift2, wr2, stri, tm_tok)
    counts = cnt[0, :N_EXPERTS].astype(jnp.int32)
    padded = (counts + tm - 1) // tm * tm
    pad_end = jnp.cumsum(padded)
    pad_start = pad_end - padded
    e_ids, ranks = idx128[:, 0:TOP_K], idx128[:, TOP_K:2 * TOP_K]
    start_of = jnp.sum(jnp.where(e_ids[:, :, None] == jnp.arange(N_EXPERTS, dtype=jnp.int32), pad_start, 0), axis=-1)
    dest = (start_of + ranks).reshape(n_asg)
    nb = n_asg // tm + N_EXPERTS
    n_rows = nb * tm
    n_fill = n_rows - n_asg
    fill_end = jnp.cumsum(padded - counts)
    slot = jnp.arange(n_fill, dtype=jnp.int32)
    slot_e = jnp.sum((slot[:, None] >= fill_end[None, :]).astype(jnp.int32), axis=1)
    in_group = slot_e < N_EXPERTS
    ge = jnp.minimum(slot_e, N_EXPERTS - 1)
    fill_dst = jnp.where(in_group, (pad_start + counts)[ge] + slot - (fill_end - (padded - counts))[ge],
                         pad_end[-1] + slot - fill_end[-1])
    x_tiles = _group_rows(h_tiles, dest, fill_dst, n_rows, _tile(n_tok, 2048))
    block_row = jnp.arange(nb, dtype=jnp.int32) * tm
    block_e = jnp.minimum(jnp.sum((block_row[:, None] >= pad_end[None, :]).astype(jnp.int32), axis=1),
                          N_EXPERTS - 1)
    n_used = (pad_end[-1:] // tm).astype(jnp.int32)
    rows_in_block = (pad_start + counts)[block_e] - block_row
    n_halves = jnp.where(rows_in_block > tm // 2, 2, 1).astype(jnp.int32)
    y_tiles = _experts(x_tiles, nb, block_e, n_used, n_halves, wg, wu, wd, tm, tf)
    return _combine(y_tiles, dest, w128, x1, gate2, ln_g, ln_b, alpha, tc)


def _layer_weights(layer, w_in, b_fgate, w_conv, q_norm_g, kv_norm_g, w_uq, w_ukv, head_norm_g, w_o,
                   ln1_g, ln1_b):
    d = w_in.shape[1]
    sizes = [FOX_W, FOX_W, FOX_W, H_FOX, CONV_W, CONV_W, CONV_W, Q_LORA, KV_LORA, QK_ROPE]
    offs = np.concatenate([[0], np.cumsum(sizes)])
    wi = w_in[layer]
    fq, fk, fv, fl, bg, cg, hin, cq, ckv, kr = [wi[:, offs[i]:offs[i + 1]] for i in range(len(sizes))]
    half = QK_ROPE // 2
    z64 = jnp.zeros((d, QK_NOPE), F32)
    z32 = jnp.zeros((d, LANES - QK_NOPE - QK_ROPE), F32)
    wa = jnp.concatenate([
        fq, fk, fv,
        jnp.pad(jnp.repeat(fl, _N_SPLIT, axis=1), ((0, 0), (0, LANES - _N_SPLIT * H_FOX))), bg, cg, hin, cq, ckv,
        jnp.concatenate([z64, kr, z32], axis=1),
        jnp.concatenate([z64, kr[:, half:], kr[:, :half], z32], axis=1)], axis=1)
    assert wa.shape[1] == _NA
    q3 = w_uq[layer].reshape(Q_LORA, H_MLA, QK_NOPE + QK_ROPE)
    q_main = jnp.pad(q3, ((0, 0), (0, 0), (0, LANES - QK_NOPE - QK_ROPE)))
    q_swap = jnp.concatenate([jnp.zeros((Q_LORA, H_MLA, QK_NOPE), F32), q3[:, :, QK_NOPE + half:],
                              q3[:, :, QK_NOPE:QK_NOPE + half],
                              jnp.zeros((Q_LORA, H_MLA, LANES - QK_NOPE - QK_ROPE), F32)], axis=2)
    wuq = jnp.concatenate([q_main.reshape(Q_LORA, H_MLA * LANES), q_swap.reshape(Q_LORA, H_MLA * LANES)],
                          axis=1).astype(BF16)
    kv3 = w_ukv[layer].reshape(KV_LORA, H_MLA, QK_NOPE + V_DIM)
    k_nope = jnp.pad(kv3[:, :, :QK_NOPE], ((0, 0), (0, 0), (0, LANES - QK_NOPE))).reshape(KV_LORA, H_MLA * LANES)
    wukv = jnp.concatenate([k_nope, kv3[:, :, QK_NOPE:].reshape(KV_LORA, MLA_W)], axis=1).astype(BF16)
    hg = head_norm_g[layer]
    wo = w_o[layer]
    c0, c1 = FOX_W, FOX_W + CONV_W
    return {
        "wa": wa, "wuq": wuq, "wukv": wukv,
        "qg": q_norm_g[layer].reshape(1, Q_LORA), "kvg": kv_norm_g[layer].reshape(1, KV_LORA),
        "bf": jnp.pad(jnp.repeat(b_fgate[layer], _N_SPLIT), (0, LANES - _N_SPLIT * H_FOX)).reshape(1, LANES),
        "wconv": jnp.pad(w_conv[layer], ((0, SUBLANES - CONV_K), (0, 0))),
        "hg_conv": hg[c0:c1].reshape(1, CONV_W),
        "hg_attn": jnp.concatenate([hg[:c0], hg[c1:]]).reshape(1, FOX_W + MLA_W),
        "wo_a": jnp.concatenate([wo[:c0], wo[c1:]], axis=0).astype(BF16),
        "wo_c": wo[c0:c1].astype(BF16),
        "ln1_g": ln1_g[layer].reshape(1, d), "ln1_b": ln1_b[layer].reshape(1, d),
    }


def _constants(seq, ts):
    pos = jnp.arange(seq, dtype=F32)
    inv_freq = ROPE_THETA ** (-jnp.arange(0, QK_ROPE, 2, dtype=F32) / QK_ROPE)
    ang = pos[:, None] * inv_freq[None, :]
    cos, sin = jnp.cos(ang), jnp.sin(ang)
    pad_r = jnp.zeros((seq, LANES - QK_NOPE - QK_ROPE), F32)
    ctab = jnp.concatenate([jnp.ones((seq, QK_NOPE), F32), cos, cos, pad_r], axis=1)
    stab = jnp.concatenate([jnp.zeros((seq, QK_NOPE), F32), -sin, sin, pad_r], axis=1)
    q_scale = (QK_NOPE + QK_ROPE) ** -0.5 * LOG2E
    lanes = np.arange(LANES)
    lm = np.zeros((SUBLANES, LANES), np.float32)
    for part in range(_N_SPLIT):
        lm[part] = (lanes % _N_SPLIT == part) & (lanes < _N_SPLIT * H_FOX)
    for row, base in ((3, HEAD_DIM), (5, 0)):
        lm[row] = (lanes >= base) & (lanes < base + _N_SPLIT)
        lm[row + 1] = (lanes >= base + _N_SPLIT) & (lanes < base + 2 * _N_SPLIT)

    def group_mean(n):
        gidx = np.arange(n) // HEAD_DIM
        return (gidx[:, None] == gidx[None, :]).astype(np.float32) / HEAD_DIM

    return {
        "cq": ctab * q_scale, "sq": stab * q_scale, "ck": ctab, "sk": stab,
        "tri": jnp.asarray(np.tril(np.ones((ts, ts), np.float32)), BF16),
        "stri": jnp.asarray(np.tril(np.ones((ts, ts), np.float32), k=-1), BF16),
        "lane_masks": jnp.asarray(lm),
        "gm256": jnp.asarray(group_mean(CONV_W), BF16), "gm128": jnp.asarray(group_mean(LANES), BF16),
    }


def _tile(n, pref):
    t = min(n, pref)
    assert n % t == 0, (n, pref)
    return t


def kernel(x, c, w_mod, b_mod, w_in, b_fgate, w_conv, q_norm_g, kv_norm_g, w_uq, w_ukv, head_norm_g, w_o, ln1_g, ln1_b, ln2_g, ln2_b, ffn_w_gate, ffn_w_up, ffn_w_down, router_w, exp_w_gate, exp_w_up, exp_w_down):
    bsz, seq, d = x.shape
    depth = w_mod.shape[0]
    assert d == D_MODEL and bsz <= SUBLANES
    alpha = (2 * depth) ** 0.25
    ts = _tile(seq, 512)
    tq = _tile(seq, 512)
    tm_e = 1024
    tf_e = 1792
    tc = _tile(seq, 256)
    assert (bsz * seq * TOP_K) % tm_e == 0 and exp_w_gate.shape[-1] % tf_e == 0

    consts = _constants(seq, ts)
    c_pad = jnp.pad(c, ((0, SUBLANES - bsz), (0, 0)))
    mod = _modulation(c_pad, w_mod, b_mod)[:, :bsz, :]

    for layer in range(depth):
        m6 = mod[layer].reshape(bsz, 6, 1, d)
        shift1, scale1, gate1, shift2, scale2, gate2 = [m6[:, i] for i in range(6)]
        lw = _layer_weights(layer, w_in, b_fgate, w_conv, q_norm_g, kv_norm_g, w_uq, w_ukv, head_norm_g,
                            w_o, ln1_g, ln1_b)
        qp, kp, vt, oc = _inproj(x, scale1, shift1, lw, consts, ts)
        oa = _attention(qp, kp, vt, lw["hg_attn"], consts["gm128"], tq, 12)
        j = layer // 2
        g2, b2 = ln2_g[layer].reshape(1, d), ln2_b[layer].reshape(1, d)
        if layer % 2 == 0:
            x = _ffn_dense(oa, oc, lw, x, gate1, scale2, shift2, gate2, ffn_w_gate[j].astype(BF16),
                           ffn_w_up[j].astype(BF16), ffn_w_down[j].astype(BF16), g2, b2, alpha, ts)
        else:
            wr = jnp.pad(router_w[j], ((0, 0), (0, LANES - N_EXPERTS)))
            wr_hi = wr.astype(BF16)
            wr_lo = (wr - wr_hi.astype(F32)).astype(BF16)
            x = _moe(oa, oc, lw, x, gate1, scale2, shift2, gate2, jnp.stack([wr_hi, wr_lo]), consts["stri"],
                     exp_w_gate[j].astype(BF16), exp_w_up[j].astype(BF16), exp_w_down[j].astype(BF16),
                     g2, b2, alpha, ts, tm_e, tf_e, tc)
    return x
```

```python
import functools

import numpy as np
import jax
import jax.numpy as jnp
from jax import lax
from jax.experimental import pallas as pl
from jax.experimental.pallas import tpu as pltpu

F32 = jnp.float32
BF16 = jnp.bfloat16

D_MODEL = 1024
HEAD_DIM = 64
H_FOX = 6
FOX_W = H_FOX * HEAD_DIM
CONV_W = 256
CONV_K = 3
H_MLA = 6
Q_LORA = 256
KV_LORA = 256
QK_NOPE = 64
QK_ROPE = 32
V_DIM = 64
MLA_W = H_MLA * V_DIM
N_HEADS = H_FOX + H_MLA
ROPE_THETA = 10000.0
N_EXPERTS = 8
TOP_K = 2
LN_EPS = 1e-5
RMS_EPS = 1e-6

LANES = 128
SUBLANES = 8
VMEM_LIMIT_BYTES = 56 * 1024 * 1024

_HW = 6 * LANES
_QF0, _KF0, _VF0, _FL0 = 0, FOX_W, 2 * FOX_W, 3 * FOX_W
_BG0 = _FL0 + LANES
_CG0, _HI0, _CQ0, _CKV0 = _BG0 + CONV_W, _BG0 + 2 * CONV_W, _BG0 + 3 * CONV_W, _BG0 + 3 * CONV_W + Q_LORA
_KR0 = _CKV0 + KV_LORA
_KRS0 = _KR0 + LANES
_NA = _KRS0 + LANES
LOG2E = 1.4426950408889634
VT_ROWS = LANES
_N_SPLIT = 3


def _const_spec(shape):
    zeros = (0,) * len(shape)
    return pl.BlockSpec(shape, lambda *_: zeros, pipeline_mode=pl.Buffered(1))


def _silu(v):
    return v * (1.0 / (1.0 + jnp.exp(-v)))


def _split3(v):
    hi = v.astype(BF16)
    r1 = v - hi.astype(F32)
    mid = r1.astype(BF16)
    lo = (r1 - mid.astype(F32)).astype(BF16)
    return hi, mid, lo


def _group_mean_sq(v, gmat_ref):
    sq = v * v
    hi = sq.astype(BF16)
    lo = (sq - hi.astype(F32)).astype(BF16)
    g = gmat_ref[...]
    return (jnp.dot(hi, g, preferred_element_type=F32) + jnp.dot(lo, g, preferred_element_type=F32))


def _layer_norm_rows(r, g, b):
    mu = jnp.mean(r, axis=-1, keepdims=True)
    rc = r - mu
    var = jnp.mean(rc * rc, axis=-1, keepdims=True)
    return rc * lax.rsqrt(var + LN_EPS) * g + b


def _mod_kernel(c_ref, w_ref, b_ref, o_ref):
    act = _silu(c_ref[...]).astype(BF16)
    o_ref[0] = jnp.dot(act, w_ref[0].astype(BF16), preferred_element_type=F32) + b_ref[0]


def _modulation(c_pad, w_mod, b_mod):
    depth, d, n = w_mod.shape
    tn = 1024
    return pl.pallas_call(
        _mod_kernel,
        grid=(depth, n // tn),
        in_specs=[pl.BlockSpec((SUBLANES, d), lambda l, j: (0, 0)),
                  pl.BlockSpec((1, d, tn), lambda l, j: (l, 0, j)),
                  pl.BlockSpec((1, 1, tn), lambda l, j: (l, 0, j))],
        out_specs=pl.BlockSpec((1, SUBLANES, tn), lambda l, j: (l, 0, j)),
        out_shape=jax.ShapeDtypeStruct((depth, SUBLANES, n), F32),
        compiler_params=pltpu.CompilerParams(dimension_semantics=("arbitrary", "arbitrary")),
        name="modulation",
    )(c_pad, w_mod, b_mod.reshape(depth, 1, n))


def _inproj_kernel(x_ref, sc_ref, sh_ref, wa_ref, cq_ref, sq_ref, ck_ref, sk_ref, wuq_ref, wukv_ref,
                   qg_ref, kvg_ref, bf_ref, wconv_ref, hgc_ref, tri_ref, lm_ref, gm_ref,
                   qp_ref, kp_ref, vt_ref, oc_ref, fcarry_ref, ubuf_ref, wab_ref, *, ts):
    @pl.when(jnp.logical_and(pl.program_id(0) == 0, pl.program_id(1) == 0))
    def _():
        for c in range(_NA // LANES):
            sl = slice(c * LANES, (c + 1) * LANES)
            wab_ref[:, sl] = wa_ref[:, sl].astype(BF16)

    @pl.when(pl.program_id(1) == 0)
    def _():
        fcarry_ref[...] = jnp.zeros_like(fcarry_ref)
        ubuf_ref[pl.ds(0, SUBLANES), :] = jnp.zeros((SUBLANES, CONV_W), F32)

    hb = (x_ref[0] * (1.0 + sc_ref[0]) + sh_ref[0]).astype(BF16)

    def proj(lo, hi):
        return jnp.dot(hb, wab_ref[:, lo:hi], preferred_element_type=F32)

    def put_values_t(head, v_half_t):
        vt_ref[0, head, 0:V_DIM, :] = v_half_t.astype(BF16)
        vt_ref[0, head, V_DIM:VT_ROWS, :] = jnp.ones((VT_ROWS - V_DIM, ts), BF16)

    z_fl = proj(_FL0, _FL0 + LANES)
    z_cq = proj(_CQ0, _CQ0 + Q_LORA)
    z_ckv = proj(_CKV0, _CKV0 + KV_LORA)
    z_cg = proj(_CG0, _CG0 + CONV_W)
    z_hi = proj(_HI0, _HI0 + CONV_W)
    z_bg = proj(_BG0, _BG0 + CONV_W)
    zq = proj(_QF0, _QF0 + FOX_W) * (HEAD_DIM ** -0.5 * LOG2E)
    zk = proj(_KF0, _KF0 + FOX_W)
    zv = proj(_VF0, _VF0 + FOX_W)
    z_kr = proj(_KR0, _KR0 + LANES)
    z_krs = proj(_KRS0, _KRS0 + LANES)

    a = z_fl + bf_ref[...]
    logf = jnp.minimum(a, 0.0) - jnp.log1p(jnp.exp(-jnp.abs(a)))
    tri = tri_ref[...]
    csum = sum(jnp.dot(tri, part, preferred_element_type=F32) for part in _split3(logf))
    fcum = fcarry_ref[...] + csum
    fcarry_ref[...] = fcum[ts - 1:ts, :]
    f_hi, f_mid, f_lo = _split3(fcum * LOG2E)
    lm = lm_ref[...]
    fparts = (f_hi.astype(F32) * lm[0:1, :] + f_mid.astype(F32) * lm[1:2, :]
              + f_lo.astype(F32) * lm[2:3, :])
    lane = lax.broadcasted_iota(jnp.int32, (ts, LANES), 1)
    low_half = lane < HEAD_DIM
    for h in range(H_FOX):
        blk = slice((h // 2) * LANES, (h // 2 + 1) * LANES)
        if h % 2 == 0:
            base, f_mask, one_mask, keep = HEAD_DIM, lm[3:4, :], lm[4:5, :], low_half
        else:
            base, f_mask, one_mask, keep = 0, lm[5:6, :], lm[6:7, :], jnp.logical_not(low_half)
        f_at_q = pltpu.roll(fparts, (base - _N_SPLIT * h) % LANES, axis=1)
        f_at_k = pltpu.roll(fparts, (base + _N_SPLIT - _N_SPLIT * h) % LANES, axis=1)
        qp_ref[0, h] = jnp.where(keep, zq[:, blk], f_at_q * f_mask + one_mask).astype(BF16)
        kp_ref[0, h] = jnp.where(keep, zk[:, blk], f_mask - f_at_k * one_mask).astype(BF16)
    for j in range(H_FOX // 2):
        vt = zv[:, j * LANES:(j + 1) * LANES].T
        put_values_t(2 * j, vt[0:V_DIM])
        put_values_t(2 * j + 1, vt[V_DIM:LANES])

    u = z_cg * z_hi
    ubuf_ref[pl.ds(SUBLANES, ts), :] = u
    u1 = ubuf_ref[pl.ds(SUBLANES - 1, ts), :]
    u2 = ubuf_ref[pl.ds(SUBLANES - 2, ts), :]
    ubuf_ref[pl.ds(0, SUBLANES), :] = u[ts - SUBLANES:ts, :]
    wc = wconv_ref[...]
    oc = z_bg * (wc[0:1, :] * u2 + wc[1:2, :] * u1 + wc[2:3, :] * u)
    ocn = oc * lax.rsqrt(_group_mean_sq(oc, gm_ref) + RMS_EPS) * hgc_ref[...]
    oc_ref[0] = ocn.astype(BF16)

    def rms(v, g):
        return (v * lax.rsqrt(jnp.mean(v * v, axis=-1, keepdims=True) + RMS_EPS) * g).astype(BF16)

    cqn = rms(z_cq, qg_ref[...])
    qm = jnp.dot(cqn, wuq_ref[:, 0:_HW], preferred_element_type=F32)
    qs = jnp.dot(cqn, wuq_ref[:, _HW:2 * _HW], preferred_element_type=F32)
    kvn = rms(z_ckv, kvg_ref[...])
    kn = jnp.dot(kvn, wukv_ref[:, 0:_HW], preferred_element_type=F32)
    vm = jnp.dot(kvn, wukv_ref[:, _HW:_HW + MLA_W], preferred_element_type=F32)
    krr = z_kr * ck_ref[...] + z_krs * sk_ref[...]
    cq, sq = cq_ref[...], sq_ref[...]
    for h in range(H_MLA):
        sl = slice(h * LANES, (h + 1) * LANES)
        qp_ref[0, H_FOX + h] = (qm[:, sl] * cq + qs[:, sl] * sq).astype(BF16)
        kp_ref[0, H_FOX + h] = (kn[:, sl] + krr).astype(BF16)
    for j in range(H_MLA // 2):
        vt = vm[:, j * LANES:(j + 1) * LANES].T
        put_values_t(H_FOX + 2 * j, vt[0:V_DIM])
        put_values_t(H_FOX + 2 * j + 1, vt[V_DIM:LANES])


def _inproj(x, scale1, shift1, lw, consts, ts):
    bsz, seq, d = x.shape
    kern = functools.partial(_inproj_kernel, ts=ts)
    row = lambda b, j: (b, 0, 0)
    tab = pl.BlockSpec((ts, LANES), lambda b, j: (j, 0))
    return pl.pallas_call(
        kern,
        grid=(bsz, seq // ts),
        in_specs=[pl.BlockSpec((1, ts, d), lambda b, j: (b, j, 0)),
                  pl.BlockSpec((1, 1, d), row), pl.BlockSpec((1, 1, d), row),
                  _const_spec((d, _NA)), tab, tab, tab, tab,
                  _const_spec((Q_LORA, 2 * _HW)), _const_spec((KV_LORA, _HW + MLA_W)),
                  _const_spec((1, Q_LORA)), _const_spec((1, KV_LORA)), _const_spec((1, LANES)),
                  _const_spec((SUBLANES, CONV_W)), _const_spec((1, CONV_W)),
                  _const_spec((ts, ts)), _const_spec((SUBLANES, LANES)), _const_spec((CONV_W, CONV_W))],
        out_specs=[pl.BlockSpec((1, N_HEADS, ts, LANES), lambda b, j: (b, 0, j, 0)),
                   pl.BlockSpec((1, N_HEADS, ts, LANES), lambda b, j: (b, 0, j, 0)),
                   pl.BlockSpec((1, N_HEADS, VT_ROWS, ts), lambda b, j: (b, 0, 0, j)),
                   pl.BlockSpec((1, ts, CONV_W), lambda b, j: (b, j, 0))],
        out_shape=[jax.ShapeDtypeStruct((bsz, N_HEADS, seq, LANES), BF16),
                   jax.ShapeDtypeStruct((bsz, N_HEADS, seq, LANES), BF16),
                   jax.ShapeDtypeStruct((bsz, N_HEADS, VT_ROWS, seq), BF16),
                   jax.ShapeDtypeStruct((bsz, seq, CONV_W), BF16)],
        scratch_shapes=[pltpu.VMEM((1, LANES), F32), pltpu.VMEM((ts + SUBLANES, CONV_W), F32),
                        pltpu.VMEM((d, _NA), BF16)],
        compiler_params=pltpu.CompilerParams(dimension_semantics=("arbitrary", "arbitrary"),
                                             vmem_limit_bytes=VMEM_LIMIT_BYTES),
        name="inproj",
    )(x, scale1, shift1, lw["wa"], consts["cq"], consts["sq"], consts["ck"], consts["sk"],
      lw["wuq"], lw["wukv"], lw["qg"], lw["kvg"], lw["bf"], lw["wconv"], lw["hg_conv"],
      consts["tri"], consts["lane_masks"], consts["gm256"])


def _attn_kernel(qi_ref, kj_ref, q_ref, k_ref, vt_ref, g_ref, gm_ref, o_ref, m_ref, acc_ref, s_ref, *, nh):
    step = pl.program_id(2)
    qi = qi_ref[step]
    kj = kj_ref[step]

    @pl.when(kj == 0)
    def _():
        m_ref[...] = jnp.full(m_ref.shape, -jnp.inf, F32)
        acc_ref[...] = jnp.zeros_like(acc_ref)

    def update(masked):
        if masked:
            tile = (q_ref.shape[2], q_ref.shape[2])
            causal = lax.broadcasted_iota(jnp.int32, tile, 0) <= lax.broadcasted_iota(jnp.int32, tile, 1)

        def scores(h):
            st = lax.dot_general(k_ref[0, h], q_ref[0, h], (((1,), (1,)), ((), ())),
                                 preferred_element_type=F32)
            if masked:
                st = jnp.where(causal, st, -jnp.inf)
            s_ref[h] = st

        for h in range(nh):
            scores(h)
        pts, alphas = [], []
        for h in range(nh):
            m_prev = m_ref[h]
            m_new = jnp.maximum(m_prev, jnp.max(s_ref[h], axis=0, keepdims=True))
            pts.append(jnp.exp2(s_ref[h] - m_new).astype(BF16))
            alphas.append(jnp.exp2(m_prev - m_new))
            m_ref[h] = m_new
        for h in range(nh):
            acc_ref[h] = alphas[h] * acc_ref[h] + jnp.dot(vt_ref[0, h], pts[h], preferred_element_type=F32)

    @pl.when(kj < qi)
    def _():
        update(False)

    @pl.when(kj == qi)
    def _():
        update(True)
        for pair in range(nh // 2):
            ot = jnp.concatenate([acc_ref[h, 0:V_DIM, :] / acc_ref[h, V_DIM:2 * V_DIM, :]
                                  for h in (2 * pair, 2 * pair + 1)], axis=0)
            o = ot.T
            sl = slice(pair * LANES, (pair + 1) * LANES)
            on = o * lax.rsqrt(_group_mean_sq(o, gm_ref) + RMS_EPS) * g_ref[:, sl]
            o_ref[0, :, sl] = on.astype(BF16)


def _attention(qp, kp, vt, g_attn, gm128, tq, nh):
    bsz, n_heads, seq, _ = qp.shape
    nq = seq // tq
    ow = (nh // 2) * LANES
    qi_tab = np.concatenate([np.full(i + 1, i) for i in range(nq)]).astype(np.int32)
    kj_tab = np.concatenate([np.arange(i + 1) for i in range(nq)]).astype(np.int32)
    grid_spec = pltpu.PrefetchScalarGridSpec(
        num_scalar_prefetch=2,
        grid=(bsz, n_heads // nh, len(qi_tab)),
        in_specs=[pl.BlockSpec((1, nh, tq, LANES), lambda b, p, s, qi, kj: (b, p, qi[s], 0)),
                  pl.BlockSpec((1, nh, tq, LANES), lambda b, p, s, qi, kj: (b, p, kj[s], 0)),
                  pl.BlockSpec((1, nh, VT_ROWS, tq), lambda b, p, s, qi, kj: (b, p, 0, kj[s])),
                  pl.BlockSpec((1, ow), lambda b, p, s, qi, kj: (0, p)),
                  _const_spec((LANES, LANES))],
        out_specs=pl.BlockSpec((1, tq, ow), lambda b, p, s, qi, kj: (b, qi[s], p)),
        scratch_shapes=[pltpu.VMEM((nh, 1, tq), F32), pltpu.VMEM((nh, VT_ROWS, tq), F32),
                        pltpu.VMEM((nh, tq, tq), F32)])
    return pl.pallas_call(
        functools.partial(_attn_kernel, nh=nh),
        grid_spec=grid_spec,
        out_shape=jax.ShapeDtypeStruct((bsz, seq, (n_heads // 2) * LANES), BF16),
        compiler_params=pltpu.CompilerParams(
            dimension_semantics=("arbitrary", "arbitrary", "arbitrary"),
            vmem_limit_bytes=VMEM_LIMIT_BYTES),
        name="attention",
    )(jnp.asarray(qi_tab), jnp.asarray(kj_tab), qp, kp, vt, g_attn, gm128)


def _mixer_out_rows(oa_ref, oc_ref, woa_ref, woc_ref, x_ref, gate_ref, g_ref, b_ref, alpha):
    y = (jnp.dot(oa_ref[0], woa_ref[...], preferred_element_type=F32)
         + jnp.dot(oc_ref[0], woc_ref[...], preferred_element_type=F32))
    r = alpha * x_ref[0] + (1.0 + gate_ref[0]) * y
    return _layer_norm_rows(r, g_ref[...], b_ref[...])


def _mixer_out_specs(oa, d, tm, act, row):
    wa_rows = oa.shape[-1]
    return [pl.BlockSpec((1, tm, wa_rows), act), pl.BlockSpec((1, tm, CONV_W), act),
            _const_spec((wa_rows, d)), _const_spec((CONV_W, d)),
            pl.BlockSpec((1, tm, d), act), pl.BlockSpec((1, 1, d), row), _const_spec((1, d)), _const_spec((1, d))]


def _outproj_kernel(oa_ref, oc_ref, woa_ref, woc_ref, x_ref, gate_ref, g_ref, b_ref, o_ref, *, alpha):
    o_ref[0] = _mixer_out_rows(oa_ref, oc_ref, woa_ref, woc_ref, x_ref, gate_ref, g_ref, b_ref, alpha)


def _outproj(oa, oc, lw, x, gate1, alpha, tm):
    bsz, seq, d = x.shape
    act = lambda b, j: (b, j, 0)
    return pl.pallas_call(
        functools.partial(_outproj_kernel, alpha=alpha),
        grid=(bsz, seq // tm),
        in_specs=_mixer_out_specs(oa, d, tm, act, lambda b, j: (b, 0, 0)),
        out_specs=pl.BlockSpec((1, tm, d), act),
        out_shape=jax.ShapeDtypeStruct((bsz, seq, d), F32),
        compiler_params=pltpu.CompilerParams(dimension_semantics=("arbitrary", "arbitrary"),
                                             vmem_limit_bytes=VMEM_LIMIT_BYTES),
        name="outproj",
    )(oa, oc, lw["wo_a"], lw["wo_c"], x, gate1, lw["ln1_g"], lw["ln1_b"])


def _ffn_kernel(oa_ref, oc_ref, woa_ref, woc_ref, xin_ref, gate1_ref, g1_ref, b1_ref,
                sc_ref, sh_ref, gate_ref, wg_ref, wu_ref, wd_ref, g_ref, b_ref, o_ref, *, alpha, tf):
    x = _mixer_out_rows(oa_ref, oc_ref, woa_ref, woc_ref, xin_ref, gate1_ref, g1_ref, b1_ref, alpha)
    hb = (x * (1.0 + sc_ref[0]) + sh_ref[0]).astype(BF16)
    chunks = [slice(c * tf, (c + 1) * tf) for c in range(wg_ref.shape[1] // tf)]
    acts = []
    for sl in chunks:
        gt = jnp.dot(hb, wg_ref[:, sl], preferred_element_type=F32)
        up = jnp.dot(hb, wu_ref[:, sl], preferred_element_type=F32)
        acts.append((_silu(gt) * up).astype(BF16))
    acc = sum(jnp.dot(act, wd_ref[sl, :], preferred_element_type=F32) for act, sl in zip(acts, chunks))
    r = alpha * x + (1.0 + gate_ref[0]) * acc
    o_ref[0] = _layer_norm_rows(r, g_ref[...], b_ref[...])


def _ffn_dense(oa, oc, lw, x, gate1, scale2, shift2, gate2, wg, wu, wd, ln_g, ln_b, alpha, tm):
    bsz, seq, d = x.shape
    dff = wg.shape[1]
    tf = next(t for t in (704, 512, 256, 128, dff) if dff % t == 0 and t % LANES == 0)
    nj = seq // tm
    act = lambda b, j: (b, j, 0)
    row = lambda b, j: (b, 0, 0)
    return pl.pallas_call(
        functools.partial(_ffn_kernel, alpha=alpha, tf=tf),
        grid=(bsz, nj),
        in_specs=_mixer_out_specs(oa, d, tm, act, row) + [
            pl.BlockSpec((1, 1, d), row), pl.BlockSpec((1, 1, d), row), pl.BlockSpec((1, 1, d), row),
            _const_spec((d, dff)), _const_spec((d, dff)), _const_spec((dff, d)),
            _const_spec((1, d)), _const_spec((1, d))],
        out_specs=pl.BlockSpec((1, tm, d), act),
        out_shape=jax.ShapeDtypeStruct((bsz, seq, d), F32),
        compiler_params=pltpu.CompilerParams(dimension_semantics=("arbitrary", "arbitrary"),
                                             vmem_limit_bytes=VMEM_LIMIT_BYTES),
        name="ffn_dense",
    )(oa, oc, lw["wo_a"], lw["wo_c"], x, gate1, lw["ln1_g"], lw["ln1_b"],
      scale2, shift2, gate2, wg, wu, wd, ln_g, ln_b)


def _to_token_tiles(ref, v, n):
    for c in range(v.shape[1] // LANES):
        ref[pl.ds(c, n, stride=SUBLANES), :] = v[:, c * LANES:(c + 1) * LANES]


def _from_token_tiles(ref, n, first=0, tiles_per_row=1):
    stride = tiles_per_row * SUBLANES
    return jnp.concatenate([ref[pl.ds(first * SUBLANES + c, n, stride=stride), :] for c in range(SUBLANES)],
                           axis=1)


def _router_kernel(x_ref, sc_ref, sh_ref, wr_ref, stri_ref, h_ref, idx_ref, w_ref, cnt_ref, carry_ref, *, tm):
    @pl.when(pl.program_id(0) == 0)
    def _():
        carry_ref[...] = jnp.zeros_like(carry_ref)

    h = x_ref[0] * (1.0 + sc_ref[0]) + sh_ref[0]
    _to_token_tiles(h_ref, h, tm)
    h_hi = h.astype(BF16)
    h_lo = (h - h_hi.astype(F32)).astype(BF16)
    w_hi, w_lo = wr_ref[0], wr_ref[1]
    logits = (jnp.dot(h_hi, w_hi, preferred_element_type=F32)
              + jnp.dot(h_hi, w_lo, preferred_element_type=F32)
              + jnp.dot(h_lo, w_hi, preferred_element_type=F32))
    lane = lax.broadcasted_iota(jnp.int32, logits.shape, 1).astype(F32)
    neg = -jnp.inf
    lg = jnp.where(lane < N_EXPERTS, logits, neg)
    m1 = jnp.max(lg, axis=-1, keepdims=True)
    i1 = jnp.min(jnp.where(lg == m1, lane, float(LANES)), axis=-1, keepdims=True)
    lg2 = jnp.where(lane == i1, neg, lg)
    m2 = jnp.max(lg2, axis=-1, keepdims=True)
    i2 = jnp.min(jnp.where(lg2 == m2, lane, float(LANES)), axis=-1, keepdims=True)
    e2 = jnp.exp(m2 - m1)
    denom = 1.0 + e2
    w_ref[...] = jnp.where(lane == 0.0, 1.0 / denom, e2 / denom)
    first, second = lane == i1, lane == i2
    chosen = jnp.logical_or(first, second).astype(F32)
    before = carry_ref[...] + jnp.dot(stri_ref[...], chosen.astype(BF16), preferred_element_type=F32)
    rank1 = jnp.sum(jnp.where(first, before, 0.0), axis=-1, keepdims=True)
    rank2 = jnp.sum(jnp.where(second, before, 0.0), axis=-1, keepdims=True)
    idx_ref[...] = jnp.where(lane == 0.0, i1, jnp.where(lane == 1.0, i2, jnp.where(lane == 2.0, rank1, rank2))
                             ).astype(jnp.int32)
    total = carry_ref[...] + jnp.sum(chosen, axis=0, keepdims=True)
    carry_ref[...] = total
    cnt_ref[...] = jnp.broadcast_to(total, cnt_ref.shape)


def _router(x, scale2, shift2, wr2, stri, tm):
    bsz, seq, d = x.shape
    assert d == SUBLANES * LANES and bsz * seq * TOP_K < 2 ** 24
    nj = seq // tm
    n_tiles = bsz * nj
    row = lambda i: (i // nj, 0, 0)
    tok = lambda i: (i, 0)
    return pl.pallas_call(
        functools.partial(_router_kernel, tm=tm),
        grid=(n_tiles,),
        in_specs=[pl.BlockSpec((1, tm, d), lambda i: (i // nj, i % nj, 0)),
                  pl.BlockSpec((1, 1, d), row), pl.BlockSpec((1, 1, d), row),
                  _const_spec((2, d, LANES)), _const_spec((tm, tm))],
        out_specs=[pl.BlockSpec((tm * SUBLANES, LANES), tok),
                   pl.BlockSpec((tm, LANES), tok), pl.BlockSpec((tm, LANES), tok),
                   pl.BlockSpec((SUBLANES, LANES), lambda i: (0, 0))],
        out_shape=[jax.ShapeDtypeStruct((n_tiles * tm * SUBLANES, LANES), F32),
                   jax.ShapeDtypeStruct((bsz * seq, LANES), jnp.int32),
                   jax.ShapeDtypeStruct((bsz * seq, LANES), F32),
                   jax.ShapeDtypeStruct((SUBLANES, LANES), F32)],
        scratch_shapes=[pltpu.VMEM((1, LANES), F32)],
        compiler_params=pltpu.CompilerParams(dimension_semantics=("arbitrary",),
                                             vmem_limit_bytes=VMEM_LIMIT_BYTES),
        name="router",
    )(x, scale2, shift2, wr2, stri)


_COPY_UNROLL = 8


def _tile_at(ref, t):
    return ref.at[pl.ds(pl.multiple_of(t * SUBLANES, SUBLANES), SUBLANES)]


def _issue_tile_copies(n, copy_of):
    def group(g, carry):
        copies = [copy_of(g, u) for u in range(_COPY_UNROLL)]
        for u, (src, dst, sem) in enumerate(copies):
            pltpu.make_async_copy(src, dst, sem).start(priority=u % 2)
        return carry

    assert n % _COPY_UNROLL == 0
    lax.fori_loop(0, n // _COPY_UNROLL, group, 0)


def _group_kernel(di_ref, fi_ref, src_ref, dst_hbm, zero_ref, sem, *, tb, nf):
    zero_ref[...] = jnp.zeros_like(zero_ref)
    per_group = _COPY_UNROLL // TOP_K
    _issue_tile_copies(TOP_K * tb, lambda g, u: (_tile_at(src_ref, g * per_group + u // TOP_K),
                                                 _tile_at(dst_hbm, di_ref[0, 0, g * _COPY_UNROLL + u]), sem))
    _issue_tile_copies(nf, lambda g, u: (zero_ref, _tile_at(dst_hbm, fi_ref[0, 0, g * _COPY_UNROLL + u]), sem))
    block = pl.ds(0, tb * SUBLANES)
    for _ in range(TOP_K):
        pltpu.make_async_copy(src_ref, dst_hbm.at[block], sem).wait()

    def wait_fill(r, carry):
        pltpu.make_async_copy(zero_ref, _tile_at(dst_hbm, 0), sem).wait()
        return carry

    lax.fori_loop(0, nf, wait_fill, 0)


def _group_rows(h_tiles, dest, fill_dst, n_rows, tb):
    n_asg = dest.shape[0]
    steps = n_asg // (TOP_K * tb)
    nf = fill_dst.shape[0] // steps
    assert steps * TOP_K * tb == n_asg and steps * nf == fill_dst.shape[0]
    return pl.pallas_call(
        functools.partial(_group_kernel, tb=tb, nf=nf),
        grid=(steps,),
        in_specs=[pl.BlockSpec((1, 1, TOP_K * tb), lambda i: (i, 0, 0), memory_space=pltpu.SMEM),
                  pl.BlockSpec((1, 1, nf), lambda i: (i, 0, 0), memory_space=pltpu.SMEM),
                  pl.BlockSpec((tb * SUBLANES, LANES), lambda i: (i, 0))],
        out_specs=pl.BlockSpec(memory_space=pl.ANY),
        out_shape=jax.ShapeDtypeStruct((n_rows * SUBLANES, LANES), h_tiles.dtype),
        scratch_shapes=[pltpu.VMEM((SUBLANES, LANES), h_tiles.dtype), pltpu.SemaphoreType.DMA(())],
        compiler_params=pltpu.CompilerParams(dimension_semantics=("arbitrary",), has_side_effects=True),
        name="moe_group",
    )(dest.astype(jnp.int32).reshape(steps, 1, TOP_K * tb), fill_dst.astype(jnp.int32).reshape(steps, 1, nf),
      h_tiles)


def _expert_kernel(be_ref, nu_ref, nh_ref, x_ref, wg_ref, wu_ref, wd_ref, o_ref, acc_ref, xb_ref, *, tm, nf):
    i = pl.program_id(0)
    f = pl.program_id(1)
    used = i < nu_ref[0]
    whole = jnp.logical_and(used, nh_ref[i] == 2)
    first_half_only = jnp.logical_and(used, nh_ref[i] == 1)

    last = f == nf - 1

    def swiglu_step(rows, is_first, is_last):
        if is_first:
            xb = _from_token_tiles(x_ref, rows).astype(BF16)
            xb_ref[0:rows, :] = xb
        else:
            xb = xb_ref[0:rows, :]
        tf = wg_ref.shape[2]
        halves = [slice(c, c + 256) for c in range(0, tf, 256)]
        acts = []
        for sl in halves:
            gt = jnp.dot(xb, wg_ref[0, :, sl], preferred_element_type=F32)
            up = jnp.dot(xb, wu_ref[0, :, sl], preferred_element_type=F32)
            acts.append((_silu(gt) * up).astype(BF16))
        total = sum(jnp.dot(act, wd_ref[0, sl, :], preferred_element_type=F32) for act, sl in zip(acts, halves))
        if not is_first:
            total = acc_ref[0:rows, :] + total
        if is_last:
            _to_token_tiles(o_ref, total, rows)
            if rows < tm:
                o_ref[pl.ds(rows * SUBLANES, (tm - rows) * SUBLANES), :] = jnp.zeros(
                    ((tm - rows) * SUBLANES, LANES), F32)
        else:
            acc_ref[0:rows, :] = total

    for rows, active in ((tm, whole), (tm // 2, first_half_only)):
        if nf == 1:
            variants = [(True, True, f == 0)]
        else:
            variants = [(True, False, f == 0), (False, True, last)]
            if nf > 2:
                variants.append((False, False, jnp.logical_and(f > 0, f < nf - 1)))
        for is_first, is_last, on_step in variants:
            pl.when(jnp.logical_and(active, on_step))(functools.partial(swiglu_step, rows, is_first, is_last))

    @pl.when(jnp.logical_and(jnp.logical_not(used), last))
    def _():
        o_ref[...] = jnp.zeros_like(o_ref)


def _experts(x_tiles, nb, block_e, n_used, n_halves, wg, wu, wd, tm, tf):
    n_exp, d, dff = wg.shape
    nf = dff // tf

    def xmap(i, f, be, nu, nh):
        return (jnp.minimum(i, nu[0] - 1), 0)

    def fidx(i, f, nu):
        return jnp.where(i < nu[0], f, nf - 1)

    grid_spec = pltpu.PrefetchScalarGridSpec(
        num_scalar_prefetch=3,
        grid=(nb, nf),
        in_specs=[pl.BlockSpec((tm * SUBLANES, LANES), xmap),
                  pl.BlockSpec((1, d, tf), lambda i, f, be, nu, nh: (be[i], 0, fidx(i, f, nu))),
                  pl.BlockSpec((1, d, tf), lambda i, f, be, nu, nh: (be[i], 0, fidx(i, f, nu))),
                  pl.BlockSpec((1, tf, d), lambda i, f, be, nu, nh: (be[i], fidx(i, f, nu), 0))],
        out_specs=pl.BlockSpec((tm * SUBLANES, LANES), lambda i, f, be, nu, nh: (i, 0)),
        scratch_shapes=[pltpu.VMEM((tm, d), F32), pltpu.VMEM((tm, d), BF16)])
    return pl.pallas_call(
        functools.partial(_expert_kernel, tm=tm, nf=nf),
        grid_spec=grid_spec,
        out_shape=jax.ShapeDtypeStruct((nb * tm * SUBLANES, LANES), F32),
        compiler_params=pltpu.CompilerParams(dimension_semantics=("arbitrary", "arbitrary"),
                                             vmem_limit_bytes=VMEM_LIMIT_BYTES),
        name="moe_experts",
    )(block_e, n_used, n_halves, x_tiles, wg, wu, wd)


def _combine_kernel(cur_ref, nxt_ref, y_hbm, w_ref, x_ref, gate_ref, g_ref, b_ref, o_ref, ybuf_ref, sems,
                    *, tc, alpha):
    i = pl.program_id(0)
    nt = TOP_K * tc

    def fetch(idx_ref, slot):
        buf, sem = ybuf_ref.at[slot], sems.at[slot]
        _issue_tile_copies(nt, lambda g, u: (_tile_at(y_hbm, idx_ref[0, 0, g * _COPY_UNROLL + u]),
                                             _tile_at(buf, g * _COPY_UNROLL + u), sem))

    @pl.when(i == 0)
    def _():
        fetch(cur_ref, 0)

    @pl.when(i + 1 < pl.num_programs(0))
    def _():
        fetch(nxt_ref, (i + 1) % 2)

    slot = i % 2
    yb = ybuf_ref.at[slot]
    pltpu.make_async_copy(y_hbm.at[pl.ds(0, nt * SUBLANES)], yb, sems.at[slot]).wait()
    w = w_ref[...]
    y = (w[:, 0:1] * _from_token_tiles(yb, tc, 0, TOP_K)
         + w[:, 1:2] * _from_token_tiles(yb, tc, 1, TOP_K))
    r = alpha * x_ref[0] + (1.0 + gate_ref[0]) * y
    o_ref[0] = _layer_norm_rows(r, g_ref[...], b_ref[...])


def _combine(y_tiles, dest, top_w, x, gate2, ln_g, ln_b, alpha, tc):
    bsz, seq, d = x.shape
    nj = seq // tc
    steps = bsz * nj
    nt = TOP_K * tc
    dest_blocks = dest.astype(jnp.int32).reshape(steps, 1, nt)
    return pl.pallas_call(
        functools.partial(_combine_kernel, tc=tc, alpha=alpha),
        grid=(steps,),
        in_specs=[pl.BlockSpec((1, 1, nt), lambda i: (i, 0, 0), memory_space=pltpu.SMEM),
                  pl.BlockSpec((1, 1, nt), lambda i: (jnp.minimum(i + 1, steps - 1), 0, 0),
                               memory_space=pltpu.SMEM),
                  pl.BlockSpec(memory_space=pl.ANY),
                  pl.BlockSpec((tc, LANES), lambda i: (i, 0)),
                  pl.BlockSpec((1, tc, d), lambda i: (i // nj, i % nj, 0)),
                  pl.BlockSpec((1, 1, d), lambda i: (i // nj, 0, 0)), _const_spec((1, d)), _const_spec((1, d))],
        out_specs=pl.BlockSpec((1, tc, d), lambda i: (i // nj, i % nj, 0)),
        out_shape=jax.ShapeDtypeStruct((bsz, seq, d), F32),
        scratch_shapes=[pltpu.VMEM((2, nt * SUBLANES, LANES), F32), pltpu.SemaphoreType.DMA((2,))],
        compiler_params=pltpu.CompilerParams(dimension_semantics=("arbitrary",),
                                             vmem_limit_bytes=VMEM_LIMIT_BYTES),
        name="moe_combine",
    )(dest_blocks, dest_blocks, y_tiles, top_w, x, gate2, ln_g, ln_b)


def _moe(oa, oc, lw, x, gate1, scale2, shift2, gate2, wr2, stri, wg, wu, wd, ln_g, ln_b, alpha,
         tm_tok, tm, tf, tc):
    bsz, seq, d = x.shape
    n_tok = bsz * seq
    n_asg = n_tok * TOP_K
    x1 = _outproj(oa, oc, lw, x, gate1, alpha, tm_tok)
    h_tiles, idx128, w128, cnt = _router(x1, scale2, shift2, wr2, stri, tm_tok)
    counts = cnt[0, :N_EXPERTS].astype(jnp.int32)
    padded = (counts + tm - 1) // tm * tm
    pad_end = jnp.cumsum(padded)
    pad_start = pad_end - padded
    e_ids, ranks = idx128[:, 0:TOP_K], idx128[:, TOP_K:2 * TOP_K]
    start_of = jnp.sum(jnp.where(e_ids[:, :, None] == jnp.arange(N_EXPERTS, dtype=jnp.int32), pad_start, 0), axis=-1)
    dest = (start_of + ranks).reshape(n_asg)
    nb = n_asg // tm + N_EXPERTS
    n_rows = nb * tm
    n_fill = n_rows - n_asg
    fill_end = jnp.cumsum(padded - counts)
    slot = jnp.arange(n_fill, dtype=jnp.int32)
    slot_e = jnp.sum((slot[:, None] >= fill_end[None, :]).astype(jnp.int32), axis=1)
    in_group = slot_e < N_EXPERTS
    ge = jnp.minimum(slot_e, N_EXPERTS - 1)
    fill_dst = jnp.where(in_group, (pad_start + counts)[ge] + slot - (fill_end - (padded - counts))[ge],
                         pad_end[-1] + slot - fill_end[-1])
    x_tiles = _group_rows(h_tiles, dest, fill_dst, n_rows, _tile(n_tok, 2048))
    block_row = jnp.arange(nb, dtype=jnp.int32) * tm
    block_e = jnp.minimum(jnp.sum((block_row[:, None] >= pad_end[None, :]).astype(jnp.int32), axis=1),
                          N_EXPERTS - 1)
    n_used = (pad_end[-1:] // tm).astype(jnp.int32)
    rows_in_block = (pad_start + counts)[block_e] - block_row
    n_halves = jnp.where(rows_in_block > tm // 2, 2, 1).astype(jnp.int32)
    y_tiles = _experts(x_tiles, nb, block_e, n_used, n_halves, wg, wu, wd, tm, tf)
    return _combine(y_tiles, dest, w128, x1, gate2, ln_g, ln_b, alpha, tc)


def _layer_weights(layer, w_in, b_fgate, w_conv, q_norm_g, kv_norm_g, w_uq, w_ukv, head_norm_g, w_o,
                   ln1_g, ln1_b):
    d = w_in.shape[1]
    sizes = [FOX_W, FOX_W, FOX_W, H_FOX, CONV_W, CONV_W, CONV_W, Q_LORA, KV_LORA, QK_ROPE]
    offs = np.concatenate([[0], np.cumsum(sizes)])
    wi = w_in[layer]
    fq, fk, fv, fl, bg, cg, hin, cq, ckv, kr = [wi[:, offs[i]:offs[i + 1]] for i in range(len(sizes))]
    half = QK_ROPE // 2
    z64 = jnp.zeros((d, QK_NOPE), F32)
    z32 = jnp.zeros((d, LANES - QK_NOPE - QK_ROPE), F32)
    wa = jnp.concatenate([
        fq, fk, fv,
        jnp.pad(jnp.repeat(fl, _N_SPLIT, axis=1), ((0, 0), (0, LANES - _N_SPLIT * H_FOX))), bg, cg, hin, cq, ckv,
        jnp.concatenate([z64, kr, z32], axis=1),
        jnp.concatenate([z64, kr[:, half:], kr[:, :half], z32], axis=1)], axis=1)
    assert wa.shape[1] == _NA
    q3 = w_uq[layer].reshape(Q_LORA, H_MLA, QK_NOPE + QK_ROPE)
    q_main = jnp.pad(q3, ((0, 0), (0, 0), (0, LANES - QK_NOPE - QK_ROPE)))
    q_swap = jnp.concatenate([jnp.zeros((Q_LORA, H_MLA, QK_NOPE), F32), q3[:, :, QK_NOPE + half:],
                              q3[:, :, QK_NOPE:QK_NOPE + half],
                              jnp.zeros((Q_LORA, H_MLA, LANES - QK_NOPE - QK_ROPE), F32)], axis=2)
    wuq = jnp.concatenate([q_main.reshape(Q_LORA, H_MLA * LANES), q_swap.reshape(Q_LORA, H_MLA * LANES)],
                          axis=1).astype(BF16)
    kv3 = w_ukv[layer].reshape(KV_LORA, H_MLA, QK_NOPE + V_DIM)
    k_nope = jnp.pad(kv3[:, :, :QK_NOPE], ((0, 0), (0, 0), (0, LANES - QK_NOPE))).reshape(KV_LORA, H_MLA * LANES)
    wukv = jnp.concatenate([k_nope, kv3[:, :, QK_NOPE:].reshape(KV_LORA, MLA_W)], axis=1).astype(BF16)
    hg = head_norm_g[layer]
    wo = w_o[layer]
    c0, c1 = FOX_W, FOX_W + CONV_W
    return {
        "wa": wa, "wuq": wuq, "wukv": wukv,
        "qg": q_norm_g[layer].reshape(1, Q_LORA), "kvg": kv_norm_g[layer].reshape(1, KV_LORA),
        "bf": jnp.pad(jnp.repeat(b_fgate[layer], _N_SPLIT), (0, LANES - _N_SPLIT * H_FOX)).reshape(1, LANES),
        "wconv": jnp.pad(w_conv[layer], ((0, SUBLANES - CONV_K), (0, 0))),
        "hg_conv": hg[c0:c1].reshape(1, CONV_W),
        "hg_attn": jnp.concatenate([hg[:c0], hg[c1:]]).reshape(1, FOX_W + MLA_W),
        "wo_a": jnp.concatenate([wo[:c0], wo[c1:]], axis=0).astype(BF16),
        "wo_c": wo[c0:c1].astype(BF16),
        "ln1_g": ln1_g[layer].reshape(1, d), "ln1_b": ln1_b[layer].reshape(1, d),
    }


def _constants(seq, ts):
    pos = jnp.arange(seq, dtype=F32)
    inv_freq = ROPE_THETA ** (-jnp.arange(0, QK_ROPE, 2, dtype=F32) / QK_ROPE)
    ang = pos[:, None] * inv_freq[None, :]
    cos, sin = jnp.cos(ang), jnp.sin(ang)
    pad_r = jnp.zeros((seq, LANES - QK_NOPE - QK_ROPE), F32)
    ctab = jnp.concatenate([jnp.ones((seq, QK_NOPE), F32), cos, cos, pad_r], axis=1)
    stab = jnp.concatenate([jnp.zeros((seq, QK_NOPE), F32), -sin, sin, pad_r], axis=1)
    q_scale = (QK_NOPE + QK_ROPE) ** -0.5 * LOG2E
    lanes = np.arange(LANES)
    lm = np.zeros((SUBLANES, LANES), np.float32)
    for part in range(_N_SPLIT):
        lm[part] = (lanes % _N_SPLIT == part) & (lanes < _N_SPLIT * H_FOX)
    for row, base in ((3, HEAD_DIM), (5, 0)):
        lm[row] = (lanes >= base) & (lanes < base + _N_SPLIT)
        lm[row + 1] = (lanes >= base + _N_SPLIT) & (lanes < base + 2 * _N_SPLIT)

    def group_mean(n):
        gidx = np.arange(n) // HEAD_DIM
        return (gidx[:, None] == gidx[None, :]).astype(np.float32) / HEAD_DIM

    return {
        "cq": ctab * q_scale, "sq": stab * q_scale, "ck": ctab, "sk": stab,
        "tri": jnp.asarray(np.tril(np.ones((ts, ts), np.float32)), BF16),
        "stri": jnp.asarray(np.tril(np.ones((ts, ts), np.float32), k=-1), BF16),
        "lane_masks": jnp.asarray(lm),
        "gm256": jnp.asarray(group_mean(CONV_W), BF16), "gm128": jnp.asarray(group_mean(LANES), BF16),
    }


def _tile(n, pref):
    t = min(n, pref)
    assert n % t == 0, (n, pref)
    return t


def kernel(x, c, w_mod, b_mod, w_in, b_fgate, w_conv, q_norm_g, kv_norm_g, w_uq, w_ukv, head_norm_g, w_o, ln1_g, ln1_b, ln2_g, ln2_b, ffn_w_gate, ffn_w_up, ffn_w_down, router_w, exp_w_gate, exp_w_up, exp_w_down):
    bsz, seq, d = x.shape
    depth = w_mod.shape[0]
    assert d == D_MODEL and bsz <= SUBLANES
    alpha = (2 * depth) ** 0.25
    ts = _tile(seq, 512)
    tq = _tile(seq, 512)
    tm_e = 1024
    tf_e = 1792
    tc = _tile(seq, 256)
    assert (bsz * seq * TOP_K) % tm_e == 0 and exp_w_gate.shape[-1] % tf_e == 0

    consts = _constants(seq, ts)
    c_pad = jnp.pad(c, ((0, SUBLANES - bsz), (0, 0)))
    mod = _modulation(c_pad, w_mod, b_mod)[:, :bsz, :]

    for layer in range(depth):
        m6 = mod[layer].reshape(bsz, 6, 1, d)
        shift1, scale1, gate1, shift2, scale2, gate2 = [m6[:, i] for i in range(6)]
        lw = _layer_weights(layer, w_in, b_fgate, w_conv, q_norm_g, kv_norm_g, w_uq, w_ukv, head_norm_g,
                            w_o, ln1_g, ln1_b)
        qp, kp, vt, oc = _inproj(x, scale1, shift1, lw, consts, ts)
        oa = _attention(qp, kp, vt, lw["hg_attn"], consts["gm128"], tq, 12)
        j = layer // 2
        g2, b2 = ln2_g[layer].reshape(1, d), ln2_b[layer].reshape(1, d)
        if layer % 2 == 0:
            x = _ffn_dense(oa, oc, lw, x, gate1, scale2, shift2, gate2, ffn_w_gate[j].astype(BF16),
                           ffn_w_up[j].astype(BF16), ffn_w_down[j].astype(BF16), g2, b2, alpha, ts)
        else:
            wr = jnp.pad(router_w[j], ((0, 0), (0, LANES - N_EXPERTS)))
            wr_hi = wr.astype(BF16)
            wr_lo = (wr - wr_hi.astype(F32)).astype(BF16)
            x = _moe(oa, oc, lw, x, gate1, scale2, shift2, gate2, jnp.stack([wr_hi, wr_lo]), consts["stri"],
                     exp_w_gate[j].astype(BF16), exp_w_up[j].astype(BF16), exp_w_down[j].astype(BF16),
                     g2, b2, alpha, ts, tm_e, tf_e, tc)
    return x
```
